```python
import math
import jax, jax.numpy as jnp
from jax import lax
import numpy as np

D_MODEL = 2048
BATCH = 4
SEQ = 2048
DEPTH = 1

RMS_EPS = 1e-6
GLA_HEADS = 4
GLA_DK = D_MODEL // 2 // GLA_HEADS
GLA_DV = D_MODEL // GLA_HEADS
GLA_QK = GLA_HEADS * GLA_DK
GLA_V = GLA_HEADS * GLA_DV
GLA_GATE_RANK = 16
GLA_TAU = 16.0
GLA_CHUNK = 64
DSA_HEADS = 16
DSA_HEAD_DIM = 128
DSA_W = DSA_HEADS * DSA_HEAD_DIM
IDX_HEADS = 8
IDX_DIM = 64
IDX_TOPK_MAX = 256
DSA_QBLOCK = 64
ROPE_THETA = 500000.0
ROPE_FRACTION = 4
D_FF = -(-8 * D_MODEL // (3 * 256)) * 256
N_MOD = 6

IN_SPLITS = (GLA_QK, GLA_QK, GLA_V, GLA_V, GLA_GATE_RANK,
             DSA_W, DSA_W, DSA_W,
             IDX_HEADS * IDX_DIM, IDX_DIM, IDX_HEADS,
             D_MODEL, D_MODEL)
IN_WIDTH = sum(IN_SPLITS)

kernel_name = "hybrid_gla_dsa_gated_merge_adaln"


def rms_norm(x, gain):
    xf = x.astype(jnp.float32)
    y = xf * lax.rsqrt(jnp.mean(xf * xf, axis=-1, keepdims=True) + RMS_EPS)
    return (y * gain.astype(jnp.float32)).astype(x.dtype)


def partial_rope(x, positions):
    d = x.shape[-1]
    rot = d // ROPE_FRACTION
    half = rot // 2
    inv_freq = jnp.power(ROPE_THETA, -jnp.arange(0, rot, 2, dtype=jnp.float32) / rot)
    ang = positions.astype(jnp.float32)[..., None] * inv_freq
    cos = jnp.cos(ang)[:, :, None, :]
    sin = jnp.sin(ang)[:, :, None, :]
    xf = x.astype(jnp.float32)
    x1, x2, rest = xf[..., :half], xf[..., half:rot], xf[..., rot:]
    out = jnp.concatenate([x1 * cos - x2 * sin, x1 * sin + x2 * cos, rest], axis=-1)
    return out.astype(x.dtype)


def gla_chunked(q, k, v, log_g):
    B, S, H, dk = q.shape
    dv = v.shape[-1]
    C = GLA_CHUNK
    N = S // C

    def chunks(a):
        return jnp.moveaxis(a.astype(jnp.float32).reshape(B, N, C, H, a.shape[-1]), 1, 0)

    qc, kc, vc = chunks(q), chunks(k), chunks(v)
    bc = jnp.cumsum(chunks(log_g), axis=2)
    causal = jnp.tril(jnp.ones((C, C), dtype=bool))[None, :, :, None, None]

    def step(state, inp):
        q_, k_, v_, b_ = inp
        diff = b_[:, :, None] - b_[:, None, :]
        decay = jnp.exp(jnp.where(causal, diff, -jnp.inf))
        attn = jnp.einsum('bihd,bjhd,bijhd->bijh', q_, k_, decay)
        o = (jnp.einsum('bijh,bjhv->bihv', attn, v_)
             + jnp.einsum('bihd,bhdv->bihv', q_ * jnp.exp(b_), state))
        b_last = b_[:, -1]
        k_dec = k_ * jnp.exp(b_last[:, None] - b_)
        state = state * jnp.exp(b_last)[..., None] + jnp.einsum('bjhd,bjhv->bhdv', k_dec, v_)
        return state, o

    state0 = jnp.zeros((B, H, dk, dv), jnp.float32)
    _, o = lax.scan(step, state0, (qc, kc, vc, bc))
    return jnp.moveaxis(o, 0, 1).reshape(B, S, H, dv)


def dsa_attention(q, k, v, iq, ik, iw):
    B, S, H, Dh = q.shape
    topk = min(IDX_TOPK_MAX, S // 4)
    QB = DSA_QBLOCK
    nblk = S // QB
    key_pos = jnp.arange(S)

    def blocks(a):
        return jnp.moveaxis(a.reshape((B, nblk, QB) + a.shape[2:]), 1, 0)

    ik32 = ik.astype(jnp.float32)

    def one_block(inp):
        qb, iqb, iwb, start = inp
        qpos = start + jnp.arange(QB)
        dots = jnp.einsum('bqhd,bsd->bqhs', iqb.astype(jnp.float32), ik32) * (IDX_DIM ** -0.5)
        score = jnp.einsum('bqh,bqhs->bqs', iwb.astype(jnp.float32), jax.nn.relu(dots))
        admissible = key_pos[None, :] <= qpos[:, None]
        score = jnp.where(admissible[None], score, -jnp.inf)
        _, idx = lax.top_k(score, topk)
        kg = jax.vmap(lambda kk, ii: kk[ii])(k, idx)
        vg = jax.vmap(lambda vv, ii: vv[ii])(v, idx)
        valid = idx <= qpos[None, :, None]
        logits = jnp.einsum('bqhd,bqkhd->bqhk', qb, kg).astype(jnp.float32) * (Dh ** -0.5)
        logits = jnp.where(valid[:, :, None, :], logits, -jnp.inf)
        p = jax.nn.softmax(logits, axis=-1)
        return jnp.einsum('bqhk,bqkhd->bqhd', p.astype(vg.dtype), vg)

    starts = jnp.arange(nblk) * QB
    out = lax.map(one_block, (blocks(q), blocks(iq), blocks(iw), starts))
    return jnp.moveaxis(out, 0, 1).reshape(B, S, H, Dh)


def hybrid_mixer(h, positions, w_in, gla_gate_up, gla_gate_bias, gla_norm_gain,
                 w_branch_gla, w_branch_dsa, w_merge_out):
    B, S, _ = h.shape
    proj = h @ w_in
    offsets = [int(o) for o in np.cumsum(IN_SPLITS)[:-1]]
    (g_q, g_k, g_v, g_r, g_lr, d_q, d_k, d_v, i_q, i_k, i_w,
     gate_a, gate_b) = jnp.split(proj, offsets, axis=-1)

    q_a = g_q.reshape(B, S, GLA_HEADS, GLA_DK) * (GLA_DK ** -0.5)
    k_a = g_k.reshape(B, S, GLA_HEADS, GLA_DK)
    v_a = g_v.reshape(B, S, GLA_HEADS, GLA_DV)
    log_g = jax.nn.log_sigmoid((g_lr @ gla_gate_up + gla_gate_bias).astype(jnp.float32)) / GLA_TAU
    o_a = gla_chunked(q_a, k_a, v_a, log_g.reshape(B, S, GLA_HEADS, GLA_DK))
    o_a = o_a * lax.rsqrt(jnp.mean(o_a * o_a, axis=-1, keepdims=True) + RMS_EPS)
    o_a = o_a * gla_norm_gain.astype(jnp.float32).reshape(GLA_HEADS, GLA_DV)
    o_a = (o_a.reshape(B, S, GLA_V) * jax.nn.silu(g_r.astype(jnp.float32))).astype(h.dtype)
    y_a = o_a @ w_branch_gla

    q_b = partial_rope(d_q.reshape(B, S, DSA_HEADS, DSA_HEAD_DIM), positions)
    k_b = partial_rope(d_k.reshape(B, S, DSA_HEADS, DSA_HEAD_DIM), positions)
    v_b = d_v.reshape(B, S, DSA_HEADS, DSA_HEAD_DIM)
    iq = partial_rope(i_q.reshape(B, S, IDX_HEADS, IDX_DIM), positions)
    ik = partial_rope(i_k[:, :, None, :], positions)[:, :, 0, :]
    iw = i_w * (IDX_HEADS ** -0.5)
    o_b = dsa_attention(q_b, k_b, v_b, iq, ik, iw).reshape(B, S, DSA_W)
    y_b = o_b @ w_branch_dsa

    merged = jax.nn.sigmoid(gate_a) * y_a + jax.nn.sigmoid(gate_b) * y_b
    return merged @ w_merge_out


def swiglu(h, w_gate_up, w_down):
    g, u = jnp.split(h @ w_gate_up, 2, axis=-1)
    return (jax.nn.silu(g) * u) @ w_down


def setup_inputs(seed: int = 0) -> dict:
    key = jax.random.key(seed)
    ks = jax.random.split(key, 17)
    f32 = jnp.float32

    def dense(k, shape, fan_in):
        return jax.random.normal(k, shape, f32) * (fan_in ** -0.5)

    x = jax.random.normal(ks[0], (BATCH, SEQ, D_MODEL), f32)
    c = jax.random.normal(ks[1], (BATCH, D_MODEL), f32)
    positions = jnp.broadcast_to(jnp.arange(SEQ, dtype=jnp.int32)[None, :], (BATCH, SEQ))
    norm1_gain = 1.0 + 0.02 * jax.random.normal(ks[2], (DEPTH, D_MODEL), f32)
    norm2_gain = 1.0 + 0.02 * jax.random.normal(ks[3], (DEPTH, D_MODEL), f32)
    w_ada = dense(ks[4], (DEPTH, D_MODEL, N_MOD * D_MODEL), D_MODEL)
    b_ada = 0.02 * jax.random.normal(ks[5], (DEPTH, N_MOD * D_MODEL), f32)
    w_in = dense(ks[6], (DEPTH, D_MODEL, IN_WIDTH), D_MODEL)
    gla_gate_up = dense(ks[7], (DEPTH, GLA_GATE_RANK, GLA_QK), GLA_GATE_RANK)
    gla_gate_bias = 0.1 * jax.random.normal(ks[8], (DEPTH, GLA_QK), f32)
    gla_norm_gain = 1.0 + 0.02 * jax.random.normal(ks[9], (DEPTH, GLA_V), f32)
    w_branch_gla = dense(ks[10], (DEPTH, GLA_V, D_MODEL), GLA_V)
    w_branch_dsa = dense(ks[11], (DEPTH, DSA_W, D_MODEL), DSA_W)
    w_merge_out = dense(ks[12], (DEPTH, D_MODEL, D_MODEL), D_MODEL)
    w_ffn_gate_up = dense(ks[13], (DEPTH, D_MODEL, 2 * D_FF), D_MODEL)
    w_ffn_down = dense(ks[14], (DEPTH, D_FF, D_MODEL), D_FF)
    final_norm_gain = 1.0 + 0.02 * jax.random.normal(ks[15], (D_MODEL,), f32)
    return {"x": x, "c": c, "positions": positions,
            "norm1_gain": norm1_gain, "norm2_gain": norm2_gain,
            "w_ada": w_ada, "b_ada": b_ada, "w_in": w_in,
            "gla_gate_up": gla_gate_up, "gla_gate_bias": gla_gate_bias,
            "gla_norm_gain": gla_norm_gain, "w_branch_gla": w_branch_gla,
            "w_branch_dsa": w_branch_dsa, "w_merge_out": w_merge_out,
            "w_ffn_gate_up": w_ffn_gate_up, "w_ffn_down": w_ffn_down,
            "final_norm_gain": final_norm_gain}


def reference(x, c, positions, norm1_gain, norm2_gain, w_ada, b_ada, w_in,
              gla_gate_up, gla_gate_bias, gla_norm_gain, w_branch_gla, w_branch_dsa,
              w_merge_out, w_ffn_gate_up, w_ffn_down, final_norm_gain):
    for layer in range(DEPTH):
        mod = jax.nn.silu(c) @ w_ada[layer] + b_ada[layer]
        sh1, sc1, g1, sh2, sc2, g2 = [m[:, None, :] for m in jnp.split(mod, N_MOD, axis=-1)]
        h = rms_norm(x, norm1_gain[layer]) * (1.0 + sc1) + sh1
        x = x + g1 * hybrid_mixer(h, positions, w_in[layer], gla_gate_up[layer],
                                  gla_gate_bias[layer], gla_norm_gain[layer],
                                  w_branch_gla[layer], w_branch_dsa[layer], w_merge_out[layer])
        h = rms_norm(x, norm2_gain[layer]) * (1.0 + sc2) + sh2
        x = x + g2 * swiglu(h, w_ffn_gate_up[layer], w_ffn_down[layer])
    return rms_norm(x, final_norm_gain)
```

```python
import functools

import jax
import jax.numpy as jnp
from jax import lax
from jax.experimental import pallas as pl
from jax.experimental.pallas import tpu as pltpu

F32 = jnp.float32
BF16 = jnp.bfloat16
HIGHEST = lax.Precision.HIGHEST

RMS_EPS = 1e-6
GLA_HEADS = 4
GLA_GATE_RANK = 16
GLA_TAU = 16.0
GLA_CHUNK = 64
GLA_SUB = 16
DSA_HEADS = 16
DSA_HEAD_DIM = 128
IDX_HEADS = 8
IDX_DIM = 64
IDX_TOPK_MAX = 256
ROPE_THETA = 500000.0
ROPE_FRACTION = 4
N_MOD = 6

LANES = 128
VMEM_LIMIT = 48 * 1024 * 1024


def _params(*semantics):
    return pltpu.CompilerParams(dimension_semantics=semantics, vmem_limit_bytes=VMEM_LIMIT)


def _silu(x):
    return x * jax.nn.sigmoid(x)


def _mod_kernel(c_ref, w_ref, b_ref, o_ref):
    a = _silu(c_ref[...])
    o_ref[...] = jnp.dot(a, w_ref[...], precision=HIGHEST, preferred_element_type=F32) + b_ref[...]


def _modulation(c_pad, w_ada, b_ada, tn=1024):
    rows, d = c_pad.shape
    n = w_ada.shape[1]
    return pl.pallas_call(
        _mod_kernel,
        grid=(n // tn,),
        in_specs=[pl.BlockSpec((rows, d), lambda j: (0, 0)),
                  pl.BlockSpec((d, tn), lambda j: (0, j)),
                  pl.BlockSpec((1, tn), lambda j: (0, j))],
        out_specs=pl.BlockSpec((rows, tn), lambda j: (0, j)),
        out_shape=jax.ShapeDtypeStruct((rows, n), F32),
        compiler_params=_params("parallel"),
        name="adaln_mod",
    )(c_pad, w_ada, b_ada)


def _rms_modulate(x, gain, scale, shift):
    y = x * lax.rsqrt(jnp.mean(x * x, axis=-1, keepdims=True) + RMS_EPS)
    return (y * gain) * (1.0 + scale) + shift


def _norm_mod_kernel(x_ref, gain_ref, sc_ref, sh_ref, o_ref):
    o_ref[...] = _rms_modulate(x_ref[...], gain_ref[...], sc_ref[0], sh_ref[0]).astype(o_ref.dtype)


def _norm_modulate(x2d, gain, sc, sh, seq, tm=512):
    t, d = x2d.shape
    per_batch = seq // tm
    return pl.pallas_call(
        _norm_mod_kernel,
        grid=(t // tm,),
        in_specs=[pl.BlockSpec((tm, d), lambda i: (i, 0)),
                  pl.BlockSpec((1, d), lambda i: (0, 0)),
                  pl.BlockSpec((1, 1, d), lambda i: (i // per_batch, 0, 0)),
                  pl.BlockSpec((1, 1, d), lambda i: (i // per_batch, 0, 0))],
        out_specs=pl.BlockSpec((tm, d), lambda i: (i, 0)),
        out_shape=jax.ShapeDtypeStruct((t, d), BF16),
        compiler_params=_params("parallel"),
        name="norm1_modulate",
    )(x2d, gain, sc, sh)


def _rope_lanes(x, cos, sin, half, period):
    lane = lax.broadcasted_iota(jnp.int32, x.shape, 1) % period
    upper = pltpu.roll(x, LANES - half, axis=1)
    lower = pltpu.roll(x, half, axis=1)
    first = lane < half
    second = jnp.logical_and(lane >= half, lane < 2 * half)
    return jnp.where(first, x * cos - upper * sin,
                     jnp.where(second, lower * sin + x * cos, x))


def _proj_kernel(h_ref, w_ref, *rest, mode, half, period):
    o_ref = rest[-1]
    acc = jnp.dot(h_ref[...], w_ref[...], preferred_element_type=F32)
    if mode == "scale":
        o_ref[...] = (acc * rest[0][...]).astype(o_ref.dtype)
    elif mode == "rope":
        cos = rest[0][...]
        sin = rest[1][...]
        for g in range(acc.shape[1] // LANES):
            cols = slice(g * LANES, (g + 1) * LANES)
            o_ref[:, cols] = _rope_lanes(acc[:, cols], cos, sin, half, period).astype(o_ref.dtype)
    else:
        o_ref[...] = acc.astype(o_ref.dtype)


def _project(h, w, out_dtype, *, mode="plain", extra=(), half=0, period=LANES, tm=1024, tn=512,
             name="proj"):
    t, d = h.shape
    n = w.shape[1]
    tn = min(tn, n)
    tm = min(tm, t)
    in_specs = [pl.BlockSpec((tm, d), lambda i, j: (i, 0)),
                pl.BlockSpec((d, tn), lambda i, j: (0, j))]
    if mode == "scale":
        in_specs.append(pl.BlockSpec((1, tn), lambda i, j: (0, j)))
    elif mode == "rope":
        in_specs += [pl.BlockSpec((tm, LANES), lambda i, j: (i, 0)),
                     pl.BlockSpec((tm, LANES), lambda i, j: (i, 0))]
    return pl.pallas_call(
        functools.partial(_proj_kernel, mode=mode, half=half, period=period),
        grid=(t // tm, n // tn),
        in_specs=in_specs,
        out_specs=pl.BlockSpec((tm, tn), lambda i, j: (i, j)),
        out_shape=jax.ShapeDtypeStruct((t, n), out_dtype),
        compiler_params=_params("parallel", "parallel"),
        name=name,
    )(h, w, *extra)


def _gla_kernel(q_ref, k_ref, v_ref, gr_ref, sm_ref, gup_ref, gb_ref, ng_ref, o_ref, st_ref):
    @pl.when(pl.program_id(2) == 0)
    def _():
        st_ref[...] = jnp.zeros_like(st_ref)

    rows = q_ref.shape[0]
    dk = q_ref.shape[1]
    c, sub = GLA_CHUNK, GLA_SUB
    n_sub = c // sub

    z = jnp.dot(sm_ref[...], gup_ref[...], precision=HIGHEST, preferred_element_type=F32) + gb_ref[...]
    log_g = -(jnp.maximum(-z, 0.0) + jnp.log1p(jnp.exp(-jnp.abs(z)))) * (1.0 / GLA_TAU)
    r_i = lax.broadcasted_iota(jnp.int32, (rows, rows), 0)
    c_i = lax.broadcasted_iota(jnp.int32, (rows, rows), 1)
    tri = jnp.where(jnp.logical_and(r_i // c == c_i // c, c_i <= r_i), 1.0, 0.0).astype(F32)
    b = jnp.dot(tri, log_g, precision=HIGHEST, preferred_element_type=F32)

    q = q_ref[...]
    k = k_ref[...]

    row = lax.broadcasted_iota(jnp.int32, (rows, 1), 0)
    row_in_sub = row % sub
    col = lax.broadcasted_iota(jnp.int32, (rows, c), 1)
    target = lax.broadcasted_iota(jnp.int32, (rows, c), 0) % c
    a_diag = jnp.zeros((rows, c), F32)
    for delta in range(sub):
        if delta == 0:
            k_d, b_d = k, b
        else:
            k_d = pltpu.roll(k, delta, axis=0)
            b_d = pltpu.roll(b, delta, axis=0)
        decay = jnp.exp(jnp.where(row_in_sub >= delta, b - b_d, -jnp.inf))
        a = jnp.sum(q * k_d * decay, axis=-1, keepdims=True)
        a_diag = jnp.where(col == target - delta, a, a_diag)

    for ch in range(rows // c):
        base = ch * c
        qc = q[base:base + c]
        kc = k[base:base + c]
        bc = b[base:base + c]
        vc = v_ref[base:base + c, :]
        st = st_ref[...]

        o_inter = lax.dot_general((qc * jnp.exp(bc)).astype(BF16), st.astype(BF16),
                                  (((1,), (1,)), ((), ())), preferred_element_type=F32)

        blocks = [jnp.zeros((sub, c), F32)]
        for i_sub in range(1, n_sub):
            lo = i_sub * sub
            ref = bc[lo - 1:lo]
            q_hat = qc[lo:lo + sub] * jnp.exp(bc[lo:lo + sub] - ref)
            k_hat = kc[0:lo] * jnp.exp(ref - bc[0:lo])
            k_pad = jnp.concatenate([k_hat, jnp.zeros((c - lo, dk), F32)], axis=0)
            blocks.append(lax.dot_general(q_hat.astype(BF16), k_pad.astype(BF16),
                                          (((1,), (1,)), ((), ())), preferred_element_type=F32))
        attn = jnp.concatenate(blocks, axis=0) + a_diag[base:base + c]

        o = jnp.dot(attn.astype(BF16), vc, preferred_element_type=F32) + o_inter

        b_last = bc[c - 1:c]
        k_dec = (kc * jnp.exp(b_last - bc)).astype(BF16)
        st_ref[...] = st * jnp.exp(b_last) + lax.dot_general(
            vc, k_dec, (((0,), (0,)), ((), ())), preferred_element_type=F32)

        o = o * lax.rsqrt(jnp.mean(o * o, axis=-1, keepdims=True) + RMS_EPS) * ng_ref[...]
        o_ref[base:base + c, :] = (o * _silu(gr_ref[base:base + c, :])).astype(o_ref.dtype)


def _gla(qk, v, gr, small, gup_pad, gbias, ngain, batch, seq, rows=256):
    dk = qk.shape[1] // (2 * GLA_HEADS)
    dv = v.shape[1] // GLA_HEADS
    t = qk.shape[0]
    nr = seq // rows

    def row_map(col_fn):
        return lambda b, h, r: (b * nr + r, col_fn(h))

    return pl.pallas_call(
        _gla_kernel,
        grid=(batch, GLA_HEADS, nr),
        in_specs=[pl.BlockSpec((rows, dk), row_map(lambda h: h)),
                  pl.BlockSpec((rows, dk), row_map(lambda h: GLA_HEADS + h)),
                  pl.BlockSpec((rows, dv), row_map(lambda h: h)),
                  pl.BlockSpec((rows, dv), row_map(lambda h: h)),
                  pl.BlockSpec((rows, LANES), row_map(lambda h: 0)),
                  pl.BlockSpec((LANES, dk), lambda b, h, r: (0, h)),
                  pl.BlockSpec((1, dk), lambda b, h, r: (0, h)),
                  pl.BlockSpec((1, dv), lambda b, h, r: (0, h))],
        out_specs=pl.BlockSpec((rows, dv), row_map(lambda h: h)),
        out_shape=jax.ShapeDtypeStruct((t, GLA_HEADS * dv), BF16),
        scratch_shapes=[pltpu.VMEM((dv, dk), F32)],
        compiler_params=_params("parallel", "parallel", "arbitrary"),
        name="gla",
    )(qk, qk, v, gr, small, gup_pad, gbias, ngain)


def _topk_mask_kernel(iq_ref, ik_ref, iw_ref, tri_ref, o_ref, key_ref, *, topk, tq, tk):
    qi = pl.program_id(1)
    seq = ik_ref.shape[0]
    nk = seq // tk
    iq = iq_ref[...]
    w = iw_ref[0] * (IDX_HEADS ** -0.5)
    t_pos = qi * tq + lax.broadcasted_iota(jnp.int32, (tk, tq), 1)

    for kb in range(nk):
        ik = ik_ref[kb * tk:(kb + 1) * tk, 0:IDX_DIM]
        score = jnp.zeros((tk, tq), F32)
        for h in range(IDX_HEADS):
            dots = lax.dot_general(ik, iq[:, h * IDX_DIM:(h + 1) * IDX_DIM],
                                   (((1,), (1,)), ((), ())), precision=HIGHEST,
                                   preferred_element_type=F32)
            score = score + w[h:h + 1, :] * jnp.maximum(dots, 0.0)
        s_pos = kb * tk + lax.broadcasted_iota(jnp.int32, (tk, tq), 0)
        score = jnp.where(s_pos <= t_pos, score + 0.0, -jnp.inf)
        bits = lax.bitcast_convert_type(score, jnp.int32)
        key_ref[kb * tk:(kb + 1) * tk, :] = jnp.where(bits < 0, bits ^ jnp.int32(0x7FFFFFFF), bits)

    def search(i, ans):
        cand = ans ^ jnp.left_shift(jnp.int32(1), 31 - i)
        cnt = jnp.sum(jnp.where(key_ref[...] >= cand, 1.0, 0.0), axis=0, keepdims=True)
        return jnp.where(cnt >= float(topk), cand, ans)

    thr = lax.fori_loop(0, 32, search, jnp.full((1, tq), jnp.iinfo(jnp.int32).min, jnp.int32))

    keys = key_ref[...]
    need = float(topk) - jnp.sum(jnp.where(keys > thr, 1.0, 0.0), axis=0, keepdims=True)
    seen = jnp.zeros((1, tq), F32)
    for kb in range(nk):
        kblk = keys[kb * tk:(kb + 1) * tk]
        eq = jnp.where(kblk == thr, 1.0, 0.0)
        rank = jnp.dot(tri_ref[...], eq.astype(BF16), preferred_element_type=F32) + seen
        seen = seen + jnp.sum(eq, axis=0, keepdims=True)
        chosen = jnp.logical_or(kblk > thr, jnp.logical_and(kblk == thr, rank <= need))
        s_pos = kb * tk + lax.broadcasted_iota(jnp.int32, (tk, tq), 0)
        o_ref[0, kb * tk:(kb + 1) * tk, :] = jnp.where(
            jnp.logical_and(chosen, s_pos <= t_pos), 1.0, 0.0).astype(o_ref.dtype)


def _topk_mask(idx, iw_t, batch, seq, topk, tq=256, tk=256):
    tq = min(tq, seq)
    tk = min(tk, seq)
    nq = seq // tq
    width = IDX_HEADS * IDX_DIM
    tri = jnp.tril(jnp.ones((tk, tk), BF16))
    return pl.pallas_call(
        functools.partial(_topk_mask_kernel, topk=topk, tq=tq, tk=tk),
        grid=(batch, nq),
        in_specs=[pl.BlockSpec((tq, width), lambda b, i: (b * nq + i, 0)),
                  pl.BlockSpec((seq, LANES), lambda b, i: (b, width // LANES)),
                  pl.BlockSpec((1, IDX_HEADS, tq), lambda b, i: (b, 0, i)),
                  pl.BlockSpec((tk, tk), lambda b, i: (0, 0))],
        out_specs=pl.BlockSpec((1, seq, tq), lambda b, i: (b, 0, i)),
        out_shape=jax.ShapeDtypeStruct((batch, seq, seq), F32),
        scratch_shapes=[pltpu.VMEM((seq, tq), jnp.int32)],
        compiler_params=_params("parallel", "parallel"),
        name="indexer_topk_mask",
    )(idx, idx, iw_t, tri)


def _attn_kernel(q_ref, k_ref, v_ref, m_ref, o_ref, *, tk):
    qi = pl.program_id(1)
    tq, dh = q_ref.shape
    q = q_ref[...]
    scale = dh ** -0.5

    def body(kj, carry):
        m, l, acc = carry
        rows = pl.ds(pl.multiple_of(kj * tk, tk), tk)
        s = lax.dot_general(k_ref[rows, :], q, (((1,), (1,)), ((), ())),
                            preferred_element_type=F32) * scale
        s = jnp.where(m_ref[0, rows, :] > 0.0, s, -jnp.inf)
        m_new = jnp.maximum(m, jnp.max(s, axis=0, keepdims=True))
        m_safe = jnp.where(m_new == -jnp.inf, 0.0, m_new)
        alpha = jnp.exp(m - m_safe)
        p = jnp.exp(s - m_safe)
        l = alpha * l + jnp.sum(p, axis=0, keepdims=True)
        acc = alpha * acc + lax.dot_general(v_ref[rows, :], p.astype(BF16), (((0,), (0,)), ((), ())),
                                            preferred_element_type=F32)
        return m_new, l, acc

    n_blocks = (qi * tq + tq + tk - 1) // tk
    init = (jnp.full((1, tq), -jnp.inf, F32), jnp.zeros((1, tq), F32), jnp.zeros((dh, tq), F32))
    _, l, acc = lax.fori_loop(0, n_blocks, body, init)
    o_ref[...] = (acc / l).T.astype(o_ref.dtype)


def _attention(qk, v, mask_t, batch, seq, tq=256, tk=256):
    tq = min(tq, seq)
    tk = min(tk, seq)
    nq = seq // tq
    dh = DSA_HEAD_DIM
    t = qk.shape[0]
    return pl.pallas_call(
        functools.partial(_attn_kernel, tk=tk),
        grid=(batch, nq, DSA_HEADS),
        in_specs=[pl.BlockSpec((tq, dh), lambda b, i, h: (b * nq + i, h)),
                  pl.BlockSpec((seq, dh), lambda b, i, h: (b, DSA_HEADS + h)),
                  pl.BlockSpec((seq, dh), lambda b, i, h: (b, h)),
                  pl.BlockSpec((1, seq, tq), lambda b, i, h: (b, 0, i))],
        out_specs=pl.BlockSpec((tq, dh), lambda b, i, h: (b * nq + i, h)),
        out_shape=jax.ShapeDtypeStruct((t, DSA_HEADS * dh), BF16),
        compiler_params=_params("parallel", "parallel", "arbitrary"),
        name="dsa_attention",
    )(qk, qk, v, mask_t)


def _merge_kernel(oa_ref, ob_ref, wa_ref, wb_ref, ga_ref, gb_ref, o_ref):
    ya = jnp.dot(oa_ref[...], wa_ref[...], preferred_element_type=F32)
    yb = jnp.dot(ob_ref[...], wb_ref[...], preferred_element_type=F32)
    o_ref[...] = (jax.nn.sigmoid(ga_ref[...]) * ya + jax.nn.sigmoid(gb_ref[...]) * yb).astype(o_ref.dtype)


def _merge(o_a, o_b, w_a, w_b, gates, tm=512, tn=512):
    t, d = o_a.shape
    n = w_a.shape[1]
    tm = min(tm, t)
    nj = n // tn
    return pl.pallas_call(
        _merge_kernel,
        grid=(t // tm, nj),
        in_specs=[pl.BlockSpec((tm, d), lambda i, j: (i, 0)),
                  pl.BlockSpec((tm, d), lambda i, j: (i, 0)),
                  pl.BlockSpec((d, tn), lambda i, j: (0, j)),
                  pl.BlockSpec((d, tn), lambda i, j: (0, j)),
                  pl.BlockSpec((tm, tn), lambda i, j: (i, j)),
                  pl.BlockSpec((tm, tn), lambda i, j: (i, nj + j))],
        out_specs=pl.BlockSpec((tm, tn), lambda i, j: (i, j)),
        out_shape=jax.ShapeDtypeStruct((t, n), BF16),
        compiler_params=_params("parallel", "parallel"),
        name="branch_merge",
    )(o_a, o_b, w_a, w_b, gates, gates)


def _mixer_out_kernel(m_ref, w_ref, x_ref, g_ref, gain_ref, sc_ref, sh_ref, x1_ref, h2_ref):
    y = jnp.dot(m_ref[...], w_ref[...], preferred_element_type=F32)
    x1 = x_ref[...] + g_ref[0] * y
    x1_ref[...] = x1
    h2_ref[...] = _rms_modulate(x1, gain_ref[...], sc_ref[0], sh_ref[0]).astype(h2_ref.dtype)


def _mixer_out(merged, w, x2d, g1, gain2, sc2, sh2, seq, tm=256):
    t, d = x2d.shape
    tm = min(tm, seq)
    per_batch = seq // tm
    vec = pl.BlockSpec((1, 1, d), lambda i: (i // per_batch, 0, 0))
    row = pl.BlockSpec((tm, d), lambda i: (i, 0))
    return pl.pallas_call(
        _mixer_out_kernel,
        grid=(t // tm,),
        in_specs=[row, pl.BlockSpec((d, d), lambda i: (0, 0)), row, vec,
                  pl.BlockSpec((1, d), lambda i: (0, 0)), vec, vec],
        out_specs=[row, row],
        out_shape=[jax.ShapeDtypeStruct((t, d), F32), jax.ShapeDtypeStruct((t, d), BF16)],
        compiler_params=_params("parallel"),
        name="mixer_out_norm2",
    )(merged, w, x2d, g1, gain2, sc2, sh2)


def _ffn_up_kernel(h_ref, wg_ref, wu_ref, o_ref):
    g = jnp.dot(h_ref[...], wg_ref[...], preferred_element_type=F32)
    u = jnp.dot(h_ref[...], wu_ref[...], preferred_element_type=F32)
    o_ref[...] = (_silu(g) * u).astype(o_ref.dtype)


def _ffn_up(h2, w_gate_up, tm=1024, tn=512):
    t, d = h2.shape
    d_ff = w_gate_up.shape[1] // 2
    tm = min(tm, t)
    nj = d_ff // tn
    return pl.pallas_call(
        _ffn_up_kernel,
        grid=(t // tm, nj),
        in_specs=[pl.BlockSpec((tm, d), lambda i, j: (i, 0)),
                  pl.BlockSpec((d, tn), lambda i, j: (0, j)),
                  pl.BlockSpec((d, tn), lambda i, j: (0, nj + j))],
        out_specs=pl.BlockSpec((tm, tn), lambda i, j: (i, j)),
        out_shape=jax.ShapeDtypeStruct((t, d_ff), BF16),
        compiler_params=_params("parallel", "parallel"),
        name="ffn_up",
    )(h2, w_gate_up, w_gate_up)


def _ffn_down_kernel(a_ref, w_ref, x_ref, g_ref, gain_ref, o_ref, acc_ref):
    kk = pl.program_id(1)

    @pl.when(kk == 0)
    def _():
        acc_ref[...] = jnp.zeros_like(acc_ref)

    acc_ref[...] += jnp.dot(a_ref[...], w_ref[...], preferred_element_type=F32)

    @pl.when(kk == pl.num_programs(1) - 1)
    def _():
        x2 = x_ref[...] + g_ref[0] * acc_ref[...]
        y = x2 * lax.rsqrt(jnp.mean(x2 * x2, axis=-1, keepdims=True) + RMS_EPS)
        o_ref[...] = y * gain_ref[...]


def _ffn_down(act, w_down, x1, g2, final_gain, seq, tm=512, tk=512):
    t, d_ff = act.shape
    d = w_down.shape[1]
    tm = min(tm, seq)
    per_batch = seq // tm
    return pl.pallas_call(
        _ffn_down_kernel,
        grid=(t // tm, d_ff // tk),
        in_specs=[pl.BlockSpec((tm, tk), lambda i, k: (i, k)),
                  pl.BlockSpec((tk, d), lambda i, k: (k, 0)),
                  pl.BlockSpec((tm, d), lambda i, k: (i, 0)),
                  pl.BlockSpec((1, 1, d), lambda i, k: (i // per_batch, 0, 0)),
                  pl.BlockSpec((1, d), lambda i, k: (0, 0))],
        out_specs=pl.BlockSpec((tm, d), lambda i, k: (i, 0)),
        out_shape=jax.ShapeDtypeStruct((t, d), F32),
        scratch_shapes=[pltpu.VMEM((tm, d), F32)],
        compiler_params=_params("parallel", "arbitrary"),
        name="ffn_down_final_norm",
    )(act, w_down, x1, g2, final_gain)


def _rope_tables(positions_flat, rot, period):
    half = rot // 2
    inv_freq = jnp.power(ROPE_THETA, -jnp.arange(0, rot, 2, dtype=F32) / rot)
    ang = positions_flat.astype(F32)[:, None] * inv_freq[None, :]
    lane_freq = (jnp.arange(LANES) % period) % half
    return jnp.cos(ang)[:, lane_freq], jnp.sin(ang)[:, lane_freq]


def _layer(x2d, mod, positions_flat, batch, seq, norm1_gain, norm2_gain, w_in, gla_gate_up,
           gla_gate_bias, gla_norm_gain, w_branch_gla, w_branch_dsa, w_merge_out, w_ffn_gate_up,
           w_ffn_down, final_gain):
    d = x2d.shape[1]
    gla_qk = d // 2
    gla_v = d
    dsa_w = DSA_HEADS * DSA_HEAD_DIM
    idx_w = IDX_HEADS * IDX_DIM
    widths = (gla_qk, gla_qk, gla_v, gla_v, GLA_GATE_RANK, dsa_w, dsa_w, dsa_w,
              idx_w, IDX_DIM, IDX_HEADS, d, d)
    offs = [0]
    for wd in widths:
        offs.append(offs[-1] + wd)
    (w_gq, w_gk, w_gv, w_gr, w_glr, w_dq, w_dk, w_dv, w_iq, w_ik, w_iw, w_ga, w_gb) = [
        w_in[:, offs[i]:offs[i + 1]] for i in range(len(widths))]

    sh1, sc1, g1, sh2, sc2, g2 = [mod[:, i * d:(i + 1) * d][:, None, :] for i in range(N_MOD)]

    h = _norm_modulate(x2d, norm1_gain[None, :], sc1, sh1, seq)

    dk = gla_qk // GLA_HEADS
    qk_scale = jnp.concatenate([jnp.full((1, gla_qk), dk ** -0.5, F32), jnp.ones((1, gla_qk), F32)], axis=1)
    gla_qk_act = _project(h, jnp.concatenate([w_gq, w_gk], axis=1).astype(BF16), F32,
                          mode="scale", extra=(qk_scale,), name="proj_gla_qk")
    gla_v_act = _project(h, w_gv.astype(BF16), BF16, name="proj_gla_v")
    gla_r_act = _project(h, w_gr.astype(BF16), F32, name="proj_gla_r")
    small_w = jnp.zeros((d, LANES), F32).at[:, :GLA_GATE_RANK].set(w_glr)
    small_w = small_w.at[:, GLA_GATE_RANK:GLA_GATE_RANK + IDX_HEADS].set(w_iw)
    small = _project(h, small_w.astype(BF16), F32, name="proj_small")
    gup_pad = jnp.zeros((LANES, gla_qk), F32).at[:GLA_GATE_RANK].set(gla_gate_up)
    o_a = _gla(gla_qk_act, gla_v_act, gla_r_act, small, gup_pad, gla_gate_bias[None, :],
               gla_norm_gain[None, :], batch, seq, rows=min(256, seq))

    rot = DSA_HEAD_DIM // ROPE_FRACTION
    cos_d, sin_d = _rope_tables(positions_flat, rot, DSA_HEAD_DIM)
    dsa_qk = _project(h, jnp.concatenate([w_dq, w_dk], axis=1).astype(BF16), BF16, mode="rope",
                      extra=(cos_d, sin_d), half=rot // 2, period=DSA_HEAD_DIM, name="proj_dsa_qk")
    dsa_v = _project(h, w_dv.astype(BF16), BF16, name="proj_dsa_v")
    rot_i = IDX_DIM // ROPE_FRACTION
    cos_i, sin_i = _rope_tables(positions_flat, rot_i, IDX_DIM)
    idx_w_cat = jnp.concatenate([w_iq * (IDX_DIM ** -0.5), w_ik,
                                 jnp.zeros((d, LANES - IDX_DIM), F32)], axis=1)
    idx = _project(h, idx_w_cat.astype(BF16), F32, mode="rope", extra=(cos_i, sin_i),
                   half=rot_i // 2, period=IDX_DIM, tn=idx_w + LANES, name="proj_idx")
    iw_t = jnp.transpose(
        small[:, GLA_GATE_RANK:GLA_GATE_RANK + IDX_HEADS].reshape(batch, seq, IDX_HEADS), (0, 2, 1))
    topk = min(IDX_TOPK_MAX, seq // 4)
    mask_t = _topk_mask(idx, iw_t, batch, seq, topk)
    o_b = _attention(dsa_qk, dsa_v, mask_t, batch, seq)

    gates = _project(h, jnp.concatenate([w_ga, w_gb], axis=1).astype(BF16), F32, name="proj_gates")
    merged = _merge(o_a, o_b, w_branch_gla.astype(BF16), w_branch_dsa.astype(BF16), gates)
    x1, h2 = _mixer_out(merged, w_merge_out.astype(BF16), x2d, g1, norm2_gain[None, :], sc2, sh2, seq)
    act = _ffn_up(h2, w_ffn_gate_up.astype(BF16))
    return _ffn_down(act, w_ffn_down.astype(BF16), x1, g2, final_gain[None, :], seq)


def kernel(x, c, positions, norm1_gain, norm2_gain, w_ada, b_ada, w_in, gla_gate_up, gla_gate_bias,
           gla_norm_gain, w_branch_gla, w_branch_dsa, w_merge_out, w_ffn_gate_up, w_ffn_down,
           final_norm_gain):
    batch, seq, d = x.shape
    depth = w_in.shape[0]
    assert depth == 1, "the final RMSNorm is fused into the single layer's FFN kernel"
    x2d = x.reshape(batch * seq, d)
    c_pad = jnp.zeros((8, d), F32).at[:batch].set(c)
    mod = _modulation(c_pad, w_ada[0], b_ada[0][None, :])[:batch]
    out = _layer(x2d, mod, positions.reshape(-1), batch, seq, norm1_gain[0], norm2_gain[0], w_in[0],
                 gla_gate_up[0], gla_gate_bias[0], gla_norm_gain[0], w_branch_gla[0], w_branch_dsa[0],
                 w_merge_out[0], w_ffn_gate_up[0], w_ffn_down[0], final_norm_gain)
    return out.reshape(batch, seq, d)
```

```python
import functools

import jax
import jax.numpy as jnp
from jax import lax
from jax.experimental import pallas as pl
from jax.experimental.pallas import tpu as pltpu

F32 = jnp.float32
BF16 = jnp.bfloat16
HIGHEST = lax.Precision.HIGHEST

RMS_EPS = 1e-6
GLA_HEADS = 4
GLA_GATE_RANK = 16
GLA_TAU = 16.0
GLA_CHUNK = 64
GLA_SUB = 16
DSA_HEADS = 16
DSA_HEAD_DIM = 128
IDX_HEADS = 8
IDX_DIM = 64
IDX_TOPK_MAX = 256
ROPE_THETA = 500000.0
ROPE_FRACTION = 4
N_MOD = 6

LANES = 128
VMEM_LIMIT = 48 * 1024 * 1024


def _params(*semantics):
    return pltpu.CompilerParams(dimension_semantics=semantics, vmem_limit_bytes=VMEM_LIMIT)


def _silu(x):
    return x * jax.nn.sigmoid(x)


def _mod_kernel(c_ref, w_ref, b_ref, o_ref):
    a = _silu(c_ref[...])
    o_ref[...] = jnp.dot(a, w_ref[...], precision=HIGHEST, preferred_element_type=F32) + b_ref[...]


def _modulation(c_pad, w_ada, b_ada, tn=1024):
    rows, d = c_pad.shape
    n = w_ada.shape[1]
    return pl.pallas_call(
        _mod_kernel,
        grid=(n // tn,),
        in_specs=[pl.BlockSpec((rows, d), lambda j: (0, 0)),
                  pl.BlockSpec((d, tn), lambda j: (0, j)),
                  pl.BlockSpec((1, tn), lambda j: (0, j))],
        out_specs=pl.BlockSpec((rows, tn), lambda j: (0, j)),
        out_shape=jax.ShapeDtypeStruct((rows, n), F32),
        compiler_params=_params("parallel"),
        name="adaln_mod",
    )(c_pad, w_ada, b_ada)


def _rms_modulate(x, gain, scale, shift):
    y = x * lax.rsqrt(jnp.mean(x * x, axis=-1, keepdims=True) + RMS_EPS)
    return (y * gain) * (1.0 + scale) + shift


def _norm_mod_kernel(x_ref, gain_ref, sc_ref, sh_ref, o_ref):
    o_ref[...] = _rms_modulate(x_ref[...], gain_ref[...], sc_ref[0], sh_ref[0]).astype(o_ref.dtype)


def _norm_modulate(x2d, gain, sc, sh, seq, tm=512):
    t, d = x2d.shape
    per_batch = seq // tm
    return pl.pallas_call(
        _norm_mod_kernel,
        grid=(t // tm,),
        in_specs=[pl.BlockSpec((tm, d), lambda i: (i, 0)),
                  pl.BlockSpec((1, d), lambda i: (0, 0)),
                  pl.BlockSpec((1, 1, d), lambda i: (i // per_batch, 0, 0)),
                  pl.BlockSpec((1, 1, d), lambda i: (i // per_batch, 0, 0))],
        out_specs=pl.BlockSpec((tm, d), lambda i: (i, 0)),
        out_shape=jax.ShapeDtypeStruct((t, d), BF16),
        compiler_params=_params("parallel"),
        name="norm1_modulate",
    )(x2d, gain, sc, sh)


def _rope_lanes(x, cos, sin, half, period):
    lane = lax.broadcasted_iota(jnp.int32, x.shape, 1) % period
    upper = pltpu.roll(x, LANES - half, axis=1)
    lower = pltpu.roll(x, half, axis=1)
    first = lane < half
    second = jnp.logical_and(lane >= half, lane < 2 * half)
    return jnp.where(first, x * cos - upper * sin,
                     jnp.where(second, lower * sin + x * cos, x))


def _proj_kernel(h_ref, w_ref, *rest, mode, half, period):
    o_ref = rest[-1]
    acc = jnp.dot(h_ref[...], w_ref[...], preferred_element_type=F32)
    if mode == "scale":
        o_ref[...] = (acc * rest[0][...]).astype(o_ref.dtype)
    elif mode == "rope":
        cos = rest[0][...]
        sin = rest[1][...]
        for g in range(acc.shape[1] // LANES):
            cols = slice(g * LANES, (g + 1) * LANES)
            o_ref[:, cols] = _rope_lanes(acc[:, cols], cos, sin, half, period).astype(o_ref.dtype)
    else:
        o_ref[...] = acc.astype(o_ref.dtype)


def _project(h, w, out_dtype, *, mode="plain", extra=(), half=0, period=LANES, tm=1024, tn=512,
             name="proj"):
    t, d = h.shape
    n = w.shape[1]
    tn = min(tn, n)
    tm = min(tm, t)
    in_specs = [pl.BlockSpec((tm, d), lambda i, j: (i, 0)),
                pl.BlockSpec((d, tn), lambda i, j: (0, j))]
    if mode == "scale":
        in_specs.append(pl.BlockSpec((1, tn), lambda i, j: (0, j)))
    elif mode == "rope":
        in_specs += [pl.BlockSpec((tm, LANES), lambda i, j: (i, 0)),
                     pl.BlockSpec((tm, LANES), lambda i, j: (i, 0))]
    return pl.pallas_call(
        functools.partial(_proj_kernel, mode=mode, half=half, period=period),
        grid=(t // tm, n // tn),
        in_specs=in_specs,
        out_specs=pl.BlockSpec((tm, tn), lambda i, j: (i, j)),
        out_shape=jax.ShapeDtypeStruct((t, n), out_dtype),
        compiler_params=_params("parallel", "parallel"),
        name=name,
    )(h, w, *extra)


def _gla_kernel(q_ref, k_ref, v_ref, gr_ref, sm_ref, gup_ref, gb_ref, ng_ref, o_ref, st_ref):
    @pl.when(pl.program_id(2) == 0)
    def _():
        st_ref[...] = jnp.zeros_like(st_ref)

    rows = q_ref.shape[0]
    dk = q_ref.shape[1]
    c, sub = GLA_CHUNK, GLA_SUB
    n_sub = c // sub

    z = jnp.dot(sm_ref[...], gup_ref[...], precision=HIGHEST, preferred_element_type=F32) + gb_ref[...]
    log_g = -(jnp.maximum(-z, 0.0) + jnp.log1p(jnp.exp(-jnp.abs(z)))) * (1.0 / GLA_TAU)
    r_i = lax.broadcasted_iota(jnp.int32, (rows, rows), 0)
    c_i = lax.broadcasted_iota(jnp.int32, (rows, rows), 1)
    tri = jnp.where(jnp.logical_and(r_i // c == c_i // c, c_i <= r_i), 1.0, 0.0).astype(F32)
    b = jnp.dot(tri, log_g, precision=HIGHEST, preferred_element_type=F32)

    q = q_ref[...]
    k = k_ref[...]

    row = lax.broadcasted_iota(jnp.int32, (rows, 1), 0)
    row_in_sub = row % sub
    col = lax.broadcasted_iota(jnp.int32, (rows, c), 1)
    target = lax.broadcasted_iota(jnp.int32, (rows, c), 0) % c
    a_diag = jnp.zeros((rows, c), F32)
    for delta in range(sub):
        if delta == 0:
            k_d, b_d = k, b
        else:
            k_d = pltpu.roll(k, delta, axis=0)
            b_d = pltpu.roll(b, delta, axis=0)
        decay = jnp.exp(jnp.where(row_in_sub >= delta, b - b_d, -jnp.inf))
        a = jnp.sum(q * k_d * decay, axis=-1, keepdims=True)
        a_diag = jnp.where(col == target - delta, a, a_diag)

    for ch in range(rows // c):
        base = ch * c
        qc = q[base:base + c]
        kc = k[base:base + c]
        bc = b[base:base + c]
        vc = v_ref[base:base + c, :]
        st = st_ref[...]

        o_inter = lax.dot_general((qc * jnp.exp(bc)).astype(BF16), st.astype(BF16),
                                  (((1,), (1,)), ((), ())), preferred_element_type=F32)

        blocks = [jnp.zeros((sub, c), F32)]
        for i_sub in range(1, n_sub):
            lo = i_sub * sub
            ref = bc[lo - 1:lo]
            q_hat = qc[lo:lo + sub] * jnp.exp(bc[lo:lo + sub] - ref)
            k_hat = kc[0:lo] * jnp.exp(ref - bc[0:lo])
            k_pad = jnp.concatenate([k_hat, jnp.zeros((c - lo, dk), F32)], axis=0)
            blocks.append(lax.dot_general(q_hat.astype(BF16), k_pad.astype(BF16),
                                          (((1,), (1,)), ((), ())), preferred_element_type=F32))
        attn = jnp.concatenate(blocks, axis=0) + a_diag[base:base + c]

        o = jnp.dot(attn.astype(BF16), vc, preferred_element_type=F32) + o_inter

        b_last = bc[c - 1:c]
        k_dec = (kc * jnp.exp(b_last - bc)).astype(BF16)
        st_ref[...] = st * jnp.exp(b_last) + lax.dot_general(
            vc, k_dec, (((0,), (0,)), ((), ())), preferred_element_type=F32)

        o = o * lax.rsqrt(jnp.mean(o * o, axis=-1, keepdims=True) + RMS_EPS) * ng_ref[...]
        o_ref[base:base + c, :] = (o * _silu(gr_ref[base:base + c, :])).astype(o_ref.dtype)


def _gla(qk, v, gr, small, gup_pad, gbias, ngain, batch, seq, rows=256):
    dk = qk.shape[1] // (2 * GLA_HEADS)
    dv = v.shape[1] // GLA_HEADS
    t = qk.shape[0]
    nr = seq // rows

    def row_map(col_fn):
        return lambda b, h, r: (b * nr + r, col_fn(h))

    return pl.pallas_call(
        _gla_kernel,
        grid=(batch, GLA_HEADS, nr),
        in_specs=[pl.BlockSpec((rows, dk), row_map(lambda h: h)),
                  pl.BlockSpec((rows, dk), row_map(lambda h: GLA_HEADS + h)),
                  pl.BlockSpec((rows, dv), row_map(lambda h: h)),
                  pl.BlockSpec((rows, dv), row_map(lambda h: h)),
                  pl.BlockSpec((rows, LANES), row_map(lambda h: 0)),
                  pl.BlockSpec((LANES, dk), lambda b, h, r: (0, h)),
                  pl.BlockSpec((1, dk), lambda b, h, r: (0, h)),
                  pl.BlockSpec((1, dv), lambda b, h, r: (0, h))],
        out_specs=pl.BlockSpec((rows, dv), row_map(lambda h: h)),
        out_shape=jax.ShapeDtypeStruct((t, GLA_HEADS * dv), BF16),
        scratch_shapes=[pltpu.VMEM((dv, dk), F32)],
        compiler_params=_params("parallel", "parallel", "arbitrary"),
        name="gla",
    )(qk, qk, v, gr, small, gup_pad, gbias, ngain)


def _split_bf16(x):
    hi = x.astype(BF16).astype(F32)
    return hi, x - hi


def _sortable_bits_to_float(key):
    return lax.bitcast_convert_type(jnp.where(key < 0, key ^ jnp.int32(0x7FFFFFFF), key), F32)


SORT_KEY_NEG_INF = -2139095041


def _topk_bias_kernel(iq_ref, ik_ref, iw_ref, tri_ref, o_ref, score_ref, rhs_ref, *, topk, tq):
    qi = pl.program_id(1)
    tk = tq
    nk = ik_ref.shape[0] // tk
    n_live = qi + 1
    w = iw_ref[0] * (IDX_HEADS ** -0.5)
    pad_q = jnp.zeros((tq, IDX_DIM), F32)
    for h in range(IDX_HEADS):
        hi, lo = _split_bf16(iq_ref[:, h * IDX_DIM:(h + 1) * IDX_DIM])
        rhs_ref[h] = jnp.concatenate([hi, hi, lo, pad_q], axis=1).astype(BF16)
    t_pos = qi * tq + lax.broadcasted_iota(jnp.int32, (tk, tq), 1)
    s_off = lax.broadcasted_iota(jnp.int32, (tk, tq), 0)
    pad_k = jnp.zeros((tk, IDX_DIM), F32)

    def block_rows(kb):
        return pl.ds(pl.multiple_of(kb * tk, tk), tk)

    def score_block(kb, carry):
        hi, lo = _split_bf16(ik_ref[block_rows(kb), 0:IDX_DIM])
        lhs = jnp.concatenate([hi, lo, hi, pad_k], axis=1).astype(BF16)
        score = jnp.zeros((tk, tq), F32)
        for h in range(IDX_HEADS):
            dots = lax.dot_general(lhs, rhs_ref[h], (((1,), (1,)), ((), ())), preferred_element_type=F32)
            score = score + w[h:h + 1, :] * jnp.maximum(dots, 0.0)
        score_ref[block_rows(kb), :] = jnp.where(kb * tk + s_off <= t_pos, score + 0.0, -jnp.inf)
        return carry

    lax.fori_loop(0, n_live, score_block, 0)

    def count(pred):
        def blk(kb, part):
            hit = jnp.where(pred(score_ref[block_rows(kb), :]), 1.0, 0.0)
            return part + jnp.sum(hit.reshape(tk // 8, 8, tq), axis=0)
        part = lax.fori_loop(0, n_live, blk, jnp.zeros((8, tq), F32))
        return jnp.sum(part, axis=0, keepdims=True)

    def search(i, ans):
        cand = ans ^ jnp.left_shift(jnp.int32(1), 31 - i)
        cand_f = _sortable_bits_to_float(cand)
        cnt = count(lambda s: s >= cand_f)
        accept = jnp.logical_or(cnt >= float(topk), cand < SORT_KEY_NEG_INF)
        return jnp.where(accept, cand, ans)

    thr = _sortable_bits_to_float(
        lax.fori_loop(0, 32, search, jnp.full((1, tq), jnp.iinfo(jnp.int32).min, jnp.int32)))
    need = float(topk) - count(lambda s: s > thr)

    def emit(kb, seen):
        blk = score_ref[block_rows(kb), :]
        eq = jnp.where(blk == thr, 1.0, 0.0)
        rank = jnp.dot(tri_ref[...], eq.astype(BF16), preferred_element_type=F32) + seen
        chosen = jnp.logical_or(blk > thr, jnp.logical_and(blk == thr, rank <= need))
        o_ref[0, block_rows(kb), :] = jnp.where(
            jnp.logical_and(chosen, kb * tk + s_off <= t_pos), 0.0, -jnp.inf)
        return seen + jnp.sum(eq, axis=0, keepdims=True)

    lax.fori_loop(0, n_live, emit, jnp.zeros((1, tq), F32))

    def fill(kb, carry):
        o_ref[0, block_rows(kb), :] = jnp.full((tk, tq), -jnp.inf, F32)
        return carry

    lax.fori_loop(n_live, nk, fill, 0)


def _topk_bias(idx, iw_t, batch, seq, topk, tq=256):
    tq = min(tq, seq)
    nq = seq // tq
    width = IDX_HEADS * IDX_DIM
    tri = jnp.tril(jnp.ones((tq, tq), BF16))
    return pl.pallas_call(
        functools.partial(_topk_bias_kernel, topk=topk, tq=tq),
        grid=(batch, nq),
        in_specs=[pl.BlockSpec((tq, width), lambda b, i: (b * nq + i, 0)),
                  pl.BlockSpec((seq, LANES), lambda b, i: (b, width // LANES)),
                  pl.BlockSpec((1, IDX_HEADS, tq), lambda b, i: (b, 0, i)),
                  pl.BlockSpec((tq, tq), lambda b, i: (0, 0))],
        out_specs=pl.BlockSpec((1, seq, tq), lambda b, i: (b, 0, i)),
        out_shape=jax.ShapeDtypeStruct((batch, seq, seq), F32),
        scratch_shapes=[pltpu.VMEM((seq, tq), F32),
                        pltpu.VMEM((IDX_HEADS, tq, 4 * IDX_DIM), BF16)],
        compiler_params=_params("parallel", "parallel"),
        name="indexer_topk_bias",
    )(idx, idx, iw_t, tri)


ATTN_HEADS_PER_STEP = 4


def _attn_kernel(q_ref, k_ref, v_ref, bias_ref, o_ref, acc_ref):
    qi = pl.program_id(1)
    tq = q_ref.shape[0]
    tk = tq
    dh = DSA_HEAD_DIM
    group = q_ref.shape[1] // dh
    scale = dh ** -0.5
    acc_ref[...] = jnp.zeros_like(acc_ref)

    def body(kj, carry):
        rows = pl.ds(pl.multiple_of(kj * tk, tk), tk)
        bias = bias_ref[0, rows, :]
        heads = [slice(g * dh, (g + 1) * dh) for g in range(group)]
        logits = [lax.dot_general(k_ref[rows, cols], q_ref[:, cols], (((1,), (1,)), ((), ())),
                                  preferred_element_type=F32) for cols in heads]
        new, probs, alphas = [], [], []
        for g in range(group):
            m, l = carry[g]
            s = logits[g] * scale + bias
            m_new = jnp.maximum(m, jnp.max(s, axis=0, keepdims=True))
            m_safe = jnp.where(m_new == -jnp.inf, 0.0, m_new)
            alpha = jnp.exp(m - m_safe)
            p = jnp.exp(s - m_safe)
            new.append((m_new, alpha * l + jnp.sum(p, axis=0, keepdims=True)))
            probs.append(p.astype(BF16))
            alphas.append(alpha)
        updates = [lax.dot_general(v_ref[rows, heads[g]], probs[g], (((0,), (0,)), ((), ())),
                                   preferred_element_type=F32) for g in range(group)]
        for g in range(group):
            acc_ref[g] = alphas[g] * acc_ref[g] + updates[g]
        return tuple(new)

    init = tuple((jnp.full((1, tq), -jnp.inf, F32), jnp.zeros((1, tq), F32)) for _ in range(group))
    final = lax.fori_loop(0, qi + 1, body, init)
    for g in range(group):
        o_ref[:, g * dh:(g + 1) * dh] = (acc_ref[g] / final[g][1]).T.astype(o_ref.dtype)


def _attention(qk, v, bias_t, batch, seq, tq=256):
    tq = min(tq, seq)
    nq = seq // tq
    width = ATTN_HEADS_PER_STEP * DSA_HEAD_DIM
    n_groups = DSA_HEADS // ATTN_HEADS_PER_STEP
    t = qk.shape[0]
    return pl.pallas_call(
        _attn_kernel,
        grid=(batch, nq, n_groups),
        in_specs=[pl.BlockSpec((tq, width), lambda b, i, h: (b * nq + i, h)),
                  pl.BlockSpec((seq, width), lambda b, i, h: (b, n_groups + h)),
                  pl.BlockSpec((seq, width), lambda b, i, h: (b, h)),
                  pl.BlockSpec((1, seq, tq), lambda b, i, h: (b, 0, i))],
        out_specs=pl.BlockSpec((tq, width), lambda b, i, h: (b * nq + i, h)),
        out_shape=jax.ShapeDtypeStruct((t, DSA_HEADS * DSA_HEAD_DIM), BF16),
        scratch_shapes=[pltpu.VMEM((ATTN_HEADS_PER_STEP, DSA_HEAD_DIM, tq), F32)],
        compiler_params=_params("parallel", "parallel", "arbitrary"),
        name="dsa_attention",
    )(qk, qk, v, bias_t)


def _merge_kernel(oa_ref, ob_ref, wa_ref, wb_ref, ga_ref, gb_ref, o_ref):
    ya = jnp.dot(oa_ref[...], wa_ref[...], preferred_element_type=F32)
    yb = jnp.dot(ob_ref[...], wb_ref[...], preferred_element_type=F32)
    o_ref[...] = (jax.nn.sigmoid(ga_ref[...]) * ya + jax.nn.sigmoid(gb_ref[...]) * yb).astype(o_ref.dtype)


def _merge(o_a, o_b, w_a, w_b, gates, tm=512, tn=512):
    t, d = o_a.shape
    n = w_a.shape[1]
    tm = min(tm, t)
    nj = n // tn
    return pl.pallas_call(
        _merge_kernel,
        grid=(t // tm, nj),
        in_specs=[pl.BlockSpec((tm, d), lambda i, j: (i, 0)),
                  pl.BlockSpec((tm, d), lambda i, j: (i, 0)),
                  pl.BlockSpec((d, tn), lambda i, j: (0, j)),
                  pl.BlockSpec((d, tn), lambda i, j: (0, j)),
                  pl.BlockSpec((tm, tn), lambda i, j: (i, j)),
                  pl.BlockSpec((tm, tn), lambda i, j: (i, nj + j))],
        out_specs=pl.BlockSpec((tm, tn), lambda i, j: (i, j)),
        out_shape=jax.ShapeDtypeStruct((t, n), BF16),
        compiler_params=_params("parallel", "parallel"),
        name="branch_merge",
    )(o_a, o_b, w_a, w_b, gates, gates)


def _mixer_out_kernel(m_ref, w_ref, x_ref, g_ref, gain_ref, sc_ref, sh_ref, x1_ref, h2_ref):
    y = jnp.dot(m_ref[...], w_ref[...], preferred_element_type=F32)
    x1 = x_ref[...] + g_ref[0] * y
    x1_ref[...] = x1
    h2_ref[...] = _rms_modulate(x1, gain_ref[...], sc_ref[0], sh_ref[0]).astype(h2_ref.dtype)


def _mixer_out(merged, w, x2d, g1, gain2, sc2, sh2, seq, tm=256):
    t, d = x2d.shape
    tm = min(tm, seq)
    per_batch = seq // tm
    vec = pl.BlockSpec((1, 1, d), lambda i: (i // per_batch, 0, 0))
    row = pl.BlockSpec((tm, d), lambda i: (i, 0))
    return pl.pallas_call(
        _mixer_out_kernel,
        grid=(t // tm,),
        in_specs=[row, pl.BlockSpec((d, d), lambda i: (0, 0)), row, vec,
                  pl.BlockSpec((1, d), lambda i: (0, 0)), vec, vec],
        out_specs=[row, row],
        out_shape=[jax.ShapeDtypeStruct((t, d), F32), jax.ShapeDtypeStruct((t, d), BF16)],
        compiler_params=_params("parallel"),
        name="mixer_out_norm2",
    )(merged, w, x2d, g1, gain2, sc2, sh2)


def _ffn_up_kernel(h_ref, wg_ref, wu_ref, o_ref):
    g = jnp.dot(h_ref[...], wg_ref[...], preferred_element_type=F32)
    u = jnp.dot(h_ref[...], wu_ref[...], preferred_element_type=F32)
    o_ref[...] = (_silu(g) * u).astype(o_ref.dtype)


def _ffn_up(h2, w_gate_up, tm=1024, tn=512):
    t, d = h2.shape
    d_ff = w_gate_up.shape[1] // 2
    tm = min(tm, t)
    nj = d_ff // tn
    return pl.pallas_call(
        _ffn_up_kernel,
        grid=(t // tm, nj),
        in_specs=[pl.BlockSpec((tm, d), lambda i, j: (i, 0)),
                  pl.BlockSpec((d, tn), lambda i, j: (0, j)),
                  pl.BlockSpec((d, tn), lambda i, j: (0, nj + j))],
        out_specs=pl.BlockSpec((tm, tn), lambda i, j: (i, j)),
        out_shape=jax.ShapeDtypeStruct((t, d_ff), BF16),
        compiler_params=_params("parallel", "parallel"),
        name="ffn_up",
    )(h2, w_gate_up, w_gate_up)


def _ffn_down_kernel(a_ref, w_ref, x_ref, g_ref, gain_ref, o_ref, acc_ref):
    kk = pl.program_id(1)

    @pl.when(kk == 0)
    def _():
        acc_ref[...] = jnp.zeros_like(acc_ref)

    acc_ref[...] += jnp.dot(a_ref[...], w_ref[...], preferred_element_type=F32)

    @pl.when(kk == pl.num_programs(1) - 1)
    def _():
        x2 = x_ref[...] + g_ref[0] * acc_ref[...]
        y = x2 * lax.rsqrt(jnp.mean(x2 * x2, axis=-1, keepdims=True) + RMS_EPS)
        o_ref[...] = y * gain_ref[...]


def _ffn_down(act, w_down, x1, g2, final_gain, seq, tm=512, tk=512):
    t, d_ff = act.shape
    d = w_down.shape[1]
    tm = min(tm, seq)
    per_batch = seq // tm
    return pl.pallas_call(
        _ffn_down_kernel,
        grid=(t // tm, d_ff // tk),
        in_specs=[pl.BlockSpec((tm, tk), lambda i, k: (i, k)),
                  pl.BlockSpec((tk, d), lambda i, k: (k, 0)),
                  pl.BlockSpec((tm, d), lambda i, k: (i, 0)),
                  pl.BlockSpec((1, 1, d), lambda i, k: (i // per_batch, 0, 0)),
                  pl.BlockSpec((1, d), lambda i, k: (0, 0))],
        out_specs=pl.BlockSpec((tm, d), lambda i, k: (i, 0)),
        out_shape=jax.ShapeDtypeStruct((t, d), F32),
        scratch_shapes=[pltpu.VMEM((tm, d), F32)],
        compiler_params=_params("parallel", "arbitrary"),
        name="ffn_down_final_norm",
    )(act, w_down, x1, g2, final_gain)


def _rope_tables(positions_flat, rot, period):
    half = rot // 2
    inv_freq = jnp.power(ROPE_THETA, -jnp.arange(0, rot, 2, dtype=F32) / rot)
    ang = positions_flat.astype(F32)[:, None] * inv_freq[None, :]
    lane_freq = (jnp.arange(LANES) % period) % half
    return jnp.cos(ang)[:, lane_freq], jnp.sin(ang)[:, lane_freq]


def _layer(x2d, mod, positions_flat, batch, seq, norm1_gain, norm2_gain, w_in, gla_gate_up,
           gla_gate_bias, gla_norm_gain, w_branch_gla, w_branch_dsa, w_merge_out, w_ffn_gate_up,
           w_ffn_down, final_gain):
    d = x2d.shape[1]
    gla_qk = d // 2
    gla_v = d
    dsa_w = DSA_HEADS * DSA_HEAD_DIM
    idx_w = IDX_HEADS * IDX_DIM
    widths = (gla_qk, gla_qk, gla_v, gla_v, GLA_GATE_RANK, dsa_w, dsa_w, dsa_w,
              idx_w, IDX_DIM, IDX_HEADS, d, d)
    offs = [0]
    for wd in widths:
        offs.append(offs[-1] + wd)
    (w_gq, w_gk, w_gv, w_gr, w_glr, w_dq, w_dk, w_dv, w_iq, w_ik, w_iw, w_ga, w_gb) = [
        w_in[:, offs[i]:offs[i + 1]] for i in range(len(widths))]

    sh1, sc1, g1, sh2, sc2, g2 = [mod[:, i * d:(i + 1) * d][:, None, :] for i in range(N_MOD)]

    h = _norm_modulate(x2d, norm1_gain[None, :], sc1, sh1, seq)

    dk = gla_qk // GLA_HEADS
    qk_scale = jnp.concatenate([jnp.full((1, gla_qk), dk ** -0.5, F32), jnp.ones((1, gla_qk), F32)], axis=1)
    gla_qk_act = _project(h, jnp.concatenate([w_gq, w_gk], axis=1).astype(BF16), F32,
                          mode="scale", extra=(qk_scale,), name="proj_gla_qk")
    gla_v_act = _project(h, w_gv.astype(BF16), BF16, name="proj_gla_v")
    gla_r_act = _project(h, w_gr.astype(BF16), F32, name="proj_gla_r")
    small_w = jnp.zeros((d, LANES), F32).at[:, :GLA_GATE_RANK].set(w_glr)
    small_w = small_w.at[:, GLA_GATE_RANK:GLA_GATE_RANK + IDX_HEADS].set(w_iw)
    small = _project(h, small_w.astype(BF16), F32, name="proj_small")
    gup_pad = jnp.zeros((LANES, gla_qk), F32).at[:GLA_GATE_RANK].set(gla_gate_up)
    o_a = _gla(gla_qk_act, gla_v_act, gla_r_act, small, gup_pad, gla_gate_bias[None, :],
               gla_norm_gain[None, :], batch, seq, rows=min(256, seq))

    rot = DSA_HEAD_DIM // ROPE_FRACTION
    cos_d, sin_d = _rope_tables(positions_flat, rot, DSA_HEAD_DIM)
    dsa_qk = _project(h, jnp.concatenate([w_dq, w_dk], axis=1).astype(BF16), BF16, mode="rope",
                      extra=(cos_d, sin_d), half=rot // 2, period=DSA_HEAD_DIM, name="proj_dsa_qk")
    dsa_v = _project(h, w_dv.astype(BF16), BF16, name="proj_dsa_v")
    rot_i = IDX_DIM // ROPE_FRACTION
    cos_i, sin_i = _rope_tables(positions_flat, rot_i, IDX_DIM)
    idx_w_cat = jnp.concatenate([w_iq * (IDX_DIM ** -0.5), w_ik,
                                 jnp.zeros((d, LANES - IDX_DIM), F32)], axis=1)
    idx = _project(h, idx_w_cat.astype(BF16), F32, mode="rope", extra=(cos_i, sin_i),
                   half=rot_i // 2, period=IDX_DIM, tn=idx_w + LANES, name="proj_idx")
    iw_t = jnp.transpose(
        small[:, GLA_GATE_RANK:GLA_GATE_RANK + IDX_HEADS].reshape(batch, seq, IDX_HEADS), (0, 2, 1))
    topk = min(IDX_TOPK_MAX, seq // 4)
    bias_t = _topk_bias(idx, iw_t, batch, seq, topk)
    o_b = _attention(dsa_qk, dsa_v, bias_t, batch, seq)

    gates = _project(h, jnp.concatenate([w_ga, w_gb], axis=1).astype(BF16), F32, name="proj_gates")
    merged = _merge(o_a, o_b, w_branch_gla.astype(BF16), w_branch_dsa.astype(BF16), gates)
    x1, h2 = _mixer_out(merged, w_merge_out.astype(BF16), x2d, g1, norm2_gain[None, :], sc2, sh2, seq)
    act = _ffn_up(h2, w_ffn_gate_up.astype(BF16))
    return _ffn_down(act, w_ffn_down.astype(BF16), x1, g2, final_gain[None, :], seq)


def kernel(x, c, positions, norm1_gain, norm2_gain, w_ada, b_ada, w_in, gla_gate_up, gla_gate_bias,
           gla_norm_gain, w_branch_gla, w_branch_dsa, w_merge_out, w_ffn_gate_up, w_ffn_down,
           final_norm_gain):
    batch, seq, d = x.shape
    depth = w_in.shape[0]
    assert depth == 1, "the final RMSNorm is fused into the single layer's FFN kernel"
    x2d = x.reshape(batch * seq, d)
    c_pad = jnp.zeros((8, d), F32).at[:batch].set(c)
    mod = _modulation(c_pad, w_ada[0], b_ada[0][None, :])[:batch]
    out = _layer(x2d, mod, positions.reshape(-1), batch, seq, norm1_gain[0], norm2_gain[0], w_in[0],
                 gla_gate_up[0], gla_gate_bias[0], gla_norm_gain[0], w_branch_gla[0], w_branch_dsa[0],
                 w_merge_out[0], w_ffn_gate_up[0], w_ffn_down[0], final_norm_gain)
    return out.reshape(batch, seq, d)
```

```python
import functools

import jax
import jax.numpy as jnp
from jax import lax
from jax.experimental import pallas as pl
from jax.experimental.pallas import tpu as pltpu

F32 = jnp.float32
BF16 = jnp.bfloat16
HIGHEST = lax.Precision.HIGHEST

RMS_EPS = 1e-6
GLA_HEADS = 4
GLA_GATE_RANK = 16
GLA_TAU = 16.0
GLA_CHUNK = 64
GLA_SUB = 16
DSA_HEADS = 16
DSA_HEAD_DIM = 128
IDX_HEADS = 8
IDX_DIM = 64
IDX_TOPK_MAX = 256
ROPE_THETA = 500000.0
ROPE_FRACTION = 4
N_MOD = 6

LANES = 128
SUBLANES = 8
VMEM_LIMIT = 48 * 1024 * 1024


def _params(*semantics):
    return pltpu.CompilerParams(dimension_semantics=semantics, vmem_limit_bytes=VMEM_LIMIT)


def _silu(x):
    return x * jax.nn.sigmoid(x)


def _mod_kernel(c_ref, w_ref, b_ref, o_ref):
    a = _silu(c_ref[...])
    o_ref[...] = jnp.dot(a, w_ref[...], precision=HIGHEST, preferred_element_type=F32) + b_ref[...]


def _modulation(c_pad, w_ada, b_ada, tn=1024):
    rows, d = c_pad.shape
    n = w_ada.shape[1]
    return pl.pallas_call(
        _mod_kernel,
        grid=(n // tn,),
        in_specs=[pl.BlockSpec((rows, d), lambda j: (0, 0)),
                  pl.BlockSpec((d, tn), lambda j: (0, j)),
                  pl.BlockSpec((1, tn), lambda j: (0, j))],
        out_specs=pl.BlockSpec((rows, tn), lambda j: (0, j)),
        out_shape=jax.ShapeDtypeStruct((rows, n), F32),
        compiler_params=_params("parallel"),
        name="adaln_mod",
    )(c_pad, w_ada, b_ada)


def _rms_modulate(x, gain, scale, shift):
    y = x * lax.rsqrt(jnp.mean(x * x, axis=-1, keepdims=True) + RMS_EPS)
    return (y * gain) * (1.0 + scale) + shift


def _norm_mod_kernel(x_ref, gain_ref, sc_ref, sh_ref, o_ref):
    o_ref[...] = _rms_modulate(x_ref[...], gain_ref[...], sc_ref[0], sh_ref[0]).astype(o_ref.dtype)


def _norm_modulate(x2d, gain, sc, sh, seq, tm=512):
    t, d = x2d.shape
    per_batch = seq // tm
    return pl.pallas_call(
        _norm_mod_kernel,
        grid=(t // tm,),
        in_specs=[pl.BlockSpec((tm, d), lambda i: (i, 0)),
                  pl.BlockSpec((1, d), lambda i: (0, 0)),
                  pl.BlockSpec((1, 1, d), lambda i: (i // per_batch, 0, 0)),
                  pl.BlockSpec((1, 1, d), lambda i: (i // per_batch, 0, 0))],
        out_specs=pl.BlockSpec((tm, d), lambda i: (i, 0)),
        out_shape=jax.ShapeDtypeStruct((t, d), BF16),
        compiler_params=_params("parallel"),
        name="norm1_modulate",
    )(x2d, gain, sc, sh)


def _rope_lanes(x, cos, sin, half, period, limit=LANES):
    lane = lax.broadcasted_iota(jnp.int32, x.shape, 1)
    in_head = lane % period
    upper = pltpu.roll(x, LANES - half, axis=1)
    lower = pltpu.roll(x, half, axis=1)
    live = lane < limit
    first = jnp.logical_and(live, in_head < half)
    second = jnp.logical_and(live, jnp.logical_and(in_head >= half, in_head < 2 * half))
    return jnp.where(first, x * cos - upper * sin,
                     jnp.where(second, lower * sin + x * cos, x))


PLAIN32, GLA_Q32, IDX_Q32, IDX_K32, PLAIN16, DSA_ROPE16 = range(6)
IN_PROJ_TN = 512


def _in_proj_kernel(off_ref, kind_ref, h_ref, wt_ref, cos_d_ref, sin_d_ref, cos_i_ref, sin_i_ref,
                    o32_ref, o16_ref):
    kind = kind_ref[pl.program_id(1)]
    acc = lax.dot_general(h_ref[...], wt_ref[...].astype(BF16), (((1,), (1,)), ((), ())),
                          preferred_element_type=F32)
    n_groups = acc.shape[1] // LANES
    rot_d = DSA_HEAD_DIM // ROPE_FRACTION
    rot_i = IDX_DIM // ROPE_FRACTION

    @pl.when(kind == PLAIN32)
    def _():
        o32_ref[...] = acc

    @pl.when(kind == GLA_Q32)
    def _():
        o32_ref[...] = acc * ((wt_ref.shape[1] // 2 // GLA_HEADS) ** -0.5)

    @pl.when(kind == IDX_Q32)
    def _():
        for g in range(n_groups):
            cols = slice(g * LANES, (g + 1) * LANES)
            o32_ref[:, cols] = _rope_lanes(acc[:, cols] * (IDX_DIM ** -0.5), cos_i_ref[...], sin_i_ref[...],
                                           rot_i // 2, IDX_DIM)

    @pl.when(kind == IDX_K32)
    def _():
        o32_ref[...] = acc
        o32_ref[:, 0:LANES] = _rope_lanes(acc[:, 0:LANES], cos_i_ref[...], sin_i_ref[...],
                                          rot_i // 2, IDX_DIM, limit=IDX_DIM)

    @pl.when(kind == PLAIN16)
    def _():
        o16_ref[...] = acc.astype(BF16)

    @pl.when(kind == DSA_ROPE16)
    def _():
        for g in range(n_groups):
            cols = slice(g * LANES, (g + 1) * LANES)
            o16_ref[:, cols] = _rope_lanes(acc[:, cols], cos_d_ref[...], sin_d_ref[...],
                                           rot_d // 2, DSA_HEAD_DIM).astype(BF16)


def _in_proj_plan(d):
    tn = IN_PROJ_TN
    gla_qk, dsa_w, idx_w = d // 2, DSA_HEADS * DSA_HEAD_DIM, IDX_HEADS * IDX_DIM
    names = ("g_q", "g_k", "g_v", "g_r", "g_lr", "d_q", "d_k", "d_v", "i_q", "i_k", "i_w", "gate_a", "gate_b")
    widths = (gla_qk, gla_qk, d, d, GLA_GATE_RANK, dsa_w, dsa_w, dsa_w, idx_w, IDX_DIM, IDX_HEADS, d, d)
    start, pos = {}, 0
    for nm, wd in zip(names, widths):
        start[nm] = pos
        pos += wd
    width = dict(zip(names, widths))
    assert start["i_w"] == start["i_k"] + IDX_DIM and IDX_DIM + IDX_HEADS <= LANES
    f32_groups = (("g_q", GLA_Q32), ("g_k", PLAIN32), ("g_r", PLAIN32), ("g_lr", PLAIN32), ("i_q", IDX_Q32),
                  ("i_k", IDX_K32), ("gate_a", PLAIN32), ("gate_b", PLAIN32))
    bf16_groups = (("g_v", PLAIN16), ("d_q", DSA_ROPE16), ("d_k", DSA_ROPE16), ("d_v", PLAIN16))
    offsets, kinds, tile_of = [], [], {}
    for groups in (f32_groups, bf16_groups):
        base = len(offsets)
        for nm, kind in groups:
            tile_of[nm] = len(offsets) - base
            for t in range(-(-width[nm] // tn)):
                offsets.append(start[nm] + t * tn)
                kinds.append(kind)
        if groups is f32_groups:
            n32 = len(offsets)
    assert all(o % SUBLANES == 0 and o + tn <= pos for o in offsets)
    return offsets, kinds, n32, tile_of


def _in_project(h, w_in_t, tables, plan, tm=1024):
    t, d = h.shape
    tn = IN_PROJ_TN
    offsets, kinds, n32, _ = plan
    n16 = len(offsets) - n32
    tm = min(tm, t)
    table_spec = pl.BlockSpec((tm, LANES), lambda i, j, off, kind: (i, 0))
    grid_spec = pltpu.PrefetchScalarGridSpec(
        num_scalar_prefetch=2,
        grid=(t // tm, len(offsets)),
        in_specs=[pl.BlockSpec((tm, d), lambda i, j, off, kind: (i, 0)),
                  pl.BlockSpec((pl.Element(tn), pl.Element(d)),
                               lambda i, j, off, kind: (pl.multiple_of(off[j], SUBLANES), 0)),
                  table_spec, table_spec, table_spec, table_spec],
        out_specs=[pl.BlockSpec((tm, tn), lambda i, j, off, kind: (i, jnp.minimum(j, n32 - 1))),
                   pl.BlockSpec((tm, tn), lambda i, j, off, kind: (i, jnp.maximum(j - n32, 0)))],
    )
    return pl.pallas_call(
        _in_proj_kernel,
        grid_spec=grid_spec,
        out_shape=[jax.ShapeDtypeStruct((t, n32 * tn), F32), jax.ShapeDtypeStruct((t, n16 * tn), BF16)],
        compiler_params=_params("parallel", "arbitrary"),
        name="in_proj",
    )(jnp.asarray(offsets, jnp.int32), jnp.asarray(kinds, jnp.int32), h, w_in_t, *tables)


def _gla_kernel(q_ref, k_ref, v_ref, gr_ref, sm_ref, gup_ref, gb_ref, ng_ref, o_ref, st_ref):
    @pl.when(pl.program_id(2) == 0)
    def _():
        st_ref[...] = jnp.zeros_like(st_ref)

    rows = q_ref.shape[0]
    dk = q_ref.shape[1]
    c, sub = GLA_CHUNK, GLA_SUB
    n_sub = c // sub

    z = jnp.dot(sm_ref[...], gup_ref[...], precision=HIGHEST, preferred_element_type=F32) + gb_ref[...]
    log_g = -(jnp.maximum(-z, 0.0) + jnp.log1p(jnp.exp(-jnp.abs(z)))) * (1.0 / GLA_TAU)
    r_i = lax.broadcasted_iota(jnp.int32, (rows, rows), 0)
    c_i = lax.broadcasted_iota(jnp.int32, (rows, rows), 1)
    tri = jnp.where(jnp.logical_and(r_i // c == c_i // c, c_i <= r_i), 1.0, 0.0).astype(F32)
    b = jnp.dot(tri, log_g, precision=HIGHEST, preferred_element_type=F32)

    q = q_ref[...]
    k = k_ref[...]

    row = lax.broadcasted_iota(jnp.int32, (rows, 1), 0)
    row_in_sub = row % sub
    col = lax.broadcasted_iota(jnp.int32, (rows, c), 1)
    target = lax.broadcasted_iota(jnp.int32, (rows, c), 0) % c
    a_diag = jnp.zeros((rows, c), F32)
    for delta in range(sub):
        if delta == 0:
            k_d, b_d = k, b
        else:
            k_d = pltpu.roll(k, delta, axis=0)
            b_d = pltpu.roll(b, delta, axis=0)
        decay = jnp.exp(jnp.where(row_in_sub >= delta, b - b_d, -jnp.inf))
        a = jnp.sum(q * k_d * decay, axis=-1, keepdims=True)
        a_diag = jnp.where(col == target - delta, a, a_diag)

    for ch in range(rows // c):
        base = ch * c
        qc = q[base:base + c]
        kc = k[base:base + c]
        bc = b[base:base + c]
        vc = v_ref[base:base + c, :]
        st = st_ref[...]

        o_inter = lax.dot_general((qc * jnp.exp(bc)).astype(BF16), st.astype(BF16),
                                  (((1,), (1,)), ((), ())), preferred_element_type=F32)

        blocks = [jnp.zeros((sub, c), F32)]
        for i_sub in range(1, n_sub):
            lo = i_sub * sub
            ref = bc[lo - 1:lo]
            q_hat = qc[lo:lo + sub] * jnp.exp(bc[lo:lo + sub] - ref)
            k_hat = kc[0:lo] * jnp.exp(ref - bc[0:lo])
            k_pad = jnp.concatenate([k_hat, jnp.zeros((c - lo, dk), F32)], axis=0)
            blocks.append(lax.dot_general(q_hat.astype(BF16), k_pad.astype(BF16),
                                          (((1,), (1,)), ((), ())), preferred_element_type=F32))
        attn = jnp.concatenate(blocks, axis=0) + a_diag[base:base + c]

        o = jnp.dot(attn.astype(BF16), vc, preferred_element_type=F32) + o_inter

        b_last = bc[c - 1:c]
        k_dec = (kc * jnp.exp(b_last - bc)).astype(BF16)
        st_ref[...] = st * jnp.exp(b_last) + lax.dot_general(
            vc, k_dec, (((0,), (0,)), ((), ())), preferred_element_type=F32)

        o = o * lax.rsqrt(jnp.mean(o * o, axis=-1, keepdims=True) + RMS_EPS) * ng_ref[...]
        o_ref[base:base + c, :] = (o * _silu(gr_ref[base:base + c, :])).astype(o_ref.dtype)


def _gla(act32, act16, tile_of, gup_pad, gbias, ngain, batch, seq, rows=256):
    dk = gup_pad.shape[1] // GLA_HEADS
    dv = ngain.shape[1] // GLA_HEADS
    t = act32.shape[0]
    nr = seq // rows

    def cols(group, width):
        first = tile_of[group] * IN_PROJ_TN // width
        return lambda b, h, r: (b * nr + r, first + h)

    lr_block = tile_of["g_lr"] * IN_PROJ_TN // LANES
    return pl.pallas_call(
        _gla_kernel,
        grid=(batch, GLA_HEADS, nr),
        in_specs=[pl.BlockSpec((rows, dk), cols("g_q", dk)),
                  pl.BlockSpec((rows, dk), cols("g_k", dk)),
                  pl.BlockSpec((rows, dv), cols("g_v", dv)),
                  pl.BlockSpec((rows, dv), cols("g_r", dv)),
                  pl.BlockSpec((rows, LANES), lambda b, h, r: (b * nr + r, lr_block)),
                  pl.BlockSpec((LANES, dk), lambda b, h, r: (0, h)),
                  pl.BlockSpec((1, dk), lambda b, h, r: (0, h)),
                  pl.BlockSpec((1, dv), lambda b, h, r: (0, h))],
        out_specs=pl.BlockSpec((rows, dv), lambda b, h, r: (b * nr + r, h)),
        out_shape=jax.ShapeDtypeStruct((t, GLA_HEADS * dv), BF16),
        scratch_shapes=[pltpu.VMEM((dv, dk), F32)],
        compiler_params=_params("parallel", "parallel", "arbitrary"),
        name="gla",
    )(act32, act32, act16, act32, act32, gup_pad, gbias, ngain)


def _split_bf16(x):
    hi = x.astype(BF16).astype(F32)
    return hi, x - hi


def _sortable_bits_to_float(key):
    return lax.bitcast_convert_type(jnp.where(key < 0, key ^ jnp.int32(0x7FFFFFFF), key), F32)


SORT_KEY_NEG_INF = -2139095041


def _topk_bias_kernel(iq_ref, ik_ref, iw_ref, tri_ref, o_ref, score_ref, rhs_ref, *, topk, tq):
    qi = pl.program_id(1)
    tk = tq
    nk = ik_ref.shape[0] // tk
    n_live = qi + 1
    w = iw_ref[0] * (IDX_HEADS ** -0.5)
    pad_q = jnp.zeros((tq, IDX_DIM), F32)
    for h in range(IDX_HEADS):
        hi, lo = _split_bf16(iq_ref[:, h * IDX_DIM:(h + 1) * IDX_DIM])
        rhs_ref[h] = jnp.concatenate([hi, hi, lo, pad_q], axis=1).astype(BF16)
    t_pos = qi * tq + lax.broadcasted_iota(jnp.int32, (tk, tq), 1)
    s_off = lax.broadcasted_iota(jnp.int32, (tk, tq), 0)
    pad_k = jnp.zeros((tk, IDX_DIM), F32)

    def block_rows(kb):
        return pl.ds(pl.multiple_of(kb * tk, tk), tk)

    def score_block(kb, carry):
        hi, lo = _split_bf16(ik_ref[block_rows(kb), 0:IDX_DIM])
        lhs = jnp.concatenate([hi, lo, hi, pad_k], axis=1).astype(BF16)
        score = jnp.zeros((tk, tq), F32)
        for h in range(IDX_HEADS):
            dots = lax.dot_general(lhs, rhs_ref[h], (((1,), (1,)), ((), ())), preferred_element_type=F32)
            score = score + w[h:h + 1, :] * jnp.maximum(dots, 0.0)
        score_ref[block_rows(kb), :] = jnp.where(kb * tk + s_off <= t_pos, score + 0.0, -jnp.inf)
        return carry

    lax.fori_loop(0, n_live, score_block, 0)

    def count(pred):
        def blk(kb, part):
            hit = jnp.where(pred(score_ref[block_rows(kb), :]), 1.0, 0.0)
            return part + jnp.sum(hit.reshape(tk // SUBLANES, SUBLANES, tq), axis=0)
        part = lax.fori_loop(0, n_live, blk, jnp.zeros((SUBLANES, tq), F32))
        return jnp.sum(part, axis=0, keepdims=True)

    def search(i, ans):
        cand = ans ^ jnp.left_shift(jnp.int32(1), 31 - i)
        cand_f = _sortable_bits_to_float(cand)
        cnt = count(lambda s: s >= cand_f)
        accept = jnp.logical_or(cnt >= float(topk), cand < SORT_KEY_NEG_INF)
        return jnp.where(accept, cand, ans)

    thr = _sortable_bits_to_float(
        lax.fori_loop(0, 32, search, jnp.full((1, tq), jnp.iinfo(jnp.int32).min, jnp.int32)))
    need = float(topk) - count(lambda s: s > thr)

    def emit(kb, seen):
        blk = score_ref[block_rows(kb), :]
        eq = jnp.where(blk == thr, 1.0, 0.0)
        rank = jnp.dot(tri_ref[...], eq.astype(BF16), preferred_element_type=F32) + seen
        chosen = jnp.logical_or(blk > thr, jnp.logical_and(blk == thr, rank <= need))
        o_ref[0, block_rows(kb), :] = jnp.where(
            jnp.logical_and(chosen, kb * tk + s_off <= t_pos), 0.0, -jnp.inf)
        return seen + jnp.sum(eq, axis=0, keepdims=True)

    lax.fori_loop(0, n_live, emit, jnp.zeros((1, tq), F32))

    def fill(kb, carry):
        o_ref[0, block_rows(kb), :] = jnp.full((tk, tq), -jnp.inf, F32)
        return carry

    lax.fori_loop(n_live, nk, fill, 0)


def _topk_bias(act32, tile_of, iw_t, batch, seq, topk, tq=256):
    tq = min(tq, seq)
    nq = seq // tq
    width = IDX_HEADS * IDX_DIM
    iq_block = tile_of["i_q"] * IN_PROJ_TN // width
    ik_block = tile_of["i_k"] * IN_PROJ_TN // LANES
    tri = jnp.tril(jnp.ones((tq, tq), BF16))
    return pl.pallas_call(
        functools.partial(_topk_bias_kernel, topk=topk, tq=tq),
        grid=(batch, nq),
        in_specs=[pl.BlockSpec((tq, width), lambda b, i: (b * nq + i, iq_block)),
                  pl.BlockSpec((seq, LANES), lambda b, i: (b, ik_block)),
                  pl.BlockSpec((1, IDX_HEADS, tq), lambda b, i: (b, 0, i)),
                  pl.BlockSpec((tq, tq), lambda b, i: (0, 0))],
        out_specs=pl.BlockSpec((1, seq, tq), lambda b, i: (b, 0, i)),
        out_shape=jax.ShapeDtypeStruct((batch, seq, seq), F32),
        scratch_shapes=[pltpu.VMEM((seq, tq), F32),
                        pltpu.VMEM((IDX_HEADS, tq, 4 * IDX_DIM), BF16)],
        compiler_params=_params("parallel", "parallel"),
        name="indexer_topk_bias",
    )(act32, act32, iw_t, tri)


ATTN_HEADS_PER_STEP = 4


def _attn_kernel(q_ref, k_ref, v_ref, bias_ref, o_ref, acc_ref):
    qi = pl.program_id(1)
    tq = q_ref.shape[0]
    tk = tq
    dh = DSA_HEAD_DIM
    group = q_ref.shape[1] // dh
    scale = dh ** -0.5
    acc_ref[...] = jnp.zeros_like(acc_ref)

    def body(kj, carry):
        rows = pl.ds(pl.multiple_of(kj * tk, tk), tk)
        bias = bias_ref[0, rows, :]
        heads = [slice(g * dh, (g + 1) * dh) for g in range(group)]
        logits = [lax.dot_general(k_ref[rows, cols], q_ref[:, cols], (((1,), (1,)), ((), ())),
                                  preferred_element_type=F32) for cols in heads]
        new, probs, alphas = [], [], []
        for g in range(group):
            m, l = carry[g]
            s = logits[g] * scale + bias
            m_new = jnp.maximum(m, jnp.max(s, axis=0, keepdims=True))
            m_safe = jnp.where(m_new == -jnp.inf, 0.0, m_new)
            alpha = jnp.exp(m - m_safe)
            p = jnp.exp(s - m_safe)
            new.append((m_new, alpha * l + jnp.sum(p, axis=0, keepdims=True)))
            probs.append(p.astype(BF16))
            alphas.append(alpha)
        updates = [lax.dot_general(v_ref[rows, heads[g]], probs[g], (((0,), (0,)), ((), ())),
                                   preferred_element_type=F32) for g in range(group)]
        for g in range(group):
            acc_ref[g] = alphas[g] * acc_ref[g] + updates[g]
        return tuple(new)

    init = tuple((jnp.full((1, tq), -jnp.inf, F32), jnp.zeros((1, tq), F32)) for _ in range(group))
    final = lax.fori_loop(0, qi + 1, body, init)
    for g in range(group):
        o_ref[:, g * dh:(g + 1) * dh] = (acc_ref[g] / final[g][1]).T.astype(o_ref.dtype)


def _attention(act16, tile_of, bias_t, batch, seq, tq=256):
    tq = min(tq, seq)
    nq = seq // tq
    width = ATTN_HEADS_PER_STEP * DSA_HEAD_DIM
    n_groups = DSA_HEADS // ATTN_HEADS_PER_STEP
    t = act16.shape[0]
    q_block, k_block, v_block = (tile_of[nm] * IN_PROJ_TN // width for nm in ("d_q", "d_k", "d_v"))
    return pl.pallas_call(
        _attn_kernel,
        grid=(batch, nq, n_groups),
        in_specs=[pl.BlockSpec((tq, width), lambda b, i, h: (b * nq + i, q_block + h)),
                  pl.BlockSpec((seq, width), lambda b, i, h: (b, k_block + h)),
                  pl.BlockSpec((seq, width), lambda b, i, h: (b, v_block + h)),
                  pl.BlockSpec((1, seq, tq), lambda b, i, h: (b, 0, i))],
        out_specs=pl.BlockSpec((tq, width), lambda b, i, h: (b * nq + i, h)),
        out_shape=jax.ShapeDtypeStruct((t, DSA_HEADS * DSA_HEAD_DIM), BF16),
        scratch_shapes=[pltpu.VMEM((ATTN_HEADS_PER_STEP, DSA_HEAD_DIM, tq), F32)],
        compiler_params=_params("parallel", "parallel", "arbitrary"),
        name="dsa_attention",
    )(act16, act16, act16, bias_t)


def _merge_kernel(oa_ref, ob_ref, wa_ref, wb_ref, ga_ref, gb_ref, o_ref):
    ya = jnp.dot(oa_ref[...], wa_ref[...], preferred_element_type=F32)
    yb = jnp.dot(ob_ref[...], wb_ref[...], preferred_element_type=F32)
    o_ref[...] = (jax.nn.sigmoid(ga_ref[...]) * ya + jax.nn.sigmoid(gb_ref[...]) * yb).astype(o_ref.dtype)


def _merge(o_a, o_b, w_a, w_b, act32, tile_of, tm=512):
    t, d = o_a.shape
    n = w_a.shape[1]
    tn = IN_PROJ_TN
    tm = min(tm, t)
    ga_tile, gb_tile = tile_of["gate_a"], tile_of["gate_b"]
    return pl.pallas_call(
        _merge_kernel,
        grid=(t // tm, n // tn),
        in_specs=[pl.BlockSpec((tm, d), lambda i, j: (i, 0)),
                  pl.BlockSpec((tm, d), lambda i, j: (i, 0)),
                  pl.BlockSpec((d, tn), lambda i, j: (0, j)),
                  pl.BlockSpec((d, tn), lambda i, j: (0, j)),
                  pl.BlockSpec((tm, tn), lambda i, j: (i, ga_tile + j)),
                  pl.BlockSpec((tm, tn), lambda i, j: (i, gb_tile + j))],
        out_specs=pl.BlockSpec((tm, tn), lambda i, j: (i, j)),
        out_shape=jax.ShapeDtypeStruct((t, n), BF16),
        compiler_params=_params("parallel", "parallel"),
        name="branch_merge",
    )(o_a, o_b, w_a, w_b, act32, act32)


def _mixer_out_kernel(m_ref, w_ref, x_ref, g_ref, gain_ref, sc_ref, sh_ref, x1_ref, h2_ref):
    y = jnp.dot(m_ref[...], w_ref[...], preferred_element_type=F32)
    x1 = x_ref[...] + g_ref[0] * y
    x1_ref[...] = x1
    h2_ref[...] = _rms_modulate(x1, gain_ref[...], sc_ref[0], sh_ref[0]).astype(h2_ref.dtype)


def _mixer_out(merged, w, x2d, g1, gain2, sc2, sh2, seq, tm=256):
    t, d = x2d.shape
    tm = min(tm, seq)
    per_batch = seq // tm
    vec = pl.BlockSpec((1, 1, d), lambda i: (i // per_batch, 0, 0))
    row = pl.BlockSpec((tm, d), lambda i: (i, 0))
    return pl.pallas_call(
        _mixer_out_kernel,
        grid=(t // tm,),
        in_specs=[row, pl.BlockSpec((d, d), lambda i: (0, 0)), row, vec,
                  pl.BlockSpec((1, d), lambda i: (0, 0)), vec, vec],
        out_specs=[row, row],
        out_shape=[jax.ShapeDtypeStruct((t, d), F32), jax.ShapeDtypeStruct((t, d), BF16)],
        compiler_params=_params("parallel"),
        name="mixer_out_norm2",
    )(merged, w, x2d, g1, gain2, sc2, sh2)


def _ffn_up_kernel(h_ref, wg_ref, wu_ref, o_ref):
    g = jnp.dot(h_ref[...], wg_ref[...].astype(BF16), preferred_element_type=F32)
    u = jnp.dot(h_ref[...], wu_ref[...].astype(BF16), preferred_element_type=F32)
    o_ref[...] = (_silu(g) * u).astype(o_ref.dtype)


def _ffn_up(h2, w_gate_up, tm=1024, tn=512):
    t, d = h2.shape
    d_ff = w_gate_up.shape[1] // 2
    tm = min(tm, t)
    nj = d_ff // tn
    return pl.pallas_call(
        _ffn_up_kernel,
        grid=(t // tm, nj),
        in_specs=[pl.BlockSpec((tm, d), lambda i, j: (i, 0)),
                  pl.BlockSpec((d, tn), lambda i, j: (0, j)),
                  pl.BlockSpec((d, tn), lambda i, j: (0, nj + j))],
        out_specs=pl.BlockSpec((tm, tn), lambda i, j: (i, j)),
        out_shape=jax.ShapeDtypeStruct((t, d_ff), BF16),
        compiler_params=_params("parallel", "parallel"),
        name="ffn_up",
    )(h2, w_gate_up, w_gate_up)


def _ffn_down_kernel(a_ref, w_ref, x_ref, g_ref, gain_ref, o_ref, acc_ref):
    kk = pl.program_id(1)

    @pl.when(kk == 0)
    def _():
        acc_ref[...] = jnp.zeros_like(acc_ref)

    acc_ref[...] += jnp.dot(a_ref[...], w_ref[...], preferred_element_type=F32)

    @pl.when(kk == pl.num_programs(1) - 1)
    def _():
        x2 = x_ref[...] + g_ref[0] * acc_ref[...]
        y = x2 * lax.rsqrt(jnp.mean(x2 * x2, axis=-1, keepdims=True) + RMS_EPS)
        o_ref[...] = y * gain_ref[...]


def _ffn_down(act, w_down, x1, g2, final_gain, seq, tm=512, tk=1408):
    t, d_ff = act.shape
    d = w_down.shape[1]
    tm = min(tm, seq)
    per_batch = seq // tm
    return pl.pallas_call(
        _ffn_down_kernel,
        grid=(t // tm, d_ff // tk),
        in_specs=[pl.BlockSpec((tm, tk), lambda i, k: (i, k)),
                  pl.BlockSpec((tk, d), lambda i, k: (k, 0)),
                  pl.BlockSpec((tm, d), lambda i, k: (i, 0)),
                  pl.BlockSpec((1, 1, d), lambda i, k: (i // per_batch, 0, 0)),
                  pl.BlockSpec((1, d), lambda i, k: (0, 0))],
        out_specs=pl.BlockSpec((tm, d), lambda i, k: (i, 0)),
        out_shape=jax.ShapeDtypeStruct((t, d), F32),
        scratch_shapes=[pltpu.VMEM((tm, d), F32)],
        compiler_params=_params("parallel", "arbitrary"),
        name="ffn_down_final_norm",
    )(act, w_down, x1, g2, final_gain)


def _rope_tables(positions_flat, rot, period):
    half = rot // 2
    inv_freq = jnp.power(ROPE_THETA, -jnp.arange(0, rot, 2, dtype=F32) / rot)
    ang = positions_flat.astype(F32)[:, None] * inv_freq[None, :]
    lane_freq = (jnp.arange(LANES) % period) % half
    return jnp.cos(ang)[:, lane_freq], jnp.sin(ang)[:, lane_freq]


def _layer(x2d, mod, positions_flat, batch, seq, norm1_gain, norm2_gain, w_in, gla_gate_up,
           gla_gate_bias, gla_norm_gain, w_branch_gla, w_branch_dsa, w_merge_out, w_ffn_gate_up,
           w_ffn_down, final_gain):
    d = x2d.shape[1]
    sh1, sc1, g1, sh2, sc2, g2 = [mod[:, i * d:(i + 1) * d][:, None, :] for i in range(N_MOD)]

    h = _norm_modulate(x2d, norm1_gain[None, :], sc1, sh1, seq)

    plan = _in_proj_plan(d)
    tile_of = plan[3]
    cos_d, sin_d = _rope_tables(positions_flat, DSA_HEAD_DIM // ROPE_FRACTION, DSA_HEAD_DIM)
    cos_i, sin_i = _rope_tables(positions_flat, IDX_DIM // ROPE_FRACTION, IDX_DIM)
    act32, act16 = _in_project(h, w_in.T, (cos_d, sin_d, cos_i, sin_i), plan)

    gup_pad = jnp.zeros((LANES, d // 2), F32).at[:GLA_GATE_RANK].set(gla_gate_up)
    o_a = _gla(act32, act16, tile_of, gup_pad, gla_gate_bias[None, :], gla_norm_gain[None, :],
               batch, seq, rows=min(256, seq))

    iw_col = tile_of["i_k"] * IN_PROJ_TN + IDX_DIM
    iw_t = jnp.transpose(act32[:, iw_col:iw_col + IDX_HEADS].reshape(batch, seq, IDX_HEADS), (0, 2, 1))
    topk = min(IDX_TOPK_MAX, seq // 4)
    bias_t = _topk_bias(act32, tile_of, iw_t, batch, seq, topk)
    o_b = _attention(act16, tile_of, bias_t, batch, seq)

    merged = _merge(o_a, o_b, w_branch_gla.astype(BF16), w_branch_dsa.astype(BF16), act32, tile_of)
    x1, h2 = _mixer_out(merged, w_merge_out.astype(BF16), x2d, g1, norm2_gain[None, :], sc2, sh2, seq)
    act = _ffn_up(h2, w_ffn_gate_up)
    return _ffn_down(act, w_ffn_down.astype(BF16), x1, g2, final_gain[None, :], seq)


def kernel(x, c, positions, norm1_gain, norm2_gain, w_ada, b_ada, w_in, gla_gate_up, gla_gate_bias,
           gla_norm_gain, w_branch_gla, w_branch_dsa, w_merge_out, w_ffn_gate_up, w_ffn_down,
           final_norm_gain):
    batch, seq, d = x.shape
    depth = w_in.shape[0]
    assert depth == 1, "the final RMSNorm is fused into the single layer's FFN kernel"
    x2d = x.reshape(batch * seq, d)
    c_pad = jnp.zeros((SUBLANES, d), F32).at[:batch].set(c)
    mod = _modulation(c_pad, w_ada[0], b_ada[0][None, :])[:batch]
    out = _layer(x2d, mod, positions.reshape(-1), batch, seq, norm1_gain[0], norm2_gain[0], w_in[0],
                 gla_gate_up[0], gla_gate_bias[0], gla_norm_gain[0], w_branch_gla[0], w_branch_dsa[0],
                 w_merge_out[0], w_ffn_gate_up[0], w_ffn_down[0], final_norm_gain)
    return out.reshape(batch, seq, d)
```

```python
import functools

import jax
import jax.numpy as jnp
from jax import lax
from jax.experimental import pallas as pl
from jax.experimental.pallas import tpu as pltpu

F32 = jnp.float32
BF16 = jnp.bfloat16
HIGHEST = lax.Precision.HIGHEST

RMS_EPS = 1e-6
GLA_HEADS = 4
GLA_GATE_RANK = 16
GLA_TAU = 16.0
GLA_CHUNK = 64
GLA_SUB = 8
DSA_HEADS = 16
DSA_HEAD_DIM = 128
IDX_HEADS = 8
IDX_DIM = 64
IDX_TOPK_MAX = 256
ROPE_THETA = 500000.0
ROPE_FRACTION = 4
N_MOD = 6
LOG2_E = 1.4426950408889634

LANES = 128
SUBLANES = 8
VMEM_LIMIT = 48 * 1024 * 1024


def _params(*semantics):
    return pltpu.CompilerParams(dimension_semantics=semantics, vmem_limit_bytes=VMEM_LIMIT)


def _silu(x):
    return x * jax.nn.sigmoid(x)


def _mod_kernel(c_ref, w_ref, b_ref, o_ref):
    a = _silu(c_ref[...])
    o_ref[...] = jnp.dot(a, w_ref[...], precision=HIGHEST, preferred_element_type=F32) + b_ref[...]


def _modulation(c_pad, w_ada, b_ada, tn=1024):
    rows, d = c_pad.shape
    n = w_ada.shape[1]
    return pl.pallas_call(
        _mod_kernel,
        grid=(n // tn,),
        in_specs=[pl.BlockSpec((rows, d), lambda j: (0, 0)),
                  pl.BlockSpec((d, tn), lambda j: (0, j)),
                  pl.BlockSpec((1, tn), lambda j: (0, j))],
        out_specs=pl.BlockSpec((rows, tn), lambda j: (0, j)),
        out_shape=jax.ShapeDtypeStruct((rows, n), F32),
        compiler_params=_params("parallel"),
        name="adaln_mod",
    )(c_pad, w_ada, b_ada)


def _rms_modulate(x, gain, scale, shift):
    y = x * lax.rsqrt(jnp.mean(x * x, axis=-1, keepdims=True) + RMS_EPS)
    return (y * gain) * (1.0 + scale) + shift


def _norm_mod_kernel(x_ref, gain_ref, sc_ref, sh_ref, o_ref):
    o_ref[...] = _rms_modulate(x_ref[...], gain_ref[...], sc_ref[0], sh_ref[0]).astype(o_ref.dtype)


def _norm_modulate(x2d, gain, sc, sh, seq, tm=512):
    t, d = x2d.shape
    per_batch = seq // tm
    return pl.pallas_call(
        _norm_mod_kernel,
        grid=(t // tm,),
        in_specs=[pl.BlockSpec((tm, d), lambda i: (i, 0)),
                  pl.BlockSpec((1, d), lambda i: (0, 0)),
                  pl.BlockSpec((1, 1, d), lambda i: (i // per_batch, 0, 0)),
                  pl.BlockSpec((1, 1, d), lambda i: (i // per_batch, 0, 0))],
        out_specs=pl.BlockSpec((tm, d), lambda i: (i, 0)),
        out_shape=jax.ShapeDtypeStruct((t, d), BF16),
        compiler_params=_params("parallel"),
        name="norm1_modulate",
    )(x2d, gain, sc, sh)


def _rope_lanes(x, cos, sin, half, period, limit=LANES):
    lane = lax.broadcasted_iota(jnp.int32, x.shape, 1)
    in_head = lane % period
    upper = pltpu.roll(x, LANES - half, axis=1)
    lower = pltpu.roll(x, half, axis=1)
    live = lane < limit
    first = jnp.logical_and(live, in_head < half)
    second = jnp.logical_and(live, jnp.logical_and(in_head >= half, in_head < 2 * half))
    return jnp.where(first, x * cos - upper * sin,
                     jnp.where(second, lower * sin + x * cos, x))


PLAIN32, GLA_Q32, IDX_Q32, IDX_K32, PLAIN16, DSA_ROPE16 = range(6)
IN_PROJ_ROPE_ROWS = 256
IN_PROJ_TN = 512


def _in_proj_kernel(off_ref, kind_ref, h_ref, wt_ref, cos_d_ref, sin_d_ref, cos_i_ref, sin_i_ref,
                    o32_ref, o16_ref, acc_ref):
    kind = kind_ref[pl.program_id(1)]
    tm, tn = o32_ref.shape
    n_groups = tn // LANES
    rot_d = DSA_HEAD_DIM // ROPE_FRACTION
    rot_i = IDX_DIM // ROPE_FRACTION

    def product():
        return lax.dot_general(h_ref[...], wt_ref[...].astype(BF16), (((1,), (1,)), ((), ())),
                               preferred_element_type=F32)

    def rope_rows(cos_ref, sin_ref, out_ref, half, period, groups, pre_scale=1.0, limit=LANES):
        def chunk(r, carry):
            rows = pl.ds(pl.multiple_of(r * IN_PROJ_ROPE_ROWS, IN_PROJ_ROPE_ROWS), IN_PROJ_ROPE_ROWS)
            cos = cos_ref[rows, :]
            sin = sin_ref[rows, :]
            for g in groups:
                cols = slice(g * LANES, (g + 1) * LANES)
                x = acc_ref[rows, cols]
                if pre_scale != 1.0:
                    x = x * pre_scale
                out_ref[rows, cols] = _rope_lanes(x, cos, sin, half, period, limit).astype(out_ref.dtype)
            return carry
        lax.fori_loop(0, tm // IN_PROJ_ROPE_ROWS, chunk, 0)

    @pl.when(kind == PLAIN32)
    def _():
        o32_ref[...] = product()

    @pl.when(kind == GLA_Q32)
    def _():
        o32_ref[...] = product() * ((wt_ref.shape[1] // 2 // GLA_HEADS) ** -0.5)

    @pl.when(kind == IDX_Q32)
    def _():
        acc_ref[...] = product()
        rope_rows(cos_i_ref, sin_i_ref, o32_ref, rot_i // 2, IDX_DIM, range(n_groups), pre_scale=IDX_DIM ** -0.5)

    @pl.when(kind == IDX_K32)
    def _():
        acc = product()
        acc_ref[...] = acc
        o32_ref[...] = acc
        rope_rows(cos_i_ref, sin_i_ref, o32_ref, rot_i // 2, IDX_DIM, range(1), limit=IDX_DIM)

    @pl.when(kind == PLAIN16)
    def _():
        o16_ref[...] = product().astype(BF16)

    @pl.when(kind == DSA_ROPE16)
    def _():
        acc_ref[...] = product()
        rope_rows(cos_d_ref, sin_d_ref, o16_ref, rot_d // 2, DSA_HEAD_DIM, range(n_groups))


def _in_proj_plan(d):
    tn = IN_PROJ_TN
    gla_qk, dsa_w, idx_w = d // 2, DSA_HEADS * DSA_HEAD_DIM, IDX_HEADS * IDX_DIM
    names = ("g_q", "g_k", "g_v", "g_r", "g_lr", "d_q", "d_k", "d_v", "i_q", "i_k", "i_w", "gate_a", "gate_b")
    widths = (gla_qk, gla_qk, d, d, GLA_GATE_RANK, dsa_w, dsa_w, dsa_w, idx_w, IDX_DIM, IDX_HEADS, d, d)
    start, pos = {}, 0
    for nm, wd in zip(names, widths):
        start[nm] = pos
        pos += wd
    width = dict(zip(names, widths))
    assert start["i_w"] == start["i_k"] + IDX_DIM and IDX_DIM + IDX_HEADS <= LANES
    f32_groups = (("g_q", GLA_Q32), ("g_k", PLAIN32), ("g_r", PLAIN32), ("g_lr", PLAIN32), ("i_q", IDX_Q32),
                  ("i_k", IDX_K32), ("gate_a", PLAIN32), ("gate_b", PLAIN32))
    bf16_groups = (("g_v", PLAIN16), ("d_q", DSA_ROPE16), ("d_k", DSA_ROPE16), ("d_v", PLAIN16))
    offsets, kinds, tile_of = [], [], {}
    for groups in (f32_groups, bf16_groups):
        base = len(offsets)
        for nm, kind in groups:
            tile_of[nm] = len(offsets) - base
            for t in range(-(-width[nm] // tn)):
                offsets.append(start[nm] + t * tn)
                kinds.append(kind)
        if groups is f32_groups:
            n32 = len(offsets)
    assert all(o % SUBLANES == 0 and o + tn <= pos for o in offsets)
    return offsets, kinds, n32, tile_of


def _in_project(h, w_in_t, tables, plan, tm=1024):
    t, d = h.shape
    tn = IN_PROJ_TN
    offsets, kinds, n32, _ = plan
    n16 = len(offsets) - n32
    tm = min(tm, t)
    table_spec = pl.BlockSpec((tm, LANES), lambda i, j, off, kind: (i, 0))
    grid_spec = pltpu.PrefetchScalarGridSpec(
        num_scalar_prefetch=2,
        grid=(t // tm, len(offsets)),
        in_specs=[pl.BlockSpec((tm, d), lambda i, j, off, kind: (i, 0)),
                  pl.BlockSpec((pl.Element(tn), pl.Element(d)),
                               lambda i, j, off, kind: (pl.multiple_of(off[j], SUBLANES), 0)),
                  table_spec, table_spec, table_spec, table_spec],
        out_specs=[pl.BlockSpec((tm, tn), lambda i, j, off, kind: (i, jnp.minimum(j, n32 - 1))),
                   pl.BlockSpec((tm, tn), lambda i, j, off, kind: (i, jnp.maximum(j - n32, 0)))],
        scratch_shapes=[pltpu.VMEM((tm, tn), F32)],
    )
    return pl.pallas_call(
        _in_proj_kernel,
        grid_spec=grid_spec,
        out_shape=[jax.ShapeDtypeStruct((t, n32 * tn), F32), jax.ShapeDtypeStruct((t, n16 * tn), BF16)],
        compiler_params=_params("parallel", "arbitrary"),
        name="in_proj",
    )(jnp.asarray(offsets, jnp.int32), jnp.asarray(kinds, jnp.int32), h, w_in_t, *tables)


def _split3_bf16(x):
    hi, rest = _split_bf16(x)
    mid, lo = _split_bf16(rest)
    return hi, mid, lo


def _gla_kernel(q_ref, k_ref, v_ref, gr_ref, sm_ref, gup_ref, gb_ref, ng_ref, tri_ref, o_ref, st_ref):
    @pl.when(pl.program_id(2) == 0)
    def _():
        st_ref[...] = jnp.zeros_like(st_ref)

    rows = q_ref.shape[0]
    dk = q_ref.shape[1]
    c, sub = GLA_CHUNK, GLA_SUB
    n_sub = c // sub

    a_hi, a_lo = _split_bf16(sm_ref[:, 0:GLA_GATE_RANK])
    gate_lhs = jnp.concatenate(
        [a_hi, a_lo, a_hi, jnp.zeros((rows, LANES - 3 * GLA_GATE_RANK), F32)], axis=1).astype(BF16)
    z = jnp.dot(gate_lhs, gup_ref[...], preferred_element_type=F32) + gb_ref[...]
    log_g = (jnp.minimum(z, 0.0) - jnp.log(1.0 + jnp.exp(-jnp.abs(z)))) * (1.0 / GLA_TAU)
    parts = jnp.dot(tri_ref[...], jnp.concatenate(_split3_bf16(log_g), axis=1).astype(BF16),
                    preferred_element_type=F32)
    b = (parts[:, 0:dk] + parts[:, dk:2 * dk]) + parts[:, 2 * dk:3 * dk]

    q = q_ref[...]
    k = k_ref[...]

    k_tiles = k.reshape(rows // sub, sub, dk)
    b_tiles = b.reshape(rows // sub, sub, dk)
    row = lax.broadcasted_iota(jnp.int32, (rows, 1), 0)
    row_in_sub = row % sub
    col = lax.broadcasted_iota(jnp.int32, (rows, c), 1)
    target = lax.broadcasted_iota(jnp.int32, (rows, c), 0) % c
    a_diag = jnp.zeros((rows, c), F32)
    for delta in range(sub):
        if delta == 0:
            k_d, b_d = k, b
        else:
            k_d = pltpu.roll(k_tiles, delta, axis=1).reshape(rows, dk)
            b_d = pltpu.roll(b_tiles, delta, axis=1).reshape(rows, dk)
        decay = jnp.exp(jnp.where(row_in_sub >= delta, b - b_d, -jnp.inf))
        a = jnp.sum(q * k_d * decay, axis=-1, keepdims=True)
        a_diag = jnp.where(col == target - delta, a, a_diag)

    for ch in range(rows // c):
        base = ch * c
        qc = q[base:base + c]
        kc = k[base:base + c]
        bc = b[base:base + c]
        vc = v_ref[base:base + c, :]
        st = st_ref[...]

        o_inter = lax.dot_general((qc * jnp.exp(bc)).astype(BF16), st.astype(BF16),
                                  (((1,), (1,)), ((), ())), preferred_element_type=F32)

        blocks = [jnp.zeros((sub, c), F32)]
        for i_sub in range(1, n_sub):
            lo = i_sub * sub
            ref = bc[lo - 1:lo]
            q_hat = qc[lo:lo + sub] * jnp.exp(bc[lo:lo + sub] - ref)
            k_hat = kc[0:lo] * jnp.exp(ref - bc[0:lo])
            k_pad = jnp.concatenate([k_hat, jnp.zeros((c - lo, dk), F32)], axis=0)
            blocks.append(lax.dot_general(q_hat.astype(BF16), k_pad.astype(BF16),
                                          (((1,), (1,)), ((), ())), preferred_element_type=F32))
        attn = jnp.concatenate(blocks, axis=0) + a_diag[base:base + c]

        o = jnp.dot(attn.astype(BF16), vc, preferred_element_type=F32) + o_inter

        b_last = bc[c - 1:c]
        k_dec = (kc * jnp.exp(b_last - bc)).astype(BF16)
        st_ref[...] = st * jnp.exp(b_last) + lax.dot_general(
            vc, k_dec, (((0,), (0,)), ((), ())), preferred_element_type=F32)

        o = o * lax.rsqrt(jnp.mean(o * o, axis=-1, keepdims=True) + RMS_EPS) * ng_ref[...]
        o_ref[base:base + c, :] = (o * _silu(gr_ref[base:base + c, :])).astype(o_ref.dtype)


def _gla(act32, act16, tile_of, gate_up, gbias, ngain, batch, seq, rows=256):
    dk = gate_up.shape[1] // GLA_HEADS
    dv = ngain.shape[1] // GLA_HEADS
    t = act32.shape[0]
    nr = seq // rows
    g_hi, g_lo = _split_bf16(gate_up)
    gup_stack = jnp.concatenate(
        [g_hi, g_hi, g_lo, jnp.zeros((LANES - 3 * GLA_GATE_RANK, gate_up.shape[1]), F32)], axis=0).astype(BF16)
    idx = jnp.arange(rows)
    tri = jnp.logical_and(idx[:, None] // GLA_CHUNK == idx[None, :] // GLA_CHUNK,
                          idx[None, :] <= idx[:, None]).astype(BF16)

    def cols(group, width):
        first = tile_of[group] * IN_PROJ_TN // width
        return lambda b, h, r: (b * nr + r, first + h)

    lr_block = tile_of["g_lr"] * IN_PROJ_TN // LANES
    return pl.pallas_call(
        _gla_kernel,
        grid=(batch, GLA_HEADS, nr),
        in_specs=[pl.BlockSpec((rows, dk), cols("g_q", dk)),
                  pl.BlockSpec((rows, dk), cols("g_k", dk)),
                  pl.BlockSpec((rows, dv), cols("g_v", dv)),
                  pl.BlockSpec((rows, dv), cols("g_r", dv)),
                  pl.BlockSpec((rows, LANES), lambda b, h, r: (b * nr + r, lr_block)),
                  pl.BlockSpec((LANES, dk), lambda b, h, r: (0, h)),
                  pl.BlockSpec((1, dk), lambda b, h, r: (0, h)),
                  pl.BlockSpec((1, dv), lambda b, h, r: (0, h)),
                  pl.BlockSpec((rows, rows), lambda b, h, r: (0, 0))],
        out_specs=pl.BlockSpec((rows, dv), lambda b, h, r: (b * nr + r, h)),
        out_shape=jax.ShapeDtypeStruct((t, GLA_HEADS * dv), BF16),
        scratch_shapes=[pltpu.VMEM((dv, dk), F32)],
        compiler_params=_params("parallel", "parallel", "arbitrary"),
        name="gla",
    )(act32, act32, act16, act32, act32, gup_stack, gbias, ngain, tri)


def _split_bf16(x):
    hi = x.astype(BF16).astype(F32)
    return hi, x - hi


def _sortable_bits_to_float(key):
    return lax.bitcast_convert_type(jnp.where(key < 0, key ^ jnp.int32(0x7FFFFFFF), key), F32)


SORT_KEY_NEG_INF = -2139095041


def _topk_bias_kernel(iq_ref, ik_ref, iw_ref, tri_ref, o_ref, score_ref, rhs_ref, *, topk, tq):
    qi = pl.program_id(1)
    tk = tq
    nk = ik_ref.shape[0] // tk
    n_live = qi + 1
    w = iw_ref[0] * (IDX_HEADS ** -0.5)
    pad_q = jnp.zeros((tq, IDX_DIM), F32)
    for h in range(IDX_HEADS):
        hi, lo = _split_bf16(iq_ref[:, h * IDX_DIM:(h + 1) * IDX_DIM])
        rhs_ref[h] = jnp.concatenate([hi, hi, lo, pad_q], axis=1).astype(BF16)
    t_pos = qi * tq + lax.broadcasted_iota(jnp.int32, (tk, tq), 1)
    s_off = lax.broadcasted_iota(jnp.int32, (tk, tq), 0)
    pad_k = jnp.zeros((tk, IDX_DIM), F32)

    def block_rows(kb):
        return pl.ds(pl.multiple_of(kb * tk, tk), tk)

    def score_block(kb, carry):
        hi, lo = _split_bf16(ik_ref[block_rows(kb), 0:IDX_DIM])
        lhs = jnp.concatenate([hi, lo, hi, pad_k], axis=1).astype(BF16)
        score = jnp.zeros((tk, tq), F32)
        for h in range(IDX_HEADS):
            dots = lax.dot_general(lhs, rhs_ref[h], (((1,), (1,)), ((), ())), preferred_element_type=F32)
            score = score + w[h:h + 1, :] * jnp.maximum(dots, 0.0)
        score_ref[block_rows(kb), :] = jnp.where(kb * tk + s_off <= t_pos, score + 0.0, -jnp.inf)
        return carry

    lax.fori_loop(0, n_live, score_block, 0)

    def count(pred):
        def blk(kb, part):
            hit = jnp.where(pred(score_ref[block_rows(kb), :]), 1.0, 0.0)
            return part + jnp.sum(hit.reshape(tk // SUBLANES, SUBLANES, tq), axis=0)
        part = lax.fori_loop(0, n_live, blk, jnp.zeros((SUBLANES, tq), F32))
        return jnp.sum(part, axis=0, keepdims=True)

    def search(i, ans):
        cand = ans ^ jnp.left_shift(jnp.int32(1), 31 - i)
        cand_f = _sortable_bits_to_float(cand)
        cnt = count(lambda s: s >= cand_f)
        accept = jnp.logical_or(cnt >= float(topk), cand < SORT_KEY_NEG_INF)
        return jnp.where(accept, cand, ans)

    thr = _sortable_bits_to_float(
        lax.fori_loop(0, 32, search, jnp.full((1, tq), jnp.iinfo(jnp.int32).min, jnp.int32)))
    need = float(topk) - count(lambda s: s > thr)

    def emit(kb, seen):
        blk = score_ref[block_rows(kb), :]
        eq = jnp.where(blk == thr, 1.0, 0.0)
        rank = jnp.dot(tri_ref[...], eq.astype(BF16), preferred_element_type=F32) + seen
        chosen = jnp.logical_or(blk > thr, jnp.logical_and(blk == thr, rank <= need))
        o_ref[0, block_rows(kb), :] = jnp.where(
            jnp.logical_and(chosen, kb * tk + s_off <= t_pos), 0.0, -jnp.inf)
        return seen + jnp.sum(eq, axis=0, keepdims=True)

    lax.fori_loop(0, n_live, emit, jnp.zeros((1, tq), F32))

    def fill(kb, carry):
        o_ref[0, block_rows(kb), :] = jnp.full((tk, tq), -jnp.inf, F32)
        return carry

    lax.fori_loop(n_live, nk, fill, 0)


def _topk_bias(act32, tile_of, iw_t, batch, seq, topk, tq=256):
    tq = min(tq, seq)
    nq = seq // tq
    width = IDX_HEADS * IDX_DIM
    iq_block = tile_of["i_q"] * IN_PROJ_TN // width
    ik_block = tile_of["i_k"] * IN_PROJ_TN // LANES
    tri = jnp.tril(jnp.ones((tq, tq), BF16))
    return pl.pallas_call(
        functools.partial(_topk_bias_kernel, topk=topk, tq=tq),
        grid=(batch, nq),
        in_specs=[pl.BlockSpec((tq, width), lambda b, i: (b * nq + i, iq_block)),
                  pl.BlockSpec((seq, LANES), lambda b, i: (b, ik_block)),
                  pl.BlockSpec((1, IDX_HEADS, tq), lambda b, i: (b, 0, i)),
                  pl.BlockSpec((tq, tq), lambda b, i: (0, 0))],
        out_specs=pl.BlockSpec((1, seq, tq), lambda b, i: (b, 0, i)),
        out_shape=jax.ShapeDtypeStruct((batch, seq, seq), F32),
        scratch_shapes=[pltpu.VMEM((seq, tq), F32),
                        pltpu.VMEM((IDX_HEADS, tq, 4 * IDX_DIM), BF16)],
        compiler_params=_params("parallel", "parallel"),
        name="indexer_topk_bias",
    )(act32, act32, iw_t, tri)


ATTN_HEADS_PER_STEP = 8


def _attn_kernel(q_ref, k_ref, v_ref, bias_ref, o_ref, acc_ref):
    qi = pl.program_id(1)
    tq = q_ref.shape[0]
    tk = tq
    dh = DSA_HEAD_DIM
    group = q_ref.shape[1] // dh
    scale2 = (dh ** -0.5) * LOG2_E
    acc_ref[...] = jnp.zeros_like(acc_ref)

    def body(kj, carry):
        rows = pl.ds(pl.multiple_of(kj * tk, tk), tk)
        bias = bias_ref[0, rows, :]
        heads = [slice(g * dh, (g + 1) * dh) for g in range(group)]
        logits = [lax.dot_general(k_ref[rows, cols], q_ref[:, cols], (((1,), (1,)), ((), ())),
                                  preferred_element_type=F32) for cols in heads]
        new = []
        for g in range(group):
            m, l = carry[g]
            s = logits[g] * scale2 + bias
            m_new = jnp.maximum(m, jnp.max(s, axis=0, keepdims=True))
            m_safe = jnp.where(m_new == -jnp.inf, 0.0, m_new)
            alpha = jnp.exp2(m - m_safe)
            p = jnp.exp2(s - m_safe)
            new.append((m_new, alpha * l + jnp.sum(p, axis=0, keepdims=True)))
            update = lax.dot_general(v_ref[rows, heads[g]], p.astype(BF16), (((0,), (0,)), ((), ())),
                                     preferred_element_type=F32)
            acc_ref[g] = alpha * acc_ref[g] + update
        return tuple(new)

    init = tuple((jnp.full((1, tq), -jnp.inf, F32), jnp.zeros((1, tq), F32)) for _ in range(group))
    final = lax.fori_loop(0, qi + 1, body, init)
    for g in range(group):
        o_ref[:, g * dh:(g + 1) * dh] = (acc_ref[g] / final[g][1]).T.astype(o_ref.dtype)


def _attention(act16, tile_of, bias_t, batch, seq, tq=256):
    tq = min(tq, seq)
    nq = seq // tq
    width = ATTN_HEADS_PER_STEP * DSA_HEAD_DIM
    n_groups = DSA_HEADS // ATTN_HEADS_PER_STEP
    t = act16.shape[0]
    q_block, k_block, v_block = (tile_of[nm] * IN_PROJ_TN // width for nm in ("d_q", "d_k", "d_v"))
    return pl.pallas_call(
        _attn_kernel,
        grid=(batch, nq, n_groups),
        in_specs=[pl.BlockSpec((tq, width), lambda b, i, h: (b * nq + i, q_block + h)),
                  pl.BlockSpec((seq, width), lambda b, i, h: (b, k_block + h)),
                  pl.BlockSpec((seq, width), lambda b, i, h: (b, v_block + h)),
                  pl.BlockSpec((1, seq, tq), lambda b, i, h: (b, 0, i))],
        out_specs=pl.BlockSpec((tq, width), lambda b, i, h: (b * nq + i, h)),
        out_shape=jax.ShapeDtypeStruct((t, DSA_HEADS * DSA_HEAD_DIM), BF16),
        scratch_shapes=[pltpu.VMEM((ATTN_HEADS_PER_STEP, DSA_HEAD_DIM, tq), F32)],
        compiler_params=_params("parallel", "parallel", "arbitrary"),
        name="dsa_attention",
    )(act16, act16, act16, bias_t)


def _merge_kernel(oa_ref, ob_ref, wa_ref, wb_ref, ga_ref, gb_ref, o_ref):
    ya = jnp.dot(oa_ref[...], wa_ref[...], preferred_element_type=F32)
    yb = jnp.dot(ob_ref[...], wb_ref[...], preferred_element_type=F32)
    o_ref[...] = (jax.nn.sigmoid(ga_ref[...]) * ya + jax.nn.sigmoid(gb_ref[...]) * yb).astype(o_ref.dtype)


def _merge(o_a, o_b, w_a, w_b, act32, tile_of, tm=512):
    t, d = o_a.shape
    n = w_a.shape[1]
    tn = IN_PROJ_TN
    tm = min(tm, t)
    ga_tile, gb_tile = tile_of["gate_a"], tile_of["gate_b"]
    return pl.pallas_call(
        _merge_kernel,
        grid=(t // tm, n // tn),
        in_specs=[pl.BlockSpec((tm, d), lambda i, j: (i, 0)),
                  pl.BlockSpec((tm, d), lambda i, j: (i, 0)),
                  pl.BlockSpec((d, tn), lambda i, j: (0, j)),
                  pl.BlockSpec((d, tn), lambda i, j: (0, j)),
                  pl.BlockSpec((tm, tn), lambda i, j: (i, ga_tile + j)),
                  pl.BlockSpec((tm, tn), lambda i, j: (i, gb_tile + j))],
        out_specs=pl.BlockSpec((tm, tn), lambda i, j: (i, j)),
        out_shape=jax.ShapeDtypeStruct((t, n), BF16),
        compiler_params=_params("parallel", "parallel"),
        name="branch_merge",
    )(o_a, o_b, w_a, w_b, act32, act32)


def _mixer_out_kernel(m_ref, w_ref, x_ref, g_ref, gain_ref, sc_ref, sh_ref, x1_ref, h2_ref):
    y = jnp.dot(m_ref[...], w_ref[...], preferred_element_type=F32)
    x1 = x_ref[...] + g_ref[0] * y
    x1_ref[...] = x1
    h2_ref[...] = _rms_modulate(x1, gain_ref[...], sc_ref[0], sh_ref[0]).astype(h2_ref.dtype)


def _mixer_out(merged, w, x2d, g1, gain2, sc2, sh2, seq, tm=256):
    t, d = x2d.shape
    tm = min(tm, seq)
    per_batch = seq // tm
    vec = pl.BlockSpec((1, 1, d), lambda i: (i // per_batch, 0, 0))
    row = pl.BlockSpec((tm, d), lambda i: (i, 0))
    return pl.pallas_call(
        _mixer_out_kernel,
        grid=(t // tm,),
        in_specs=[row, pl.BlockSpec((d, d), lambda i: (0, 0)), row, vec,
                  pl.BlockSpec((1, d), lambda i: (0, 0)), vec, vec],
        out_specs=[row, row],
        out_shape=[jax.ShapeDtypeStruct((t, d), F32), jax.ShapeDtypeStruct((t, d), BF16)],
        compiler_params=_params("parallel"),
        name="mixer_out_norm2",
    )(merged, w, x2d, g1, gain2, sc2, sh2)


def _ffn_up_kernel(h_ref, wg_ref, wu_ref, o_ref):
    g = jnp.dot(h_ref[...], wg_ref[...].astype(BF16), preferred_element_type=F32)
    u = jnp.dot(h_ref[...], wu_ref[...].astype(BF16), preferred_element_type=F32)
    o_ref[...] = (_silu(g) * u).astype(o_ref.dtype)


def _ffn_up(h2, w_gate_up, tm=1024, tn=512):
    t, d = h2.shape
    d_ff = w_gate_up.shape[1] // 2
    tm = min(tm, t)
    nj = d_ff // tn
    return pl.pallas_call(
        _ffn_up_kernel,
        grid=(t // tm, nj),
        in_specs=[pl.BlockSpec((tm, d), lambda i, j: (i, 0)),
                  pl.BlockSpec((d, tn), lambda i, j: (0, j)),
                  pl.BlockSpec((d, tn), lambda i, j: (0, nj + j))],
        out_specs=pl.BlockSpec((tm, tn), lambda i, j: (i, j)),
        out_shape=jax.ShapeDtypeStruct((t, d_ff), BF16),
        compiler_params=_params("parallel", "parallel"),
        name="ffn_up",
    )(h2, w_gate_up, w_gate_up)


def _ffn_down_kernel(a_ref, w_ref, x_ref, g_ref, gain_ref, o_ref, acc_ref):
    kk = pl.program_id(1)

    @pl.when(kk == 0)
    def _():
        acc_ref[...] = jnp.zeros_like(acc_ref)

    acc_ref[...] += jnp.dot(a_ref[...], w_ref[...], preferred_element_type=F32)

    @pl.when(kk == pl.num_programs(1) - 1)
    def _():
        x2 = x_ref[...] + g_ref[0] * acc_ref[...]
        y = x2 * lax.rsqrt(jnp.mean(x2 * x2, axis=-1, keepdims=True) + RMS_EPS)
        o_ref[...] = y * gain_ref[...]


def _ffn_down(act, w_down, x1, g2, final_gain, seq, tm=512, tk=1408):
    t, d_ff = act.shape
    d = w_down.shape[1]
    tm = min(tm, seq)
    per_batch = seq // tm
    return pl.pallas_call(
        _ffn_down_kernel,
        grid=(t // tm, d_ff // tk),
        in_specs=[pl.BlockSpec((tm, tk), lambda i, k: (i, k)),
                  pl.BlockSpec((tk, d), lambda i, k: (k, 0)),
                  pl.BlockSpec((tm, d), lambda i, k: (i, 0)),
                  pl.BlockSpec((1, 1, d), lambda i, k: (i // per_batch, 0, 0)),
                  pl.BlockSpec((1, d), lambda i, k: (0, 0))],
        out_specs=pl.BlockSpec((tm, d), lambda i, k: (i, 0)),
        out_shape=jax.ShapeDtypeStruct((t, d), F32),
        scratch_shapes=[pltpu.VMEM((tm, d), F32)],
        compiler_params=_params("parallel", "arbitrary"),
        name="ffn_down_final_norm",
    )(act, w_down, x1, g2, final_gain)


def _rope_tables(positions_flat, rot, period):
    half = rot // 2
    inv_freq = jnp.power(ROPE_THETA, -jnp.arange(0, rot, 2, dtype=F32) / rot)
    ang = positions_flat.astype(F32)[:, None] * inv_freq[None, :]
    lane_freq = (jnp.arange(LANES) % period) % half
    return jnp.cos(ang)[:, lane_freq], jnp.sin(ang)[:, lane_freq]


def _layer(x2d, mod, positions_flat, batch, seq, norm1_gain, norm2_gain, w_in, gla_gate_up,
           gla_gate_bias, gla_norm_gain, w_branch_gla, w_branch_dsa, w_merge_out, w_ffn_gate_up,
           w_ffn_down, final_gain):
    d = x2d.shape[1]
    sh1, sc1, g1, sh2, sc2, g2 = [mod[:, i * d:(i + 1) * d][:, None, :] for i in range(N_MOD)]

    h = _norm_modulate(x2d, norm1_gain[None, :], sc1, sh1, seq)

    plan = _in_proj_plan(d)
    tile_of = plan[3]
    cos_d, sin_d = _rope_tables(positions_flat, DSA_HEAD_DIM // ROPE_FRACTION, DSA_HEAD_DIM)
    cos_i, sin_i = _rope_tables(positions_flat, IDX_DIM // ROPE_FRACTION, IDX_DIM)
    act32, act16 = _in_project(h, w_in.T, (cos_d, sin_d, cos_i, sin_i), plan)

    o_a = _gla(act32, act16, tile_of, gla_gate_up, gla_gate_bias[None, :], gla_norm_gain[None, :],
               batch, seq, rows=min(256, seq))

    iw_col = tile_of["i_k"] * IN_PROJ_TN + IDX_DIM
    iw_t = jnp.transpose(act32[:, iw_col:iw_col + IDX_HEADS].reshape(batch, seq, IDX_HEADS), (0, 2, 1))
    topk = min(IDX_TOPK_MAX, seq // 4)
    bias_t = _topk_bias(act32, tile_of, iw_t, batch, seq, topk)
    o_b = _attention(act16, tile_of, bias_t, batch, seq)

    merged = _merge(o_a, o_b, w_branch_gla.astype(BF16), w_branch_dsa.astype(BF16), act32, tile_of)
    x1, h2 = _mixer_out(merged, w_merge_out.astype(BF16), x2d, g1, norm2_gain[None, :], sc2, sh2, seq)
    act = _ffn_up(h2, w_ffn_gate_up)
    return _ffn_down(act, w_ffn_down.astype(BF16), x1, g2, final_gain[None, :], seq)


def kernel(x, c, positions, norm1_gain, norm2_gain, w_ada, b_ada, w_in, gla_gate_up, gla_gate_bias,
           gla_norm_gain, w_branch_gla, w_branch_dsa, w_merge_out, w_ffn_gate_up, w_ffn_down,
           final_norm_gain):
    batch, seq, d = x.shape
    depth = w_in.shape[0]
    assert depth == 1, "the final RMSNorm is fused into the single layer's FFN kernel"
    x2d = x.reshape(batch * seq, d)
    c_pad = jnp.zeros((SUBLANES, d), F32).at[:batch].set(c)
    mod = _modulation(c_pad, w_ada[0], b_ada[0][None, :])[:batch]
    out = _layer(x2d, mod, positions.reshape(-1), batch, seq, norm1_gain[0], norm2_gain[0], w_in[0],
                 gla_gate_up[0], gla_gate_bias[0], gla_norm_gain[0], w_branch_gla[0], w_branch_dsa[0],
                 w_merge_out[0], w_ffn_gate_up[0], w_ffn_down[0], final_norm_gain)
    return out.reshape(batch, seq, d)
```

```python
import functools

import jax
import jax.numpy as jnp
from jax import lax
from jax.experimental import pallas as pl
from jax.experimental.pallas import tpu as pltpu

F32 = jnp.float32
BF16 = jnp.bfloat16
HIGHEST = lax.Precision.HIGHEST

RMS_EPS = 1e-6
GLA_HEADS = 4
GLA_GATE_RANK = 16
GLA_TAU = 16.0
GLA_CHUNK = 64
GLA_SUB = 8
DSA_HEADS = 16
DSA_HEAD_DIM = 128
IDX_HEADS = 8
IDX_DIM = 64
IDX_TOPK_MAX = 256
ROPE_THETA = 500000.0
ROPE_FRACTION = 4
N_MOD = 6
LOG2_E = 1.4426950408889634

LANES = 128
SUBLANES = 8
VMEM_LIMIT = 48 * 1024 * 1024


def _params(*semantics):
    return pltpu.CompilerParams(dimension_semantics=semantics, vmem_limit_bytes=VMEM_LIMIT)


def _silu(x):
    return x * jax.nn.sigmoid(x)


def _mod_kernel(c_ref, w_ref, b_ref, o_ref):
    a = _silu(c_ref[...])
    o_ref[...] = jnp.dot(a, w_ref[...], precision=HIGHEST, preferred_element_type=F32) + b_ref[...]


def _modulation(c_pad, w_ada, b_ada, tn=2048):
    rows, d = c_pad.shape
    n = w_ada.shape[1]
    return pl.pallas_call(
        _mod_kernel,
        grid=(n // tn,),
        in_specs=[pl.BlockSpec((rows, d), lambda j: (0, 0)),
                  pl.BlockSpec((d, tn), lambda j: (0, j)),
                  pl.BlockSpec((1, tn), lambda j: (0, j))],
        out_specs=pl.BlockSpec((rows, tn), lambda j: (0, j)),
        out_shape=jax.ShapeDtypeStruct((rows, n), F32),
        compiler_params=_params("parallel"),
        name="adaln_mod",
    )(c_pad, w_ada, b_ada)


def _rms_modulate(x, gain, scale, shift):
    y = x * lax.rsqrt(jnp.mean(x * x, axis=-1, keepdims=True) + RMS_EPS)
    return (y * gain) * (1.0 + scale) + shift


def _rope_lanes(x, cos, sin, half, period, limit=LANES):
    lane = lax.broadcasted_iota(jnp.int32, x.shape, 1)
    in_head = lane % period
    upper = pltpu.roll(x, LANES - half, axis=1)
    lower = pltpu.roll(x, half, axis=1)
    live = lane < limit
    first = jnp.logical_and(live, in_head < half)
    second = jnp.logical_and(live, jnp.logical_and(in_head >= half, in_head < 2 * half))
    return jnp.where(first, x * cos - upper * sin,
                     jnp.where(second, lower * sin + x * cos, x))


PLAIN32, GLA_Q32, IDX_Q32, IDX_K32, PLAIN16, DSA_ROPE16 = range(6)
IN_PROJ_ROPE_ROWS = 256
IN_PROJ_TN = 512


def _in_proj_kernel(off_ref, kind_ref, x_ref, gain_ref, sc_ref, sh_ref, wt_ref, cos_d_ref, sin_d_ref,
                    cos_i_ref, sin_i_ref, o32_ref, o16_ref, h_ref, acc_ref):
    kind = kind_ref[pl.program_id(1)]
    tm, tn = o32_ref.shape
    n_groups = tn // LANES
    rot_d = DSA_HEAD_DIM // ROPE_FRACTION
    rot_i = IDX_DIM // ROPE_FRACTION

    @pl.when(pl.program_id(1) == 0)
    def _():
        h_ref[...] = _rms_modulate(x_ref[...], gain_ref[...], sc_ref[0], sh_ref[0]).astype(BF16)

    def product():
        return lax.dot_general(h_ref[...], wt_ref[...].astype(BF16), (((1,), (1,)), ((), ())),
                               preferred_element_type=F32)

    def rope_rows(cos_ref, sin_ref, out_ref, half, period, groups, pre_scale=1.0, limit=LANES):
        def chunk(r, carry):
            rows = pl.ds(pl.multiple_of(r * IN_PROJ_ROPE_ROWS, IN_PROJ_ROPE_ROWS), IN_PROJ_ROPE_ROWS)
            cos = cos_ref[rows, :]
            sin = sin_ref[rows, :]
            for g in groups:
                cols = slice(g * LANES, (g + 1) * LANES)
                x = acc_ref[rows, cols]
                if pre_scale != 1.0:
                    x = x * pre_scale
                out_ref[rows, cols] = _rope_lanes(x, cos, sin, half, period, limit).astype(out_ref.dtype)
            return carry
        lax.fori_loop(0, tm // IN_PROJ_ROPE_ROWS, chunk, 0)

    @pl.when(kind == PLAIN32)
    def _():
        o32_ref[...] = product()

    @pl.when(kind == GLA_Q32)
    def _():
        o32_ref[...] = product() * ((wt_ref.shape[1] // 2 // GLA_HEADS) ** -0.5)

    @pl.when(kind == IDX_Q32)
    def _():
        acc_ref[...] = product()
        rope_rows(cos_i_ref, sin_i_ref, o32_ref, rot_i // 2, IDX_DIM, range(n_groups), pre_scale=IDX_DIM ** -0.5)

    @pl.when(kind == IDX_K32)
    def _():
        acc = product()
        acc_ref[...] = acc
        o32_ref[...] = acc
        rope_rows(cos_i_ref, sin_i_ref, o32_ref, rot_i // 2, IDX_DIM, range(1), limit=IDX_DIM)

    @pl.when(kind == PLAIN16)
    def _():
        o16_ref[...] = product().astype(BF16)

    @pl.when(kind == DSA_ROPE16)
    def _():
        acc_ref[...] = product()
        rope_rows(cos_d_ref, sin_d_ref, o16_ref, rot_d // 2, DSA_HEAD_DIM, range(n_groups))


def _in_proj_plan(d):
    tn = IN_PROJ_TN
    gla_qk, dsa_w, idx_w = d // 2, DSA_HEADS * DSA_HEAD_DIM, IDX_HEADS * IDX_DIM
    names = ("g_q", "g_k", "g_v", "g_r", "g_lr", "d_q", "d_k", "d_v", "i_q", "i_k", "i_w", "gate_a", "gate_b")
    widths = (gla_qk, gla_qk, d, d, GLA_GATE_RANK, dsa_w, dsa_w, dsa_w, idx_w, IDX_DIM, IDX_HEADS, d, d)
    start, pos = {}, 0
    for nm, wd in zip(names, widths):
        start[nm] = pos
        pos += wd
    width = dict(zip(names, widths))
    assert start["i_w"] == start["i_k"] + IDX_DIM and IDX_DIM + IDX_HEADS <= LANES
    f32_groups = (("g_q", GLA_Q32), ("g_k", PLAIN32), ("g_r", PLAIN32), ("g_lr", PLAIN32), ("i_q", IDX_Q32),
                  ("i_k", IDX_K32), ("gate_a", PLAIN32), ("gate_b", PLAIN32))
    bf16_groups = (("g_v", PLAIN16), ("d_q", DSA_ROPE16), ("d_k", DSA_ROPE16), ("d_v", PLAIN16))
    offsets, kinds, tile_of = [], [], {}
    for groups in (f32_groups, bf16_groups):
        base = len(offsets)
        for nm, kind in groups:
            tile_of[nm] = len(offsets) - base
            for t in range(-(-width[nm] // tn)):
                offsets.append(start[nm] + t * tn)
                kinds.append(kind)
        if groups is f32_groups:
            n32 = len(offsets)
    assert all(o % SUBLANES == 0 and o + tn <= pos for o in offsets)
    return offsets, kinds, n32, tile_of


def _in_project(x2d, gain, sc, sh, seq, w_in_t, tables, plan, tm=1024):
    t, d = x2d.shape
    tn = IN_PROJ_TN
    offsets, kinds, n32, _ = plan
    n16 = len(offsets) - n32
    tm = min(tm, seq)
    per_batch = seq // tm
    table_spec = pl.BlockSpec((tm, LANES), lambda i, j, off, kind: (i, 0))
    vec_spec = pl.BlockSpec((1, 1, d), lambda i, j, off, kind: (i // per_batch, 0, 0))
    grid_spec = pltpu.PrefetchScalarGridSpec(
        num_scalar_prefetch=2,
        grid=(t // tm, len(offsets)),
        in_specs=[pl.BlockSpec((tm, d), lambda i, j, off, kind: (i, 0)),
                  pl.BlockSpec((1, d), lambda i, j, off, kind: (0, 0)),
                  vec_spec, vec_spec,
                  pl.BlockSpec((pl.Element(tn), pl.Element(d)),
                               lambda i, j, off, kind: (pl.multiple_of(off[j], SUBLANES), 0)),
                  table_spec, table_spec, table_spec, table_spec],
        out_specs=[pl.BlockSpec((tm, tn), lambda i, j, off, kind: (i, jnp.minimum(j, n32 - 1))),
                   pl.BlockSpec((tm, tn), lambda i, j, off, kind: (i, jnp.maximum(j - n32, 0)))],
        scratch_shapes=[pltpu.VMEM((tm, d), BF16), pltpu.VMEM((tm, tn), F32)],
    )
    return pl.pallas_call(
        _in_proj_kernel,
        grid_spec=grid_spec,
        out_shape=[jax.ShapeDtypeStruct((t, n32 * tn), F32), jax.ShapeDtypeStruct((t, n16 * tn), BF16)],
        compiler_params=_params("parallel", "arbitrary"),
        name="norm1_in_proj",
    )(jnp.asarray(offsets, jnp.int32), jnp.asarray(kinds, jnp.int32), x2d, gain, sc, sh, w_in_t, *tables)


def _split3_bf16(x):
    hi, rest = _split_bf16(x)
    mid, lo = _split_bf16(rest)
    return hi, mid, lo


def _gla_kernel(q_ref, k_ref, v_ref, gr_ref, sm_ref, gup_ref, gb_ref, ng_ref, tri_ref, o_ref, st_ref):
    @pl.when(pl.program_id(2) == 0)
    def _():
        st_ref[...] = jnp.zeros_like(st_ref)

    rows = q_ref.shape[0]
    dk = q_ref.shape[1]
    c, sub = GLA_CHUNK, GLA_SUB
    n_sub = c // sub

    a_hi, a_lo = _split_bf16(sm_ref[:, 0:GLA_GATE_RANK])
    gate_lhs = jnp.concatenate(
        [a_hi, a_lo, a_hi, jnp.zeros((rows, LANES - 3 * GLA_GATE_RANK), F32)], axis=1).astype(BF16)
    z = jnp.dot(gate_lhs, gup_ref[...], preferred_element_type=F32) + gb_ref[...]
    log_g = (jnp.minimum(z, 0.0) - jnp.log(1.0 + jnp.exp(-jnp.abs(z)))) * (1.0 / GLA_TAU)
    parts = jnp.dot(tri_ref[...], jnp.concatenate(_split3_bf16(log_g), axis=1).astype(BF16),
                    preferred_element_type=F32)
    b = (parts[:, 0:dk] + parts[:, dk:2 * dk]) + parts[:, 2 * dk:3 * dk]

    q = q_ref[...]
    k = k_ref[...]

    k_tiles = k.reshape(rows // sub, sub, dk)
    b_tiles = b.reshape(rows // sub, sub, dk)
    row = lax.broadcasted_iota(jnp.int32, (rows, 1), 0)
    row_in_sub = row % sub
    col = lax.broadcasted_iota(jnp.int32, (rows, c), 1)
    target = lax.broadcasted_iota(jnp.int32, (rows, c), 0) % c
    a_diag = jnp.zeros((rows, c), F32)
    for delta in range(sub):
        if delta == 0:
            k_d, b_d = k, b
        else:
            k_d = pltpu.roll(k_tiles, delta, axis=1).reshape(rows, dk)
            b_d = pltpu.roll(b_tiles, delta, axis=1).reshape(rows, dk)
        decay = jnp.exp(jnp.where(row_in_sub >= delta, b - b_d, -jnp.inf))
        a = jnp.sum(q * k_d * decay, axis=-1, keepdims=True)
        a_diag = jnp.where(col == target - delta, a, a_diag)

    for ch in range(rows // c):
        base = ch * c
        qc = q[base:base + c]
        kc = k[base:base + c]
        bc = b[base:base + c]
        vc = v_ref[base:base + c, :]
        st = st_ref[...]

        o_inter = lax.dot_general((qc * jnp.exp(bc)).astype(BF16), st.astype(BF16),
                                  (((1,), (1,)), ((), ())), preferred_element_type=F32)

        blocks = [jnp.zeros((sub, c), F32)]
        for i_sub in range(1, n_sub):
            lo = i_sub * sub
            ref = bc[lo - 1:lo]
            q_hat = qc[lo:lo + sub] * jnp.exp(bc[lo:lo + sub] - ref)
            k_hat = kc[0:lo] * jnp.exp(ref - bc[0:lo])
            k_pad = jnp.concatenate([k_hat, jnp.zeros((c - lo, dk), F32)], axis=0)
            blocks.append(lax.dot_general(q_hat.astype(BF16), k_pad.astype(BF16),
                                          (((1,), (1,)), ((), ())), preferred_element_type=F32))
        attn = jnp.concatenate(blocks, axis=0) + a_diag[base:base + c]

        o = jnp.dot(attn.astype(BF16), vc, preferred_element_type=F32) + o_inter

        b_last = bc[c - 1:c]
        k_dec = (kc * jnp.exp(b_last - bc)).astype(BF16)
        st_ref[...] = st * jnp.exp(b_last) + lax.dot_general(
            vc, k_dec, (((0,), (0,)), ((), ())), preferred_element_type=F32)

        o = o * lax.rsqrt(jnp.mean(o * o, axis=-1, keepdims=True) + RMS_EPS) * ng_ref[...]
        o_ref[base:base + c, :] = (o * _silu(gr_ref[base:base + c, :])).astype(o_ref.dtype)


def _gla(act32, act16, tile_of, gate_up, gbias, ngain, batch, seq, rows=256):
    dk = gate_up.shape[1] // GLA_HEADS
    dv = ngain.shape[1] // GLA_HEADS
    t = act32.shape[0]
    nr = seq // rows
    g_hi, g_lo = _split_bf16(gate_up)
    gup_stack = jnp.concatenate(
        [g_hi, g_hi, g_lo, jnp.zeros((LANES - 3 * GLA_GATE_RANK, gate_up.shape[1]), F32)], axis=0).astype(BF16)
    idx = jnp.arange(rows)
    tri = jnp.logical_and(idx[:, None] // GLA_CHUNK == idx[None, :] // GLA_CHUNK,
                          idx[None, :] <= idx[:, None]).astype(BF16)

    def cols(group, width):
        first = tile_of[group] * IN_PROJ_TN // width
        return lambda b, h, r: (b * nr + r, first + h)

    lr_block = tile_of["g_lr"] * IN_PROJ_TN // LANES
    return pl.pallas_call(
        _gla_kernel,
        grid=(batch, GLA_HEADS, nr),
        in_specs=[pl.BlockSpec((rows, dk), cols("g_q", dk)),
                  pl.BlockSpec((rows, dk), cols("g_k", dk)),
                  pl.BlockSpec((rows, dv), cols("g_v", dv)),
                  pl.BlockSpec((rows, dv), cols("g_r", dv)),
                  pl.BlockSpec((rows, LANES), lambda b, h, r: (b * nr + r, lr_block)),
                  pl.BlockSpec((LANES, dk), lambda b, h, r: (0, h)),
                  pl.BlockSpec((1, dk), lambda b, h, r: (0, h)),
                  pl.BlockSpec((1, dv), lambda b, h, r: (0, h)),
                  pl.BlockSpec((rows, rows), lambda b, h, r: (0, 0))],
        out_specs=pl.BlockSpec((rows, dv), lambda b, h, r: (b * nr + r, h)),
        out_shape=jax.ShapeDtypeStruct((t, GLA_HEADS * dv), BF16),
        scratch_shapes=[pltpu.VMEM((dv, dk), F32)],
        compiler_params=_params("parallel", "parallel", "arbitrary"),
        name="gla",
    )(act32, act32, act16, act32, act32, gup_stack, gbias, ngain, tri)


def _split_bf16(x):
    hi = x.astype(BF16).astype(F32)
    return hi, x - hi


def _sortable_bits_to_float(key):
    return lax.bitcast_convert_type(jnp.where(key < 0, key ^ jnp.int32(0x7FFFFFFF), key), F32)


SORT_KEY_NEG_INF = -2139095041


def _topk_bias_kernel(iq_ref, ik_ref, iw_ref, tri_ref, o_ref, score_ref, rhs_ref, *, topk, tq):
    qi = pl.program_id(1)
    tk = tq
    nk = ik_ref.shape[0] // tk
    n_live = qi + 1
    w = iw_ref[0] * (IDX_HEADS ** -0.5)
    pad_q = jnp.zeros((tq, IDX_DIM), F32)
    for h in range(IDX_HEADS):
        hi, lo = _split_bf16(iq_ref[:, h * IDX_DIM:(h + 1) * IDX_DIM])
        rhs_ref[h] = jnp.concatenate([hi, hi, lo, pad_q], axis=1).astype(BF16)
    t_pos = qi * tq + lax.broadcasted_iota(jnp.int32, (tk, tq), 1)
    s_off = lax.broadcasted_iota(jnp.int32, (tk, tq), 0)
    pad_k = jnp.zeros((tk, IDX_DIM), F32)

    def block_rows(kb):
        return pl.ds(pl.multiple_of(kb * tk, tk), tk)

    def score_block(kb, carry):
        hi, lo = _split_bf16(ik_ref[block_rows(kb), 0:IDX_DIM])
        lhs = jnp.concatenate([hi, lo, hi, pad_k], axis=1).astype(BF16)
        score = jnp.zeros((tk, tq), F32)
        for h in range(IDX_HEADS):
            dots = lax.dot_general(lhs, rhs_ref[h], (((1,), (1,)), ((), ())), preferred_element_type=F32)
            score = score + w[h:h + 1, :] * jnp.maximum(dots, 0.0)
        score_ref[block_rows(kb), :] = jnp.where(kb * tk + s_off <= t_pos, score + 0.0, -jnp.inf)
        return carry

    lax.fori_loop(0, n_live, score_block, 0)

    def count(pred):
        def blk(kb, part):
            hit = jnp.where(pred(score_ref[block_rows(kb), :]), 1.0, 0.0)
            return part + jnp.sum(hit.reshape(tk // SUBLANES, SUBLANES, tq), axis=0)
        part = lax.fori_loop(0, n_live, blk, jnp.zeros((SUBLANES, tq), F32))
        return jnp.sum(part, axis=0, keepdims=True)

    def search(i, ans):
        cand = ans ^ jnp.left_shift(jnp.int32(1), 31 - i)
        cand_f = _sortable_bits_to_float(cand)
        cnt = count(lambda s: s >= cand_f)
        accept = jnp.logical_or(cnt >= float(topk), cand < SORT_KEY_NEG_INF)
        return jnp.where(accept, cand, ans)

    thr = _sortable_bits_to_float(
        lax.fori_loop(0, 32, search, jnp.full((1, tq), jnp.iinfo(jnp.int32).min, jnp.int32)))
    need = float(topk) - count(lambda s: s > thr)

    def emit(kb, seen):
        blk = score_ref[block_rows(kb), :]
        eq = jnp.where(blk == thr, 1.0, 0.0)
        rank = jnp.dot(tri_ref[...], eq.astype(BF16), preferred_element_type=F32) + seen
        chosen = jnp.logical_or(blk > thr, jnp.logical_and(blk == thr, rank <= need))
        o_ref[0, block_rows(kb), :] = jnp.where(
            jnp.logical_and(chosen, kb * tk + s_off <= t_pos), 0.0, -jnp.inf)
        return seen + jnp.sum(eq, axis=0, keepdims=True)

    lax.fori_loop(0, n_live, emit, jnp.zeros((1, tq), F32))

    def fill(kb, carry):
        o_ref[0, block_rows(kb), :] = jnp.full((tk, tq), -jnp.inf, F32)
        return carry

    lax.fori_loop(n_live, nk, fill, 0)


def _topk_bias(act32, tile_of, iw_t, batch, seq, topk, tq=256):
    tq = min(tq, seq)
    nq = seq // tq
    width = IDX_HEADS * IDX_DIM
    iq_block = tile_of["i_q"] * IN_PROJ_TN // width
    ik_block = tile_of["i_k"] * IN_PROJ_TN // LANES
    tri = jnp.tril(jnp.ones((tq, tq), BF16))
    return pl.pallas_call(
        functools.partial(_topk_bias_kernel, topk=topk, tq=tq),
        grid=(batch, nq),
        in_specs=[pl.BlockSpec((tq, width), lambda b, i: (b * nq + i, iq_block)),
                  pl.BlockSpec((seq, LANES), lambda b, i: (b, ik_block)),
                  pl.BlockSpec((1, IDX_HEADS, tq), lambda b, i: (b, 0, i)),
                  pl.BlockSpec((tq, tq), lambda b, i: (0, 0))],
        out_specs=pl.BlockSpec((1, seq, tq), lambda b, i: (b, 0, i)),
        out_shape=jax.ShapeDtypeStruct((batch, seq, seq), F32),
        scratch_shapes=[pltpu.VMEM((seq, tq), F32),
                        pltpu.VMEM((IDX_HEADS, tq, 4 * IDX_DIM), BF16)],
        compiler_params=_params("parallel", "parallel"),
        name="indexer_topk_bias",
    )(act32, act32, iw_t, tri)


ATTN_HEADS_PER_STEP = 8


def _attn_kernel(q_ref, k_ref, v_ref, bias_ref, o_ref, acc_ref):
    qi = pl.program_id(2)
    tq = q_ref.shape[0]
    tk = tq
    dh = DSA_HEAD_DIM
    group = q_ref.shape[1] // dh
    scale2 = (dh ** -0.5) * LOG2_E
    acc_ref[...] = jnp.zeros_like(acc_ref)

    def body(kj, carry):
        rows = pl.ds(pl.multiple_of(kj * tk, tk), tk)
        bias = bias_ref[0, rows, :]
        heads = [slice(g * dh, (g + 1) * dh) for g in range(group)]
        logits = [lax.dot_general(k_ref[rows, cols], q_ref[:, cols], (((1,), (1,)), ((), ())),
                                  preferred_element_type=F32) for cols in heads]
        new = []
        for g in range(group):
            m, l = carry[g]
            s = logits[g] * scale2 + bias
            m_new = jnp.maximum(m, jnp.max(s, axis=0, keepdims=True))
            m_safe = jnp.where(m_new == -jnp.inf, 0.0, m_new)
            alpha = jnp.exp2(m - m_safe)
            p = jnp.exp2(s - m_safe)
            new.append((m_new, alpha * l + jnp.sum(p, axis=0, keepdims=True)))
            update = lax.dot_general(v_ref[rows, heads[g]], p.astype(BF16), (((0,), (0,)), ((), ())),
                                     preferred_element_type=F32)
            acc_ref[g] = alpha * acc_ref[g] + update
        return tuple(new)

    init = tuple((jnp.full((1, tq), -jnp.inf, F32), jnp.zeros((1, tq), F32)) for _ in range(group))
    final = lax.fori_loop(0, qi + 1, body, init)
    for g in range(group):
        o_ref[:, g * dh:(g + 1) * dh] = (acc_ref[g] / final[g][1]).T.astype(o_ref.dtype)


def _attention(act16, tile_of, bias_t, batch, seq, tq=256):
    tq = min(tq, seq)
    nq = seq // tq
    width = ATTN_HEADS_PER_STEP * DSA_HEAD_DIM
    n_groups = DSA_HEADS // ATTN_HEADS_PER_STEP
    t = act16.shape[0]
    q_block, k_block, v_block = (tile_of[nm] * IN_PROJ_TN // width for nm in ("d_q", "d_k", "d_v"))
    return pl.pallas_call(
        _attn_kernel,
        grid=(batch, n_groups, nq),
        in_specs=[pl.BlockSpec((tq, width), lambda b, h, i: (b * nq + i, q_block + h)),
                  pl.BlockSpec((seq, width), lambda b, h, i: (b, k_block + h)),
                  pl.BlockSpec((seq, width), lambda b, h, i: (b, v_block + h)),
                  pl.BlockSpec((1, seq, tq), lambda b, h, i: (b, 0, i))],
        out_specs=pl.BlockSpec((tq, width), lambda b, h, i: (b * nq + i, h)),
        out_shape=jax.ShapeDtypeStruct((t, DSA_HEADS * DSA_HEAD_DIM), BF16),
        scratch_shapes=[pltpu.VMEM((ATTN_HEADS_PER_STEP, DSA_HEAD_DIM, tq), F32)],
        compiler_params=_params("parallel", "parallel", "arbitrary"),
        name="dsa_attention",
    )(act16, act16, act16, bias_t)


def _merge_kernel(oa_ref, ob_ref, wa_ref, wb_ref, ga_ref, gb_ref, o_ref):
    ya = jnp.dot(oa_ref[...], wa_ref[...].astype(BF16), preferred_element_type=F32)
    yb = jnp.dot(ob_ref[...], wb_ref[...].astype(BF16), preferred_element_type=F32)
    o_ref[...] = (jax.nn.sigmoid(ga_ref[...]) * ya + jax.nn.sigmoid(gb_ref[...]) * yb).astype(o_ref.dtype)


def _merge(o_a, o_b, w_a, w_b, act32, tile_of, tm=1024):
    t, d = o_a.shape
    n = w_a.shape[1]
    tn = IN_PROJ_TN
    tm = min(tm, t)
    ga_tile, gb_tile = tile_of["gate_a"], tile_of["gate_b"]
    return pl.pallas_call(
        _merge_kernel,
        grid=(t // tm, n // tn),
        in_specs=[pl.BlockSpec((tm, d), lambda i, j: (i, 0)),
                  pl.BlockSpec((tm, d), lambda i, j: (i, 0)),
                  pl.BlockSpec((d, tn), lambda i, j: (0, j)),
                  pl.BlockSpec((d, tn), lambda i, j: (0, j)),
                  pl.BlockSpec((tm, tn), lambda i, j: (i, ga_tile + j)),
                  pl.BlockSpec((tm, tn), lambda i, j: (i, gb_tile + j))],
        out_specs=pl.BlockSpec((tm, tn), lambda i, j: (i, j)),
        out_shape=jax.ShapeDtypeStruct((t, n), BF16),
        compiler_params=_params("parallel", "parallel"),
        name="branch_merge",
    )(o_a, o_b, w_a, w_b, act32, act32)


def _mixer_out_kernel(m_ref, w_ref, x_ref, g_ref, gain_ref, sc_ref, sh_ref, x1_ref, h2_ref):
    y = jnp.dot(m_ref[...], w_ref[...], preferred_element_type=F32)
    x1 = x_ref[...] + g_ref[0] * y
    x1_ref[...] = x1
    h2_ref[...] = _rms_modulate(x1, gain_ref[...], sc_ref[0], sh_ref[0]).astype(h2_ref.dtype)


def _mixer_out(merged, w, x2d, g1, gain2, sc2, sh2, seq, tm=256):
    t, d = x2d.shape
    tm = min(tm, seq)
    per_batch = seq // tm
    vec = pl.BlockSpec((1, 1, d), lambda i: (i // per_batch, 0, 0))
    row = pl.BlockSpec((tm, d), lambda i: (i, 0))
    return pl.pallas_call(
        _mixer_out_kernel,
        grid=(t // tm,),
        in_specs=[row, pl.BlockSpec((d, d), lambda i: (0, 0)), row, vec,
                  pl.BlockSpec((1, d), lambda i: (0, 0)), vec, vec],
        out_specs=[row, row],
        out_shape=[jax.ShapeDtypeStruct((t, d), F32), jax.ShapeDtypeStruct((t, d), BF16)],
        compiler_params=_params("parallel"),
        name="mixer_out_norm2",
    )(merged, w, x2d, g1, gain2, sc2, sh2)


def _ffn_up_kernel(h_ref, wg_ref, wu_ref, o_ref):
    g = jnp.dot(h_ref[...], wg_ref[...].astype(BF16), preferred_element_type=F32)
    u = jnp.dot(h_ref[...], wu_ref[...].astype(BF16), preferred_element_type=F32)
    o_ref[...] = (_silu(g) * u).astype(o_ref.dtype)


def _ffn_up(h2, w_gate_up, tm=1024, tn=512):
    t, d = h2.shape
    d_ff = w_gate_up.shape[1] // 2
    tm = min(tm, t)
    nj = d_ff // tn
    return pl.pallas_call(
        _ffn_up_kernel,
        grid=(t // tm, nj),
        in_specs=[pl.BlockSpec((tm, d), lambda i, j: (i, 0)),
                  pl.BlockSpec((d, tn), lambda i, j: (0, j)),
                  pl.BlockSpec((d, tn), lambda i, j: (0, nj + j))],
        out_specs=pl.BlockSpec((tm, tn), lambda i, j: (i, j)),
        out_shape=jax.ShapeDtypeStruct((t, d_ff), BF16),
        compiler_params=_params("parallel", "parallel"),
        name="ffn_up",
    )(h2, w_gate_up, w_gate_up)


def _ffn_down_kernel(a_ref, w_ref, x_ref, g_ref, gain_ref, o_ref, acc_ref):
    kk = pl.program_id(1)

    @pl.when(kk == 0)
    def _():
        acc_ref[...] = jnp.zeros_like(acc_ref)

    acc_ref[...] += jnp.dot(a_ref[...], w_ref[...], preferred_element_type=F32)

    @pl.when(kk == pl.num_programs(1) - 1)
    def _():
        x2 = x_ref[...] + g_ref[0] * acc_ref[...]
        y = x2 * lax.rsqrt(jnp.mean(x2 * x2, axis=-1, keepdims=True) + RMS_EPS)
        o_ref[...] = y * gain_ref[...]


def _ffn_down(act, w_down, x1, g2, final_gain, seq, tm=512, tk=1408):
    t, d_ff = act.shape
    d = w_down.shape[1]
    tm = min(tm, seq)
    per_batch = seq // tm
    return pl.pallas_call(
        _ffn_down_kernel,
        grid=(t // tm, d_ff // tk),
        in_specs=[pl.BlockSpec((tm, tk), lambda i, k: (i, k)),
                  pl.BlockSpec((tk, d), lambda i, k: (k, 0)),
                  pl.BlockSpec((tm, d), lambda i, k: (i, 0)),
                  pl.BlockSpec((1, 1, d), lambda i, k: (i // per_batch, 0, 0)),
                  pl.BlockSpec((1, d), lambda i, k: (0, 0))],
        out_specs=pl.BlockSpec((tm, d), lambda i, k: (i, 0)),
        out_shape=jax.ShapeDtypeStruct((t, d), F32),
        scratch_shapes=[pltpu.VMEM((tm, d), F32)],
        compiler_params=_params("parallel", "arbitrary"),
        name="ffn_down_final_norm",
    )(act, w_down, x1, g2, final_gain)


def _rope_tables(positions_flat, rot, period):
    half = rot // 2
    inv_freq = jnp.power(ROPE_THETA, -jnp.arange(0, rot, 2, dtype=F32) / rot)
    ang = positions_flat.astype(F32)[:, None] * inv_freq[None, :]
    lane_freq = (jnp.arange(LANES) % period) % half
    return jnp.cos(ang)[:, lane_freq], jnp.sin(ang)[:, lane_freq]


def _layer(x2d, mod, positions_flat, batch, seq, norm1_gain, norm2_gain, w_in, gla_gate_up,
           gla_gate_bias, gla_norm_gain, w_branch_gla, w_branch_dsa, w_merge_out, w_ffn_gate_up,
           w_ffn_down, final_gain):
    d = x2d.shape[1]
    sh1, sc1, g1, sh2, sc2, g2 = [mod[:, i * d:(i + 1) * d][:, None, :] for i in range(N_MOD)]

    plan = _in_proj_plan(d)
    tile_of = plan[3]
    cos_d, sin_d = _rope_tables(positions_flat, DSA_HEAD_DIM // ROPE_FRACTION, DSA_HEAD_DIM)
    cos_i, sin_i = _rope_tables(positions_flat, IDX_DIM // ROPE_FRACTION, IDX_DIM)
    act32, act16 = _in_project(x2d, norm1_gain[None, :], sc1, sh1, seq, w_in.T,
                               (cos_d, sin_d, cos_i, sin_i), plan)

    o_a = _gla(act32, act16, tile_of, gla_gate_up, gla_gate_bias[None, :], gla_norm_gain[None, :],
               batch, seq, rows=min(256, seq))

    iw_col = tile_of["i_k"] * IN_PROJ_TN + IDX_DIM
    iw_t = jnp.transpose(act32[:, iw_col:iw_col + IDX_HEADS].reshape(batch, seq, IDX_HEADS), (0, 2, 1))
    topk = min(IDX_TOPK_MAX, seq // 4)
    bias_t = _topk_bias(act32, tile_of, iw_t, batch, seq, topk)
    o_b = _attention(act16, tile_of, bias_t, batch, seq)

    merged = _merge(o_a, o_b, w_branch_gla, w_branch_dsa, act32, tile_of)
    x1, h2 = _mixer_out(merged, w_merge_out.astype(BF16), x2d, g1, norm2_gain[None, :], sc2, sh2, seq)
    act = _ffn_up(h2, w_ffn_gate_up)
    return _ffn_down(act, w_ffn_down.astype(BF16), x1, g2, final_gain[None, :], seq)


def kernel(x, c, positions, norm1_gain, norm2_gain, w_ada, b_ada, w_in, gla_gate_up, gla_gate_bias,
           gla_norm_gain, w_branch_gla, w_branch_dsa, w_merge_out, w_ffn_gate_up, w_ffn_down,
           final_norm_gain):
    batch, seq, d = x.shape
    depth = w_in.shape[0]
    assert depth == 1, "the final RMSNorm is fused into the single layer's FFN kernel"
    x2d = x.reshape(batch * seq, d)
    c_pad = jnp.zeros((SUBLANES, d), F32).at[:batch].set(c)
    mod = _modulation(c_pad, w_ada[0], b_ada[0][None, :])[:batch]
    out = _layer(x2d, mod, positions.reshape(-1), batch, seq, norm1_gain[0], norm2_gain[0], w_in[0],
                 gla_gate_up[0], gla_gate_bias[0], gla_norm_gain[0], w_branch_gla[0], w_branch_dsa[0],
                 w_merge_out[0], w_ffn_gate_up[0], w_ffn_down[0], final_norm_gain)
    return out.reshape(batch, seq, d)
```

```python
import functools

import jax
import jax.numpy as jnp
from jax import lax
from jax.experimental import pallas as pl
from jax.experimental.pallas import tpu as pltpu

F32 = jnp.float32
BF16 = jnp.bfloat16
HIGHEST = lax.Precision.HIGHEST

RMS_EPS = 1e-6
GLA_HEADS = 4
GLA_GATE_RANK = 16
GLA_TAU = 16.0
GLA_CHUNK = 64
GLA_SUB = 8
DSA_HEADS = 16
DSA_HEAD_DIM = 128
IDX_HEADS = 8
IDX_DIM = 64
IDX_TOPK_MAX = 256
ROPE_THETA = 500000.0
ROPE_FRACTION = 4
N_MOD = 6
LOG2_E = 1.4426950408889634

LANES = 128
SUBLANES = 8
VMEM_LIMIT = 48 * 1024 * 1024


def _params(*semantics):
    return pltpu.CompilerParams(dimension_semantics=semantics, vmem_limit_bytes=VMEM_LIMIT)


def _silu(x):
    return x * jax.nn.sigmoid(x)


def _mod_kernel(c_ref, w_ref, b_ref, o_ref):
    a = _silu(c_ref[...])
    o_ref[...] = jnp.dot(a, w_ref[...], precision=HIGHEST, preferred_element_type=F32) + b_ref[...]


def _modulation(c_pad, w_ada, b_ada, tn=2048):
    rows, d = c_pad.shape
    n = w_ada.shape[1]
    return pl.pallas_call(
        _mod_kernel,
        grid=(n // tn,),
        in_specs=[pl.BlockSpec((rows, d), lambda j: (0, 0)),
                  pl.BlockSpec((d, tn), lambda j: (0, j)),
                  pl.BlockSpec((1, tn), lambda j: (0, j))],
        out_specs=pl.BlockSpec((rows, tn), lambda j: (0, j)),
        out_shape=jax.ShapeDtypeStruct((rows, n), F32),
        compiler_params=_params("parallel"),
        name="adaln_mod",
    )(c_pad, w_ada, b_ada)


def _rms_modulate(x, gain, scale, shift):
    y = x * lax.rsqrt(jnp.mean(x * x, axis=-1, keepdims=True) + RMS_EPS)
    return (y * gain) * (1.0 + scale) + shift


def _rope_lanes(x, cos, sin, half, period, limit=LANES):
    lane = lax.broadcasted_iota(jnp.int32, x.shape, 1)
    in_head = lane % period
    upper = pltpu.roll(x, LANES - half, axis=1)
    lower = pltpu.roll(x, half, axis=1)
    live = lane < limit
    first = jnp.logical_and(live, in_head < half)
    second = jnp.logical_and(live, jnp.logical_and(in_head >= half, in_head < 2 * half))
    return jnp.where(first, x * cos - upper * sin,
                     jnp.where(second, lower * sin + x * cos, x))


PLAIN32, GLA_Q32, IDX_Q32, IDX_K32, PLAIN16, DSA_ROPE16 = range(6)
IN_PROJ_ROPE_ROWS = 256
IN_PROJ_TN = 512


def _in_proj_kernel(off_ref, kind_ref, x_ref, gain_ref, sc_ref, sh_ref, wt_ref, cos_d_ref, sin_d_ref,
                    cos_i_ref, sin_i_ref, o32_ref, o16_ref, h_ref, acc_ref):
    kind = kind_ref[pl.program_id(1)]
    tm, tn = o32_ref.shape
    n_groups = tn // LANES
    rot_d = DSA_HEAD_DIM // ROPE_FRACTION
    rot_i = IDX_DIM // ROPE_FRACTION

    @pl.when(pl.program_id(1) == 0)
    def _():
        h_ref[...] = _rms_modulate(x_ref[...], gain_ref[...], sc_ref[0], sh_ref[0]).astype(BF16)

    def product():
        return lax.dot_general(h_ref[...], wt_ref[...].astype(BF16), (((1,), (1,)), ((), ())),
                               preferred_element_type=F32)

    def rope_rows(cos_ref, sin_ref, out_ref, half, period, groups, pre_scale=1.0, limit=LANES):
        def chunk(r, carry):
            rows = pl.ds(pl.multiple_of(r * IN_PROJ_ROPE_ROWS, IN_PROJ_ROPE_ROWS), IN_PROJ_ROPE_ROWS)
            cos = cos_ref[rows, :]
            sin = sin_ref[rows, :]
            for g in groups:
                cols = slice(g * LANES, (g + 1) * LANES)
                x = acc_ref[rows, cols]
                if pre_scale != 1.0:
                    x = x * pre_scale
                out_ref[rows, cols] = _rope_lanes(x, cos, sin, half, period, limit).astype(out_ref.dtype)
            return carry
        lax.fori_loop(0, tm // IN_PROJ_ROPE_ROWS, chunk, 0)

    @pl.when(kind == PLAIN32)
    def _():
        o32_ref[...] = product()

    @pl.when(kind == GLA_Q32)
    def _():
        o32_ref[...] = product() * ((wt_ref.shape[1] // 2 // GLA_HEADS) ** -0.5)

    @pl.when(kind == IDX_Q32)
    def _():
        acc_ref[...] = product()
        rope_rows(cos_i_ref, sin_i_ref, o32_ref, rot_i // 2, IDX_DIM, range(n_groups), pre_scale=IDX_DIM ** -0.5)

    @pl.when(kind == IDX_K32)
    def _():
        acc = product()
        acc_ref[...] = acc
        o32_ref[...] = acc
        rope_rows(cos_i_ref, sin_i_ref, o32_ref, rot_i // 2, IDX_DIM, range(1), limit=IDX_DIM)

    @pl.when(kind == PLAIN16)
    def _():
        o16_ref[...] = product().astype(BF16)

    @pl.when(kind == DSA_ROPE16)
    def _():
        acc_ref[...] = product()
        rope_rows(cos_d_ref, sin_d_ref, o16_ref, rot_d // 2, DSA_HEAD_DIM, range(n_groups))


def _in_proj_plan(d):
    tn = IN_PROJ_TN
    gla_qk, dsa_w, idx_w = d // 2, DSA_HEADS * DSA_HEAD_DIM, IDX_HEADS * IDX_DIM
    names = ("g_q", "g_k", "g_v", "g_r", "g_lr", "d_q", "d_k", "d_v", "i_q", "i_k", "i_w", "gate_a", "gate_b")
    widths = (gla_qk, gla_qk, d, d, GLA_GATE_RANK, dsa_w, dsa_w, dsa_w, idx_w, IDX_DIM, IDX_HEADS, d, d)
    start, pos = {}, 0
    for nm, wd in zip(names, widths):
        start[nm] = pos
        pos += wd
    width = dict(zip(names, widths))
    assert start["i_w"] == start["i_k"] + IDX_DIM and IDX_DIM + IDX_HEADS <= LANES
    f32_groups = (("g_q", GLA_Q32), ("g_k", PLAIN32), ("g_r", PLAIN32), ("g_lr", PLAIN32), ("i_q", IDX_Q32),
                  ("i_k", IDX_K32), ("gate_a", PLAIN32), ("gate_b", PLAIN32))
    bf16_groups = (("g_v", PLAIN16), ("d_q", DSA_ROPE16), ("d_k", DSA_ROPE16), ("d_v", PLAIN16))
    offsets, kinds, tile_of = [], [], {}
    for groups in (f32_groups, bf16_groups):
        base = len(offsets)
        for nm, kind in groups:
            tile_of[nm] = len(offsets) - base
            for t in range(-(-width[nm] // tn)):
                offsets.append(start[nm] + t * tn)
                kinds.append(kind)
        if groups is f32_groups:
            n32 = len(offsets)
    assert all(o % SUBLANES == 0 and o + tn <= pos for o in offsets)
    return offsets, kinds, n32, tile_of


def _in_project(x2d, gain, sc, sh, seq, w_in_t, tables, plan, tm=1024):
    t, d = x2d.shape
    tn = IN_PROJ_TN
    offsets, kinds, n32, _ = plan
    n16 = len(offsets) - n32
    tm = min(tm, seq)
    per_batch = seq // tm
    table_spec = pl.BlockSpec((tm, LANES), lambda i, j, off, kind: (i, 0))
    vec_spec = pl.BlockSpec((1, 1, d), lambda i, j, off, kind: (i // per_batch, 0, 0))
    grid_spec = pltpu.PrefetchScalarGridSpec(
        num_scalar_prefetch=2,
        grid=(t // tm, len(offsets)),
        in_specs=[pl.BlockSpec((tm, d), lambda i, j, off, kind: (i, 0)),
                  pl.BlockSpec((1, d), lambda i, j, off, kind: (0, 0)),
                  vec_spec, vec_spec,
                  pl.BlockSpec((pl.Element(tn), pl.Element(d)),
                               lambda i, j, off, kind: (pl.multiple_of(off[j], SUBLANES), 0)),
                  table_spec, table_spec, table_spec, table_spec],
        out_specs=[pl.BlockSpec((tm, tn), lambda i, j, off, kind: (i, jnp.minimum(j, n32 - 1))),
                   pl.BlockSpec((tm, tn), lambda i, j, off, kind: (i, jnp.maximum(j - n32, 0)))],
        scratch_shapes=[pltpu.VMEM((tm, d), BF16), pltpu.VMEM((tm, tn), F32)],
    )
    return pl.pallas_call(
        _in_proj_kernel,
        grid_spec=grid_spec,
        out_shape=[jax.ShapeDtypeStruct((t, n32 * tn), F32), jax.ShapeDtypeStruct((t, n16 * tn), BF16)],
        compiler_params=_params("parallel", "arbitrary"),
        name="norm1_in_proj",
    )(jnp.asarray(offsets, jnp.int32), jnp.asarray(kinds, jnp.int32), x2d, gain, sc, sh, w_in_t, *tables)


def _split3_bf16(x):
    hi, rest = _split_bf16(x)
    mid, lo = _split_bf16(rest)
    return hi, mid, lo


def _gla_kernel(q_ref, k_ref, v_ref, gr_ref, sm_ref, gup_ref, gb_ref, ng_ref, tri_ref, o_ref, st_ref):
    @pl.when(pl.program_id(2) == 0)
    def _():
        st_ref[...] = jnp.zeros_like(st_ref)

    rows = q_ref.shape[0]
    dk = q_ref.shape[1]
    c, sub = GLA_CHUNK, GLA_SUB
    n_sub = c // sub

    a_hi, a_lo = _split_bf16(sm_ref[:, 0:GLA_GATE_RANK])
    gate_lhs = jnp.concatenate(
        [a_hi, a_lo, a_hi, jnp.zeros((rows, LANES - 3 * GLA_GATE_RANK), F32)], axis=1).astype(BF16)
    z = jnp.dot(gate_lhs, gup_ref[...], preferred_element_type=F32) + gb_ref[...]
    log_g = (jnp.minimum(z, 0.0) - jnp.log(1.0 + jnp.exp(-jnp.abs(z)))) * (1.0 / GLA_TAU)
    parts = jnp.dot(tri_ref[...], jnp.concatenate(_split3_bf16(log_g), axis=1).astype(BF16),
                    preferred_element_type=F32)
    b = (parts[:, 0:dk] + parts[:, dk:2 * dk]) + parts[:, 2 * dk:3 * dk]

    q = q_ref[...]
    k = k_ref[...]

    k_tiles = k.reshape(rows // sub, sub, dk)
    b_tiles = b.reshape(rows // sub, sub, dk)
    row = lax.broadcasted_iota(jnp.int32, (rows, 1), 0)
    row_in_sub = row % sub
    col = lax.broadcasted_iota(jnp.int32, (rows, c), 1)
    target = lax.broadcasted_iota(jnp.int32, (rows, c), 0) % c
    a_diag = jnp.zeros((rows, c), F32)
    for delta in range(sub):
        if delta == 0:
            k_d, b_d = k, b
        else:
            k_d = pltpu.roll(k_tiles, delta, axis=1).reshape(rows, dk)
            b_d = pltpu.roll(b_tiles, delta, axis=1).reshape(rows, dk)
        decay = jnp.exp(jnp.where(row_in_sub >= delta, b - b_d, -jnp.inf))
        a = jnp.sum(q * k_d * decay, axis=-1, keepdims=True)
        a_diag = jnp.where(col == target - delta, a, a_diag)

    for ch in range(rows // c):
        base = ch * c
        qc = q[base:base + c]
        kc = k[base:base + c]
        bc = b[base:base + c]
        vc = v_ref[base:base + c, :]
        st = st_ref[...]

        o_inter = lax.dot_general((qc * jnp.exp(bc)).astype(BF16), st.astype(BF16),
                                  (((1,), (1,)), ((), ())), preferred_element_type=F32)

        blocks = [jnp.zeros((sub, c), F32)]
        for i_sub in range(1, n_sub):
            lo = i_sub * sub
            ref = bc[lo - 1:lo]
            q_hat = qc[lo:lo + sub] * jnp.exp(bc[lo:lo + sub] - ref)
            k_hat = kc[0:lo] * jnp.exp(ref - bc[0:lo])
            k_pad = jnp.concatenate([k_hat, jnp.zeros((c - lo, dk), F32)], axis=0)
            blocks.append(lax.dot_general(q_hat.astype(BF16), k_pad.astype(BF16),
                                          (((1,), (1,)), ((), ())), preferred_element_type=F32))
        attn = jnp.concatenate(blocks, axis=0) + a_diag[base:base + c]

        o = jnp.dot(attn.astype(BF16), vc, preferred_element_type=F32) + o_inter

        b_last = bc[c - 1:c]
        k_dec = (kc * jnp.exp(b_last - bc)).astype(BF16)
        st_ref[...] = st * jnp.exp(b_last) + lax.dot_general(
            vc, k_dec, (((0,), (0,)), ((), ())), preferred_element_type=F32)

        o = o * lax.rsqrt(jnp.mean(o * o, axis=-1, keepdims=True) + RMS_EPS) * ng_ref[...]
        o_ref[base:base + c, :] = (o * _silu(gr_ref[base:base + c, :])).astype(o_ref.dtype)


def _gla(act32, act16, tile_of, gate_up, gbias, ngain, batch, seq, rows=256):
    dk = gate_up.shape[1] // GLA_HEADS
    dv = ngain.shape[1] // GLA_HEADS
    t = act32.shape[0]
    nr = seq // rows
    g_hi, g_lo = _split_bf16(gate_up)
    gup_stack = jnp.concatenate(
        [g_hi, g_hi, g_lo, jnp.zeros((LANES - 3 * GLA_GATE_RANK, gate_up.shape[1]), F32)], axis=0).astype(BF16)
    idx = jnp.arange(rows)
    tri = jnp.logical_and(idx[:, None] // GLA_CHUNK == idx[None, :] // GLA_CHUNK,
                          idx[None, :] <= idx[:, None]).astype(BF16)

    def cols(group, width):
        first = tile_of[group] * IN_PROJ_TN // width
        return lambda b, h, r: (b * nr + r, first + h)

    lr_block = tile_of["g_lr"] * IN_PROJ_TN // LANES
    return pl.pallas_call(
        _gla_kernel,
        grid=(batch, GLA_HEADS, nr),
        in_specs=[pl.BlockSpec((rows, dk), cols("g_q", dk)),
                  pl.BlockSpec((rows, dk), cols("g_k", dk)),
                  pl.BlockSpec((rows, dv), cols("g_v", dv)),
                  pl.BlockSpec((rows, dv), cols("g_r", dv)),
                  pl.BlockSpec((rows, LANES), lambda b, h, r: (b * nr + r, lr_block)),
                  pl.BlockSpec((LANES, dk), lambda b, h, r: (0, h)),
                  pl.BlockSpec((1, dk), lambda b, h, r: (0, h)),
                  pl.BlockSpec((1, dv), lambda b, h, r: (0, h)),
                  pl.BlockSpec((rows, rows), lambda b, h, r: (0, 0))],
        out_specs=pl.BlockSpec((rows, dv), lambda b, h, r: (b * nr + r, h)),
        out_shape=jax.ShapeDtypeStruct((t, GLA_HEADS * dv), BF16),
        scratch_shapes=[pltpu.VMEM((dv, dk), F32)],
        compiler_params=_params("parallel", "parallel", "arbitrary"),
        name="gla",
    )(act32, act32, act16, act32, act32, gup_stack, gbias, ngain, tri)


def _split_bf16(x):
    hi = x.astype(BF16).astype(F32)
    return hi, x - hi


def _sortable_bits_to_float(key):
    return lax.bitcast_convert_type(jnp.where(key < 0, key ^ jnp.int32(0x7FFFFFFF), key), F32)


SORT_KEY_NEG_INF = -2139095041


def _topk_bias_kernel(iq_ref, ik_ref, iw_ref, tri_ref, o_ref, score_ref, rhs_ref, cnt_ref, *, topk, tq):
    qi = pl.program_id(1)
    tk = tq
    nk = ik_ref.shape[0] // tk
    n_live = qi + 1
    w = iw_ref[0] * (IDX_HEADS ** -0.5)
    pad_q = jnp.zeros((tq, IDX_DIM), F32)
    for h in range(IDX_HEADS):
        hi, lo = _split_bf16(iq_ref[:, h * IDX_DIM:(h + 1) * IDX_DIM])
        rhs_ref[h] = jnp.concatenate([hi, hi, lo, pad_q], axis=1).astype(BF16)
    t_pos = qi * tq + lax.broadcasted_iota(jnp.int32, (tk, tq), 1)
    s_off = lax.broadcasted_iota(jnp.int32, (tk, tq), 0)
    pad_k = jnp.zeros((tk, IDX_DIM), F32)

    def block_rows(kb):
        return pl.ds(pl.multiple_of(kb * tk, tk), tk)

    def score_block(kb, carry):
        hi, lo = _split_bf16(ik_ref[block_rows(kb), 0:IDX_DIM])
        lhs = jnp.concatenate([hi, lo, hi, pad_k], axis=1).astype(BF16)
        score = jnp.zeros((tk, tq), F32)
        for h in range(IDX_HEADS):
            dots = lax.dot_general(lhs, rhs_ref[h], (((1,), (1,)), ((), ())), preferred_element_type=F32)
            score = score + w[h:h + 1, :] * jnp.maximum(dots, 0.0)
        score_ref[block_rows(kb), :] = jnp.where(kb * tk + s_off <= t_pos, score + 0.0, -jnp.inf)
        return carry

    lax.fori_loop(0, n_live, score_block, 0)

    def count(pred):
        for n in range(1, nk + 1):
            @pl.when(n_live == n)
            def _(n=n):
                part = jnp.zeros((SUBLANES, tq), F32)
                for kb in range(n):
                    hit = jnp.where(pred(score_ref[kb * tk:(kb + 1) * tk, :]), 1.0, 0.0)
                    part = part + jnp.sum(hit.reshape(tk // SUBLANES, SUBLANES, tq), axis=0)
                cnt_ref[...] = part
        return jnp.sum(cnt_ref[...], axis=0, keepdims=True)

    def search(i, ans):
        cand = ans ^ jnp.left_shift(jnp.int32(1), 31 - i)
        cand_f = _sortable_bits_to_float(cand)
        cnt = count(lambda s: s >= cand_f)
        accept = jnp.logical_or(cnt >= float(topk), cand < SORT_KEY_NEG_INF)
        return jnp.where(accept, cand, ans)

    thr = _sortable_bits_to_float(
        lax.fori_loop(0, 32, search, jnp.full((1, tq), jnp.iinfo(jnp.int32).min, jnp.int32)))
    need = float(topk) - count(lambda s: s > thr)

    def emit(kb, seen):
        blk = score_ref[block_rows(kb), :]
        eq = jnp.where(blk == thr, 1.0, 0.0)
        rank = jnp.dot(tri_ref[...], eq.astype(BF16), preferred_element_type=F32) + seen
        chosen = jnp.logical_or(blk > thr, jnp.logical_and(blk == thr, rank <= need))
        o_ref[0, block_rows(kb), :] = jnp.where(
            jnp.logical_and(chosen, kb * tk + s_off <= t_pos), 0.0, -jnp.inf)
        return seen + jnp.sum(eq, axis=0, keepdims=True)

    lax.fori_loop(0, n_live, emit, jnp.zeros((1, tq), F32))

    def fill(kb, carry):
        o_ref[0, block_rows(kb), :] = jnp.full((tk, tq), -jnp.inf, F32)
        return carry

    lax.fori_loop(n_live, nk, fill, 0)


def _topk_bias(act32, tile_of, iw_t, batch, seq, topk, tq=256):
    tq = min(tq, seq)
    nq = seq // tq
    width = IDX_HEADS * IDX_DIM
    iq_block = tile_of["i_q"] * IN_PROJ_TN // width
    ik_block = tile_of["i_k"] * IN_PROJ_TN // LANES
    tri = jnp.tril(jnp.ones((tq, tq), BF16))
    return pl.pallas_call(
        functools.partial(_topk_bias_kernel, topk=topk, tq=tq),
        grid=(batch, nq),
        in_specs=[pl.BlockSpec((tq, width), lambda b, i: (b * nq + i, iq_block)),
                  pl.BlockSpec((seq, LANES), lambda b, i: (b, ik_block)),
                  pl.BlockSpec((1, IDX_HEADS, tq), lambda b, i: (b, 0, i)),
                  pl.BlockSpec((tq, tq), lambda b, i: (0, 0))],
        out_specs=pl.BlockSpec((1, seq, tq), lambda b, i: (b, 0, i)),
        out_shape=jax.ShapeDtypeStruct((batch, seq, seq), F32),
        scratch_shapes=[pltpu.VMEM((seq, tq), F32),
                        pltpu.VMEM((IDX_HEADS, tq, 4 * IDX_DIM), BF16),
                        pltpu.VMEM((SUBLANES, tq), F32)],
        compiler_params=_params("parallel", "parallel"),
        name="indexer_topk_bias",
    )(act32, act32, iw_t, tri)


ATTN_HEADS_PER_STEP = 8


def _attn_kernel(q_ref, k_ref, v_ref, bias_ref, o_ref, acc_ref):
    qi = pl.program_id(2)
    tq = q_ref.shape[0]
    tk = tq
    dh = DSA_HEAD_DIM
    group = q_ref.shape[1] // dh
    scale2 = (dh ** -0.5) * LOG2_E
    acc_ref[...] = jnp.zeros_like(acc_ref)

    def body(kj, carry):
        rows = pl.ds(pl.multiple_of(kj * tk, tk), tk)
        bias = bias_ref[0, rows, :]
        heads = [slice(g * dh, (g + 1) * dh) for g in range(group)]
        logits = [lax.dot_general(k_ref[rows, cols], q_ref[:, cols], (((1,), (1,)), ((), ())),
                                  preferred_element_type=F32) for cols in heads]
        new = []
        for g in range(group):
            m, l = carry[g]
            s = logits[g] * scale2 + bias
            m_new = jnp.maximum(m, jnp.max(s, axis=0, keepdims=True))
            m_safe = jnp.where(m_new == -jnp.inf, 0.0, m_new)
            alpha = jnp.exp2(m - m_safe)
            p = jnp.exp2(s - m_safe)
            new.append((m_new, alpha * l + jnp.sum(p, axis=0, keepdims=True)))
            update = lax.dot_general(v_ref[rows, heads[g]], p.astype(BF16), (((0,), (0,)), ((), ())),
                                     preferred_element_type=F32)
            acc_ref[g] = alpha * acc_ref[g] + update
        return tuple(new)

    init = tuple((jnp.full((1, tq), -jnp.inf, F32), jnp.zeros((1, tq), F32)) for _ in range(group))
    final = lax.fori_loop(0, qi + 1, body, init)
    for g in range(group):
        o_ref[:, g * dh:(g + 1) * dh] = (acc_ref[g] / final[g][1]).T.astype(o_ref.dtype)


def _attention(act16, tile_of, bias_t, batch, seq, tq=256):
    tq = min(tq, seq)
    nq = seq // tq
    width = ATTN_HEADS_PER_STEP * DSA_HEAD_DIM
    n_groups = DSA_HEADS // ATTN_HEADS_PER_STEP
    t = act16.shape[0]
    q_block, k_block, v_block = (tile_of[nm] * IN_PROJ_TN // width for nm in ("d_q", "d_k", "d_v"))
    return pl.pallas_call(
        _attn_kernel,
        grid=(batch, n_groups, nq),
        in_specs=[pl.BlockSpec((tq, width), lambda b, h, i: (b * nq + i, q_block + h)),
                  pl.BlockSpec((seq, width), lambda b, h, i: (b, k_block + h)),
                  pl.BlockSpec((seq, width), lambda b, h, i: (b, v_block + h)),
                  pl.BlockSpec((1, seq, tq), lambda b, h, i: (b, 0, i))],
        out_specs=pl.BlockSpec((tq, width), lambda b, h, i: (b * nq + i, h)),
        out_shape=jax.ShapeDtypeStruct((t, DSA_HEADS * DSA_HEAD_DIM), BF16),
        scratch_shapes=[pltpu.VMEM((ATTN_HEADS_PER_STEP, DSA_HEAD_DIM, tq), F32)],
        compiler_params=_params("parallel", "parallel", "arbitrary"),
        name="dsa_attention",
    )(act16, act16, act16, bias_t)


def _merge_kernel(oa_ref, ob_ref, wa_ref, wb_ref, ga_ref, gb_ref, o_ref):
    ya = jnp.dot(oa_ref[...], wa_ref[...].astype(BF16), preferred_element_type=F32)
    yb = jnp.dot(ob_ref[...], wb_ref[...].astype(BF16), preferred_element_type=F32)
    o_ref[...] = (jax.nn.sigmoid(ga_ref[...]) * ya + jax.nn.sigmoid(gb_ref[...]) * yb).astype(o_ref.dtype)


def _merge(o_a, o_b, w_a, w_b, act32, tile_of, tm=1024):
    t, d = o_a.shape
    n = w_a.shape[1]
    tn = IN_PROJ_TN
    tm = min(tm, t)
    ga_tile, gb_tile = tile_of["gate_a"], tile_of["gate_b"]
    return pl.pallas_call(
        _merge_kernel,
        grid=(t // tm, n // tn),
        in_specs=[pl.BlockSpec((tm, d), lambda i, j: (i, 0)),
                  pl.BlockSpec((tm, d), lambda i, j: (i, 0)),
                  pl.BlockSpec((d, tn), lambda i, j: (0, j)),
                  pl.BlockSpec((d, tn), lambda i, j: (0, j)),
                  pl.BlockSpec((tm, tn), lambda i, j: (i, ga_tile + j)),
                  pl.BlockSpec((tm, tn), lambda i, j: (i, gb_tile + j))],
        out_specs=pl.BlockSpec((tm, tn), lambda i, j: (i, j)),
        out_shape=jax.ShapeDtypeStruct((t, n), BF16),
        compiler_params=_params("parallel", "parallel"),
        name="branch_merge",
    )(o_a, o_b, w_a, w_b, act32, act32)


def _mixer_out_kernel(m_ref, w_ref, x_ref, g_ref, gain_ref, sc_ref, sh_ref, x1_ref, h2_ref):
    y = jnp.dot(m_ref[...], w_ref[...], preferred_element_type=F32)
    x1 = x_ref[...] + g_ref[0] * y
    x1_ref[...] = x1
    h2_ref[...] = _rms_modulate(x1, gain_ref[...], sc_ref[0], sh_ref[0]).astype(h2_ref.dtype)


def _mixer_out(merged, w, x2d, g1, gain2, sc2, sh2, seq, tm=256):
    t, d = x2d.shape
    tm = min(tm, seq)
    per_batch = seq // tm
    vec = pl.BlockSpec((1, 1, d), lambda i: (i // per_batch, 0, 0))
    row = pl.BlockSpec((tm, d), lambda i: (i, 0))
    return pl.pallas_call(
        _mixer_out_kernel,
        grid=(t // tm,),
        in_specs=[row, pl.BlockSpec((d, d), lambda i: (0, 0)), row, vec,
                  pl.BlockSpec((1, d), lambda i: (0, 0)), vec, vec],
        out_specs=[row, row],
        out_shape=[jax.ShapeDtypeStruct((t, d), F32), jax.ShapeDtypeStruct((t, d), BF16)],
        compiler_params=_params("parallel"),
        name="mixer_out_norm2",
    )(merged, w, x2d, g1, gain2, sc2, sh2)


def _ffn_up_kernel(h_ref, wg_ref, wu_ref, o_ref):
    g = jnp.dot(h_ref[...], wg_ref[...].astype(BF16), preferred_element_type=F32)
    u = jnp.dot(h_ref[...], wu_ref[...].astype(BF16), preferred_element_type=F32)
    o_ref[...] = (_silu(g) * u).astype(o_ref.dtype)


def _ffn_up(h2, w_gate_up, tm=1024, tn=512):
    t, d = h2.shape
    d_ff = w_gate_up.shape[1] // 2
    tm = min(tm, t)
    nj = d_ff // tn
    return pl.pallas_call(
        _ffn_up_kernel,
        grid=(t // tm, nj),
        in_specs=[pl.BlockSpec((tm, d), lambda i, j: (i, 0)),
                  pl.BlockSpec((d, tn), lambda i, j: (0, j)),
                  pl.BlockSpec((d, tn), lambda i, j: (0, nj + j))],
        out_specs=pl.BlockSpec((tm, tn), lambda i, j: (i, j)),
        out_shape=jax.ShapeDtypeStruct((t, d_ff), BF16),
        compiler_params=_params("parallel", "parallel"),
        name="ffn_up",
    )(h2, w_gate_up, w_gate_up)


def _ffn_down_kernel(a_ref, w_ref, x_ref, g_ref, gain_ref, o_ref, acc_ref):
    kk = pl.program_id(1)

    @pl.when(kk == 0)
    def _():
        acc_ref[...] = jnp.zeros_like(acc_ref)

    acc_ref[...] += jnp.dot(a_ref[...], w_ref[...], preferred_element_type=F32)

    @pl.when(kk == pl.num_programs(1) - 1)
    def _():
        x2 = x_ref[...] + g_ref[0] * acc_ref[...]
        y = x2 * lax.rsqrt(jnp.mean(x2 * x2, axis=-1, keepdims=True) + RMS_EPS)
        o_ref[...] = y * gain_ref[...]


def _ffn_down(act, w_down, x1, g2, final_gain, seq, tm=512, tk=1408):
    t, d_ff = act.shape
    d = w_down.shape[1]
    tm = min(tm, seq)
    per_batch = seq // tm
    return pl.pallas_call(
        _ffn_down_kernel,
        grid=(t // tm, d_ff // tk),
        in_specs=[pl.BlockSpec((tm, tk), lambda i, k: (i, k)),
                  pl.BlockSpec((tk, d), lambda i, k: (k, 0)),
                  pl.BlockSpec((tm, d), lambda i, k: (i, 0)),
                  pl.BlockSpec((1, 1, d), lambda i, k: (i // per_batch, 0, 0)),
                  pl.BlockSpec((1, d), lambda i, k: (0, 0))],
        out_specs=pl.BlockSpec((tm, d), lambda i, k: (i, 0)),
        out_shape=jax.ShapeDtypeStruct((t, d), F32),
        scratch_shapes=[pltpu.VMEM((tm, d), F32)],
        compiler_params=_params("parallel", "arbitrary"),
        name="ffn_down_final_norm",
    )(act, w_down, x1, g2, final_gain)


def _rope_tables(positions_flat, rot, period):
    half = rot // 2
    assert period % half == 0 and LANES % half == 0
    inv_freq = jnp.power(ROPE_THETA, -jnp.arange(0, rot, 2, dtype=F32) / rot)
    ang = positions_flat.astype(F32)[:, None] * inv_freq[None, :]
    reps = (1, LANES // half)
    return jnp.tile(jnp.cos(ang), reps), jnp.tile(jnp.sin(ang), reps)


def _layer(x2d, mod, positions_flat, batch, seq, norm1_gain, norm2_gain, w_in, gla_gate_up,
           gla_gate_bias, gla_norm_gain, w_branch_gla, w_branch_dsa, w_merge_out, w_ffn_gate_up,
           w_ffn_down, final_gain):
    d = x2d.shape[1]
    sh1, sc1, g1, sh2, sc2, g2 = [mod[:, i * d:(i + 1) * d][:, None, :] for i in range(N_MOD)]

    plan = _in_proj_plan(d)
    tile_of = plan[3]
    cos_d, sin_d = _rope_tables(positions_flat, DSA_HEAD_DIM // ROPE_FRACTION, DSA_HEAD_DIM)
    cos_i, sin_i = _rope_tables(positions_flat, IDX_DIM // ROPE_FRACTION, IDX_DIM)
    act32, act16 = _in_project(x2d, norm1_gain[None, :], sc1, sh1, seq, w_in.T,
                               (cos_d, sin_d, cos_i, sin_i), plan)

    o_a = _gla(act32, act16, tile_of, gla_gate_up, gla_gate_bias[None, :], gla_norm_gain[None, :],
               batch, seq, rows=min(256, seq))

    iw_col = tile_of["i_k"] * IN_PROJ_TN + IDX_DIM
    iw_t = jnp.transpose(act32[:, iw_col:iw_col + IDX_HEADS].reshape(batch, seq, IDX_HEADS), (0, 2, 1))
    topk = min(IDX_TOPK_MAX, seq // 4)
    bias_t = _topk_bias(act32, tile_of, iw_t, batch, seq, topk)
    o_b = _attention(act16, tile_of, bias_t, batch, seq)

    merged = _merge(o_a, o_b, w_branch_gla, w_branch_dsa, act32, tile_of)
    x1, h2 = _mixer_out(merged, w_merge_out.astype(BF16), x2d, g1, norm2_gain[None, :], sc2, sh2, seq)
    act = _ffn_up(h2, w_ffn_gate_up)
    return _ffn_down(act, w_ffn_down.astype(BF16), x1, g2, final_gain[None, :], seq)


def kernel(x, c, positions, norm1_gain, norm2_gain, w_ada, b_ada, w_in, gla_gate_up, gla_gate_bias,
           gla_norm_gain, w_branch_gla, w_branch_dsa, w_merge_out, w_ffn_gate_up, w_ffn_down,
           final_norm_gain):
    batch, seq, d = x.shape
    depth = w_in.shape[0]
    assert depth == 1, "the final RMSNorm is fused into the single layer's FFN kernel"
    x2d = x.reshape(batch * seq, d)
    c_pad = jnp.zeros((SUBLANES, d), F32).at[:batch].set(c)
    mod = _modulation(c_pad, w_ada[0], b_ada[0][None, :])[:batch]
    out = _layer(x2d, mod, positions.reshape(-1), batch, seq, norm1_gain[0], norm2_gain[0], w_in[0],
                 gla_gate_up[0], gla_gate_bias[0], gla_norm_gain[0], w_branch_gla[0], w_branch_dsa[0],
                 w_merge_out[0], w_ffn_gate_up[0], w_ffn_down[0], final_norm_gain)
    return out.reshape(batch, seq, d)
```

```python
import functools

import jax
import jax.numpy as jnp
from jax import lax
from jax.experimental import pallas as pl
from jax.experimental.pallas import tpu as pltpu

F32 = jnp.float32
BF16 = jnp.bfloat16
HIGHEST = lax.Precision.HIGHEST

RMS_EPS = 1e-6
GLA_HEADS = 4
GLA_GATE_RANK = 16
GLA_TAU = 16.0
GLA_CHUNK = 64
GLA_SUB = 8
DSA_HEADS = 16
DSA_HEAD_DIM = 128
IDX_HEADS = 8
IDX_DIM = 64
IDX_TOPK_MAX = 256
ROPE_THETA = 500000.0
ROPE_FRACTION = 4
N_MOD = 6
LOG2_E = 1.4426950408889634

LANES = 128
SUBLANES = 8
VMEM_LIMIT = 48 * 1024 * 1024


def _params(*semantics):
    return pltpu.CompilerParams(dimension_semantics=semantics, vmem_limit_bytes=VMEM_LIMIT)


def _silu(x):
    return x * jax.nn.sigmoid(x)


def _mod_kernel(c_ref, w_ref, b_ref, o_ref):
    a = _silu(c_ref[...])
    o_ref[...] = jnp.dot(a, w_ref[...], precision=HIGHEST, preferred_element_type=F32) + b_ref[...]


def _modulation(c_pad, w_ada, b_ada, tn=2048):
    rows, d = c_pad.shape
    n = w_ada.shape[1]
    return pl.pallas_call(
        _mod_kernel,
        grid=(n // tn,),
        in_specs=[pl.BlockSpec((rows, d), lambda j: (0, 0)),
                  pl.BlockSpec((d, tn), lambda j: (0, j)),
                  pl.BlockSpec((1, tn), lambda j: (0, j))],
        out_specs=pl.BlockSpec((rows, tn), lambda j: (0, j)),
        out_shape=jax.ShapeDtypeStruct((rows, n), F32),
        compiler_params=_params("parallel"),
        name="adaln_mod",
    )(c_pad, w_ada, b_ada)


def _rms_modulate(x, gain, scale, shift):
    y = x * lax.rsqrt(jnp.mean(x * x, axis=-1, keepdims=True) + RMS_EPS)
    return (y * gain) * (1.0 + scale) + shift


def _rope_lanes(x, cos, sin, half, period, limit=LANES):
    lane = lax.broadcasted_iota(jnp.int32, x.shape, 1)
    in_head = lane % period
    upper = pltpu.roll(x, LANES - half, axis=1)
    lower = pltpu.roll(x, half, axis=1)
    live = lane < limit
    first = jnp.logical_and(live, in_head < half)
    second = jnp.logical_and(live, jnp.logical_and(in_head >= half, in_head < 2 * half))
    return jnp.where(first, x * cos - upper * sin,
                     jnp.where(second, lower * sin + x * cos, x))


PLAIN32, GLA_Q32, IDX_Q32, IDX_K32, PLAIN16, DSA_ROPE16 = range(6)
IN_PROJ_ROPE_ROWS = 256
IN_PROJ_TN = 512


def _norm_mod_kernel(x_ref, gain_ref, sc_ref, sh_ref, o_ref):
    o_ref[...] = _rms_modulate(x_ref[...], gain_ref[...], sc_ref[0], sh_ref[0]).astype(o_ref.dtype)


def _norm_modulate(x2d, gain, sc, sh, seq, tm=512):
    t, d = x2d.shape
    per_batch = seq // tm
    return pl.pallas_call(
        _norm_mod_kernel,
        grid=(t // tm,),
        in_specs=[pl.BlockSpec((tm, d), lambda i: (i, 0)),
                  pl.BlockSpec((1, d), lambda i: (0, 0)),
                  pl.BlockSpec((1, 1, d), lambda i: (i // per_batch, 0, 0)),
                  pl.BlockSpec((1, 1, d), lambda i: (i // per_batch, 0, 0))],
        out_specs=pl.BlockSpec((tm, d), lambda i: (i, 0)),
        out_shape=jax.ShapeDtypeStruct((t, d), BF16),
        compiler_params=_params("parallel"),
        name="norm1_modulate",
    )(x2d, gain, sc, sh)


def _in_proj_kernel(off_ref, kind_ref, h_ref, wt_ref, cos_d_ref, sin_d_ref, cos_i_ref, sin_i_ref,
                    o32_ref, o16_ref, acc_ref):
    kind = kind_ref[pl.program_id(1)]
    tm, tn = o32_ref.shape
    n_groups = tn // LANES
    rot_d = DSA_HEAD_DIM // ROPE_FRACTION
    rot_i = IDX_DIM // ROPE_FRACTION

    def product():
        return lax.dot_general(h_ref[...], wt_ref[...].astype(BF16), (((1,), (1,)), ((), ())),
                               preferred_element_type=F32)

    def rope_rows(cos_ref, sin_ref, out_ref, half, period, groups, pre_scale=1.0, limit=LANES):
        def chunk(r, carry):
            rows = pl.ds(pl.multiple_of(r * IN_PROJ_ROPE_ROWS, IN_PROJ_ROPE_ROWS), IN_PROJ_ROPE_ROWS)
            cos = cos_ref[rows, :]
            sin = sin_ref[rows, :]
            for g in groups:
                cols = slice(g * LANES, (g + 1) * LANES)
                x = acc_ref[rows, cols]
                if pre_scale != 1.0:
                    x = x * pre_scale
                out_ref[rows, cols] = _rope_lanes(x, cos, sin, half, period, limit).astype(out_ref.dtype)
            return carry
        lax.fori_loop(0, tm // IN_PROJ_ROPE_ROWS, chunk, 0)

    @pl.when(kind == PLAIN32)
    def _():
        o32_ref[...] = product()

    @pl.when(kind == GLA_Q32)
    def _():
        o32_ref[...] = product() * ((wt_ref.shape[1] // 2 // GLA_HEADS) ** -0.5)

    @pl.when(kind == IDX_Q32)
    def _():
        acc_ref[...] = product()
        rope_rows(cos_i_ref, sin_i_ref, o32_ref, rot_i // 2, IDX_DIM, range(n_groups), pre_scale=IDX_DIM ** -0.5)

    @pl.when(kind == IDX_K32)
    def _():
        acc = product()
        acc_ref[...] = acc
        o32_ref[...] = acc
        rope_rows(cos_i_ref, sin_i_ref, o32_ref, rot_i // 2, IDX_DIM, range(1), limit=IDX_DIM)

    @pl.when(kind == PLAIN16)
    def _():
        o16_ref[...] = product().astype(BF16)

    @pl.when(kind == DSA_ROPE16)
    def _():
        acc_ref[...] = product()
        rope_rows(cos_d_ref, sin_d_ref, o16_ref, rot_d // 2, DSA_HEAD_DIM, range(n_groups))


def _in_proj_plan(d):
    tn = IN_PROJ_TN
    gla_qk, dsa_w, idx_w = d // 2, DSA_HEADS * DSA_HEAD_DIM, IDX_HEADS * IDX_DIM
    names = ("g_q", "g_k", "g_v", "g_r", "g_lr", "d_q", "d_k", "d_v", "i_q", "i_k", "i_w", "gate_a", "gate_b")
    widths = (gla_qk, gla_qk, d, d, GLA_GATE_RANK, dsa_w, dsa_w, dsa_w, idx_w, IDX_DIM, IDX_HEADS, d, d)
    start, pos = {}, 0
    for nm, wd in zip(names, widths):
        start[nm] = pos
        pos += wd
    width = dict(zip(names, widths))
    assert start["i_w"] == start["i_k"] + IDX_DIM and IDX_DIM + IDX_HEADS <= LANES
    f32_groups = (("g_q", GLA_Q32), ("g_k", PLAIN32), ("g_r", PLAIN32), ("g_lr", PLAIN32), ("i_q", IDX_Q32),
                  ("i_k", IDX_K32), ("gate_a", PLAIN32), ("gate_b", PLAIN32))
    bf16_groups = (("g_v", PLAIN16), ("d_q", DSA_ROPE16), ("d_k", DSA_ROPE16), ("d_v", PLAIN16))
    offsets, kinds, tile_of = [], [], {}
    for groups in (f32_groups, bf16_groups):
        base = len(offsets)
        for nm, kind in groups:
            tile_of[nm] = len(offsets) - base
            for t in range(-(-width[nm] // tn)):
                offsets.append(start[nm] + t * tn)
                kinds.append(kind)
        if groups is f32_groups:
            n32 = len(offsets)
    assert all(o % SUBLANES == 0 and o + tn <= pos for o in offsets)
    return offsets, kinds, n32, tile_of


def _in_project(h, w_in_t, tables, plan, tm=2048):
    t, d = h.shape
    tn = IN_PROJ_TN
    offsets, kinds, n32, _ = plan
    n16 = len(offsets) - n32
    tm = min(tm, t)
    once = pl.Buffered(1)
    table_spec = pl.BlockSpec((tm, LANES), lambda i, j, off, kind: (i, 0), pipeline_mode=once)
    grid_spec = pltpu.PrefetchScalarGridSpec(
        num_scalar_prefetch=2,
        grid=(t // tm, len(offsets)),
        in_specs=[pl.BlockSpec((tm, d), lambda i, j, off, kind: (i, 0), pipeline_mode=once),
                  pl.BlockSpec((pl.Element(tn), pl.Element(d)),
                               lambda i, j, off, kind: (pl.multiple_of(off[j], SUBLANES), 0)),
                  table_spec, table_spec, table_spec, table_spec],
        out_specs=[pl.BlockSpec((tm, tn), lambda i, j, off, kind: (i, jnp.minimum(j, n32 - 1))),
                   pl.BlockSpec((tm, tn), lambda i, j, off, kind: (i, jnp.maximum(j - n32, 0)))],
        scratch_shapes=[pltpu.VMEM((tm, tn), F32)],
    )
    return pl.pallas_call(
        _in_proj_kernel,
        grid_spec=grid_spec,
        out_shape=[jax.ShapeDtypeStruct((t, n32 * tn), F32), jax.ShapeDtypeStruct((t, n16 * tn), BF16)],
        compiler_params=_params("parallel", "arbitrary"),
        name="in_proj",
    )(jnp.asarray(offsets, jnp.int32), jnp.asarray(kinds, jnp.int32), h, w_in_t, *tables)


def _split3_bf16(x):
    hi, rest = _split_bf16(x)
    mid, lo = _split_bf16(rest)
    return hi, mid, lo


def _gla_kernel(q_ref, k_ref, v_ref, gr_ref, sm_ref, gup_ref, gb_ref, ng_ref, tri_ref, o_ref, st_ref):
    @pl.when(pl.program_id(2) == 0)
    def _():
        st_ref[...] = jnp.zeros_like(st_ref)

    rows = q_ref.shape[0]
    dk = q_ref.shape[1]
    c, sub = GLA_CHUNK, GLA_SUB
    n_sub = c // sub

    a_hi, a_lo = _split_bf16(sm_ref[:, 0:GLA_GATE_RANK])
    gate_lhs = jnp.concatenate(
        [a_hi, a_lo, a_hi, jnp.zeros((rows, LANES - 3 * GLA_GATE_RANK), F32)], axis=1).astype(BF16)
    z = jnp.dot(gate_lhs, gup_ref[...], preferred_element_type=F32) + gb_ref[...]
    log_g = (jnp.minimum(z, 0.0) - jnp.log(1.0 + jnp.exp(-jnp.abs(z)))) * (1.0 / GLA_TAU)
    parts = jnp.dot(tri_ref[...], jnp.concatenate(_split3_bf16(log_g), axis=1).astype(BF16),
                    preferred_element_type=F32)
    b = (parts[:, 0:dk] + parts[:, dk:2 * dk]) + parts[:, 2 * dk:3 * dk]

    q = q_ref[...]
    k = k_ref[...]

    k_tiles = k.reshape(rows // sub, sub, dk)
    b_tiles = b.reshape(rows // sub, sub, dk)
    row = lax.broadcasted_iota(jnp.int32, (rows, 1), 0)
    row_in_sub = row % sub
    col = lax.broadcasted_iota(jnp.int32, (rows, c), 1)
    target = lax.broadcasted_iota(jnp.int32, (rows, c), 0) % c
    a_diag = jnp.zeros((rows, c), F32)
    for delta in range(sub):
        if delta == 0:
            k_d, b_d = k, b
        else:
            k_d = pltpu.roll(k_tiles, delta, axis=1).reshape(rows, dk)
            b_d = pltpu.roll(b_tiles, delta, axis=1).reshape(rows, dk)
        decay = jnp.exp(jnp.where(row_in_sub >= delta, b - b_d, -jnp.inf))
        a = jnp.sum(q * k_d * decay, axis=-1, keepdims=True)
        a_diag = jnp.where(col == target - delta, a, a_diag)

    for ch in range(rows // c):
        base = ch * c
        qc = q[base:base + c]
        kc = k[base:base + c]
        bc = b[base:base + c]
        vc = v_ref[base:base + c, :]
        st = st_ref[...]

        o_inter = lax.dot_general((qc * jnp.exp(bc)).astype(BF16), st.astype(BF16),
                                  (((1,), (1,)), ((), ())), preferred_element_type=F32)

        blocks = [jnp.zeros((sub, c), F32)]
        for i_sub in range(1, n_sub):
            lo = i_sub * sub
            ref = bc[lo - 1:lo]
            q_hat = qc[lo:lo + sub] * jnp.exp(bc[lo:lo + sub] - ref)
            k_hat = kc[0:lo] * jnp.exp(ref - bc[0:lo])
            k_pad = jnp.concatenate([k_hat, jnp.zeros((c - lo, dk), F32)], axis=0)
            blocks.append(lax.dot_general(q_hat.astype(BF16), k_pad.astype(BF16),
                                          (((1,), (1,)), ((), ())), preferred_element_type=F32))
        attn = jnp.concatenate(blocks, axis=0) + a_diag[base:base + c]

        o = jnp.dot(attn.astype(BF16), vc, preferred_element_type=F32) + o_inter

        b_last = bc[c - 1:c]
        k_dec = (kc * jnp.exp(b_last - bc)).astype(BF16)
        st_ref[...] = st * jnp.exp(b_last) + lax.dot_general(
            vc, k_dec, (((0,), (0,)), ((), ())), preferred_element_type=F32)

        o = o * lax.rsqrt(jnp.mean(o * o, axis=-1, keepdims=True) + RMS_EPS) * ng_ref[...]
        o_ref[base:base + c, :] = (o * _silu(gr_ref[base:base + c, :])).astype(o_ref.dtype)


def _gla(act32, act16, tile_of, gate_up, gbias, ngain, batch, seq, rows=256):
    dk = gate_up.shape[1] // GLA_HEADS
    dv = ngain.shape[1] // GLA_HEADS
    t = act32.shape[0]
    nr = seq // rows
    g_hi, g_lo = _split_bf16(gate_up)
    gup_stack = jnp.concatenate(
        [g_hi, g_hi, g_lo, jnp.zeros((LANES - 3 * GLA_GATE_RANK, gate_up.shape[1]), F32)], axis=0).astype(BF16)
    idx = jnp.arange(rows)
    tri = jnp.logical_and(idx[:, None] // GLA_CHUNK == idx[None, :] // GLA_CHUNK,
                          idx[None, :] <= idx[:, None]).astype(BF16)

    def cols(group, width):
        first = tile_of[group] * IN_PROJ_TN // width
        return lambda b, h, r: (b * nr + r, first + h)

    lr_block = tile_of["g_lr"] * IN_PROJ_TN // LANES
    return pl.pallas_call(
        _gla_kernel,
        grid=(batch, GLA_HEADS, nr),
        in_specs=[pl.BlockSpec((rows, dk), cols("g_q", dk)),
                  pl.BlockSpec((rows, dk), cols("g_k", dk)),
                  pl.BlockSpec((rows, dv), cols("g_v", dv)),
                  pl.BlockSpec((rows, dv), cols("g_r", dv)),
                  pl.BlockSpec((rows, LANES), lambda b, h, r: (b * nr + r, lr_block)),
                  pl.BlockSpec((LANES, dk), lambda b, h, r: (0, h)),
                  pl.BlockSpec((1, dk), lambda b, h, r: (0, h)),
                  pl.BlockSpec((1, dv), lambda b, h, r: (0, h)),
                  pl.BlockSpec((rows, rows), lambda b, h, r: (0, 0))],
        out_specs=pl.BlockSpec((rows, dv), lambda b, h, r: (b * nr + r, h)),
        out_shape=jax.ShapeDtypeStruct((t, GLA_HEADS * dv), BF16),
        scratch_shapes=[pltpu.VMEM((dv, dk), F32)],
        compiler_params=_params("parallel", "parallel", "arbitrary"),
        name="gla",
    )(act32, act32, act16, act32, act32, gup_stack, gbias, ngain, tri)


def _split_bf16(x):
    hi = x.astype(BF16).astype(F32)
    return hi, x - hi


def _sortable_bits_to_float(key):
    return lax.bitcast_convert_type(jnp.where(key < 0, key ^ jnp.int32(0x7FFFFFFF), key), F32)


SORT_KEY_NEG_INF = -2139095041


def _topk_bias_kernel(iq_ref, ik_ref, iw_ref, tri_ref, o_ref, score_ref, rhs_ref, cnt_ref, *, topk, tq):
    qi = pl.program_id(1)
    tk = tq
    nk = ik_ref.shape[0] // tk
    n_live = qi + 1
    w = iw_ref[0] * (IDX_HEADS ** -0.5)
    pad_q = jnp.zeros((tq, IDX_DIM), F32)
    for h in range(IDX_HEADS):
        hi, lo = _split_bf16(iq_ref[:, h * IDX_DIM:(h + 1) * IDX_DIM])
        rhs_ref[h] = jnp.concatenate([hi, hi, lo, pad_q], axis=1).astype(BF16)
    t_pos = qi * tq + lax.broadcasted_iota(jnp.int32, (tk, tq), 1)
    s_off = lax.broadcasted_iota(jnp.int32, (tk, tq), 0)
    pad_k = jnp.zeros((tk, IDX_DIM), F32)

    def block_rows(kb):
        return pl.ds(pl.multiple_of(kb * tk, tk), tk)

    def score_block(kb, carry):
        hi, lo = _split_bf16(ik_ref[block_rows(kb), 0:IDX_DIM])
        lhs = jnp.concatenate([hi, lo, hi, pad_k], axis=1).astype(BF16)
        score = jnp.zeros((tk, tq), F32)
        for h in range(IDX_HEADS):
            dots = lax.dot_general(lhs, rhs_ref[h], (((1,), (1,)), ((), ())), preferred_element_type=F32)
            score = score + w[h:h + 1, :] * jnp.maximum(dots, 0.0)
        score_ref[block_rows(kb), :] = jnp.where(kb * tk + s_off <= t_pos, score + 0.0, -jnp.inf)
        return carry

    lax.fori_loop(0, n_live, score_block, 0)

    def count(pred):
        for n in range(1, nk + 1):
            @pl.when(n_live == n)
            def _(n=n):
                part = jnp.zeros((SUBLANES, tq), F32)
                for kb in range(n):
                    hit = jnp.where(pred(score_ref[kb * tk:(kb + 1) * tk, :]), 1.0, 0.0)
                    part = part + jnp.sum(hit.reshape(tk // SUBLANES, SUBLANES, tq), axis=0)
                cnt_ref[...] = part
        return jnp.sum(cnt_ref[...], axis=0, keepdims=True)

    def search(i, ans):
        cand = ans ^ jnp.left_shift(jnp.int32(1), 31 - i)
        cand_f = _sortable_bits_to_float(cand)
        cnt = count(lambda s: s >= cand_f)
        accept = jnp.logical_or(cnt >= float(topk), cand < SORT_KEY_NEG_INF)
        return jnp.where(accept, cand, ans)

    thr = _sortable_bits_to_float(
        lax.fori_loop(0, 32, search, jnp.full((1, tq), jnp.iinfo(jnp.int32).min, jnp.int32)))
    need = float(topk) - count(lambda s: s > thr)

    def emit(kb, seen):
        blk = score_ref[block_rows(kb), :]
        eq = jnp.where(blk == thr, 1.0, 0.0)
        rank = jnp.dot(tri_ref[...], eq.astype(BF16), preferred_element_type=F32) + seen
        chosen = jnp.logical_or(blk > thr, jnp.logical_and(blk == thr, rank <= need))
        o_ref[0, block_rows(kb), :] = jnp.where(
            jnp.logical_and(chosen, kb * tk + s_off <= t_pos), 0.0, -jnp.inf)
        return seen + jnp.sum(eq, axis=0, keepdims=True)

    lax.fori_loop(0, n_live, emit, jnp.zeros((1, tq), F32))

    def fill(kb, carry):
        o_ref[0, block_rows(kb), :] = jnp.full((tk, tq), -jnp.inf, F32)
        return carry

    lax.fori_loop(n_live, nk, fill, 0)


def _topk_bias(act32, tile_of, iw_t, batch, seq, topk, tq=256):
    tq = min(tq, seq)
    nq = seq // tq
    width = IDX_HEADS * IDX_DIM
    iq_block = tile_of["i_q"] * IN_PROJ_TN // width
    ik_block = tile_of["i_k"] * IN_PROJ_TN // LANES
    tri = jnp.tril(jnp.ones((tq, tq), BF16))
    return pl.pallas_call(
        functools.partial(_topk_bias_kernel, topk=topk, tq=tq),
        grid=(batch, nq),
        in_specs=[pl.BlockSpec((tq, width), lambda b, i: (b * nq + i, iq_block)),
                  pl.BlockSpec((seq, LANES), lambda b, i: (b, ik_block)),
                  pl.BlockSpec((1, IDX_HEADS, tq), lambda b, i: (b, 0, i)),
                  pl.BlockSpec((tq, tq), lambda b, i: (0, 0))],
        out_specs=pl.BlockSpec((1, seq, tq), lambda b, i: (b, 0, i)),
        out_shape=jax.ShapeDtypeStruct((batch, seq, seq), F32),
        scratch_shapes=[pltpu.VMEM((seq, tq), F32),
                        pltpu.VMEM((IDX_HEADS, tq, 4 * IDX_DIM), BF16),
                        pltpu.VMEM((SUBLANES, tq), F32)],
        compiler_params=_params("parallel", "parallel"),
        name="indexer_topk_bias",
    )(act32, act32, iw_t, tri)


ATTN_HEADS_PER_STEP = 8


def _attn_kernel(q_ref, k_ref, v_ref, bias_ref, o_ref, acc_ref):
    qi = pl.program_id(2)
    tq = q_ref.shape[0]
    tk = tq
    dh = DSA_HEAD_DIM
    group = q_ref.shape[1] // dh
    scale2 = (dh ** -0.5) * LOG2_E
    acc_ref[...] = jnp.zeros_like(acc_ref)

    def body(kj, carry):
        rows = pl.ds(pl.multiple_of(kj * tk, tk), tk)
        bias = bias_ref[0, rows, :]
        heads = [slice(g * dh, (g + 1) * dh) for g in range(group)]
        logits = [lax.dot_general(k_ref[rows, cols], q_ref[:, cols], (((1,), (1,)), ((), ())),
                                  preferred_element_type=F32) for cols in heads]
        new = []
        for g in range(group):
            m, l = carry[g]
            s = logits[g] * scale2 + bias
            m_new = jnp.maximum(m, jnp.max(s, axis=0, keepdims=True))
            m_safe = jnp.where(m_new == -jnp.inf, 0.0, m_new)
            alpha = jnp.exp2(m - m_safe)
            p = jnp.exp2(s - m_safe)
            new.append((m_new, alpha * l + jnp.sum(p, axis=0, keepdims=True)))
            update = lax.dot_general(v_ref[rows, heads[g]], p.astype(BF16), (((0,), (0,)), ((), ())),
                                     preferred_element_type=F32)
            acc_ref[g] = alpha * acc_ref[g] + update
        return tuple(new)

    init = tuple((jnp.full((1, tq), -jnp.inf, F32), jnp.zeros((1, tq), F32)) for _ in range(group))
    final = lax.fori_loop(0, qi + 1, body, init)
    for g in range(group):
        o_ref[:, g * dh:(g + 1) * dh] = (acc_ref[g] / final[g][1]).T.astype(o_ref.dtype)


def _attention(act16, tile_of, bias_t, batch, seq, tq=256):
    tq = min(tq, seq)
    nq = seq // tq
    width = ATTN_HEADS_PER_STEP * DSA_HEAD_DIM
    n_groups = DSA_HEADS // ATTN_HEADS_PER_STEP
    t = act16.shape[0]
    q_block, k_block, v_block = (tile_of[nm] * IN_PROJ_TN // width for nm in ("d_q", "d_k", "d_v"))
    return pl.pallas_call(
        _attn_kernel,
        grid=(batch, n_groups, nq),
        in_specs=[pl.BlockSpec((tq, width), lambda b, h, i: (b * nq + i, q_block + h)),
                  pl.BlockSpec((seq, width), lambda b, h, i: (b, k_block + h)),
                  pl.BlockSpec((seq, width), lambda b, h, i: (b, v_block + h)),
                  pl.BlockSpec((1, seq, tq), lambda b, h, i: (b, 0, i))],
        out_specs=pl.BlockSpec((tq, width), lambda b, h, i: (b * nq + i, h)),
        out_shape=jax.ShapeDtypeStruct((t, DSA_HEADS * DSA_HEAD_DIM), BF16),
        scratch_shapes=[pltpu.VMEM((ATTN_HEADS_PER_STEP, DSA_HEAD_DIM, tq), F32)],
        compiler_params=_params("parallel", "parallel", "arbitrary"),
        name="dsa_attention",
    )(act16, act16, act16, bias_t)


def _merge_kernel(oa_ref, ob_ref, wa_ref, wb_ref, ga_ref, gb_ref, o_ref):
    ya = jnp.dot(oa_ref[...], wa_ref[...].astype(BF16), preferred_element_type=F32)
    yb = jnp.dot(ob_ref[...], wb_ref[...].astype(BF16), preferred_element_type=F32)
    o_ref[...] = (jax.nn.sigmoid(ga_ref[...]) * ya + jax.nn.sigmoid(gb_ref[...]) * yb).astype(o_ref.dtype)


def _merge(o_a, o_b, w_a, w_b, act32, tile_of, tm=1024):
    t, d = o_a.shape
    n = w_a.shape[1]
    tn = IN_PROJ_TN
    tm = min(tm, t)
    ga_tile, gb_tile = tile_of["gate_a"], tile_of["gate_b"]
    return pl.pallas_call(
        _merge_kernel,
        grid=(t // tm, n // tn),
        in_specs=[pl.BlockSpec((tm, d), lambda i, j: (i, 0)),
                  pl.BlockSpec((tm, d), lambda i, j: (i, 0)),
                  pl.BlockSpec((d, tn), lambda i, j: (0, j)),
                  pl.BlockSpec((d, tn), lambda i, j: (0, j)),
                  pl.BlockSpec((tm, tn), lambda i, j: (i, ga_tile + j)),
                  pl.BlockSpec((tm, tn), lambda i, j: (i, gb_tile + j))],
        out_specs=pl.BlockSpec((tm, tn), lambda i, j: (i, j)),
        out_shape=jax.ShapeDtypeStruct((t, n), BF16),
        compiler_params=_params("parallel", "parallel"),
        name="branch_merge",
    )(o_a, o_b, w_a, w_b, act32, act32)


def _mixer_out_kernel(m_ref, w_ref, x_ref, g_ref, gain_ref, sc_ref, sh_ref, x1_ref, h2_ref):
    y = jnp.dot(m_ref[...], w_ref[...], preferred_element_type=F32)
    x1 = x_ref[...] + g_ref[0] * y
    x1_ref[...] = x1
    h2_ref[...] = _rms_modulate(x1, gain_ref[...], sc_ref[0], sh_ref[0]).astype(h2_ref.dtype)


def _mixer_out(merged, w, x2d, g1, gain2, sc2, sh2, seq, tm=256):
    t, d = x2d.shape
    tm = min(tm, seq)
    per_batch = seq // tm
    vec = pl.BlockSpec((1, 1, d), lambda i: (i // per_batch, 0, 0))
    row = pl.BlockSpec((tm, d), lambda i: (i, 0))
    return pl.pallas_call(
        _mixer_out_kernel,
        grid=(t // tm,),
        in_specs=[row, pl.BlockSpec((d, d), lambda i: (0, 0)), row, vec,
                  pl.BlockSpec((1, d), lambda i: (0, 0)), vec, vec],
        out_specs=[row, row],
        out_shape=[jax.ShapeDtypeStruct((t, d), F32), jax.ShapeDtypeStruct((t, d), BF16)],
        compiler_params=_params("parallel"),
        name="mixer_out_norm2",
    )(merged, w, x2d, g1, gain2, sc2, sh2)


def _ffn_up_kernel(h_ref, wg_ref, wu_ref, o_ref):
    g = jnp.dot(h_ref[...], wg_ref[...].astype(BF16), preferred_element_type=F32)
    u = jnp.dot(h_ref[...], wu_ref[...].astype(BF16), preferred_element_type=F32)
    o_ref[...] = (_silu(g) * u).astype(o_ref.dtype)


def _ffn_up(h2, w_gate_up, tm=2048, tn=512):
    t, d = h2.shape
    d_ff = w_gate_up.shape[1] // 2
    tm = min(tm, t)
    nj = d_ff // tn
    return pl.pallas_call(
        _ffn_up_kernel,
        grid=(t // tm, nj),
        in_specs=[pl.BlockSpec((tm, d), lambda i, j: (i, 0), pipeline_mode=pl.Buffered(1)),
                  pl.BlockSpec((d, tn), lambda i, j: (0, j)),
                  pl.BlockSpec((d, tn), lambda i, j: (0, nj + j))],
        out_specs=pl.BlockSpec((tm, tn), lambda i, j: (i, j)),
        out_shape=jax.ShapeDtypeStruct((t, d_ff), BF16),
        compiler_params=_params("parallel", "parallel"),
        name="ffn_up",
    )(h2, w_gate_up, w_gate_up)


def _ffn_down_kernel(a_ref, w_ref, x_ref, g_ref, gain_ref, o_ref, acc_ref):
    kk = pl.program_id(1)

    @pl.when(kk == 0)
    def _():
        acc_ref[...] = jnp.zeros_like(acc_ref)

    acc_ref[...] += jnp.dot(a_ref[...], w_ref[...], preferred_element_type=F32)

    @pl.when(kk == pl.num_programs(1) - 1)
    def _():
        x2 = x_ref[...] + g_ref[0] * acc_ref[...]
        y = x2 * lax.rsqrt(jnp.mean(x2 * x2, axis=-1, keepdims=True) + RMS_EPS)
        o_ref[...] = y * gain_ref[...]


def _ffn_down(act, w_down, x1, g2, final_gain, seq, tm=512, tk=1408):
    t, d_ff = act.shape
    d = w_down.shape[1]
    tm = min(tm, seq)
    per_batch = seq // tm
    return pl.pallas_call(
        _ffn_down_kernel,
        grid=(t // tm, d_ff // tk),
        in_specs=[pl.BlockSpec((tm, tk), lambda i, k: (i, k)),
                  pl.BlockSpec((tk, d), lambda i, k: (k, 0)),
                  pl.BlockSpec((tm, d), lambda i, k: (i, 0)),
                  pl.BlockSpec((1, 1, d), lambda i, k: (i // per_batch, 0, 0)),
                  pl.BlockSpec((1, d), lambda i, k: (0, 0))],
        out_specs=pl.BlockSpec((tm, d), lambda i, k: (i, 0)),
        out_shape=jax.ShapeDtypeStruct((t, d), F32),
        scratch_shapes=[pltpu.VMEM((tm, d), F32)],
        compiler_params=_params("parallel", "arbitrary"),
        name="ffn_down_final_norm",
    )(act, w_down, x1, g2, final_gain)


def _rope_tables(positions_flat, rot, period):
    half = rot // 2
    assert period % half == 0 and LANES % half == 0
    inv_freq = jnp.power(ROPE_THETA, -jnp.arange(0, rot, 2, dtype=F32) / rot)
    ang = positions_flat.astype(F32)[:, None] * inv_freq[None, :]
    reps = (1, LANES // half)
    return jnp.tile(jnp.cos(ang), reps), jnp.tile(jnp.sin(ang), reps)


def _layer(x2d, mod, positions_flat, batch, seq, norm1_gain, norm2_gain, w_in, gla_gate_up,
           gla_gate_bias, gla_norm_gain, w_branch_gla, w_branch_dsa, w_merge_out, w_ffn_gate_up,
           w_ffn_down, final_gain):
    d = x2d.shape[1]
    sh1, sc1, g1, sh2, sc2, g2 = [mod[:, i * d:(i + 1) * d][:, None, :] for i in range(N_MOD)]

    plan = _in_proj_plan(d)
    tile_of = plan[3]
    cos_d, sin_d = _rope_tables(positions_flat, DSA_HEAD_DIM // ROPE_FRACTION, DSA_HEAD_DIM)
    cos_i, sin_i = _rope_tables(positions_flat, IDX_DIM // ROPE_FRACTION, IDX_DIM)
    h = _norm_modulate(x2d, norm1_gain[None, :], sc1, sh1, seq)
    act32, act16 = _in_project(h, w_in.T, (cos_d, sin_d, cos_i, sin_i), plan)

    o_a = _gla(act32, act16, tile_of, gla_gate_up, gla_gate_bias[None, :], gla_norm_gain[None, :],
               batch, seq, rows=min(256, seq))

    iw_col = tile_of["i_k"] * IN_PROJ_TN + IDX_DIM
    iw_t = jnp.transpose(act32[:, iw_col:iw_col + IDX_HEADS].reshape(batch, seq, IDX_HEADS), (0, 2, 1))
    topk = min(IDX_TOPK_MAX, seq // 4)
    bias_t = _topk_bias(act32, tile_of, iw_t, batch, seq, topk)
    o_b = _attention(act16, tile_of, bias_t, batch, seq)

    merged = _merge(o_a, o_b, w_branch_gla, w_branch_dsa, act32, tile_of)
    x1, h2 = _mixer_out(merged, w_merge_out.astype(BF16), x2d, g1, norm2_gain[None, :], sc2, sh2, seq)
    act = _ffn_up(h2, w_ffn_gate_up)
    return _ffn_down(act, w_ffn_down.astype(BF16), x1, g2, final_gain[None, :], seq)


def kernel(x, c, positions, norm1_gain, norm2_gain, w_ada, b_ada, w_in, gla_gate_up, gla_gate_bias,
           gla_norm_gain, w_branch_gla, w_branch_dsa, w_merge_out, w_ffn_gate_up, w_ffn_down,
           final_norm_gain):
    batch, seq, d = x.shape
    depth = w_in.shape[0]
    assert depth == 1, "the final RMSNorm is fused into the single layer's FFN kernel"
    x2d = x.reshape(batch * seq, d)
    c_pad = jnp.zeros((SUBLANES, d), F32).at[:batch].set(c)
    mod = _modulation(c_pad, w_ada[0], b_ada[0][None, :])[:batch]
    out = _layer(x2d, mod, positions.reshape(-1), batch, seq, norm1_gain[0], norm2_gain[0], w_in[0],
                 gla_gate_up[0], gla_gate_bias[0], gla_norm_gain[0], w_branch_gla[0], w_branch_dsa[0],
                 w_merge_out[0], w_ffn_gate_up[0], w_ffn_down[0], final_norm_gain)
    return out.reshape(batch, seq, d)
```

```python
import functools

import jax
import jax.numpy as jnp
from jax import lax
from jax.experimental import pallas as pl
from jax.experimental.pallas import tpu as pltpu

F32 = jnp.float32
BF16 = jnp.bfloat16
HIGHEST = lax.Precision.HIGHEST

RMS_EPS = 1e-6
GLA_HEADS = 4
GLA_GATE_RANK = 16
GLA_TAU = 16.0
GLA_CHUNK = 64
GLA_SUB = 8
DSA_HEADS = 16
DSA_HEAD_DIM = 128
IDX_HEADS = 8
IDX_DIM = 64
IDX_TOPK_MAX = 256
ROPE_THETA = 500000.0
ROPE_FRACTION = 4
N_MOD = 6
LOG2_E = 1.4426950408889634

LANES = 128
SUBLANES = 8
VMEM_LIMIT = 48 * 1024 * 1024


VMEM_LIMIT_LARGE = 56 * 1024 * 1024


def _params(*semantics, vmem_limit=VMEM_LIMIT):
    return pltpu.CompilerParams(dimension_semantics=semantics, vmem_limit_bytes=vmem_limit)


def _silu(x):
    return x * jax.nn.sigmoid(x)


def _mod_kernel(c_ref, w_ref, b_ref, o_ref):
    a = _silu(c_ref[...])
    o_ref[...] = jnp.dot(a, w_ref[...], precision=HIGHEST, preferred_element_type=F32) + b_ref[...]


def _modulation(c_pad, w_ada, b_ada, tn=2048):
    rows, d = c_pad.shape
    n = w_ada.shape[1]
    return pl.pallas_call(
        _mod_kernel,
        grid=(n // tn,),
        in_specs=[pl.BlockSpec((rows, d), lambda j: (0, 0)),
                  pl.BlockSpec((d, tn), lambda j: (0, j)),
                  pl.BlockSpec((1, tn), lambda j: (0, j))],
        out_specs=pl.BlockSpec((rows, tn), lambda j: (0, j)),
        out_shape=jax.ShapeDtypeStruct((rows, n), F32),
        compiler_params=_params("parallel"),
        name="adaln_mod",
    )(c_pad, w_ada, b_ada)


def _rms_modulate(x, gain, scale, shift):
    y = x * lax.rsqrt(jnp.mean(x * x, axis=-1, keepdims=True) + RMS_EPS)
    return (y * gain) * (1.0 + scale) + shift


def _rope_lanes(x, cos, sin, half, period, limit=LANES):
    lane = lax.broadcasted_iota(jnp.int32, x.shape, 1)
    in_head = lane % period
    upper = pltpu.roll(x, LANES - half, axis=1)
    lower = pltpu.roll(x, half, axis=1)
    live = lane < limit
    first = jnp.logical_and(live, in_head < half)
    second = jnp.logical_and(live, jnp.logical_and(in_head >= half, in_head < 2 * half))
    return jnp.where(first, x * cos - upper * sin,
                     jnp.where(second, lower * sin + x * cos, x))


PLAIN32, GLA_Q32, IDX_Q32, IDX_K32, PLAIN16, DSA_ROPE16 = range(6)
IN_PROJ_ROPE_ROWS = 256
IN_PROJ_TN = 512


def _norm_mod_kernel(x_ref, gain_ref, sc_ref, sh_ref, o_ref):
    o_ref[...] = _rms_modulate(x_ref[...], gain_ref[...], sc_ref[0], sh_ref[0]).astype(o_ref.dtype)


def _norm_modulate(x2d, gain, sc, sh, seq, tm=512):
    t, d = x2d.shape
    per_batch = seq // tm
    return pl.pallas_call(
        _norm_mod_kernel,
        grid=(t // tm,),
        in_specs=[pl.BlockSpec((tm, d), lambda i: (i, 0)),
                  pl.BlockSpec((1, d), lambda i: (0, 0)),
                  pl.BlockSpec((1, 1, d), lambda i: (i // per_batch, 0, 0)),
                  pl.BlockSpec((1, 1, d), lambda i: (i // per_batch, 0, 0))],
        out_specs=pl.BlockSpec((tm, d), lambda i: (i, 0)),
        out_shape=jax.ShapeDtypeStruct((t, d), BF16),
        compiler_params=_params("parallel"),
        name="norm1_modulate",
    )(x2d, gain, sc, sh)


def _in_proj_kernel(off_ref, kind_ref, h_ref, wt_ref, cos_d_ref, sin_d_ref, cos_i_ref, sin_i_ref,
                    o32_ref, o16_ref, acc_ref):
    kind = kind_ref[pl.program_id(1)]
    tm, tn = o32_ref.shape
    n_groups = tn // LANES
    rot_d = DSA_HEAD_DIM // ROPE_FRACTION
    rot_i = IDX_DIM // ROPE_FRACTION

    def product():
        return lax.dot_general(h_ref[...], wt_ref[...].astype(BF16), (((1,), (1,)), ((), ())),
                               preferred_element_type=F32)

    def rope_rows(cos_ref, sin_ref, out_ref, half, period, groups, pre_scale=1.0, limit=LANES):
        def chunk(r, carry):
            rows = pl.ds(pl.multiple_of(r * IN_PROJ_ROPE_ROWS, IN_PROJ_ROPE_ROWS), IN_PROJ_ROPE_ROWS)
            cos = cos_ref[rows, :]
            sin = sin_ref[rows, :]
            for g in groups:
                cols = slice(g * LANES, (g + 1) * LANES)
                x = acc_ref[rows, cols]
                if pre_scale != 1.0:
                    x = x * pre_scale
                out_ref[rows, cols] = _rope_lanes(x, cos, sin, half, period, limit).astype(out_ref.dtype)
            return carry
        lax.fori_loop(0, tm // IN_PROJ_ROPE_ROWS, chunk, 0)

    @pl.when(kind == PLAIN32)
    def _():
        o32_ref[...] = product()

    @pl.when(kind == GLA_Q32)
    def _():
        o32_ref[...] = product() * ((wt_ref.shape[1] // 2 // GLA_HEADS) ** -0.5)

    @pl.when(kind == IDX_Q32)
    def _():
        acc_ref[...] = product()
        rope_rows(cos_i_ref, sin_i_ref, o32_ref, rot_i // 2, IDX_DIM, range(n_groups), pre_scale=IDX_DIM ** -0.5)

    @pl.when(kind == IDX_K32)
    def _():
        acc = product()
        acc_ref[...] = acc
        o32_ref[...] = acc
        rope_rows(cos_i_ref, sin_i_ref, o32_ref, rot_i // 2, IDX_DIM, range(1), limit=IDX_DIM)

    @pl.when(kind == PLAIN16)
    def _():
        o16_ref[...] = product().astype(BF16)

    @pl.when(kind == DSA_ROPE16)
    def _():
        acc_ref[...] = product()
        rope_rows(cos_d_ref, sin_d_ref, o16_ref, rot_d // 2, DSA_HEAD_DIM, range(n_groups))


def _in_proj_plan(d):
    tn = IN_PROJ_TN
    gla_qk, dsa_w, idx_w = d // 2, DSA_HEADS * DSA_HEAD_DIM, IDX_HEADS * IDX_DIM
    names = ("g_q", "g_k", "g_v", "g_r", "g_lr", "d_q", "d_k", "d_v", "i_q", "i_k", "i_w", "gate_a", "gate_b")
    widths = (gla_qk, gla_qk, d, d, GLA_GATE_RANK, dsa_w, dsa_w, dsa_w, idx_w, IDX_DIM, IDX_HEADS, d, d)
    start, pos = {}, 0
    for nm, wd in zip(names, widths):
        start[nm] = pos
        pos += wd
    width = dict(zip(names, widths))
    assert start["i_w"] == start["i_k"] + IDX_DIM and IDX_DIM + IDX_HEADS <= LANES
    f32_groups = (("g_q", GLA_Q32), ("g_k", PLAIN32), ("g_r", PLAIN32), ("g_lr", PLAIN32), ("i_q", IDX_Q32),
                  ("i_k", IDX_K32), ("gate_a", PLAIN32), ("gate_b", PLAIN32))
    bf16_groups = (("g_v", PLAIN16), ("d_q", DSA_ROPE16), ("d_k", DSA_ROPE16), ("d_v", PLAIN16))
    offsets, kinds, tile_of = [], [], {}
    for groups in (f32_groups, bf16_groups):
        base = len(offsets)
        for nm, kind in groups:
            tile_of[nm] = len(offsets) - base
            for t in range(-(-width[nm] // tn)):
                offsets.append(start[nm] + t * tn)
                kinds.append(kind)
        if groups is f32_groups:
            n32 = len(offsets)
    assert all(o % SUBLANES == 0 and o + tn <= pos for o in offsets)
    return offsets, kinds, n32, tile_of


def _in_project(h, w_in_t, tables, plan, tm=2048):
    t, d = h.shape
    tn = IN_PROJ_TN
    offsets, kinds, n32, _ = plan
    n16 = len(offsets) - n32
    tm = min(tm, t)
    once = pl.Buffered(1)
    table_spec = pl.BlockSpec((tm, LANES), lambda i, j, off, kind: (i, 0), pipeline_mode=once)
    grid_spec = pltpu.PrefetchScalarGridSpec(
        num_scalar_prefetch=2,
        grid=(t // tm, len(offsets)),
        in_specs=[pl.BlockSpec((tm, d), lambda i, j, off, kind: (i, 0), pipeline_mode=once),
                  pl.BlockSpec((pl.Element(tn), pl.Element(d)),
                               lambda i, j, off, kind: (pl.multiple_of(off[j], SUBLANES), 0)),
                  table_spec, table_spec, table_spec, table_spec],
        out_specs=[pl.BlockSpec((tm, tn), lambda i, j, off, kind: (i, jnp.minimum(j, n32 - 1))),
                   pl.BlockSpec((tm, tn), lambda i, j, off, kind: (i, jnp.maximum(j - n32, 0)))],
        scratch_shapes=[pltpu.VMEM((tm, tn), F32)],
    )
    return pl.pallas_call(
        _in_proj_kernel,
        grid_spec=grid_spec,
        out_shape=[jax.ShapeDtypeStruct((t, n32 * tn), F32), jax.ShapeDtypeStruct((t, n16 * tn), BF16)],
        compiler_params=_params("parallel", "arbitrary"),
        name="in_proj",
    )(jnp.asarray(offsets, jnp.int32), jnp.asarray(kinds, jnp.int32), h, w_in_t, *tables)


def _split3_bf16(x):
    hi, rest = _split_bf16(x)
    mid, lo = _split_bf16(rest)
    return hi, mid, lo


def _gla_kernel(q_ref, k_ref, v_ref, gr_ref, sm_ref, gup_ref, gb_ref, ng_ref, tri_ref, o_ref, st_ref):
    @pl.when(pl.program_id(2) == 0)
    def _():
        st_ref[...] = jnp.zeros_like(st_ref)

    rows = q_ref.shape[0]
    dk = q_ref.shape[1]
    c, sub = GLA_CHUNK, GLA_SUB
    n_sub = c // sub

    a_hi, a_lo = _split_bf16(sm_ref[:, 0:GLA_GATE_RANK])
    gate_lhs = jnp.concatenate(
        [a_hi, a_lo, a_hi, jnp.zeros((rows, LANES - 3 * GLA_GATE_RANK), F32)], axis=1).astype(BF16)
    z = jnp.dot(gate_lhs, gup_ref[...], preferred_element_type=F32) + gb_ref[...]
    log_g = (jnp.minimum(z, 0.0) - jnp.log(1.0 + jnp.exp(-jnp.abs(z)))) * (LOG2_E / GLA_TAU)
    parts = jnp.dot(tri_ref[...], jnp.concatenate(_split3_bf16(log_g), axis=1).astype(BF16),
                    preferred_element_type=F32)
    b = (parts[:, 0:dk] + parts[:, dk:2 * dk]) + parts[:, 2 * dk:3 * dk]

    q = q_ref[...]
    k = k_ref[...]

    k_tiles = k.reshape(rows // sub, sub, dk)
    b_tiles = b.reshape(rows // sub, sub, dk)
    row = lax.broadcasted_iota(jnp.int32, (rows, 1), 0)
    row_in_sub = row % sub
    col = lax.broadcasted_iota(jnp.int32, (rows, c), 1)
    target = lax.broadcasted_iota(jnp.int32, (rows, c), 0) % c
    a_diag = jnp.zeros((rows, c), F32)
    for delta in range(sub):
        if delta == 0:
            k_d, b_d = k, b
        else:
            k_d = pltpu.roll(k_tiles, delta, axis=1).reshape(rows, dk)
            b_d = pltpu.roll(b_tiles, delta, axis=1).reshape(rows, dk)
        decay = jnp.exp2(jnp.where(row_in_sub >= delta, b - b_d, -jnp.inf))
        a = jnp.sum(q * k_d * decay, axis=-1, keepdims=True)
        a_diag = jnp.where(col == target - delta, a, a_diag)

    for ch in range(rows // c):
        base = ch * c
        qc = q[base:base + c]
        kc = k[base:base + c]
        bc = b[base:base + c]
        vc = v_ref[base:base + c, :]
        st = st_ref[...]

        o_inter = lax.dot_general((qc * jnp.exp2(bc)).astype(BF16), st.astype(BF16),
                                  (((1,), (1,)), ((), ())), preferred_element_type=F32)

        blocks = [jnp.zeros((sub, c), F32)]
        for i_sub in range(1, n_sub):
            lo = i_sub * sub
            ref = bc[lo - 1:lo]
            q_hat = qc[lo:lo + sub] * jnp.exp2(bc[lo:lo + sub] - ref)
            k_hat = kc[0:lo] * jnp.exp2(ref - bc[0:lo])
            k_pad = jnp.concatenate([k_hat, jnp.zeros((c - lo, dk), F32)], axis=0)
            blocks.append(lax.dot_general(q_hat.astype(BF16), k_pad.astype(BF16),
                                          (((1,), (1,)), ((), ())), preferred_element_type=F32))
        attn = jnp.concatenate(blocks, axis=0) + a_diag[base:base + c]

        o = jnp.dot(attn.astype(BF16), vc, preferred_element_type=F32) + o_inter

        b_last = bc[c - 1:c]
        k_dec = (kc * jnp.exp2(b_last - bc)).astype(BF16)
        st_ref[...] = st * jnp.exp2(b_last) + lax.dot_general(
            vc, k_dec, (((0,), (0,)), ((), ())), preferred_element_type=F32)

        o = o * lax.rsqrt(jnp.mean(o * o, axis=-1, keepdims=True) + RMS_EPS) * ng_ref[...]
        o_ref[base:base + c, :] = (o * _silu(gr_ref[base:base + c, :])).astype(o_ref.dtype)


def _gla(act32, act16, tile_of, gate_up, gbias, ngain, batch, seq, rows=256):
    dk = gate_up.shape[1] // GLA_HEADS
    dv = ngain.shape[1] // GLA_HEADS
    t = act32.shape[0]
    nr = seq // rows
    g_hi, g_lo = _split_bf16(gate_up)
    gup_stack = jnp.concatenate(
        [g_hi, g_hi, g_lo, jnp.zeros((LANES - 3 * GLA_GATE_RANK, gate_up.shape[1]), F32)], axis=0).astype(BF16)
    idx = jnp.arange(rows)
    tri = jnp.logical_and(idx[:, None] // GLA_CHUNK == idx[None, :] // GLA_CHUNK,
                          idx[None, :] <= idx[:, None]).astype(BF16)

    def cols(group, width):
        first = tile_of[group] * IN_PROJ_TN // width
        return lambda b, h, r: (b * nr + r, first + h)

    lr_block = tile_of["g_lr"] * IN_PROJ_TN // LANES
    return pl.pallas_call(
        _gla_kernel,
        grid=(batch, GLA_HEADS, nr),
        in_specs=[pl.BlockSpec((rows, dk), cols("g_q", dk)),
                  pl.BlockSpec((rows, dk), cols("g_k", dk)),
                  pl.BlockSpec((rows, dv), cols("g_v", dv)),
                  pl.BlockSpec((rows, dv), cols("g_r", dv)),
                  pl.BlockSpec((rows, LANES), lambda b, h, r: (b * nr + r, lr_block)),
                  pl.BlockSpec((LANES, dk), lambda b, h, r: (0, h)),
                  pl.BlockSpec((1, dk), lambda b, h, r: (0, h)),
                  pl.BlockSpec((1, dv), lambda b, h, r: (0, h)),
                  pl.BlockSpec((rows, rows), lambda b, h, r: (0, 0))],
        out_specs=pl.BlockSpec((rows, dv), lambda b, h, r: (b * nr + r, h)),
        out_shape=jax.ShapeDtypeStruct((t, GLA_HEADS * dv), BF16),
        scratch_shapes=[pltpu.VMEM((dv, dk), F32)],
        compiler_params=_params("parallel", "parallel", "arbitrary"),
        name="gla",
    )(act32, act32, act16, act32, act32, gup_stack, gbias, ngain, tri)


def _split_bf16(x):
    hi = x.astype(BF16).astype(F32)
    return hi, x - hi


def _sortable_bits_to_float(key):
    return lax.bitcast_convert_type(jnp.where(key < 0, key ^ jnp.int32(0x7FFFFFFF), key), F32)


SORT_KEY_NEG_INF = -2139095041


def _topk_bias_kernel(iq_ref, ik_ref, iw_ref, tri_ref, o_ref, score_ref, rhs_ref, cnt_ref, *, topk, tq):
    qi = pl.program_id(1)
    tk = tq
    nk = ik_ref.shape[0] // tk
    n_live = qi + 1
    w = iw_ref[0] * (IDX_HEADS ** -0.5)
    pad_q = jnp.zeros((tq, IDX_DIM), F32)
    for h in range(IDX_HEADS):
        hi, lo = _split_bf16(iq_ref[:, h * IDX_DIM:(h + 1) * IDX_DIM])
        rhs_ref[h] = jnp.concatenate([hi, hi, lo, pad_q], axis=1).astype(BF16)
    t_pos = qi * tq + lax.broadcasted_iota(jnp.int32, (tk, tq), 1)
    s_off = lax.broadcasted_iota(jnp.int32, (tk, tq), 0)
    pad_k = jnp.zeros((tk, IDX_DIM), F32)

    def block_rows(kb):
        return pl.ds(pl.multiple_of(kb * tk, tk), tk)

    def score_block(kb, carry):
        hi, lo = _split_bf16(ik_ref[block_rows(kb), 0:IDX_DIM])
        lhs = jnp.concatenate([hi, lo, hi, pad_k], axis=1).astype(BF16)
        score = jnp.zeros((tk, tq), F32)
        for h in range(IDX_HEADS):
            dots = lax.dot_general(lhs, rhs_ref[h], (((1,), (1,)), ((), ())), preferred_element_type=F32)
            score = score + w[h:h + 1, :] * jnp.maximum(dots, 0.0)
        score_ref[block_rows(kb), :] = jnp.where(kb * tk + s_off <= t_pos, score + 0.0, -jnp.inf)
        return carry

    lax.fori_loop(0, n_live, score_block, 0)

    def count(pred):
        for n in range(1, nk + 1):
            @pl.when(n_live == n)
            def _(n=n):
                part = jnp.zeros((SUBLANES, tq), F32)
                for kb in range(n):
                    hit = jnp.where(pred(score_ref[kb * tk:(kb + 1) * tk, :]), 1.0, 0.0)
                    part = part + jnp.sum(hit.reshape(tk // SUBLANES, SUBLANES, tq), axis=0)
                cnt_ref[...] = part
        return jnp.sum(cnt_ref[...], axis=0, keepdims=True)

    def search(i, ans):
        cand = ans ^ jnp.left_shift(jnp.int32(1), 31 - i)
        cand_f = _sortable_bits_to_float(cand)
        cnt = count(lambda s: s >= cand_f)
        accept = jnp.logical_or(cnt >= float(topk), cand < SORT_KEY_NEG_INF)
        return jnp.where(accept, cand, ans)

    thr = _sortable_bits_to_float(
        lax.fori_loop(0, 32, search, jnp.full((1, tq), jnp.iinfo(jnp.int32).min, jnp.int32)))
    need = float(topk) - count(lambda s: s > thr)

    def emit(kb, seen):
        blk = score_ref[block_rows(kb), :]
        eq = jnp.where(blk == thr, 1.0, 0.0)
        rank = jnp.dot(tri_ref[...], eq.astype(BF16), preferred_element_type=F32) + seen
        chosen = jnp.logical_or(blk > thr, jnp.logical_and(blk == thr, rank <= need))
        o_ref[0, block_rows(kb), :] = jnp.where(
            jnp.logical_and(chosen, kb * tk + s_off <= t_pos), 0.0, -jnp.inf)
        return seen + jnp.sum(eq, axis=0, keepdims=True)

    lax.fori_loop(0, n_live, emit, jnp.zeros((1, tq), F32))

    def fill(kb, carry):
        o_ref[0, block_rows(kb), :] = jnp.full((tk, tq), -jnp.inf, F32)
        return carry

    lax.fori_loop(n_live, nk, fill, 0)


def _topk_bias(act32, tile_of, iw_t, batch, seq, topk, tq=256):
    tq = min(tq, seq)
    nq = seq // tq
    width = IDX_HEADS * IDX_DIM
    iq_block = tile_of["i_q"] * IN_PROJ_TN // width
    ik_block = tile_of["i_k"] * IN_PROJ_TN // LANES
    tri = jnp.tril(jnp.ones((tq, tq), BF16))
    return pl.pallas_call(
        functools.partial(_topk_bias_kernel, topk=topk, tq=tq),
        grid=(batch, nq),
        in_specs=[pl.BlockSpec((tq, width), lambda b, i: (b * nq + i, iq_block)),
                  pl.BlockSpec((seq, LANES), lambda b, i: (b, ik_block)),
                  pl.BlockSpec((1, IDX_HEADS, tq), lambda b, i: (b, 0, i)),
                  pl.BlockSpec((tq, tq), lambda b, i: (0, 0))],
        out_specs=pl.BlockSpec((1, seq, tq), lambda b, i: (b, 0, i)),
        out_shape=jax.ShapeDtypeStruct((batch, seq, seq), F32),
        scratch_shapes=[pltpu.VMEM((seq, tq), F32),
                        pltpu.VMEM((IDX_HEADS, tq, 4 * IDX_DIM), BF16),
                        pltpu.VMEM((SUBLANES, tq), F32)],
        compiler_params=_params("parallel", "parallel"),
        name="indexer_topk_bias",
    )(act32, act32, iw_t, tri)


ATTN_HEADS_PER_STEP = 8


def _attn_kernel(q_ref, k_ref, v_ref, bias_ref, o_ref, acc_ref):
    qi = pl.program_id(2)
    tq = q_ref.shape[0]
    tk = tq
    dh = DSA_HEAD_DIM
    group = q_ref.shape[1] // dh
    scale2 = (dh ** -0.5) * LOG2_E
    acc_ref[...] = jnp.zeros_like(acc_ref)

    def body(kj, carry):
        rows = pl.ds(pl.multiple_of(kj * tk, tk), tk)
        bias = bias_ref[0, rows, :]
        heads = [slice(g * dh, (g + 1) * dh) for g in range(group)]
        logits = [lax.dot_general(k_ref[rows, cols], q_ref[:, cols], (((1,), (1,)), ((), ())),
                                  preferred_element_type=F32) for cols in heads]
        new = []
        for g in range(group):
            m, l = carry[g]
            s = logits[g] * scale2 + bias
            m_new = jnp.maximum(m, jnp.max(s, axis=0, keepdims=True))
            m_safe = jnp.where(m_new == -jnp.inf, 0.0, m_new)
            alpha = jnp.exp2(m - m_safe)
            p = jnp.exp2(s - m_safe)
            new.append((m_new, alpha * l + jnp.sum(p, axis=0, keepdims=True)))
            update = lax.dot_general(v_ref[rows, heads[g]], p.astype(BF16), (((0,), (0,)), ((), ())),
                                     preferred_element_type=F32)
            acc_ref[g] = alpha * acc_ref[g] + update
        return tuple(new)

    init = tuple((jnp.full((1, tq), -jnp.inf, F32), jnp.zeros((1, tq), F32)) for _ in range(group))
    final = lax.fori_loop(0, qi + 1, body, init)
    for g in range(group):
        o_ref[:, g * dh:(g + 1) * dh] = (acc_ref[g] / final[g][1]).T.astype(o_ref.dtype)


def _attention(act16, tile_of, bias_t, batch, seq, tq=256):
    tq = min(tq, seq)
    nq = seq // tq
    width = ATTN_HEADS_PER_STEP * DSA_HEAD_DIM
    n_groups = DSA_HEADS // ATTN_HEADS_PER_STEP
    t = act16.shape[0]
    q_block, k_block, v_block = (tile_of[nm] * IN_PROJ_TN // width for nm in ("d_q", "d_k", "d_v"))
    return pl.pallas_call(
        _attn_kernel,
        grid=(batch, n_groups, nq),
        in_specs=[pl.BlockSpec((tq, width), lambda b, h, i: (b * nq + i, q_block + h)),
                  pl.BlockSpec((seq, width), lambda b, h, i: (b, k_block + h)),
                  pl.BlockSpec((seq, width), lambda b, h, i: (b, v_block + h)),
                  pl.BlockSpec((1, seq, tq), lambda b, h, i: (b, 0, i))],
        out_specs=pl.BlockSpec((tq, width), lambda b, h, i: (b * nq + i, h)),
        out_shape=jax.ShapeDtypeStruct((t, DSA_HEADS * DSA_HEAD_DIM), BF16),
        scratch_shapes=[pltpu.VMEM((ATTN_HEADS_PER_STEP, DSA_HEAD_DIM, tq), F32)],
        compiler_params=_params("parallel", "parallel", "arbitrary"),
        name="dsa_attention",
    )(act16, act16, act16, bias_t)


def _merge_kernel(oa_ref, ob_ref, wa_ref, wb_ref, ga_ref, gb_ref, o_ref):
    ya = jnp.dot(oa_ref[...], wa_ref[...].astype(BF16), preferred_element_type=F32)
    yb = jnp.dot(ob_ref[...], wb_ref[...].astype(BF16), preferred_element_type=F32)
    o_ref[...] = (jax.nn.sigmoid(ga_ref[...]) * ya + jax.nn.sigmoid(gb_ref[...]) * yb).astype(o_ref.dtype)


def _merge(o_a, o_b, w_a, w_b, act32, tile_of, tm=1024):
    t, d = o_a.shape
    n = w_a.shape[1]
    tn = IN_PROJ_TN
    tm = min(tm, t)
    ga_tile, gb_tile = tile_of["gate_a"], tile_of["gate_b"]
    return pl.pallas_call(
        _merge_kernel,
        grid=(t // tm, n // tn),
        in_specs=[pl.BlockSpec((tm, d), lambda i, j: (i, 0)),
                  pl.BlockSpec((tm, d), lambda i, j: (i, 0)),
                  pl.BlockSpec((d, tn), lambda i, j: (0, j)),
                  pl.BlockSpec((d, tn), lambda i, j: (0, j)),
                  pl.BlockSpec((tm, tn), lambda i, j: (i, ga_tile + j)),
                  pl.BlockSpec((tm, tn), lambda i, j: (i, gb_tile + j))],
        out_specs=pl.BlockSpec((tm, tn), lambda i, j: (i, j)),
        out_shape=jax.ShapeDtypeStruct((t, n), BF16),
        compiler_params=_params("parallel", "parallel"),
        name="branch_merge",
    )(o_a, o_b, w_a, w_b, act32, act32)


def _mixer_out_kernel(m_ref, w_ref, x_ref, g_ref, gain_ref, sc_ref, sh_ref, x1_ref, h2_ref):
    y = jnp.dot(m_ref[...], w_ref[...], preferred_element_type=F32)
    x1 = x_ref[...] + g_ref[0] * y
    x1_ref[...] = x1
    h2_ref[...] = _rms_modulate(x1, gain_ref[...], sc_ref[0], sh_ref[0]).astype(h2_ref.dtype)


def _mixer_out(merged, w, x2d, g1, gain2, sc2, sh2, seq, tm=256):
    t, d = x2d.shape
    tm = min(tm, seq)
    per_batch = seq // tm
    vec = pl.BlockSpec((1, 1, d), lambda i: (i // per_batch, 0, 0))
    row = pl.BlockSpec((tm, d), lambda i: (i, 0))
    return pl.pallas_call(
        _mixer_out_kernel,
        grid=(t // tm,),
        in_specs=[row, pl.BlockSpec((d, d), lambda i: (0, 0)), row, vec,
                  pl.BlockSpec((1, d), lambda i: (0, 0)), vec, vec],
        out_specs=[row, row],
        out_shape=[jax.ShapeDtypeStruct((t, d), F32), jax.ShapeDtypeStruct((t, d), BF16)],
        compiler_params=_params("parallel"),
        name="mixer_out_norm2",
    )(merged, w, x2d, g1, gain2, sc2, sh2)


def _ffn_up_kernel(h_ref, wg_ref, wu_ref, o_ref):
    g = jnp.dot(h_ref[...], wg_ref[...].astype(BF16), preferred_element_type=F32)
    u = jnp.dot(h_ref[...], wu_ref[...].astype(BF16), preferred_element_type=F32)
    o_ref[...] = (_silu(g) * u).astype(o_ref.dtype)


def _ffn_up(h2, w_gate_up, tm=1024, tn=512):
    t, d = h2.shape
    d_ff = w_gate_up.shape[1] // 2
    tm = min(tm, t)
    nj = d_ff // tn
    return pl.pallas_call(
        _ffn_up_kernel,
        grid=(t // tm, nj),
        in_specs=[pl.BlockSpec((tm, d), lambda i, j: (i, 0)),
                  pl.BlockSpec((d, tn), lambda i, j: (0, j)),
                  pl.BlockSpec((d, tn), lambda i, j: (0, nj + j))],
        out_specs=pl.BlockSpec((tm, tn), lambda i, j: (i, j)),
        out_shape=jax.ShapeDtypeStruct((t, d_ff), BF16),
        compiler_params=_params("parallel", "parallel"),
        name="ffn_up",
    )(h2, w_gate_up, w_gate_up)


def _ffn_down_kernel(a_ref, w_ref, x_ref, g_ref, gain_ref, o_ref):
    kk = pl.program_id(1)

    @pl.when(kk == 0)
    def _():
        o_ref[...] = jnp.zeros_like(o_ref)

    o_ref[...] += jnp.dot(a_ref[...], w_ref[...], preferred_element_type=F32)

    @pl.when(kk == pl.num_programs(1) - 1)
    def _():
        x2 = x_ref[...] + g_ref[0] * o_ref[...]
        y = x2 * lax.rsqrt(jnp.mean(x2 * x2, axis=-1, keepdims=True) + RMS_EPS)
        o_ref[...] = y * gain_ref[...]


def _ffn_down(act, w_down, x1, g2, final_gain, seq, tm=1024, tk=1408):
    t, d_ff = act.shape
    d = w_down.shape[1]
    tm = min(tm, seq)
    per_batch = seq // tm
    return pl.pallas_call(
        _ffn_down_kernel,
        grid=(t // tm, d_ff // tk),
        in_specs=[pl.BlockSpec((tm, tk), lambda i, k: (i, k)),
                  pl.BlockSpec((tk, d), lambda i, k: (k, 0)),
                  pl.BlockSpec((tm, d), lambda i, k: (i, 0), pipeline_mode=pl.Buffered(1)),
                  pl.BlockSpec((1, 1, d), lambda i, k: (i // per_batch, 0, 0)),
                  pl.BlockSpec((1, d), lambda i, k: (0, 0))],
        out_specs=pl.BlockSpec((tm, d), lambda i, k: (i, 0)),
        out_shape=jax.ShapeDtypeStruct((t, d), F32),
        compiler_params=_params("parallel", "arbitrary", vmem_limit=VMEM_LIMIT_LARGE),
        name="ffn_down_final_norm",
    )(act, w_down, x1, g2, final_gain)


def _rope_tables(positions_flat, rot, period):
    half = rot // 2
    assert period % half == 0 and LANES % half == 0
    inv_freq = jnp.power(ROPE_THETA, -jnp.arange(0, rot, 2, dtype=F32) / rot)
    ang = positions_flat.astype(F32)[:, None] * inv_freq[None, :]
    reps = (1, LANES // half)
    return jnp.tile(jnp.cos(ang), reps), jnp.tile(jnp.sin(ang), reps)


def _layer(x2d, mod, positions_flat, batch, seq, norm1_gain, norm2_gain, w_in, gla_gate_up,
           gla_gate_bias, gla_norm_gain, w_branch_gla, w_branch_dsa, w_merge_out, w_ffn_gate_up,
           w_ffn_down, final_gain):
    d = x2d.shape[1]
    sh1, sc1, g1, sh2, sc2, g2 = [mod[:, i * d:(i + 1) * d][:, None, :] for i in range(N_MOD)]

    plan = _in_proj_plan(d)
    tile_of = plan[3]
    cos_d, sin_d = _rope_tables(positions_flat, DSA_HEAD_DIM // ROPE_FRACTION, DSA_HEAD_DIM)
    cos_i, sin_i = _rope_tables(positions_flat, IDX_DIM // ROPE_FRACTION, IDX_DIM)
    h = _norm_modulate(x2d, norm1_gain[None, :], sc1, sh1, seq)
    act32, act16 = _in_project(h, w_in.T, (cos_d, sin_d, cos_i, sin_i), plan)

    o_a = _gla(act32, act16, tile_of, gla_gate_up, gla_gate_bias[None, :], gla_norm_gain[None, :],
               batch, seq, rows=min(256, seq))

    iw_col = tile_of["i_k"] * IN_PROJ_TN + IDX_DIM
    iw_t = jnp.transpose(act32[:, iw_col:iw_col + IDX_HEADS].reshape(batch, seq, IDX_HEADS), (0, 2, 1))
    topk = min(IDX_TOPK_MAX, seq // 4)
    bias_t = _topk_bias(act32, tile_of, iw_t, batch, seq, topk)
    o_b = _attention(act16, tile_of, bias_t, batch, seq)

    merged = _merge(o_a, o_b, w_branch_gla, w_branch_dsa, act32, tile_of)
    x1, h2 = _mixer_out(merged, w_merge_out.astype(BF16), x2d, g1, norm2_gain[None, :], sc2, sh2, seq)
    act = _ffn_up(h2, w_ffn_gate_up)
    return _ffn_down(act, w_ffn_down.astype(BF16), x1, g2, final_gain[None, :], seq)


def kernel(x, c, positions, norm1_gain, norm2_gain, w_ada, b_ada, w_in, gla_gate_up, gla_gate_bias,
           gla_norm_gain, w_branch_gla, w_branch_dsa, w_merge_out, w_ffn_gate_up, w_ffn_down,
           final_norm_gain):
    batch, seq, d = x.shape
    depth = w_in.shape[0]
    assert depth == 1, "the final RMSNorm is fused into the single layer's FFN kernel"
    x2d = x.reshape(batch * seq, d)
    c_pad = jnp.zeros((SUBLANES, d), F32).at[:batch].set(c)
    mod = _modulation(c_pad, w_ada[0], b_ada[0][None, :])[:batch]
    out = _layer(x2d, mod, positions.reshape(-1), batch, seq, norm1_gain[0], norm2_gain[0], w_in[0],
                 gla_gate_up[0], gla_gate_bias[0], gla_norm_gain[0], w_branch_gla[0], w_branch_dsa[0],
                 w_merge_out[0], w_ffn_gate_up[0], w_ffn_down[0], final_norm_gain)
    return out.reshape(batch, seq, d)
```

```python
import functools

import jax
import jax.numpy as jnp
from jax import lax
from jax.experimental import pallas as pl
from jax.experimental.pallas import tpu as pltpu

F32 = jnp.float32
BF16 = jnp.bfloat16
HIGHEST = lax.Precision.HIGHEST

RMS_EPS = 1e-6
GLA_HEADS = 4
GLA_GATE_RANK = 16
GLA_TAU = 16.0
GLA_CHUNK = 64
GLA_SUB = 8
DSA_HEADS = 16
DSA_HEAD_DIM = 128
IDX_HEADS = 8
IDX_DIM = 64
IDX_TOPK_MAX = 256
ROPE_THETA = 500000.0
ROPE_FRACTION = 4
N_MOD = 6
LOG2_E = 1.4426950408889634

LANES = 128
SUBLANES = 8
VMEM_LIMIT = 48 * 1024 * 1024


VMEM_LIMIT_LARGE = 56 * 1024 * 1024


def _params(*semantics, vmem_limit=VMEM_LIMIT):
    return pltpu.CompilerParams(dimension_semantics=semantics, vmem_limit_bytes=vmem_limit)


def _silu(x):
    return x * jax.nn.sigmoid(x)


def _mod_kernel(c_ref, w_ref, b_ref, o_ref):
    a = _silu(c_ref[...])
    o_ref[...] = jnp.dot(a, w_ref[...], precision=HIGHEST, preferred_element_type=F32) + b_ref[...]


def _modulation(c_pad, w_ada, b_ada, tn=2048):
    rows, d = c_pad.shape
    n = w_ada.shape[1]
    return pl.pallas_call(
        _mod_kernel,
        grid=(n // tn,),
        in_specs=[pl.BlockSpec((rows, d), lambda j: (0, 0)),
                  pl.BlockSpec((d, tn), lambda j: (0, j)),
                  pl.BlockSpec((1, tn), lambda j: (0, j))],
        out_specs=pl.BlockSpec((rows, tn), lambda j: (0, j)),
        out_shape=jax.ShapeDtypeStruct((rows, n), F32),
        compiler_params=_params("parallel"),
        name="adaln_mod",
    )(c_pad, w_ada, b_ada)


def _rms_modulate(x, gain, scale, shift):
    y = x * lax.rsqrt(jnp.mean(x * x, axis=-1, keepdims=True) + RMS_EPS)
    return (y * gain) * (1.0 + scale) + shift


def _rope_lanes(x, cos, sin, half, period, limit=LANES):
    lane = lax.broadcasted_iota(jnp.int32, x.shape, 1)
    in_head = lane % period
    upper = pltpu.roll(x, LANES - half, axis=1)
    lower = pltpu.roll(x, half, axis=1)
    live = lane < limit
    first = jnp.logical_and(live, in_head < half)
    second = jnp.logical_and(live, jnp.logical_and(in_head >= half, in_head < 2 * half))
    return jnp.where(first, x * cos - upper * sin,
                     jnp.where(second, lower * sin + x * cos, x))


PLAIN32, GLA_Q32, IDX_Q32, IDX_K32, PLAIN16, DSA_ROPE16 = range(6)
IN_PROJ_ROPE_ROWS = 256
IN_PROJ_TN = 512


def _norm_mod_kernel(x_ref, gain_ref, sc_ref, sh_ref, o_ref):
    o_ref[...] = _rms_modulate(x_ref[...], gain_ref[...], sc_ref[0], sh_ref[0]).astype(o_ref.dtype)


def _norm_modulate(x2d, gain, sc, sh, seq, tm=512):
    t, d = x2d.shape
    per_batch = seq // tm
    return pl.pallas_call(
        _norm_mod_kernel,
        grid=(t // tm,),
        in_specs=[pl.BlockSpec((tm, d), lambda i: (i, 0)),
                  pl.BlockSpec((1, d), lambda i: (0, 0)),
                  pl.BlockSpec((1, 1, d), lambda i: (i // per_batch, 0, 0)),
                  pl.BlockSpec((1, 1, d), lambda i: (i // per_batch, 0, 0))],
        out_specs=pl.BlockSpec((tm, d), lambda i: (i, 0)),
        out_shape=jax.ShapeDtypeStruct((t, d), BF16),
        compiler_params=_params("parallel"),
        name="norm1_modulate",
    )(x2d, gain, sc, sh)


def _in_proj_kernel(off_ref, kind_ref, h_ref, wt_ref, cos_d_ref, sin_d_ref, cos_i_ref, sin_i_ref,
                    o32_ref, o16_ref, acc_ref):
    kind = kind_ref[pl.program_id(1)]
    tm, tn = o32_ref.shape
    n_groups = tn // LANES
    rot_d = DSA_HEAD_DIM // ROPE_FRACTION
    rot_i = IDX_DIM // ROPE_FRACTION

    def product():
        return lax.dot_general(h_ref[...], wt_ref[...].astype(BF16), (((1,), (1,)), ((), ())),
                               preferred_element_type=F32)

    def rope_rows(cos_ref, sin_ref, out_ref, half, period, groups, pre_scale=1.0, limit=LANES):
        def chunk(r, carry):
            rows = pl.ds(pl.multiple_of(r * IN_PROJ_ROPE_ROWS, IN_PROJ_ROPE_ROWS), IN_PROJ_ROPE_ROWS)
            cos = cos_ref[rows, :]
            sin = sin_ref[rows, :]
            for g in groups:
                cols = slice(g * LANES, (g + 1) * LANES)
                x = acc_ref[rows, cols]
                if pre_scale != 1.0:
                    x = x * pre_scale
                out_ref[rows, cols] = _rope_lanes(x, cos, sin, half, period, limit).astype(out_ref.dtype)
            return carry
        lax.fori_loop(0, tm // IN_PROJ_ROPE_ROWS, chunk, 0)

    @pl.when(kind == PLAIN32)
    def _():
        o32_ref[...] = product()

    @pl.when(kind == GLA_Q32)
    def _():
        o32_ref[...] = product() * ((wt_ref.shape[1] // 2 // GLA_HEADS) ** -0.5)

    @pl.when(kind == IDX_Q32)
    def _():
        acc_ref[...] = product()
        rope_rows(cos_i_ref, sin_i_ref, o32_ref, rot_i // 2, IDX_DIM, range(n_groups), pre_scale=IDX_DIM ** -0.5)

    @pl.when(kind == IDX_K32)
    def _():
        acc = product()
        acc_ref[...] = acc
        o32_ref[...] = acc
        rope_rows(cos_i_ref, sin_i_ref, o32_ref, rot_i // 2, IDX_DIM, range(1), limit=IDX_DIM)

    @pl.when(kind == PLAIN16)
    def _():
        o16_ref[...] = product().astype(BF16)

    @pl.when(kind == DSA_ROPE16)
    def _():
        acc_ref[...] = product()
        rope_rows(cos_d_ref, sin_d_ref, o16_ref, rot_d // 2, DSA_HEAD_DIM, range(n_groups))


def _in_proj_plan(d):
    tn = IN_PROJ_TN
    gla_qk, dsa_w, idx_w = d // 2, DSA_HEADS * DSA_HEAD_DIM, IDX_HEADS * IDX_DIM
    names = ("g_q", "g_k", "g_v", "g_r", "g_lr", "d_q", "d_k", "d_v", "i_q", "i_k", "i_w", "gate_a", "gate_b")
    widths = (gla_qk, gla_qk, d, d, GLA_GATE_RANK, dsa_w, dsa_w, dsa_w, idx_w, IDX_DIM, IDX_HEADS, d, d)
    start, pos = {}, 0
    for nm, wd in zip(names, widths):
        start[nm] = pos
        pos += wd
    width = dict(zip(names, widths))
    assert start["i_w"] == start["i_k"] + IDX_DIM and IDX_DIM + IDX_HEADS <= LANES
    f32_groups = (("g_q", GLA_Q32), ("g_k", PLAIN32), ("g_r", PLAIN32), ("g_lr", PLAIN32), ("i_q", IDX_Q32),
                  ("i_k", IDX_K32), ("gate_a", PLAIN32), ("gate_b", PLAIN32))
    bf16_groups = (("g_v", PLAIN16), ("d_q", DSA_ROPE16), ("d_k", DSA_ROPE16), ("d_v", PLAIN16))
    offsets, kinds, tile_of = [], [], {}
    for groups in (f32_groups, bf16_groups):
        base = len(offsets)
        for nm, kind in groups:
            tile_of[nm] = len(offsets) - base
            for t in range(-(-width[nm] // tn)):
                offsets.append(start[nm] + t * tn)
                kinds.append(kind)
        if groups is f32_groups:
            n32 = len(offsets)
    assert all(o % SUBLANES == 0 and o + tn <= pos for o in offsets)
    return offsets, kinds, n32, tile_of


def _in_project(h, w_in_t, tables, plan, tm=2048):
    t, d = h.shape
    tn = IN_PROJ_TN
    offsets, kinds, n32, _ = plan
    n16 = len(offsets) - n32
    tm = min(tm, t)
    once = pl.Buffered(1)
    table_spec = pl.BlockSpec((tm, LANES), lambda i, j, off, kind: (i, 0), pipeline_mode=once)
    grid_spec = pltpu.PrefetchScalarGridSpec(
        num_scalar_prefetch=2,
        grid=(t // tm, len(offsets)),
        in_specs=[pl.BlockSpec((tm, d), lambda i, j, off, kind: (i, 0), pipeline_mode=once),
                  pl.BlockSpec((pl.Element(tn), pl.Element(d)),
                               lambda i, j, off, kind: (pl.multiple_of(off[j], SUBLANES), 0)),
                  table_spec, table_spec, table_spec, table_spec],
        out_specs=[pl.BlockSpec((tm, tn), lambda i, j, off, kind: (i, jnp.minimum(j, n32 - 1))),
                   pl.BlockSpec((tm, tn), lambda i, j, off, kind: (i, jnp.maximum(j - n32, 0)))],
        scratch_shapes=[pltpu.VMEM((tm, tn), F32)],
    )
    return pl.pallas_call(
        _in_proj_kernel,
        grid_spec=grid_spec,
        out_shape=[jax.ShapeDtypeStruct((t, n32 * tn), F32), jax.ShapeDtypeStruct((t, n16 * tn), BF16)],
        compiler_params=_params("parallel", "arbitrary"),
        name="in_proj",
    )(jnp.asarray(offsets, jnp.int32), jnp.asarray(kinds, jnp.int32), h, w_in_t, *tables)


def _split3_bf16(x):
    hi, rest = _split_bf16(x)
    mid, lo = _split_bf16(rest)
    return hi, mid, lo


def _gla_kernel(q_ref, k_ref, v_ref, gr_ref, sm_ref, gup_ref, gb_ref, ng_ref, tri_ref, o_ref, st_ref):
    @pl.when(pl.program_id(2) == 0)
    def _():
        st_ref[...] = jnp.zeros_like(st_ref)

    rows = q_ref.shape[0]
    dk = q_ref.shape[1]
    c, sub = GLA_CHUNK, GLA_SUB
    n_sub = c // sub

    a_hi, a_lo = _split_bf16(sm_ref[:, 0:GLA_GATE_RANK])
    gate_lhs = jnp.concatenate(
        [a_hi, a_lo, a_hi, jnp.zeros((rows, LANES - 3 * GLA_GATE_RANK), F32)], axis=1).astype(BF16)
    z = jnp.dot(gate_lhs, gup_ref[...], preferred_element_type=F32) + gb_ref[...]
    log_g = (jnp.minimum(z, 0.0) - jnp.log(1.0 + jnp.exp(-jnp.abs(z)))) * (LOG2_E / GLA_TAU)
    parts = jnp.dot(tri_ref[...], jnp.concatenate(_split3_bf16(log_g), axis=1).astype(BF16),
                    preferred_element_type=F32)
    b = (parts[:, 0:dk] + parts[:, dk:2 * dk]) + parts[:, 2 * dk:3 * dk]

    q = q_ref[...]
    k = k_ref[...]

    k_tiles = k.reshape(rows // sub, sub, dk)
    b_tiles = b.reshape(rows // sub, sub, dk)
    row = lax.broadcasted_iota(jnp.int32, (rows, 1), 0)
    row_in_sub = row % sub
    col = lax.broadcasted_iota(jnp.int32, (rows, c), 1)
    target = lax.broadcasted_iota(jnp.int32, (rows, c), 0) % c
    a_diag = jnp.zeros((rows, c), F32)
    for delta in range(sub):
        if delta == 0:
            k_d, b_d = k, b
        else:
            k_d = pltpu.roll(k_tiles, delta, axis=1).reshape(rows, dk)
            b_d = pltpu.roll(b_tiles, delta, axis=1).reshape(rows, dk)
        decay = jnp.exp2(jnp.where(row_in_sub >= delta, b - b_d, -jnp.inf))
        a = jnp.sum(q * k_d * decay, axis=-1, keepdims=True)
        a_diag = jnp.where(col == target - delta, a, a_diag)

    for ch in range(rows // c):
        base = ch * c
        qc = q[base:base + c]
        kc = k[base:base + c]
        bc = b[base:base + c]
        vc = v_ref[base:base + c, :]
        st = st_ref[...]

        o_inter = lax.dot_general((qc * jnp.exp2(bc)).astype(BF16), st.astype(BF16),
                                  (((1,), (1,)), ((), ())), preferred_element_type=F32)

        blocks = [jnp.zeros((sub, c), F32)]
        for i_sub in range(1, n_sub):
            lo = i_sub * sub
            ref = bc[lo - 1:lo]
            q_hat = qc[lo:lo + sub] * jnp.exp2(bc[lo:lo + sub] - ref)
            k_hat = kc[0:lo] * jnp.exp2(ref - bc[0:lo])
            k_pad = jnp.concatenate([k_hat, jnp.zeros((c - lo, dk), F32)], axis=0)
            blocks.append(lax.dot_general(q_hat.astype(BF16), k_pad.astype(BF16),
                                          (((1,), (1,)), ((), ())), preferred_element_type=F32))
        attn = jnp.concatenate(blocks, axis=0) + a_diag[base:base + c]

        o = jnp.dot(attn.astype(BF16), vc, preferred_element_type=F32) + o_inter

        b_last = bc[c - 1:c]
        k_dec = (kc * jnp.exp2(b_last - bc)).astype(BF16)
        st_ref[...] = st * jnp.exp2(b_last) + lax.dot_general(
            vc, k_dec, (((0,), (0,)), ((), ())), preferred_element_type=F32)

        o = o * lax.rsqrt(jnp.mean(o * o, axis=-1, keepdims=True) + RMS_EPS) * ng_ref[...]
        o_ref[base:base + c, :] = (o * _silu(gr_ref[base:base + c, :])).astype(o_ref.dtype)


def _gla(act32, act16, tile_of, gate_up, gbias, ngain, batch, seq, rows=256):
    dk = gate_up.shape[1] // GLA_HEADS
    dv = ngain.shape[1] // GLA_HEADS
    t = act32.shape[0]
    nr = seq // rows
    g_hi, g_lo = _split_bf16(gate_up)
    gup_stack = jnp.concatenate(
        [g_hi, g_hi, g_lo, jnp.zeros((LANES - 3 * GLA_GATE_RANK, gate_up.shape[1]), F32)], axis=0).astype(BF16)
    idx = jnp.arange(rows)
    tri = jnp.logical_and(idx[:, None] // GLA_CHUNK == idx[None, :] // GLA_CHUNK,
                          idx[None, :] <= idx[:, None]).astype(BF16)

    def cols(group, width):
        first = tile_of[group] * IN_PROJ_TN // width
        return lambda b, h, r: (b * nr + r, first + h)

    lr_block = tile_of["g_lr"] * IN_PROJ_TN // LANES
    return pl.pallas_call(
        _gla_kernel,
        grid=(batch, GLA_HEADS, nr),
        in_specs=[pl.BlockSpec((rows, dk), cols("g_q", dk)),
                  pl.BlockSpec((rows, dk), cols("g_k", dk)),
                  pl.BlockSpec((rows, dv), cols("g_v", dv)),
                  pl.BlockSpec((rows, dv), cols("g_r", dv)),
                  pl.BlockSpec((rows, LANES), lambda b, h, r: (b * nr + r, lr_block)),
                  pl.BlockSpec((LANES, dk), lambda b, h, r: (0, h)),
                  pl.BlockSpec((1, dk), lambda b, h, r: (0, h)),
                  pl.BlockSpec((1, dv), lambda b, h, r: (0, h)),
                  pl.BlockSpec((rows, rows), lambda b, h, r: (0, 0))],
        out_specs=pl.BlockSpec((rows, dv), lambda b, h, r: (b * nr + r, h)),
        out_shape=jax.ShapeDtypeStruct((t, GLA_HEADS * dv), BF16),
        scratch_shapes=[pltpu.VMEM((dv, dk), F32)],
        compiler_params=_params("parallel", "parallel", "arbitrary"),
        name="gla",
    )(act32, act32, act16, act32, act32, gup_stack, gbias, ngain, tri)


def _split_bf16(x):
    hi = x.astype(BF16).astype(F32)
    return hi, x - hi


def _sortable_bits_to_float(key):
    return lax.bitcast_convert_type(jnp.where(key < 0, key ^ jnp.int32(0x7FFFFFFF), key), F32)


SORT_KEY_NEG_INF = -2139095041


def _topk_bias_kernel(iq_ref, ik_ref, iw_ref, tri_ref, o_ref, score_ref, rhs_ref, cnt_ref, *, topk, tq):
    qi = pl.program_id(1)
    tk = tq
    nk = ik_ref.shape[0] // tk
    n_live = qi + 1
    w = iw_ref[0] * (IDX_HEADS ** -0.5)
    pad_q = jnp.zeros((tq, IDX_DIM), F32)
    for h in range(IDX_HEADS):
        hi, lo = _split_bf16(iq_ref[:, h * IDX_DIM:(h + 1) * IDX_DIM])
        rhs_ref[h] = jnp.concatenate([hi, hi, lo, pad_q], axis=1).astype(BF16)
    t_pos = qi * tq + lax.broadcasted_iota(jnp.int32, (tk, tq), 1)
    s_off = lax.broadcasted_iota(jnp.int32, (tk, tq), 0)
    pad_k = jnp.zeros((tk, IDX_DIM), F32)

    def block_rows(kb):
        return pl.ds(pl.multiple_of(kb * tk, tk), tk)

    def score_block(kb, carry):
        hi, lo = _split_bf16(ik_ref[block_rows(kb), 0:IDX_DIM])
        lhs = jnp.concatenate([hi, lo, hi, pad_k], axis=1).astype(BF16)
        score = jnp.zeros((tk, tq), F32)
        for h in range(IDX_HEADS):
            dots = lax.dot_general(lhs, rhs_ref[h], (((1,), (1,)), ((), ())), preferred_element_type=F32)
            score = score + w[h:h + 1, :] * jnp.maximum(dots, 0.0)
        score_ref[block_rows(kb), :] = jnp.where(kb * tk + s_off <= t_pos, score + 0.0, -jnp.inf)
        return carry

    lax.fori_loop(0, n_live, score_block, 0)

    def count(pred):
        for n in range(1, nk + 1):
            @pl.when(n_live == n)
            def _(n=n):
                part = jnp.zeros((SUBLANES, tq), F32)
                for kb in range(n):
                    hit = jnp.where(pred(score_ref[kb * tk:(kb + 1) * tk, :]), 1.0, 0.0)
                    part = part + jnp.sum(hit.reshape(tk // SUBLANES, SUBLANES, tq), axis=0)
                cnt_ref[...] = part
        return jnp.sum(cnt_ref[...], axis=0, keepdims=True)

    def search(i, ans):
        cand = ans ^ jnp.left_shift(jnp.int32(1), 31 - i)
        cand_f = _sortable_bits_to_float(cand)
        cnt = count(lambda s: s >= cand_f)
        accept = jnp.logical_or(cnt >= float(topk), cand < SORT_KEY_NEG_INF)
        return jnp.where(accept, cand, ans)

    thr = _sortable_bits_to_float(
        lax.fori_loop(0, 32, search, jnp.full((1, tq), jnp.iinfo(jnp.int32).min, jnp.int32)))
    need = float(topk) - count(lambda s: s > thr)

    def emit(kb, seen):
        blk = score_ref[block_rows(kb), :]
        eq = jnp.where(blk == thr, 1.0, 0.0)
        rank = jnp.dot(tri_ref[...], eq.astype(BF16), preferred_element_type=F32) + seen
        chosen = jnp.logical_or(blk > thr, jnp.logical_and(blk == thr, rank <= need))
        o_ref[0, block_rows(kb), :] = jnp.where(
            jnp.logical_and(chosen, kb * tk + s_off <= t_pos), 0.0, -jnp.inf)
        return seen + jnp.sum(eq, axis=0, keepdims=True)

    lax.fori_loop(0, n_live, emit, jnp.zeros((1, tq), F32))

    def fill(kb, carry):
        o_ref[0, block_rows(kb), :] = jnp.full((tk, tq), -jnp.inf, F32)
        return carry

    lax.fori_loop(n_live, nk, fill, 0)


def _topk_bias(act32, tile_of, iw_t, batch, seq, topk, tq=256):
    tq = min(tq, seq)
    nq = seq // tq
    width = IDX_HEADS * IDX_DIM
    iq_block = tile_of["i_q"] * IN_PROJ_TN // width
    ik_block = tile_of["i_k"] * IN_PROJ_TN // LANES
    tri = jnp.tril(jnp.ones((tq, tq), BF16))
    return pl.pallas_call(
        functools.partial(_topk_bias_kernel, topk=topk, tq=tq),
        grid=(batch, nq),
        in_specs=[pl.BlockSpec((tq, width), lambda b, i: (b * nq + i, iq_block)),
                  pl.BlockSpec((seq, LANES), lambda b, i: (b, ik_block)),
                  pl.BlockSpec((1, IDX_HEADS, tq), lambda b, i: (b, 0, i)),
                  pl.BlockSpec((tq, tq), lambda b, i: (0, 0))],
        out_specs=pl.BlockSpec((1, seq, tq), lambda b, i: (b, 0, i)),
        out_shape=jax.ShapeDtypeStruct((batch, seq, seq), F32),
        scratch_shapes=[pltpu.VMEM((seq, tq), F32),
                        pltpu.VMEM((IDX_HEADS, tq, 4 * IDX_DIM), BF16),
                        pltpu.VMEM((SUBLANES, tq), F32)],
        compiler_params=_params("parallel", "parallel"),
        name="indexer_topk_bias",
    )(act32, act32, iw_t, tri)


ATTN_HEADS_PER_STEP = 8


def _attn_kernel(q_ref, k_ref, v_ref, bias_ref, o_ref, acc_ref):
    qi = pl.program_id(2)
    tq = q_ref.shape[0]
    tk = tq
    dh = DSA_HEAD_DIM
    group = q_ref.shape[1] // dh
    scale2 = (dh ** -0.5) * LOG2_E
    acc_ref[...] = jnp.zeros_like(acc_ref)

    def body(kj, carry):
        rows = pl.ds(pl.multiple_of(kj * tk, tk), tk)
        bias = bias_ref[0, rows, :]
        heads = [slice(g * dh, (g + 1) * dh) for g in range(group)]
        logits = [lax.dot_general(k_ref[rows, cols], q_ref[:, cols], (((1,), (1,)), ((), ())),
                                  preferred_element_type=F32) for cols in heads]
        new = []
        for g in range(group):
            m, l = carry[g]
            s = logits[g] * scale2 + bias
            m_new = jnp.maximum(m, jnp.max(s, axis=0, keepdims=True))
            m_safe = jnp.where(m_new == -jnp.inf, 0.0, m_new)
            alpha = jnp.exp2(m - m_safe)
            p = jnp.exp2(s - m_safe)
            new.append((m_new, alpha * l + jnp.sum(p, axis=0, keepdims=True)))
            update = lax.dot_general(v_ref[rows, heads[g]], p.astype(BF16), (((0,), (0,)), ((), ())),
                                     preferred_element_type=F32)
            acc_ref[g] = alpha * acc_ref[g] + update
        return tuple(new)

    init = tuple((jnp.full((1, tq), -jnp.inf, F32), jnp.zeros((1, tq), F32)) for _ in range(group))
    final = lax.fori_loop(0, qi + 1, body, init)
    for g in range(group):
        o_ref[:, g * dh:(g + 1) * dh] = (acc_ref[g] / final[g][1]).T.astype(o_ref.dtype)


def _attention(act16, tile_of, bias_t, batch, seq, tq=256):
    tq = min(tq, seq)
    nq = seq // tq
    width = ATTN_HEADS_PER_STEP * DSA_HEAD_DIM
    n_groups = DSA_HEADS // ATTN_HEADS_PER_STEP
    t = act16.shape[0]
    q_block, k_block, v_block = (tile_of[nm] * IN_PROJ_TN // width for nm in ("d_q", "d_k", "d_v"))
    return pl.pallas_call(
        _attn_kernel,
        grid=(batch, n_groups, nq),
        in_specs=[pl.BlockSpec((tq, width), lambda b, h, i: (b * nq + i, q_block + h)),
                  pl.BlockSpec((seq, width), lambda b, h, i: (b, k_block + h)),
                  pl.BlockSpec((seq, width), lambda b, h, i: (b, v_block + h)),
                  pl.BlockSpec((1, seq, tq), lambda b, h, i: (b, 0, i))],
        out_specs=pl.BlockSpec((tq, width), lambda b, h, i: (b * nq + i, h)),
        out_shape=jax.ShapeDtypeStruct((t, DSA_HEADS * DSA_HEAD_DIM), BF16),
        scratch_shapes=[pltpu.VMEM((ATTN_HEADS_PER_STEP, DSA_HEAD_DIM, tq), F32)],
        compiler_params=_params("parallel", "parallel", "arbitrary"),
        name="dsa_attention",
    )(act16, act16, act16, bias_t)


def _merge_kernel(oa_ref, ob_ref, wa_ref, wb_ref, ga_ref, gb_ref, o_ref):
    ya = jnp.dot(oa_ref[...], wa_ref[...].astype(BF16), preferred_element_type=F32)
    yb = jnp.dot(ob_ref[...], wb_ref[...].astype(BF16), preferred_element_type=F32)
    o_ref[...] = (jax.nn.sigmoid(ga_ref[...]) * ya + jax.nn.sigmoid(gb_ref[...]) * yb).astype(o_ref.dtype)


def _merge(o_a, o_b, w_a, w_b, act32, tile_of, tm=1024):
    t, d = o_a.shape
    n = w_a.shape[1]
    tn = IN_PROJ_TN
    tm = min(tm, t)
    ga_tile, gb_tile = tile_of["gate_a"], tile_of["gate_b"]
    return pl.pallas_call(
        _merge_kernel,
        grid=(t // tm, n // tn),
        in_specs=[pl.BlockSpec((tm, d), lambda i, j: (i, 0)),
                  pl.BlockSpec((tm, d), lambda i, j: (i, 0)),
                  pl.BlockSpec((d, tn), lambda i, j: (0, j)),
                  pl.BlockSpec((d, tn), lambda i, j: (0, j)),
                  pl.BlockSpec((tm, tn), lambda i, j: (i, ga_tile + j)),
                  pl.BlockSpec((tm, tn), lambda i, j: (i, gb_tile + j))],
        out_specs=pl.BlockSpec((tm, tn), lambda i, j: (i, j)),
        out_shape=jax.ShapeDtypeStruct((t, n), BF16),
        compiler_params=_params("parallel", "parallel"),
        name="branch_merge",
    )(o_a, o_b, w_a, w_b, act32, act32)


def _mixer_out_kernel(m_ref, w_ref, x_ref, g_ref, gain_ref, sc_ref, sh_ref, x1_ref, h2_ref):
    y = jnp.dot(m_ref[...], w_ref[...], preferred_element_type=F32)
    x1 = x_ref[...] + g_ref[0] * y
    x1_ref[...] = x1
    h2_ref[...] = _rms_modulate(x1, gain_ref[...], sc_ref[0], sh_ref[0]).astype(h2_ref.dtype)


def _mixer_out(merged, w, x2d, g1, gain2, sc2, sh2, seq, tm=512):
    t, d = x2d.shape
    tm = min(tm, seq)
    per_batch = seq // tm
    vec = pl.BlockSpec((1, 1, d), lambda i: (i // per_batch, 0, 0))
    row = pl.BlockSpec((tm, d), lambda i: (i, 0))
    return pl.pallas_call(
        _mixer_out_kernel,
        grid=(t // tm,),
        in_specs=[row, pl.BlockSpec((d, d), lambda i: (0, 0), pipeline_mode=pl.Buffered(1)), row, vec,
                  pl.BlockSpec((1, d), lambda i: (0, 0)), vec, vec],
        out_specs=[row, row],
        out_shape=[jax.ShapeDtypeStruct((t, d), F32), jax.ShapeDtypeStruct((t, d), BF16)],
        compiler_params=_params("parallel"),
        name="mixer_out_norm2",
    )(merged, w, x2d, g1, gain2, sc2, sh2)


def _ffn_up_kernel(h_ref, wg_ref, wu_ref, o_ref):
    g = jnp.dot(h_ref[...], wg_ref[...].astype(BF16), preferred_element_type=F32)
    u = jnp.dot(h_ref[...], wu_ref[...].astype(BF16), preferred_element_type=F32)
    o_ref[...] = (_silu(g) * u).astype(o_ref.dtype)


def _ffn_up(h2, w_gate_up, tm=1024, tn=512):
    t, d = h2.shape
    d_ff = w_gate_up.shape[1] // 2
    tm = min(tm, t)
    nj = d_ff // tn
    return pl.pallas_call(
        _ffn_up_kernel,
        grid=(t // tm, nj),
        in_specs=[pl.BlockSpec((tm, d), lambda i, j: (i, 0)),
                  pl.BlockSpec((d, tn), lambda i, j: (0, j)),
                  pl.BlockSpec((d, tn), lambda i, j: (0, nj + j))],
        out_specs=pl.BlockSpec((tm, tn), lambda i, j: (i, j)),
        out_shape=jax.ShapeDtypeStruct((t, d_ff), BF16),
        compiler_params=_params("parallel", "parallel"),
        name="ffn_up",
    )(h2, w_gate_up, w_gate_up)


def _ffn_down_kernel(a_ref, w_ref, x_ref, g_ref, gain_ref, o_ref):
    x2 = x_ref[...] + g_ref[0] * jnp.dot(a_ref[...], w_ref[...], preferred_element_type=F32)
    y = x2 * lax.rsqrt(jnp.mean(x2 * x2, axis=-1, keepdims=True) + RMS_EPS)
    o_ref[...] = y * gain_ref[...]


def _ffn_down(act, w_down, x1, g2, final_gain, seq, tm=512):
    t, d_ff = act.shape
    d = w_down.shape[1]
    tm = min(tm, seq)
    per_batch = seq // tm
    return pl.pallas_call(
        _ffn_down_kernel,
        grid=(t // tm,),
        in_specs=[pl.BlockSpec((tm, d_ff), lambda i: (i, 0)),
                  pl.BlockSpec((d_ff, d), lambda i: (0, 0), pipeline_mode=pl.Buffered(1)),
                  pl.BlockSpec((tm, d), lambda i: (i, 0)),
                  pl.BlockSpec((1, 1, d), lambda i: (i // per_batch, 0, 0)),
                  pl.BlockSpec((1, d), lambda i: (0, 0))],
        out_specs=pl.BlockSpec((tm, d), lambda i: (i, 0)),
        out_shape=jax.ShapeDtypeStruct((t, d), F32),
        compiler_params=_params("parallel", vmem_limit=VMEM_LIMIT_LARGE),
        name="ffn_down_final_norm",
    )(act, w_down, x1, g2, final_gain)


def _rope_tables(positions_flat, rot, period):
    half = rot // 2
    assert period % half == 0 and LANES % half == 0
    inv_freq = jnp.power(ROPE_THETA, -jnp.arange(0, rot, 2, dtype=F32) / rot)
    ang = positions_flat.astype(F32)[:, None] * inv_freq[None, :]
    reps = (1, LANES // half)
    return jnp.tile(jnp.cos(ang), reps), jnp.tile(jnp.sin(ang), reps)


def _layer(x2d, mod, positions_flat, batch, seq, norm1_gain, norm2_gain, w_in, gla_gate_up,
           gla_gate_bias, gla_norm_gain, w_branch_gla, w_branch_dsa, w_merge_out, w_ffn_gate_up,
           w_ffn_down, final_gain):
    d = x2d.shape[1]
    sh1, sc1, g1, sh2, sc2, g2 = [mod[:, i * d:(i + 1) * d][:, None, :] for i in range(N_MOD)]

    plan = _in_proj_plan(d)
    tile_of = plan[3]
    cos_d, sin_d = _rope_tables(positions_flat, DSA_HEAD_DIM // ROPE_FRACTION, DSA_HEAD_DIM)
    cos_i, sin_i = _rope_tables(positions_flat, IDX_DIM // ROPE_FRACTION, IDX_DIM)
    h = _norm_modulate(x2d, norm1_gain[None, :], sc1, sh1, seq)
    act32, act16 = _in_project(h, w_in.T, (cos_d, sin_d, cos_i, sin_i), plan)

    o_a = _gla(act32, act16, tile_of, gla_gate_up, gla_gate_bias[None, :], gla_norm_gain[None, :],
               batch, seq, rows=min(256, seq))

    iw_col = tile_of["i_k"] * IN_PROJ_TN + IDX_DIM
    iw_t = jnp.transpose(act32[:, iw_col:iw_col + IDX_HEADS].reshape(batch, seq, IDX_HEADS), (0, 2, 1))
    topk = min(IDX_TOPK_MAX, seq // 4)
    bias_t = _topk_bias(act32, tile_of, iw_t, batch, seq, topk)
    o_b = _attention(act16, tile_of, bias_t, batch, seq)

    merged = _merge(o_a, o_b, w_branch_gla, w_branch_dsa, act32, tile_of)
    x1, h2 = _mixer_out(merged, w_merge_out.astype(BF16), x2d, g1, norm2_gain[None, :], sc2, sh2, seq)
    act = _ffn_up(h2, w_ffn_gate_up)
    return _ffn_down(act, w_ffn_down.astype(BF16), x1, g2, final_gain[None, :], seq)


def kernel(x, c, positions, norm1_gain, norm2_gain, w_ada, b_ada, w_in, gla_gate_up, gla_gate_bias,
           gla_norm_gain, w_branch_gla, w_branch_dsa, w_merge_out, w_ffn_gate_up, w_ffn_down,
           final_norm_gain):
    batch, seq, d = x.shape
    depth = w_in.shape[0]
    assert depth == 1, "the final RMSNorm is fused into the single layer's FFN kernel"
    x2d = x.reshape(batch * seq, d)
    c_pad = jnp.zeros((SUBLANES, d), F32).at[:batch].set(c)
    mod = _modulation(c_pad, w_ada[0], b_ada[0][None, :])[:batch]
    out = _layer(x2d, mod, positions.reshape(-1), batch, seq, norm1_gain[0], norm2_gain[0], w_in[0],
                 gla_gate_up[0], gla_gate_bias[0], gla_norm_gain[0], w_branch_gla[0], w_branch_dsa[0],
                 w_merge_out[0], w_ffn_gate_up[0], w_ffn_down[0], final_norm_gain)
    return out.reshape(batch, seq, d)
```

```python
import functools

import jax
import jax.numpy as jnp
from jax import lax
from jax.experimental import pallas as pl
from jax.experimental.pallas import tpu as pltpu

F32 = jnp.float32
BF16 = jnp.bfloat16
HIGHEST = lax.Precision.HIGHEST

RMS_EPS = 1e-6
GLA_HEADS = 4
GLA_GATE_RANK = 16
GLA_TAU = 16.0
GLA_CHUNK = 64
GLA_SUB = 8
DSA_HEADS = 16
DSA_HEAD_DIM = 128
IDX_HEADS = 8
IDX_DIM = 64
IDX_TOPK_MAX = 256
ROPE_THETA = 500000.0
ROPE_FRACTION = 4
N_MOD = 6
LOG2_E = 1.4426950408889634

LANES = 128
SUBLANES = 8
VMEM_LIMIT = 48 * 1024 * 1024


VMEM_LIMIT_LARGE = 56 * 1024 * 1024


def _params(*semantics, vmem_limit=VMEM_LIMIT):
    return pltpu.CompilerParams(dimension_semantics=semantics, vmem_limit_bytes=vmem_limit)


def _silu(x):
    return x * jax.nn.sigmoid(x)


def _mod_kernel(c_ref, w_ref, b_ref, o_ref):
    a = _silu(c_ref[...])
    o_ref[...] = jnp.dot(a, w_ref[...], precision=HIGHEST, preferred_element_type=F32) + b_ref[...]


def _modulation(c_pad, w_ada, b_ada, tn=2048):
    rows, d = c_pad.shape
    n = w_ada.shape[1]
    return pl.pallas_call(
        _mod_kernel,
        grid=(n // tn,),
        in_specs=[pl.BlockSpec((rows, d), lambda j: (0, 0)),
                  pl.BlockSpec((d, tn), lambda j: (0, j)),
                  pl.BlockSpec((1, tn), lambda j: (0, j))],
        out_specs=pl.BlockSpec((rows, tn), lambda j: (0, j)),
        out_shape=jax.ShapeDtypeStruct((rows, n), F32),
        compiler_params=_params("parallel"),
        name="adaln_mod",
    )(c_pad, w_ada, b_ada)


def _rms_modulate(x, gain, scale, shift):
    y = x * lax.rsqrt(jnp.mean(x * x, axis=-1, keepdims=True) + RMS_EPS)
    return (y * gain) * (1.0 + scale) + shift


def _rope_lanes(x, cos, sin, half, period, limit=LANES):
    lane = lax.broadcasted_iota(jnp.int32, x.shape, 1)
    in_head = lane % period
    upper = pltpu.roll(x, LANES - half, axis=1)
    lower = pltpu.roll(x, half, axis=1)
    live = lane < limit
    first = jnp.logical_and(live, in_head < half)
    second = jnp.logical_and(live, jnp.logical_and(in_head >= half, in_head < 2 * half))
    return jnp.where(first, x * cos - upper * sin,
                     jnp.where(second, lower * sin + x * cos, x))


PLAIN32, GLA_Q32, IDX_Q32, IDX_K32, PLAIN16, DSA_ROPE16 = range(6)
IN_PROJ_ROPE_PARTS = 4
IN_PROJ_ROPE_ROWS = 256
IN_PROJ_TN = 512


def _norm_mod_kernel(x_ref, gain_ref, sc_ref, sh_ref, o_ref):
    o_ref[...] = _rms_modulate(x_ref[...], gain_ref[...], sc_ref[0], sh_ref[0]).astype(o_ref.dtype)


def _norm_modulate(x2d, gain, sc, sh, seq, tm=512):
    t, d = x2d.shape
    per_batch = seq // tm
    return pl.pallas_call(
        _norm_mod_kernel,
        grid=(t // tm,),
        in_specs=[pl.BlockSpec((tm, d), lambda i: (i, 0)),
                  pl.BlockSpec((1, d), lambda i: (0, 0)),
                  pl.BlockSpec((1, 1, d), lambda i: (i // per_batch, 0, 0)),
                  pl.BlockSpec((1, 1, d), lambda i: (i // per_batch, 0, 0))],
        out_specs=pl.BlockSpec((tm, d), lambda i: (i, 0)),
        out_shape=jax.ShapeDtypeStruct((t, d), BF16),
        compiler_params=_params("parallel"),
        name="norm1_modulate",
    )(x2d, gain, sc, sh)


def _in_proj_kernel(off_ref, kind_ref, h_ref, wt_ref, cos_d_ref, sin_d_ref, cos_i_ref, sin_i_ref,
                    o32_ref, o16_ref):
    kind = kind_ref[pl.program_id(1)]
    tm, tn = o32_ref.shape
    n_groups = tn // LANES
    rot_d = DSA_HEAD_DIM // ROPE_FRACTION
    rot_i = IDX_DIM // ROPE_FRACTION

    def product():
        return lax.dot_general(h_ref[...], wt_ref[...].astype(BF16), (((1,), (1,)), ((), ())),
                               preferred_element_type=F32)

    def rope_tile(cos_ref, sin_ref, out_ref, half, period, groups, pre_scale=1.0, limit=LANES):
        w = wt_ref[...].astype(BF16)
        part = tm // IN_PROJ_ROPE_PARTS
        for p in range(IN_PROJ_ROPE_PARTS):
            acc = lax.dot_general(h_ref[p * part:(p + 1) * part, :], w, (((1,), (1,)), ((), ())),
                                  preferred_element_type=F32)
            if pre_scale != 1.0:
                acc = acc * pre_scale
            for r in range(0, part, IN_PROJ_ROPE_ROWS):
                rows = slice(p * part + r, p * part + r + IN_PROJ_ROPE_ROWS)
                cos = cos_ref[rows, :]
                sin = sin_ref[rows, :]
                for g in range(n_groups):
                    cols = slice(g * LANES, (g + 1) * LANES)
                    x = acc[r:r + IN_PROJ_ROPE_ROWS, cols]
                    if g in groups:
                        x = _rope_lanes(x, cos, sin, half, period, limit)
                    out_ref[rows, cols] = x.astype(out_ref.dtype)

    @pl.when(kind == PLAIN32)
    def _():
        o32_ref[...] = product()

    @pl.when(kind == GLA_Q32)
    def _():
        o32_ref[...] = product() * ((wt_ref.shape[1] // 2 // GLA_HEADS) ** -0.5)

    @pl.when(kind == IDX_Q32)
    def _():
        rope_tile(cos_i_ref, sin_i_ref, o32_ref, rot_i // 2, IDX_DIM, range(n_groups), pre_scale=IDX_DIM ** -0.5)

    @pl.when(kind == IDX_K32)
    def _():
        rope_tile(cos_i_ref, sin_i_ref, o32_ref, rot_i // 2, IDX_DIM, range(1), limit=IDX_DIM)

    @pl.when(kind == PLAIN16)
    def _():
        o16_ref[...] = product().astype(BF16)

    @pl.when(kind == DSA_ROPE16)
    def _():
        rope_tile(cos_d_ref, sin_d_ref, o16_ref, rot_d // 2, DSA_HEAD_DIM, range(n_groups))


def _in_proj_plan(d):
    tn = IN_PROJ_TN
    gla_qk, dsa_w, idx_w = d // 2, DSA_HEADS * DSA_HEAD_DIM, IDX_HEADS * IDX_DIM
    names = ("g_q", "g_k", "g_v", "g_r", "g_lr", "d_q", "d_k", "d_v", "i_q", "i_k", "i_w", "gate_a", "gate_b")
    widths = (gla_qk, gla_qk, d, d, GLA_GATE_RANK, dsa_w, dsa_w, dsa_w, idx_w, IDX_DIM, IDX_HEADS, d, d)
    start, pos = {}, 0
    for nm, wd in zip(names, widths):
        start[nm] = pos
        pos += wd
    width = dict(zip(names, widths))
    assert start["i_w"] == start["i_k"] + IDX_DIM and IDX_DIM + IDX_HEADS <= LANES
    f32_groups = (("g_q", GLA_Q32), ("g_k", PLAIN32), ("g_r", PLAIN32), ("g_lr", PLAIN32), ("i_q", IDX_Q32),
                  ("i_k", IDX_K32), ("gate_a", PLAIN32), ("gate_b", PLAIN32))
    bf16_groups = (("g_v", PLAIN16), ("d_q", DSA_ROPE16), ("d_k", DSA_ROPE16), ("d_v", PLAIN16))
    offsets, kinds, tile_of = [], [], {}
    for groups in (f32_groups, bf16_groups):
        base = len(offsets)
        for nm, kind in groups:
            tile_of[nm] = len(offsets) - base
            for t in range(-(-width[nm] // tn)):
                offsets.append(start[nm] + t * tn)
                kinds.append(kind)
        if groups is f32_groups:
            n32 = len(offsets)
    assert all(o % SUBLANES == 0 and o + tn <= pos for o in offsets)
    return offsets, kinds, n32, tile_of


def _in_project(h, w_in_t, tables, plan, tm=2048):
    t, d = h.shape
    tn = IN_PROJ_TN
    offsets, kinds, n32, _ = plan
    n16 = len(offsets) - n32
    tm = min(tm, t)
    once = pl.Buffered(1)
    table_spec = pl.BlockSpec((tm, LANES), lambda i, j, off, kind: (i, 0), pipeline_mode=once)
    grid_spec = pltpu.PrefetchScalarGridSpec(
        num_scalar_prefetch=2,
        grid=(t // tm, len(offsets)),
        in_specs=[pl.BlockSpec((tm, d), lambda i, j, off, kind: (i, 0), pipeline_mode=once),
                  pl.BlockSpec((pl.Element(tn), pl.Element(d)),
                               lambda i, j, off, kind: (pl.multiple_of(off[j], SUBLANES), 0)),
                  table_spec, table_spec, table_spec, table_spec],
        out_specs=[pl.BlockSpec((tm, tn), lambda i, j, off, kind: (i, jnp.minimum(j, n32 - 1))),
                   pl.BlockSpec((tm, tn), lambda i, j, off, kind: (i, jnp.maximum(j - n32, 0)))],
    )
    return pl.pallas_call(
        _in_proj_kernel,
        grid_spec=grid_spec,
        out_shape=[jax.ShapeDtypeStruct((t, n32 * tn), F32), jax.ShapeDtypeStruct((t, n16 * tn), BF16)],
        compiler_params=_params("parallel", "arbitrary"),
        name="in_proj",
    )(jnp.asarray(offsets, jnp.int32), jnp.asarray(kinds, jnp.int32), h, w_in_t, *tables)


def _split3_bf16(x):
    hi, rest = _split_bf16(x)
    mid, lo = _split_bf16(rest)
    return hi, mid, lo


def _gla_kernel(q_ref, k_ref, v_ref, gr_ref, sm_ref, gup_ref, gb_ref, ng_ref, tri_ref, o_ref, st_ref):
    @pl.when(pl.program_id(2) == 0)
    def _():
        st_ref[...] = jnp.zeros_like(st_ref)

    rows = q_ref.shape[0]
    dk = q_ref.shape[1]
    c, sub = GLA_CHUNK, GLA_SUB
    n_sub = c // sub

    a_hi, a_lo = _split_bf16(sm_ref[:, 0:GLA_GATE_RANK])
    gate_lhs = jnp.concatenate(
        [a_hi, a_lo, a_hi, jnp.zeros((rows, LANES - 3 * GLA_GATE_RANK), F32)], axis=1).astype(BF16)
    z = jnp.dot(gate_lhs, gup_ref[...], preferred_element_type=F32) + gb_ref[...]
    log_g = (jnp.minimum(z, 0.0) - jnp.log(1.0 + jnp.exp(-jnp.abs(z)))) * (LOG2_E / GLA_TAU)
    parts = jnp.dot(tri_ref[...], jnp.concatenate(_split3_bf16(log_g), axis=1).astype(BF16),
                    preferred_element_type=F32)
    b = (parts[:, 0:dk] + parts[:, dk:2 * dk]) + parts[:, 2 * dk:3 * dk]

    q = q_ref[...]
    k = k_ref[...]

    k_tiles = k.reshape(rows // sub, sub, dk)
    b_tiles = b.reshape(rows // sub, sub, dk)
    row = lax.broadcasted_iota(jnp.int32, (rows, 1), 0)
    row_in_sub = row % sub
    col = lax.broadcasted_iota(jnp.int32, (rows, c), 1)
    target = lax.broadcasted_iota(jnp.int32, (rows, c), 0) % c
    a_diag = jnp.zeros((rows, c), F32)
    for delta in range(sub):
        if delta == 0:
            k_d, b_d = k, b
        else:
            k_d = pltpu.roll(k_tiles, delta, axis=1).reshape(rows, dk)
            b_d = pltpu.roll(b_tiles, delta, axis=1).reshape(rows, dk)
        decay = jnp.exp2(jnp.where(row_in_sub >= delta, b - b_d, -jnp.inf))
        a = jnp.sum(q * k_d * decay, axis=-1, keepdims=True)
        a_diag = jnp.where(col == target - delta, a, a_diag)

    for ch in range(rows // c):
        base = ch * c
        qc = q[base:base + c]
        kc = k[base:base + c]
        bc = b[base:base + c]
        vc = v_ref[base:base + c, :]
        st = st_ref[...]

        o_inter = lax.dot_general((qc * jnp.exp2(bc)).astype(BF16), st.astype(BF16),
                                  (((1,), (1,)), ((), ())), preferred_element_type=F32)

        blocks = [jnp.zeros((sub, c), F32)]
        for i_sub in range(1, n_sub):
            lo = i_sub * sub
            ref = bc[lo - 1:lo]
            q_hat = qc[lo:lo + sub] * jnp.exp2(bc[lo:lo + sub] - ref)
            k_hat = kc[0:lo] * jnp.exp2(ref - bc[0:lo])
            k_pad = jnp.concatenate([k_hat, jnp.zeros((c - lo, dk), F32)], axis=0)
            blocks.append(lax.dot_general(q_hat.astype(BF16), k_pad.astype(BF16),
                                          (((1,), (1,)), ((), ())), preferred_element_type=F32))
        attn = jnp.concatenate(blocks, axis=0) + a_diag[base:base + c]

        o = jnp.dot(attn.astype(BF16), vc, preferred_element_type=F32) + o_inter

        b_last = bc[c - 1:c]
        k_dec = (kc * jnp.exp2(b_last - bc)).astype(BF16)
        st_ref[...] = st * jnp.exp2(b_last) + lax.dot_general(
            vc, k_dec, (((0,), (0,)), ((), ())), preferred_element_type=F32)

        o = o * lax.rsqrt(jnp.mean(o * o, axis=-1, keepdims=True) + RMS_EPS) * ng_ref[...]
        o_ref[base:base + c, :] = (o * _silu(gr_ref[base:base + c, :])).astype(o_ref.dtype)


def _gla(act32, act16, tile_of, gate_up, gbias, ngain, batch, seq, rows=256):
    dk = gate_up.shape[1] // GLA_HEADS
    dv = ngain.shape[1] // GLA_HEADS
    t = act32.shape[0]
    nr = seq // rows
    g_hi, g_lo = _split_bf16(gate_up)
    gup_stack = jnp.concatenate(
        [g_hi, g_hi, g_lo, jnp.zeros((LANES - 3 * GLA_GATE_RANK, gate_up.shape[1]), F32)], axis=0).astype(BF16)
    idx = jnp.arange(rows)
    tri = jnp.logical_and(idx[:, None] // GLA_CHUNK == idx[None, :] // GLA_CHUNK,
                          idx[None, :] <= idx[:, None]).astype(BF16)

    def cols(group, width):
        first = tile_of[group] * IN_PROJ_TN // width
        return lambda b, h, r: (b * nr + r, first + h)

    lr_block = tile_of["g_lr"] * IN_PROJ_TN // LANES
    return pl.pallas_call(
        _gla_kernel,
        grid=(batch, GLA_HEADS, nr),
        in_specs=[pl.BlockSpec((rows, dk), cols("g_q", dk)),
                  pl.BlockSpec((rows, dk), cols("g_k", dk)),
                  pl.BlockSpec((rows, dv), cols("g_v", dv)),
                  pl.BlockSpec((rows, dv), cols("g_r", dv)),
                  pl.BlockSpec((rows, LANES), lambda b, h, r: (b * nr + r, lr_block)),
                  pl.BlockSpec((LANES, dk), lambda b, h, r: (0, h)),
                  pl.BlockSpec((1, dk), lambda b, h, r: (0, h)),
                  pl.BlockSpec((1, dv), lambda b, h, r: (0, h)),
                  pl.BlockSpec((rows, rows), lambda b, h, r: (0, 0))],
        out_specs=pl.BlockSpec((rows, dv), lambda b, h, r: (b * nr + r, h)),
        out_shape=jax.ShapeDtypeStruct((t, GLA_HEADS * dv), BF16),
        scratch_shapes=[pltpu.VMEM((dv, dk), F32)],
        compiler_params=_params("parallel", "parallel", "arbitrary"),
        name="gla",
    )(act32, act32, act16, act32, act32, gup_stack, gbias, ngain, tri)


def _split_bf16(x):
    hi = x.astype(BF16).astype(F32)
    return hi, x - hi


def _sortable_bits_to_float(key):
    return lax.bitcast_convert_type(jnp.where(key < 0, key ^ jnp.int32(0x7FFFFFFF), key), F32)


SORT_KEY_NEG_INF = -2139095041


def _topk_bias_kernel(iq_ref, ik_ref, iw_ref, tri_ref, o_ref, score_ref, rhs_ref, cnt_ref, *, topk, tq):
    qi = pl.program_id(1)
    tk = tq
    nk = ik_ref.shape[0] // tk
    n_live = qi + 1
    w = iw_ref[0] * (IDX_HEADS ** -0.5)
    pad_q = jnp.zeros((tq, IDX_DIM), F32)
    for h in range(IDX_HEADS):
        hi, lo = _split_bf16(iq_ref[:, h * IDX_DIM:(h + 1) * IDX_DIM])
        rhs_ref[h] = jnp.concatenate([hi, hi, lo, pad_q], axis=1).astype(BF16)
    t_pos = qi * tq + lax.broadcasted_iota(jnp.int32, (tk, tq), 1)
    s_off = lax.broadcasted_iota(jnp.int32, (tk, tq), 0)
    pad_k = jnp.zeros((tk, IDX_DIM), F32)

    def block_rows(kb):
        return pl.ds(pl.multiple_of(kb * tk, tk), tk)

    def score_block(kb, carry):
        hi, lo = _split_bf16(ik_ref[block_rows(kb), 0:IDX_DIM])
        lhs = jnp.concatenate([hi, lo, hi, pad_k], axis=1).astype(BF16)
        score = jnp.zeros((tk, tq), F32)
        for h in range(IDX_HEADS):
            dots = lax.dot_general(lhs, rhs_ref[h], (((1,), (1,)), ((), ())), preferred_element_type=F32)
            score = score + w[h:h + 1, :] * jnp.maximum(dots, 0.0)
        score_ref[block_rows(kb), :] = jnp.where(kb * tk + s_off <= t_pos, score + 0.0, -jnp.inf)
        return carry

    lax.fori_loop(0, n_live, score_block, 0)

    def count(pred):
        for n in range(1, nk + 1):
            @pl.when(n_live == n)
            def _(n=n):
                part = jnp.zeros((SUBLANES, tq), F32)
                for kb in range(n):
                    hit = jnp.where(pred(score_ref[kb * tk:(kb + 1) * tk, :]), 1.0, 0.0)
                    part = part + jnp.sum(hit.reshape(tk // SUBLANES, SUBLANES, tq), axis=0)
                cnt_ref[...] = part
        return jnp.sum(cnt_ref[...], axis=0, keepdims=True)

    def search(i, ans):
        cand = ans ^ jnp.left_shift(jnp.int32(1), 31 - i)
        cand_f = _sortable_bits_to_float(cand)
        cnt = count(lambda s: s >= cand_f)
        accept = jnp.logical_or(cnt >= float(topk), cand < SORT_KEY_NEG_INF)
        return jnp.where(accept, cand, ans)

    thr = _sortable_bits_to_float(
        lax.fori_loop(0, 32, search, jnp.full((1, tq), jnp.iinfo(jnp.int32).min, jnp.int32)))
    need = float(topk) - count(lambda s: s > thr)

    def emit(kb, seen):
        blk = score_ref[block_rows(kb), :]
        eq = jnp.where(blk == thr, 1.0, 0.0)
        rank = jnp.dot(tri_ref[...], eq.astype(BF16), preferred_element_type=F32) + seen
        chosen = jnp.logical_or(blk > thr, jnp.logical_and(blk == thr, rank <= need))
        o_ref[0, block_rows(kb), :] = jnp.where(
            jnp.logical_and(chosen, kb * tk + s_off <= t_pos), 0.0, -jnp.inf)
        return seen + jnp.sum(eq, axis=0, keepdims=True)

    lax.fori_loop(0, n_live, emit, jnp.zeros((1, tq), F32))

    def fill(kb, carry):
        o_ref[0, block_rows(kb), :] = jnp.full((tk, tq), -jnp.inf, F32)
        return carry

    lax.fori_loop(n_live, nk, fill, 0)


def _topk_bias(act32, tile_of, iw_t, batch, seq, topk, tq=256):
    tq = min(tq, seq)
    nq = seq // tq
    width = IDX_HEADS * IDX_DIM
    iq_block = tile_of["i_q"] * IN_PROJ_TN // width
    ik_block = tile_of["i_k"] * IN_PROJ_TN // LANES
    tri = jnp.tril(jnp.ones((tq, tq), BF16))
    return pl.pallas_call(
        functools.partial(_topk_bias_kernel, topk=topk, tq=tq),
        grid=(batch, nq),
        in_specs=[pl.BlockSpec((tq, width), lambda b, i: (b * nq + i, iq_block)),
                  pl.BlockSpec((seq, LANES), lambda b, i: (b, ik_block)),
                  pl.BlockSpec((1, IDX_HEADS, tq), lambda b, i: (b, 0, i)),
                  pl.BlockSpec((tq, tq), lambda b, i: (0, 0))],
        out_specs=pl.BlockSpec((1, seq, tq), lambda b, i: (b, 0, i)),
        out_shape=jax.ShapeDtypeStruct((batch, seq, seq), F32),
        scratch_shapes=[pltpu.VMEM((seq, tq), F32),
                        pltpu.VMEM((IDX_HEADS, tq, 4 * IDX_DIM), BF16),
                        pltpu.VMEM((SUBLANES, tq), F32)],
        compiler_params=_params("parallel", "parallel"),
        name="indexer_topk_bias",
    )(act32, act32, iw_t, tri)


ATTN_HEADS_PER_STEP = 8


def _attn_kernel(q_ref, k_ref, v_ref, bias_ref, o_ref, acc_ref):
    qi = pl.program_id(2)
    tq = q_ref.shape[0]
    tk = tq
    dh = DSA_HEAD_DIM
    group = q_ref.shape[1] // dh
    scale2 = (dh ** -0.5) * LOG2_E
    acc_ref[...] = jnp.zeros_like(acc_ref)

    def body(kj, carry):
        rows = pl.ds(pl.multiple_of(kj * tk, tk), tk)
        bias = bias_ref[0, rows, :]
        heads = [slice(g * dh, (g + 1) * dh) for g in range(group)]
        logits = [lax.dot_general(k_ref[rows, cols], q_ref[:, cols], (((1,), (1,)), ((), ())),
                                  preferred_element_type=F32) for cols in heads]
        new = []
        for g in range(group):
            m, l = carry[g]
            s = logits[g] * scale2 + bias
            m_new = jnp.maximum(m, jnp.max(s, axis=0, keepdims=True))
            m_safe = jnp.where(m_new == -jnp.inf, 0.0, m_new)
            alpha = jnp.exp2(m - m_safe)
            p = jnp.exp2(s - m_safe)
            new.append((m_new, alpha * l + jnp.sum(p, axis=0, keepdims=True)))
            update = lax.dot_general(v_ref[rows, heads[g]], p.astype(BF16), (((0,), (0,)), ((), ())),
                                     preferred_element_type=F32)
            acc_ref[g] = alpha * acc_ref[g] + update
        return tuple(new)

    init = tuple((jnp.full((1, tq), -jnp.inf, F32), jnp.zeros((1, tq), F32)) for _ in range(group))
    final = lax.fori_loop(0, qi + 1, body, init)
    for g in range(group):
        o_ref[:, g * dh:(g + 1) * dh] = (acc_ref[g] / final[g][1]).T.astype(o_ref.dtype)


def _attention(act16, tile_of, bias_t, batch, seq, tq=256):
    tq = min(tq, seq)
    nq = seq // tq
    width = ATTN_HEADS_PER_STEP * DSA_HEAD_DIM
    n_groups = DSA_HEADS // ATTN_HEADS_PER_STEP
    t = act16.shape[0]
    q_block, k_block, v_block = (tile_of[nm] * IN_PROJ_TN // width for nm in ("d_q", "d_k", "d_v"))
    return pl.pallas_call(
        _attn_kernel,
        grid=(batch, n_groups, nq),
        in_specs=[pl.BlockSpec((tq, width), lambda b, h, i: (b * nq + i, q_block + h)),
                  pl.BlockSpec((seq, width), lambda b, h, i: (b, k_block + h)),
                  pl.BlockSpec((seq, width), lambda b, h, i: (b, v_block + h)),
                  pl.BlockSpec((1, seq, tq), lambda b, h, i: (b, 0, i))],
        out_specs=pl.BlockSpec((tq, width), lambda b, h, i: (b * nq + i, h)),
        out_shape=jax.ShapeDtypeStruct((t, DSA_HEADS * DSA_HEAD_DIM), BF16),
        scratch_shapes=[pltpu.VMEM((ATTN_HEADS_PER_STEP, DSA_HEAD_DIM, tq), F32)],
        compiler_params=_params("parallel", "parallel", "arbitrary"),
        name="dsa_attention",
    )(act16, act16, act16, bias_t)


def _merge_kernel(oa_ref, ob_ref, wa_ref, wb_ref, ga_ref, gb_ref, o_ref):
    ya = jnp.dot(oa_ref[...], wa_ref[...].astype(BF16), preferred_element_type=F32)
    yb = jnp.dot(ob_ref[...], wb_ref[...].astype(BF16), preferred_element_type=F32)
    o_ref[...] = (jax.nn.sigmoid(ga_ref[...]) * ya + jax.nn.sigmoid(gb_ref[...]) * yb).astype(o_ref.dtype)


def _merge(o_a, o_b, w_a, w_b, act32, tile_of, tm=1024):
    t, d = o_a.shape
    n = w_a.shape[1]
    tn = IN_PROJ_TN
    tm = min(tm, t)
    ga_tile, gb_tile = tile_of["gate_a"], tile_of["gate_b"]
    return pl.pallas_call(
        _merge_kernel,
        grid=(t // tm, n // tn),
        in_specs=[pl.BlockSpec((tm, d), lambda i, j: (i, 0)),
                  pl.BlockSpec((tm, d), lambda i, j: (i, 0)),
                  pl.BlockSpec((d, tn), lambda i, j: (0, j)),
                  pl.BlockSpec((d, tn), lambda i, j: (0, j)),
                  pl.BlockSpec((tm, tn), lambda i, j: (i, ga_tile + j)),
                  pl.BlockSpec((tm, tn), lambda i, j: (i, gb_tile + j))],
        out_specs=pl.BlockSpec((tm, tn), lambda i, j: (i, j)),
        out_shape=jax.ShapeDtypeStruct((t, n), BF16),
        compiler_params=_params("parallel", "parallel"),
        name="branch_merge",
    )(o_a, o_b, w_a, w_b, act32, act32)


def _mixer_out_kernel(m_ref, w_ref, x_ref, g_ref, gain_ref, sc_ref, sh_ref, x1_ref, h2_ref):
    y = jnp.dot(m_ref[...], w_ref[...], preferred_element_type=F32)
    x1 = x_ref[...] + g_ref[0] * y
    x1_ref[...] = x1
    h2_ref[...] = _rms_modulate(x1, gain_ref[...], sc_ref[0], sh_ref[0]).astype(h2_ref.dtype)


def _mixer_out(merged, w, x2d, g1, gain2, sc2, sh2, seq, tm=512):
    t, d = x2d.shape
    tm = min(tm, seq)
    per_batch = seq // tm
    vec = pl.BlockSpec((1, 1, d), lambda i: (i // per_batch, 0, 0))
    row = pl.BlockSpec((tm, d), lambda i: (i, 0))
    return pl.pallas_call(
        _mixer_out_kernel,
        grid=(t // tm,),
        in_specs=[row, pl.BlockSpec((d, d), lambda i: (0, 0), pipeline_mode=pl.Buffered(1)), row, vec,
                  pl.BlockSpec((1, d), lambda i: (0, 0)), vec, vec],
        out_specs=[row, row],
        out_shape=[jax.ShapeDtypeStruct((t, d), F32), jax.ShapeDtypeStruct((t, d), BF16)],
        compiler_params=_params("parallel"),
        name="mixer_out_norm2",
    )(merged, w, x2d, g1, gain2, sc2, sh2)


def _ffn_up_kernel(h_ref, wg_ref, wu_ref, o_ref):
    g = jnp.dot(h_ref[...], wg_ref[...].astype(BF16), preferred_element_type=F32)
    u = jnp.dot(h_ref[...], wu_ref[...].astype(BF16), preferred_element_type=F32)
    o_ref[...] = (_silu(g) * u).astype(o_ref.dtype)


def _ffn_up(h2, w_gate_up, tm=1024, tn=512):
    t, d = h2.shape
    d_ff = w_gate_up.shape[1] // 2
    tm = min(tm, t)
    nj = d_ff // tn
    return pl.pallas_call(
        _ffn_up_kernel,
        grid=(t // tm, nj),
        in_specs=[pl.BlockSpec((tm, d), lambda i, j: (i, 0)),
                  pl.BlockSpec((d, tn), lambda i, j: (0, j)),
                  pl.BlockSpec((d, tn), lambda i, j: (0, nj + j))],
        out_specs=pl.BlockSpec((tm, tn), lambda i, j: (i, j)),
        out_shape=jax.ShapeDtypeStruct((t, d_ff), BF16),
        compiler_params=_params("parallel", "parallel"),
        name="ffn_up",
    )(h2, w_gate_up, w_gate_up)


def _ffn_down_kernel(a_ref, w_ref, x_ref, g_ref, gain_ref, o_ref):
    x2 = x_ref[...] + g_ref[0] * jnp.dot(a_ref[...], w_ref[...], preferred_element_type=F32)
    y = x2 * lax.rsqrt(jnp.mean(x2 * x2, axis=-1, keepdims=True) + RMS_EPS)
    o_ref[...] = y * gain_ref[...]


def _ffn_down(act, w_down, x1, g2, final_gain, seq, tm=512):
    t, d_ff = act.shape
    d = w_down.shape[1]
    tm = min(tm, seq)
    per_batch = seq // tm
    return pl.pallas_call(
        _ffn_down_kernel,
        grid=(t // tm,),
        in_specs=[pl.BlockSpec((tm, d_ff), lambda i: (i, 0)),
                  pl.BlockSpec((d_ff, d), lambda i: (0, 0), pipeline_mode=pl.Buffered(1)),
                  pl.BlockSpec((tm, d), lambda i: (i, 0)),
                  pl.BlockSpec((1, 1, d), lambda i: (i // per_batch, 0, 0)),
                  pl.BlockSpec((1, d), lambda i: (0, 0))],
        out_specs=pl.BlockSpec((tm, d), lambda i: (i, 0)),
        out_shape=jax.ShapeDtypeStruct((t, d), F32),
        compiler_params=_params("parallel", vmem_limit=VMEM_LIMIT_LARGE),
        name="ffn_down_final_norm",
    )(act, w_down, x1, g2, final_gain)


def _rope_tables(positions_flat, rot, period):
    half = rot // 2
    assert period % half == 0 and LANES % half == 0
    inv_freq = jnp.power(ROPE_THETA, -jnp.arange(0, rot, 2, dtype=F32) / rot)
    ang = positions_flat.astype(F32)[:, None] * inv_freq[None, :]
    reps = (1, LANES // half)
    return jnp.tile(jnp.cos(ang), reps), jnp.tile(jnp.sin(ang), reps)


def _layer(x2d, mod, positions_flat, batch, seq, norm1_gain, norm2_gain, w_in, gla_gate_up,
           gla_gate_bias, gla_norm_gain, w_branch_gla, w_branch_dsa, w_merge_out, w_ffn_gate_up,
           w_ffn_down, final_gain):
    d = x2d.shape[1]
    sh1, sc1, g1, sh2, sc2, g2 = [mod[:, i * d:(i + 1) * d][:, None, :] for i in range(N_MOD)]

    plan = _in_proj_plan(d)
    tile_of = plan[3]
    cos_d, sin_d = _rope_tables(positions_flat, DSA_HEAD_DIM // ROPE_FRACTION, DSA_HEAD_DIM)
    cos_i, sin_i = _rope_tables(positions_flat, IDX_DIM // ROPE_FRACTION, IDX_DIM)
    h = _norm_modulate(x2d, norm1_gain[None, :], sc1, sh1, seq)
    act32, act16 = _in_project(h, w_in.T, (cos_d, sin_d, cos_i, sin_i), plan)

    o_a = _gla(act32, act16, tile_of, gla_gate_up, gla_gate_bias[None, :], gla_norm_gain[None, :],
               batch, seq, rows=min(256, seq))

    iw_col = tile_of["i_k"] * IN_PROJ_TN + IDX_DIM
    iw_t = jnp.transpose(act32[:, iw_col:iw_col + IDX_HEADS].reshape(batch, seq, IDX_HEADS), (0, 2, 1))
    topk = min(IDX_TOPK_MAX, seq // 4)
    bias_t = _topk_bias(act32, tile_of, iw_t, batch, seq, topk)
    o_b = _attention(act16, tile_of, bias_t, batch, seq)

    merged = _merge(o_a, o_b, w_branch_gla, w_branch_dsa, act32, tile_of)
    x1, h2 = _mixer_out(merged, w_merge_out.astype(BF16), x2d, g1, norm2_gain[None, :], sc2, sh2, seq)
    act = _ffn_up(h2, w_ffn_gate_up)
    return _ffn_down(act, w_ffn_down.astype(BF16), x1, g2, final_gain[None, :], seq)


def kernel(x, c, positions, norm1_gain, norm2_gain, w_ada, b_ada, w_in, gla_gate_up, gla_gate_bias,
           gla_norm_gain, w_branch_gla, w_branch_dsa, w_merge_out, w_ffn_gate_up, w_ffn_down,
           final_norm_gain):
    batch, seq, d = x.shape
    depth = w_in.shape[0]
    assert depth == 1, "the final RMSNorm is fused into the single layer's FFN kernel"
    x2d = x.reshape(batch * seq, d)
    c_pad = jnp.zeros((SUBLANES, d), F32).at[:batch].set(c)
    mod = _modulation(c_pad, w_ada[0], b_ada[0][None, :])[:batch]
    out = _layer(x2d, mod, positions.reshape(-1), batch, seq, norm1_gain[0], norm2_gain[0], w_in[0],
                 gla_gate_up[0], gla_gate_bias[0], gla_norm_gain[0], w_branch_gla[0], w_branch_dsa[0],
                 w_merge_out[0], w_ffn_gate_up[0], w_ffn_down[0], final_norm_gain)
    return out.reshape(batch, seq, d)
```

```python
import functools

import jax
import jax.numpy as jnp
from jax import lax
from jax.experimental import pallas as pl
from jax.experimental.pallas import tpu as pltpu

F32 = jnp.float32
BF16 = jnp.bfloat16
HIGHEST = lax.Precision.HIGHEST

RMS_EPS = 1e-6
GLA_HEADS = 4
GLA_GATE_RANK = 16
GLA_TAU = 16.0
GLA_CHUNK = 64
GLA_SUB = 8
DSA_HEADS = 16
DSA_HEAD_DIM = 128
IDX_HEADS = 8
IDX_DIM = 64
IDX_TOPK_MAX = 256
ROPE_THETA = 500000.0
ROPE_FRACTION = 4
N_MOD = 6
LOG2_E = 1.4426950408889634

LANES = 128
SUBLANES = 8
VMEM_LIMIT = 48 * 1024 * 1024


VMEM_LIMIT_LARGE = 56 * 1024 * 1024


def _params(*semantics, vmem_limit=VMEM_LIMIT):
    return pltpu.CompilerParams(dimension_semantics=semantics, vmem_limit_bytes=vmem_limit)


def _silu(x):
    return x * jax.nn.sigmoid(x)


def _mod_kernel(c_ref, w_ref, b_ref, o_ref):
    a = _silu(c_ref[...])
    o_ref[...] = jnp.dot(a, w_ref[...], precision=HIGHEST, preferred_element_type=F32) + b_ref[...]


def _modulation(c_pad, w_ada, b_ada, tn=2048):
    rows, d = c_pad.shape
    n = w_ada.shape[1]
    return pl.pallas_call(
        _mod_kernel,
        grid=(n // tn,),
        in_specs=[pl.BlockSpec((rows, d), lambda j: (0, 0)),
                  pl.BlockSpec((d, tn), lambda j: (0, j)),
                  pl.BlockSpec((1, tn), lambda j: (0, j))],
        out_specs=pl.BlockSpec((rows, tn), lambda j: (0, j)),
        out_shape=jax.ShapeDtypeStruct((rows, n), F32),
        compiler_params=_params("parallel"),
        name="adaln_mod",
    )(c_pad, w_ada, b_ada)


def _rms_modulate(x, gain, scale, shift):
    y = x * lax.rsqrt(jnp.mean(x * x, axis=-1, keepdims=True) + RMS_EPS)
    return (y * gain) * (1.0 + scale) + shift


def _rope_lanes(x, cos, sin, half, period, limit=LANES):
    lane = lax.broadcasted_iota(jnp.int32, x.shape, 1)
    in_head = lane % period
    upper = pltpu.roll(x, LANES - half, axis=1)
    lower = pltpu.roll(x, half, axis=1)
    live = lane < limit
    first = jnp.logical_and(live, in_head < half)
    second = jnp.logical_and(live, jnp.logical_and(in_head >= half, in_head < 2 * half))
    return jnp.where(first, x * cos - upper * sin,
                     jnp.where(second, lower * sin + x * cos, x))


PLAIN32, GLA_Q32, IDX_Q32, IDX_K32, PLAIN16, DSA_ROPE16, DSA_ROPE_Q16 = range(7)
DSA_Q_SCALE = (DSA_HEAD_DIM ** -0.5) * LOG2_E
IN_PROJ_ROPE_PARTS = 4
IN_PROJ_ROPE_ROWS = 256
IN_PROJ_TN = 512


def _norm_mod_kernel(x_ref, gain_ref, sc_ref, sh_ref, o_ref):
    o_ref[...] = _rms_modulate(x_ref[...], gain_ref[...], sc_ref[0], sh_ref[0]).astype(o_ref.dtype)


def _norm_modulate(x2d, gain, sc, sh, seq, tm=512):
    t, d = x2d.shape
    per_batch = seq // tm
    return pl.pallas_call(
        _norm_mod_kernel,
        grid=(t // tm,),
        in_specs=[pl.BlockSpec((tm, d), lambda i: (i, 0)),
                  pl.BlockSpec((1, d), lambda i: (0, 0)),
                  pl.BlockSpec((1, 1, d), lambda i: (i // per_batch, 0, 0)),
                  pl.BlockSpec((1, 1, d), lambda i: (i // per_batch, 0, 0))],
        out_specs=pl.BlockSpec((tm, d), lambda i: (i, 0)),
        out_shape=jax.ShapeDtypeStruct((t, d), BF16),
        compiler_params=_params("parallel"),
        name="norm1_modulate",
    )(x2d, gain, sc, sh)


def _in_proj_kernel(off_ref, kind_ref, h_ref, wt_ref, cos_d_ref, sin_d_ref, cos_i_ref, sin_i_ref,
                    o32_ref, o16_ref):
    kind = kind_ref[pl.program_id(1)]
    tm, tn = o32_ref.shape
    n_groups = tn // LANES
    rot_d = DSA_HEAD_DIM // ROPE_FRACTION
    rot_i = IDX_DIM // ROPE_FRACTION

    def product():
        return lax.dot_general(h_ref[...], wt_ref[...].astype(BF16), (((1,), (1,)), ((), ())),
                               preferred_element_type=F32)

    def rope_tile(cos_ref, sin_ref, out_ref, half, period, groups, pre_scale=1.0, limit=LANES):
        w = wt_ref[...].astype(BF16)
        part = tm // IN_PROJ_ROPE_PARTS
        for p in range(IN_PROJ_ROPE_PARTS):
            acc = lax.dot_general(h_ref[p * part:(p + 1) * part, :], w, (((1,), (1,)), ((), ())),
                                  preferred_element_type=F32)
            if pre_scale != 1.0:
                acc = acc * pre_scale
            for r in range(0, part, IN_PROJ_ROPE_ROWS):
                rows = slice(p * part + r, p * part + r + IN_PROJ_ROPE_ROWS)
                cos = cos_ref[rows, :]
                sin = sin_ref[rows, :]
                for g in range(n_groups):
                    cols = slice(g * LANES, (g + 1) * LANES)
                    x = acc[r:r + IN_PROJ_ROPE_ROWS, cols]
                    if g in groups:
                        x = _rope_lanes(x, cos, sin, half, period, limit)
                    out_ref[rows, cols] = x.astype(out_ref.dtype)

    @pl.when(kind == PLAIN32)
    def _():
        o32_ref[...] = product()

    @pl.when(kind == GLA_Q32)
    def _():
        o32_ref[...] = product() * ((wt_ref.shape[1] // 2 // GLA_HEADS) ** -0.5)

    @pl.when(kind == IDX_Q32)
    def _():
        rope_tile(cos_i_ref, sin_i_ref, o32_ref, rot_i // 2, IDX_DIM, range(n_groups), pre_scale=IDX_DIM ** -0.5)

    @pl.when(kind == IDX_K32)
    def _():
        rope_tile(cos_i_ref, sin_i_ref, o32_ref, rot_i // 2, IDX_DIM, range(1), limit=IDX_DIM)

    @pl.when(kind == PLAIN16)
    def _():
        o16_ref[...] = product().astype(BF16)

    @pl.when(kind == DSA_ROPE16)
    def _():
        rope_tile(cos_d_ref, sin_d_ref, o16_ref, rot_d // 2, DSA_HEAD_DIM, range(n_groups))

    @pl.when(kind == DSA_ROPE_Q16)
    def _():
        rope_tile(cos_d_ref, sin_d_ref, o16_ref, rot_d // 2, DSA_HEAD_DIM, range(n_groups),
                  pre_scale=DSA_Q_SCALE)


def _in_proj_plan(d):
    tn = IN_PROJ_TN
    gla_qk, dsa_w, idx_w = d // 2, DSA_HEADS * DSA_HEAD_DIM, IDX_HEADS * IDX_DIM
    names = ("g_q", "g_k", "g_v", "g_r", "g_lr", "d_q", "d_k", "d_v", "i_q", "i_k", "i_w", "gate_a", "gate_b")
    widths = (gla_qk, gla_qk, d, d, GLA_GATE_RANK, dsa_w, dsa_w, dsa_w, idx_w, IDX_DIM, IDX_HEADS, d, d)
    start, pos = {}, 0
    for nm, wd in zip(names, widths):
        start[nm] = pos
        pos += wd
    width = dict(zip(names, widths))
    assert start["i_w"] == start["i_k"] + IDX_DIM and IDX_DIM + IDX_HEADS <= LANES
    f32_groups = (("g_q", GLA_Q32), ("g_k", PLAIN32), ("g_r", PLAIN32), ("g_lr", PLAIN32), ("i_q", IDX_Q32),
                  ("i_k", IDX_K32), ("gate_a", PLAIN32), ("gate_b", PLAIN32))
    bf16_groups = (("g_v", PLAIN16), ("d_q", DSA_ROPE_Q16), ("d_k", DSA_ROPE16), ("d_v", PLAIN16))
    offsets, kinds, tile_of = [], [], {}
    for groups in (f32_groups, bf16_groups):
        base = len(offsets)
        for nm, kind in groups:
            tile_of[nm] = len(offsets) - base
            for t in range(-(-width[nm] // tn)):
                offsets.append(start[nm] + t * tn)
                kinds.append(kind)
        if groups is f32_groups:
            n32 = len(offsets)
    assert all(o % SUBLANES == 0 and o + tn <= pos for o in offsets)
    return offsets, kinds, n32, tile_of


def _in_project(h, w_in_t, tables, plan, tm=2048):
    t, d = h.shape
    tn = IN_PROJ_TN
    offsets, kinds, n32, _ = plan
    n16 = len(offsets) - n32
    tm = min(tm, t)
    once = pl.Buffered(1)
    table_spec = pl.BlockSpec((tm, LANES), lambda i, j, off, kind: (i, 0), pipeline_mode=once)
    grid_spec = pltpu.PrefetchScalarGridSpec(
        num_scalar_prefetch=2,
        grid=(t // tm, len(offsets)),
        in_specs=[pl.BlockSpec((tm, d), lambda i, j, off, kind: (i, 0), pipeline_mode=once),
                  pl.BlockSpec((pl.Element(tn), pl.Element(d)),
                               lambda i, j, off, kind: (pl.multiple_of(off[j], SUBLANES), 0)),
                  table_spec, table_spec, table_spec, table_spec],
        out_specs=[pl.BlockSpec((tm, tn), lambda i, j, off, kind: (i, jnp.minimum(j, n32 - 1))),
                   pl.BlockSpec((tm, tn), lambda i, j, off, kind: (i, jnp.maximum(j - n32, 0)))],
    )
    return pl.pallas_call(
        _in_proj_kernel,
        grid_spec=grid_spec,
        out_shape=[jax.ShapeDtypeStruct((t, n32 * tn), F32), jax.ShapeDtypeStruct((t, n16 * tn), BF16)],
        compiler_params=_params("parallel", "arbitrary"),
        name="in_proj",
    )(jnp.asarray(offsets, jnp.int32), jnp.asarray(kinds, jnp.int32), h, w_in_t, *tables)


def _split3_bf16(x):
    hi, rest = _split_bf16(x)
    mid, lo = _split_bf16(rest)
    return hi, mid, lo


def _gla_kernel(q_ref, k_ref, v_ref, gr_ref, sm_ref, gup_ref, gb_ref, ng_ref, tri_ref, o_ref, st_ref):
    @pl.when(pl.program_id(2) == 0)
    def _():
        st_ref[...] = jnp.zeros_like(st_ref)

    rows = q_ref.shape[0]
    dk = q_ref.shape[1]
    c, sub = GLA_CHUNK, GLA_SUB
    n_sub = c // sub

    a_hi, a_lo = _split_bf16(sm_ref[:, 0:GLA_GATE_RANK])
    gate_lhs = jnp.concatenate(
        [a_hi, a_lo, a_hi, jnp.zeros((rows, LANES - 3 * GLA_GATE_RANK), F32)], axis=1).astype(BF16)
    z = jnp.dot(gate_lhs, gup_ref[...], preferred_element_type=F32) + gb_ref[...]
    log_g = (jnp.minimum(z, 0.0) - jnp.log(1.0 + jnp.exp(-jnp.abs(z)))) * (LOG2_E / GLA_TAU)
    parts = jnp.dot(tri_ref[...], jnp.concatenate(_split3_bf16(log_g), axis=1).astype(BF16),
                    preferred_element_type=F32)
    b = (parts[:, 0:dk] + parts[:, dk:2 * dk]) + parts[:, 2 * dk:3 * dk]

    q = q_ref[...]
    k = k_ref[...]

    k_tiles = k.reshape(rows // sub, sub, dk)
    b_tiles = b.reshape(rows // sub, sub, dk)
    row = lax.broadcasted_iota(jnp.int32, (rows, 1), 0)
    row_in_sub = row % sub
    col = lax.broadcasted_iota(jnp.int32, (rows, c), 1)
    target = lax.broadcasted_iota(jnp.int32, (rows, c), 0) % c
    a_diag = jnp.zeros((rows, c), F32)
    for delta in range(sub):
        if delta == 0:
            k_d, b_d = k, b
        else:
            k_d = pltpu.roll(k_tiles, delta, axis=1).reshape(rows, dk)
            b_d = pltpu.roll(b_tiles, delta, axis=1).reshape(rows, dk)
        decay = jnp.exp2(jnp.where(row_in_sub >= delta, b - b_d, -jnp.inf))
        a = jnp.sum(q * k_d * decay, axis=-1, keepdims=True)
        a_diag = jnp.where(col == target - delta, a, a_diag)

    for ch in range(rows // c):
        base = ch * c
        qc = q[base:base + c]
        kc = k[base:base + c]
        bc = b[base:base + c]
        vc = v_ref[base:base + c, :]
        st = st_ref[...]

        o_inter = lax.dot_general((qc * jnp.exp2(bc)).astype(BF16), st.astype(BF16),
                                  (((1,), (1,)), ((), ())), preferred_element_type=F32)

        blocks = [jnp.zeros((sub, c), F32)]
        for i_sub in range(1, n_sub):
            lo = i_sub * sub
            ref = bc[lo - 1:lo]
            q_hat = qc[lo:lo + sub] * jnp.exp2(bc[lo:lo + sub] - ref)
            k_hat = kc[0:lo] * jnp.exp2(ref - bc[0:lo])
            k_pad = jnp.concatenate([k_hat, jnp.zeros((c - lo, dk), F32)], axis=0)
            blocks.append(lax.dot_general(q_hat.astype(BF16), k_pad.astype(BF16),
                                          (((1,), (1,)), ((), ())), preferred_element_type=F32))
        attn = jnp.concatenate(blocks, axis=0) + a_diag[base:base + c]

        o = jnp.dot(attn.astype(BF16), vc, preferred_element_type=F32) + o_inter

        b_last = bc[c - 1:c]
        k_dec = (kc * jnp.exp2(b_last - bc)).astype(BF16)
        st_ref[...] = st * jnp.exp2(b_last) + lax.dot_general(
            vc, k_dec, (((0,), (0,)), ((), ())), preferred_element_type=F32)

        o = o * lax.rsqrt(jnp.mean(o * o, axis=-1, keepdims=True) + RMS_EPS) * ng_ref[...]
        o_ref[base:base + c, :] = (o * _silu(gr_ref[base:base + c, :])).astype(o_ref.dtype)


def _gla(act32, act16, tile_of, gate_up, gbias, ngain, batch, seq, rows=256):
    dk = gate_up.shape[1] // GLA_HEADS
    dv = ngain.shape[1] // GLA_HEADS
    t = act32.shape[0]
    nr = seq // rows
    g_hi, g_lo = _split_bf16(gate_up)
    gup_stack = jnp.concatenate(
        [g_hi, g_hi, g_lo, jnp.zeros((LANES - 3 * GLA_GATE_RANK, gate_up.shape[1]), F32)], axis=0).astype(BF16)
    idx = jnp.arange(rows)
    tri = jnp.logical_and(idx[:, None] // GLA_CHUNK == idx[None, :] // GLA_CHUNK,
                          idx[None, :] <= idx[:, None]).astype(BF16)

    def cols(group, width):
        first = tile_of[group] * IN_PROJ_TN // width
        return lambda b, h, r: (b * nr + r, first + h)

    lr_block = tile_of["g_lr"] * IN_PROJ_TN // LANES
    return pl.pallas_call(
        _gla_kernel,
        grid=(batch, GLA_HEADS, nr),
        in_specs=[pl.BlockSpec((rows, dk), cols("g_q", dk)),
                  pl.BlockSpec((rows, dk), cols("g_k", dk)),
                  pl.BlockSpec((rows, dv), cols("g_v", dv)),
                  pl.BlockSpec((rows, dv), cols("g_r", dv)),
                  pl.BlockSpec((rows, LANES), lambda b, h, r: (b * nr + r, lr_block)),
                  pl.BlockSpec((LANES, dk), lambda b, h, r: (0, h)),
                  pl.BlockSpec((1, dk), lambda b, h, r: (0, h)),
                  pl.BlockSpec((1, dv), lambda b, h, r: (0, h)),
                  pl.BlockSpec((rows, rows), lambda b, h, r: (0, 0))],
        out_specs=pl.BlockSpec((rows, dv), lambda b, h, r: (b * nr + r, h)),
        out_shape=jax.ShapeDtypeStruct((t, GLA_HEADS * dv), BF16),
        scratch_shapes=[pltpu.VMEM((dv, dk), F32)],
        compiler_params=_params("parallel", "parallel", "arbitrary"),
        name="gla",
    )(act32, act32, act16, act32, act32, gup_stack, gbias, ngain, tri)


def _split_bf16(x):
    hi = x.astype(BF16).astype(F32)
    return hi, x - hi


def _sortable_bits_to_float(key):
    return lax.bitcast_convert_type(jnp.where(key < 0, key ^ jnp.int32(0x7FFFFFFF), key), F32)


SORT_KEY_NEG_INF = -2139095041


def _topk_cap_kernel(iq_ref, ik_ref, iw_ref, tri_ref, o_ref, score_ref, rhs_ref, cnt_ref, *, topk, tq):
    qi = pl.program_id(1)
    tk = tq
    nk = ik_ref.shape[0] // tk
    n_live = qi + 1
    w = iw_ref[0] * (IDX_HEADS ** -0.5)
    pad_q = jnp.zeros((tq, IDX_DIM), F32)
    for h in range(IDX_HEADS):
        hi, lo = _split_bf16(iq_ref[:, h * IDX_DIM:(h + 1) * IDX_DIM])
        rhs_ref[h] = jnp.concatenate([hi, hi, lo, pad_q], axis=1).astype(BF16)
    t_pos = qi * tq + lax.broadcasted_iota(jnp.int32, (tk, tq), 1)
    s_off = lax.broadcasted_iota(jnp.int32, (tk, tq), 0)
    pad_k = jnp.zeros((tk, IDX_DIM), F32)

    def block_rows(kb):
        return pl.ds(pl.multiple_of(kb * tk, tk), tk)

    def score_block(kb, carry):
        hi, lo = _split_bf16(ik_ref[block_rows(kb), 0:IDX_DIM])
        lhs = jnp.concatenate([hi, lo, hi, pad_k], axis=1).astype(BF16)
        score = jnp.zeros((tk, tq), F32)
        for h in range(IDX_HEADS):
            dots = lax.dot_general(lhs, rhs_ref[h], (((1,), (1,)), ((), ())), preferred_element_type=F32)
            score = score + w[h:h + 1, :] * jnp.maximum(dots, 0.0)
        score_ref[block_rows(kb), :] = jnp.where(kb * tk + s_off <= t_pos, score + 0.0, -jnp.inf)
        return carry

    lax.fori_loop(0, n_live, score_block, 0)

    def count(pred):
        for n in range(1, nk + 1):
            @pl.when(n_live == n)
            def _(n=n):
                part = jnp.zeros((SUBLANES, tq), F32)
                for kb in range(n):
                    hit = jnp.where(pred(score_ref[kb * tk:(kb + 1) * tk, :]), 1.0, 0.0)
                    part = part + jnp.sum(hit.reshape(tk // SUBLANES, SUBLANES, tq), axis=0)
                cnt_ref[...] = part
        return jnp.sum(cnt_ref[...], axis=0, keepdims=True)

    def search(i, ans):
        cand = ans ^ jnp.left_shift(jnp.int32(1), 31 - i)
        cand_f = _sortable_bits_to_float(cand)
        cnt = count(lambda s: s >= cand_f)
        accept = jnp.logical_or(cnt >= float(topk), cand < SORT_KEY_NEG_INF)
        return jnp.where(accept, cand, ans)

    thr = _sortable_bits_to_float(
        lax.fori_loop(0, 32, search, jnp.full((1, tq), jnp.iinfo(jnp.int32).min, jnp.int32)))
    need = float(topk) - count(lambda s: s > thr)

    def emit(kb, seen):
        blk = score_ref[block_rows(kb), :]
        eq = jnp.where(blk == thr, 1.0, 0.0)
        rank = jnp.dot(tri_ref[...], eq.astype(BF16), preferred_element_type=F32) + seen
        chosen = jnp.logical_or(blk > thr, jnp.logical_and(blk == thr, rank <= need))
        o_ref[0, block_rows(kb), :] = jnp.where(
            jnp.logical_and(chosen, kb * tk + s_off <= t_pos), jnp.inf, -jnp.inf)
        return seen + jnp.sum(eq, axis=0, keepdims=True)

    lax.fori_loop(0, n_live, emit, jnp.zeros((1, tq), F32))

    def fill(kb, carry):
        o_ref[0, block_rows(kb), :] = jnp.full((tk, tq), -jnp.inf, F32)
        return carry

    lax.fori_loop(n_live, nk, fill, 0)


def _topk_cap(act32, tile_of, iw_t, batch, seq, topk, tq=256):
    tq = min(tq, seq)
    nq = seq // tq
    width = IDX_HEADS * IDX_DIM
    iq_block = tile_of["i_q"] * IN_PROJ_TN // width
    ik_block = tile_of["i_k"] * IN_PROJ_TN // LANES
    tri = jnp.tril(jnp.ones((tq, tq), BF16))
    return pl.pallas_call(
        functools.partial(_topk_cap_kernel, topk=topk, tq=tq),
        grid=(batch, nq),
        in_specs=[pl.BlockSpec((tq, width), lambda b, i: (b * nq + i, iq_block)),
                  pl.BlockSpec((seq, LANES), lambda b, i: (b, ik_block)),
                  pl.BlockSpec((1, IDX_HEADS, tq), lambda b, i: (b, 0, i)),
                  pl.BlockSpec((tq, tq), lambda b, i: (0, 0))],
        out_specs=pl.BlockSpec((1, seq, tq), lambda b, i: (b, 0, i)),
        out_shape=jax.ShapeDtypeStruct((batch, seq, seq), F32),
        scratch_shapes=[pltpu.VMEM((seq, tq), F32),
                        pltpu.VMEM((IDX_HEADS, tq, 4 * IDX_DIM), BF16),
                        pltpu.VMEM((SUBLANES, tq), F32)],
        compiler_params=_params("parallel", "parallel"),
        name="indexer_topk_cap",
    )(act32, act32, iw_t, tri)


ATTN_HEADS_PER_STEP = 8


def _attn_kernel(q_ref, k_ref, v_ref, cap_ref, o_ref, acc_ref, vt_ref):
    qi = pl.program_id(2)
    tq = q_ref.shape[0]
    tk = tq
    dh = DSA_HEAD_DIM
    group = q_ref.shape[1] // dh
    heads = [slice(g * dh, (g + 1) * dh) for g in range(group)]
    acc_ref[...] = jnp.zeros_like(acc_ref)

    @pl.when(qi == 0)
    def _():
        for g in range(group):
            for kb in range(v_ref.shape[0] // tk):
                vt_ref[g, kb] = v_ref[kb * tk:(kb + 1) * tk, heads[g]].T

    def body(kj, carry):
        rows = pl.ds(pl.multiple_of(kj * tk, tk), tk)
        cap = cap_ref[0, rows, :]
        logits = [lax.dot_general(k_ref[rows, cols], q_ref[:, cols], (((1,), (1,)), ((), ())),
                                  preferred_element_type=F32) for cols in heads]
        new = []
        for g in range(group):
            m, l = carry[g]
            s = jnp.minimum(logits[g], cap)
            m_new = jnp.maximum(m, jnp.max(s, axis=0, keepdims=True))
            m_safe = jnp.where(m_new == -jnp.inf, 0.0, m_new)
            alpha = jnp.exp2(m - m_safe)
            p = jnp.exp2(s - m_safe)
            new.append((m_new, alpha * l + jnp.sum(p, axis=0, keepdims=True)))
            update = jnp.dot(vt_ref[g, kj], p.astype(BF16), preferred_element_type=F32)
            acc_ref[g] = alpha * acc_ref[g] + update
        return tuple(new)

    init = tuple((jnp.full((1, tq), -jnp.inf, F32), jnp.zeros((1, tq), F32)) for _ in range(group))
    final = lax.fori_loop(0, qi + 1, body, init)
    for g in range(group):
        o_ref[:, g * dh:(g + 1) * dh] = (acc_ref[g] / final[g][1]).T.astype(o_ref.dtype)


def _attention(act16, tile_of, cap_t, batch, seq, tq=256):
    tq = min(tq, seq)
    nq = seq // tq
    width = ATTN_HEADS_PER_STEP * DSA_HEAD_DIM
    n_groups = DSA_HEADS // ATTN_HEADS_PER_STEP
    t = act16.shape[0]
    q_block, k_block, v_block = (tile_of[nm] * IN_PROJ_TN // width for nm in ("d_q", "d_k", "d_v"))
    return pl.pallas_call(
        _attn_kernel,
        grid=(batch, n_groups, nq),
        in_specs=[pl.BlockSpec((tq, width), lambda b, h, i: (b * nq + i, q_block + h)),
                  pl.BlockSpec((seq, width), lambda b, h, i: (b, k_block + h)),
                  pl.BlockSpec((seq, width), lambda b, h, i: (b, v_block + h)),
                  pl.BlockSpec((1, seq, tq), lambda b, h, i: (b, 0, i))],
        out_specs=pl.BlockSpec((tq, width), lambda b, h, i: (b * nq + i, h)),
        out_shape=jax.ShapeDtypeStruct((t, DSA_HEADS * DSA_HEAD_DIM), BF16),
        scratch_shapes=[pltpu.VMEM((ATTN_HEADS_PER_STEP, DSA_HEAD_DIM, tq), F32),
                        pltpu.VMEM((ATTN_HEADS_PER_STEP, seq // tq, DSA_HEAD_DIM, tq), BF16)],
        compiler_params=_params("parallel", "parallel", "arbitrary"),
        name="dsa_attention",
    )(act16, act16, act16, cap_t)


def _merge_kernel(oa_ref, ob_ref, wa_ref, wb_ref, ga_ref, gb_ref, o_ref):
    ya = jnp.dot(oa_ref[...], wa_ref[...].astype(BF16), preferred_element_type=F32)
    yb = jnp.dot(ob_ref[...], wb_ref[...].astype(BF16), preferred_element_type=F32)
    o_ref[...] = (jax.nn.sigmoid(ga_ref[...]) * ya + jax.nn.sigmoid(gb_ref[...]) * yb).astype(o_ref.dtype)


def _merge(o_a, o_b, w_a, w_b, act32, tile_of, tm=1024):
    t, d = o_a.shape
    n = w_a.shape[1]
    tn = IN_PROJ_TN
    tm = min(tm, t)
    ga_tile, gb_tile = tile_of["gate_a"], tile_of["gate_b"]
    return pl.pallas_call(
        _merge_kernel,
        grid=(t // tm, n // tn),
        in_specs=[pl.BlockSpec((tm, d), lambda i, j: (i, 0)),
                  pl.BlockSpec((tm, d), lambda i, j: (i, 0)),
                  pl.BlockSpec((d, tn), lambda i, j: (0, j)),
                  pl.BlockSpec((d, tn), lambda i, j: (0, j)),
                  pl.BlockSpec((tm, tn), lambda i, j: (i, ga_tile + j)),
                  pl.BlockSpec((tm, tn), lambda i, j: (i, gb_tile + j))],
        out_specs=pl.BlockSpec((tm, tn), lambda i, j: (i, j)),
        out_shape=jax.ShapeDtypeStruct((t, n), BF16),
        compiler_params=_params("parallel", "parallel"),
        name="branch_merge",
    )(o_a, o_b, w_a, w_b, act32, act32)


def _mixer_out_kernel(m_ref, w_ref, x_ref, g_ref, gain_ref, sc_ref, sh_ref, x1_ref, h2_ref):
    y = jnp.dot(m_ref[...], w_ref[...], preferred_element_type=F32)
    x1 = x_ref[...] + g_ref[0] * y
    x1_ref[...] = x1
    h2_ref[...] = _rms_modulate(x1, gain_ref[...], sc_ref[0], sh_ref[0]).astype(h2_ref.dtype)


def _mixer_out(merged, w, x2d, g1, gain2, sc2, sh2, seq, tm=512):
    t, d = x2d.shape
    tm = min(tm, seq)
    per_batch = seq // tm
    vec = pl.BlockSpec((1, 1, d), lambda i: (i // per_batch, 0, 0))
    row = pl.BlockSpec((tm, d), lambda i: (i, 0))
    return pl.pallas_call(
        _mixer_out_kernel,
        grid=(t // tm,),
        in_specs=[row, pl.BlockSpec((d, d), lambda i: (0, 0), pipeline_mode=pl.Buffered(1)), row, vec,
                  pl.BlockSpec((1, d), lambda i: (0, 0)), vec, vec],
        out_specs=[row, row],
        out_shape=[jax.ShapeDtypeStruct((t, d), F32), jax.ShapeDtypeStruct((t, d), BF16)],
        compiler_params=_params("parallel"),
        name="mixer_out_norm2",
    )(merged, w, x2d, g1, gain2, sc2, sh2)


def _ffn_up_kernel(h_ref, wg_ref, wu_ref, o_ref):
    g = jnp.dot(h_ref[...], wg_ref[...].astype(BF16), preferred_element_type=F32)
    u = jnp.dot(h_ref[...], wu_ref[...].astype(BF16), preferred_element_type=F32)
    o_ref[...] = (_silu(g) * u).astype(o_ref.dtype)


def _ffn_up(h2, w_gate_up, tm=1024, tn=512):
    t, d = h2.shape
    d_ff = w_gate_up.shape[1] // 2
    tm = min(tm, t)
    nj = d_ff // tn
    return pl.pallas_call(
        _ffn_up_kernel,
        grid=(t // tm, nj),
        in_specs=[pl.BlockSpec((tm, d), lambda i, j: (i, 0)),
                  pl.BlockSpec((d, tn), lambda i, j: (0, j)),
                  pl.BlockSpec((d, tn), lambda i, j: (0, nj + j))],
        out_specs=pl.BlockSpec((tm, tn), lambda i, j: (i, j)),
        out_shape=jax.ShapeDtypeStruct((t, d_ff), BF16),
        compiler_params=_params("parallel", "parallel"),
        name="ffn_up",
    )(h2, w_gate_up, w_gate_up)


def _ffn_down_kernel(a_ref, w_ref, x_ref, g_ref, gain_ref, o_ref):
    x2 = x_ref[...] + g_ref[0] * jnp.dot(a_ref[...], w_ref[...], preferred_element_type=F32)
    y = x2 * lax.rsqrt(jnp.mean(x2 * x2, axis=-1, keepdims=True) + RMS_EPS)
    o_ref[...] = y * gain_ref[...]


def _ffn_down(act, w_down, x1, g2, final_gain, seq, tm=512):
    t, d_ff = act.shape
    d = w_down.shape[1]
    tm = min(tm, seq)
    per_batch = seq // tm
    return pl.pallas_call(
        _ffn_down_kernel,
        grid=(t // tm,),
        in_specs=[pl.BlockSpec((tm, d_ff), lambda i: (i, 0)),
                  pl.BlockSpec((d_ff, d), lambda i: (0, 0), pipeline_mode=pl.Buffered(1)),
                  pl.BlockSpec((tm, d), lambda i: (i, 0)),
                  pl.BlockSpec((1, 1, d), lambda i: (i // per_batch, 0, 0)),
                  pl.BlockSpec((1, d), lambda i: (0, 0))],
        out_specs=pl.BlockSpec((tm, d), lambda i: (i, 0)),
        out_shape=jax.ShapeDtypeStruct((t, d), F32),
        compiler_params=_params("parallel", vmem_limit=VMEM_LIMIT_LARGE),
        name="ffn_down_final_norm",
    )(act, w_down, x1, g2, final_gain)


def _rope_tables(positions_flat, rot, period):
    half = rot // 2
    assert period % half == 0 and LANES % half == 0
    inv_freq = jnp.power(ROPE_THETA, -jnp.arange(0, rot, 2, dtype=F32) / rot)
    ang = positions_flat.astype(F32)[:, None] * inv_freq[None, :]
    reps = (1, LANES // half)
    return jnp.tile(jnp.cos(ang), reps), jnp.tile(jnp.sin(ang), reps)


def _layer(x2d, mod, positions_flat, batch, seq, norm1_gain, norm2_gain, w_in, gla_gate_up,
           gla_gate_bias, gla_norm_gain, w_branch_gla, w_branch_dsa, w_merge_out, w_ffn_gate_up,
           w_ffn_down, final_gain):
    d = x2d.shape[1]
    sh1, sc1, g1, sh2, sc2, g2 = [mod[:, i * d:(i + 1) * d][:, None, :] for i in range(N_MOD)]

    plan = _in_proj_plan(d)
    tile_of = plan[3]
    cos_d, sin_d = _rope_tables(positions_flat, DSA_HEAD_DIM // ROPE_FRACTION, DSA_HEAD_DIM)
    cos_i, sin_i = _rope_tables(positions_flat, IDX_DIM // ROPE_FRACTION, IDX_DIM)
    h = _norm_modulate(x2d, norm1_gain[None, :], sc1, sh1, seq)
    act32, act16 = _in_project(h, w_in.T, (cos_d, sin_d, cos_i, sin_i), plan)

    o_a = _gla(act32, act16, tile_of, gla_gate_up, gla_gate_bias[None, :], gla_norm_gain[None, :],
               batch, seq, rows=min(256, seq))

    iw_col = tile_of["i_k"] * IN_PROJ_TN + IDX_DIM
    iw_t = jnp.transpose(act32[:, iw_col:iw_col + IDX_HEADS].reshape(batch, seq, IDX_HEADS), (0, 2, 1))
    topk = min(IDX_TOPK_MAX, seq // 4)
    cap_t = _topk_cap(act32, tile_of, iw_t, batch, seq, topk)
    o_b = _attention(act16, tile_of, cap_t, batch, seq)

    merged = _merge(o_a, o_b, w_branch_gla, w_branch_dsa, act32, tile_of)
    x1, h2 = _mixer_out(merged, w_merge_out.astype(BF16), x2d, g1, norm2_gain[None, :], sc2, sh2, seq)
    act = _ffn_up(h2, w_ffn_gate_up)
    return _ffn_down(act, w_ffn_down.astype(BF16), x1, g2, final_gain[None, :], seq)


def kernel(x, c, positions, norm1_gain, norm2_gain, w_ada, b_ada, w_in, gla_gate_up, gla_gate_bias,
           gla_norm_gain, w_branch_gla, w_branch_dsa, w_merge_out, w_ffn_gate_up, w_ffn_down,
           final_norm_gain):
    batch, seq, d = x.shape
    depth = w_in.shape[0]
    assert depth == 1, "the final RMSNorm is fused into the single layer's FFN kernel"
    x2d = x.reshape(batch * seq, d)
    c_pad = jnp.zeros((SUBLANES, d), F32).at[:batch].set(c)
    mod = _modulation(c_pad, w_ada[0], b_ada[0][None, :])[:batch]
    out = _layer(x2d, mod, positions.reshape(-1), batch, seq, norm1_gain[0], norm2_gain[0], w_in[0],
                 gla_gate_up[0], gla_gate_bias[0], gla_norm_gain[0], w_branch_gla[0], w_branch_dsa[0],
                 w_merge_out[0], w_ffn_gate_up[0], w_ffn_down[0], final_norm_gain)
    return out.reshape(batch, seq, d)
```

```python
import functools

import jax
import jax.numpy as jnp
from jax import lax
from jax.experimental import pallas as pl
from jax.experimental.pallas import tpu as pltpu

F32 = jnp.float32
BF16 = jnp.bfloat16
HIGHEST = lax.Precision.HIGHEST

RMS_EPS = 1e-6
GLA_HEADS = 4
GLA_GATE_RANK = 16
GLA_TAU = 16.0
GLA_CHUNK = 64
GLA_SUB = 8
DSA_HEADS = 16
DSA_HEAD_DIM = 128
IDX_HEADS = 8
IDX_DIM = 64
IDX_TOPK_MAX = 256
ROPE_THETA = 500000.0
ROPE_FRACTION = 4
N_MOD = 6
LOG2_E = 1.4426950408889634

LANES = 128
SUBLANES = 8
VMEM_LIMIT = 48 * 1024 * 1024


VMEM_LIMIT_LARGE = 56 * 1024 * 1024


def _params(*semantics, vmem_limit=VMEM_LIMIT):
    return pltpu.CompilerParams(dimension_semantics=semantics, vmem_limit_bytes=vmem_limit)


def _silu(x):
    return x * jax.nn.sigmoid(x)


def _mod_kernel(c_ref, w_ref, b_ref, o_ref):
    a = _silu(c_ref[...])
    o_ref[...] = jnp.dot(a, w_ref[...], precision=HIGHEST, preferred_element_type=F32) + b_ref[...]


def _modulation(c_pad, w_ada, b_ada, tn=2048):
    rows, d = c_pad.shape
    n = w_ada.shape[1]
    return pl.pallas_call(
        _mod_kernel,
        grid=(n // tn,),
        in_specs=[pl.BlockSpec((rows, d), lambda j: (0, 0)),
                  pl.BlockSpec((d, tn), lambda j: (0, j)),
                  pl.BlockSpec((1, tn), lambda j: (0, j))],
        out_specs=pl.BlockSpec((rows, tn), lambda j: (0, j)),
        out_shape=jax.ShapeDtypeStruct((rows, n), F32),
        compiler_params=_params("parallel"),
        name="adaln_mod",
    )(c_pad, w_ada, b_ada)


def _rms_modulate(x, gain, scale, shift):
    y = x * lax.rsqrt(jnp.mean(x * x, axis=-1, keepdims=True) + RMS_EPS)
    return (y * gain) * (1.0 + scale) + shift


def _rope_lanes(x, cos, sin, half, period, limit=LANES):
    lane = lax.broadcasted_iota(jnp.int32, x.shape, 1)
    in_head = lane % period
    upper = pltpu.roll(x, LANES - half, axis=1)
    lower = pltpu.roll(x, half, axis=1)
    live = lane < limit
    first = jnp.logical_and(live, in_head < half)
    second = jnp.logical_and(live, jnp.logical_and(in_head >= half, in_head < 2 * half))
    return jnp.where(first, x * cos - upper * sin,
                     jnp.where(second, lower * sin + x * cos, x))


PLAIN32, GLA_Q32, IDX_Q32, SMALL32, PLAIN16, DSA_ROPE16, DSA_ROPE_Q16 = range(7)
SMALL_GLR_BLOCK, SMALL_IK_BLOCK = 0, 1
DSA_Q_SCALE = (DSA_HEAD_DIM ** -0.5) * LOG2_E
IN_PROJ_ROPE_PARTS = 4
IN_PROJ_ROPE_ROWS = 256
IN_PROJ_TN = 512


def _norm_mod_kernel(x_ref, gain_ref, sc_ref, sh_ref, o_ref):
    o_ref[...] = _rms_modulate(x_ref[...], gain_ref[...], sc_ref[0], sh_ref[0]).astype(o_ref.dtype)


def _norm_modulate(x2d, gain, sc, sh, seq, tm=512):
    t, d = x2d.shape
    per_batch = seq // tm
    return pl.pallas_call(
        _norm_mod_kernel,
        grid=(t // tm,),
        in_specs=[pl.BlockSpec((tm, d), lambda i: (i, 0)),
                  pl.BlockSpec((1, d), lambda i: (0, 0)),
                  pl.BlockSpec((1, 1, d), lambda i: (i // per_batch, 0, 0)),
                  pl.BlockSpec((1, 1, d), lambda i: (i // per_batch, 0, 0))],
        out_specs=pl.BlockSpec((tm, d), lambda i: (i, 0)),
        out_shape=jax.ShapeDtypeStruct((t, d), BF16),
        compiler_params=_params("parallel"),
        name="norm1_modulate",
    )(x2d, gain, sc, sh)


def _in_proj_kernel(off_ref, kind_ref, h_ref, wt_ref, ws_ref, cos_d_ref, sin_d_ref, cos_i_ref, sin_i_ref,
                    o32_ref, o16_ref):
    kind = kind_ref[pl.program_id(1)]
    tm, tn = o32_ref.shape
    n_groups = tn // LANES
    rot_d = DSA_HEAD_DIM // ROPE_FRACTION
    rot_i = IDX_DIM // ROPE_FRACTION

    def product():
        return lax.dot_general(h_ref[...], wt_ref[...].astype(BF16), (((1,), (1,)), ((), ())),
                               preferred_element_type=F32)

    def rope_tile(cos_ref, sin_ref, out_ref, half, period, groups, pre_scale=1.0, limit=LANES, w_ref=wt_ref):
        w = w_ref[...].astype(BF16)
        part = tm // IN_PROJ_ROPE_PARTS
        step = min(IN_PROJ_ROPE_ROWS, part)
        for p in range(IN_PROJ_ROPE_PARTS):
            acc = lax.dot_general(h_ref[p * part:(p + 1) * part, :], w, (((1,), (1,)), ((), ())),
                                  preferred_element_type=F32)
            if pre_scale != 1.0:
                acc = acc * pre_scale
            for r in range(0, part, step):
                rows = slice(p * part + r, p * part + r + step)
                cos = cos_ref[rows, :]
                sin = sin_ref[rows, :]
                for g in range(n_groups):
                    cols = slice(g * LANES, (g + 1) * LANES)
                    x = acc[r:r + step, cols]
                    if g in groups:
                        x = _rope_lanes(x, cos, sin, half, period, limit)
                    out_ref[rows, cols] = x.astype(out_ref.dtype)

    @pl.when(kind == PLAIN32)
    def _():
        o32_ref[...] = product()

    @pl.when(kind == GLA_Q32)
    def _():
        o32_ref[...] = product() * ((wt_ref.shape[1] // 2 // GLA_HEADS) ** -0.5)

    @pl.when(kind == IDX_Q32)
    def _():
        rope_tile(cos_i_ref, sin_i_ref, o32_ref, rot_i // 2, IDX_DIM, range(n_groups), pre_scale=IDX_DIM ** -0.5)

    @pl.when(kind == SMALL32)
    def _():
        rope_tile(cos_i_ref, sin_i_ref, o32_ref, rot_i // 2, IDX_DIM, (SMALL_IK_BLOCK,), limit=IDX_DIM,
                  w_ref=ws_ref)

    @pl.when(kind == PLAIN16)
    def _():
        o16_ref[...] = product().astype(BF16)

    @pl.when(kind == DSA_ROPE16)
    def _():
        rope_tile(cos_d_ref, sin_d_ref, o16_ref, rot_d // 2, DSA_HEAD_DIM, range(n_groups))

    @pl.when(kind == DSA_ROPE_Q16)
    def _():
        rope_tile(cos_d_ref, sin_d_ref, o16_ref, rot_d // 2, DSA_HEAD_DIM, range(n_groups),
                  pre_scale=DSA_Q_SCALE)


def _in_proj_plan(d):
    tn = IN_PROJ_TN
    gla_qk, dsa_w, idx_w = d // 2, DSA_HEADS * DSA_HEAD_DIM, IDX_HEADS * IDX_DIM
    names = ("g_q", "g_k", "g_v", "g_r", "g_lr", "d_q", "d_k", "d_v", "i_q", "i_k", "i_w", "gate_a", "gate_b")
    widths = (gla_qk, gla_qk, d, d, GLA_GATE_RANK, dsa_w, dsa_w, dsa_w, idx_w, IDX_DIM, IDX_HEADS, d, d)
    start, pos = {}, 0
    for nm, wd in zip(names, widths):
        start[nm] = pos
        pos += wd
    width = dict(zip(names, widths))
    start["small"], width["small"] = 0, tn
    assert start["i_w"] == start["i_k"] + IDX_DIM and IDX_DIM + IDX_HEADS <= LANES and GLA_GATE_RANK <= LANES
    f32_groups = (("g_q", GLA_Q32), ("g_k", PLAIN32), ("g_r", PLAIN32), ("small", SMALL32), ("i_q", IDX_Q32),
                  ("gate_a", PLAIN32), ("gate_b", PLAIN32))
    bf16_groups = (("g_v", PLAIN16), ("d_q", DSA_ROPE_Q16), ("d_k", DSA_ROPE16), ("d_v", PLAIN16))
    offsets, kinds, tile_of = [], [], {}
    for groups in (f32_groups, bf16_groups):
        base = len(offsets)
        for nm, kind in groups:
            tile_of[nm] = len(offsets) - base
            for t in range(-(-width[nm] // tn)):
                offsets.append(start[nm] + t * tn)
                kinds.append(kind)
        if groups is f32_groups:
            n32 = len(offsets)
    assert all(o % SUBLANES == 0 and o + tn <= pos for o in offsets)
    return offsets, kinds, n32, tile_of, start


def _in_project(h, w_in_t, tables, plan, tm=2048):
    t, d = h.shape
    tn = IN_PROJ_TN
    offsets, kinds, n32, _, start = plan
    n16 = len(offsets) - n32
    tm = min(tm, t)
    w_small = jnp.zeros((tn, d), F32)
    w_small = lax.dynamic_update_slice(
        w_small, w_in_t[start["g_lr"]:start["g_lr"] + GLA_GATE_RANK], (SMALL_GLR_BLOCK * LANES, 0))
    w_small = lax.dynamic_update_slice(
        w_small, w_in_t[start["i_k"]:start["i_k"] + IDX_DIM + IDX_HEADS], (SMALL_IK_BLOCK * LANES, 0))
    once = pl.Buffered(1)
    table_spec = pl.BlockSpec((tm, LANES), lambda i, j, off, kind: (i, 0), pipeline_mode=once)
    grid_spec = pltpu.PrefetchScalarGridSpec(
        num_scalar_prefetch=2,
        grid=(t // tm, len(offsets)),
        in_specs=[pl.BlockSpec((tm, d), lambda i, j, off, kind: (i, 0), pipeline_mode=once),
                  pl.BlockSpec((pl.Element(tn), pl.Element(d)),
                               lambda i, j, off, kind: (pl.multiple_of(off[j], SUBLANES), 0)),
                  pl.BlockSpec((tn, d), lambda i, j, off, kind: (0, 0), pipeline_mode=once),
                  table_spec, table_spec, table_spec, table_spec],
        out_specs=[pl.BlockSpec((tm, tn), lambda i, j, off, kind: (i, jnp.minimum(j, n32 - 1))),
                   pl.BlockSpec((tm, tn), lambda i, j, off, kind: (i, jnp.maximum(j - n32, 0)))],
    )
    return pl.pallas_call(
        _in_proj_kernel,
        grid_spec=grid_spec,
        out_shape=[jax.ShapeDtypeStruct((t, n32 * tn), F32), jax.ShapeDtypeStruct((t, n16 * tn), BF16)],
        compiler_params=_params("parallel", "arbitrary"),
        name="in_proj",
    )(jnp.asarray(offsets, jnp.int32), jnp.asarray(kinds, jnp.int32), h, w_in_t, w_small, *tables)


def _split3_bf16(x):
    hi, rest = _split_bf16(x)
    mid, lo = _split_bf16(rest)
    return hi, mid, lo


def _gla_kernel(q_ref, k_ref, v_ref, gr_ref, sm_ref, gup_ref, gb_ref, ng_ref, tri_ref, o_ref, st_ref):
    @pl.when(pl.program_id(2) == 0)
    def _():
        st_ref[...] = jnp.zeros_like(st_ref)

    rows = q_ref.shape[0]
    dk = q_ref.shape[1]
    c, sub = GLA_CHUNK, GLA_SUB
    n_sub = c // sub

    a_hi, a_lo = _split_bf16(sm_ref[:, 0:GLA_GATE_RANK])
    gate_lhs = jnp.concatenate(
        [a_hi, a_lo, a_hi, jnp.zeros((rows, LANES - 3 * GLA_GATE_RANK), F32)], axis=1).astype(BF16)
    z = jnp.dot(gate_lhs, gup_ref[...], preferred_element_type=F32) + gb_ref[...]
    log_g = (jnp.minimum(z, 0.0) - jnp.log(1.0 + jnp.exp(-jnp.abs(z)))) * (LOG2_E / GLA_TAU)
    parts = jnp.dot(tri_ref[...], jnp.concatenate(_split3_bf16(log_g), axis=1).astype(BF16),
                    preferred_element_type=F32)
    b = (parts[:, 0:dk] + parts[:, dk:2 * dk]) + parts[:, 2 * dk:3 * dk]

    q = q_ref[...]
    k = k_ref[...]

    k_tiles = k.reshape(rows // sub, sub, dk)
    b_tiles = b.reshape(rows // sub, sub, dk)
    row = lax.broadcasted_iota(jnp.int32, (rows, 1), 0)
    row_in_sub = row % sub
    col = lax.broadcasted_iota(jnp.int32, (rows, c), 1)
    target = lax.broadcasted_iota(jnp.int32, (rows, c), 0) % c
    a_diag = jnp.zeros((rows, c), F32)
    for delta in range(sub):
        if delta == 0:
            k_d, b_d = k, b
        else:
            k_d = pltpu.roll(k_tiles, delta, axis=1).reshape(rows, dk)
            b_d = pltpu.roll(b_tiles, delta, axis=1).reshape(rows, dk)
        decay = jnp.exp2(jnp.where(row_in_sub >= delta, b - b_d, -jnp.inf))
        a = jnp.sum(q * k_d * decay, axis=-1, keepdims=True)
        a_diag = jnp.where(col == target - delta, a, a_diag)

    for ch in range(rows // c):
        base = ch * c
        qc = q[base:base + c]
        kc = k[base:base + c]
        bc = b[base:base + c]
        vc = v_ref[base:base + c, :]
        st = st_ref[...]

        o_inter = lax.dot_general((qc * jnp.exp2(bc)).astype(BF16), st.astype(BF16),
                                  (((1,), (1,)), ((), ())), preferred_element_type=F32)

        blocks = [jnp.zeros((sub, c), F32)]
        for i_sub in range(1, n_sub):
            lo = i_sub * sub
            ref = bc[lo - 1:lo]
            q_hat = qc[lo:lo + sub] * jnp.exp2(bc[lo:lo + sub] - ref)
            k_hat = kc[0:lo] * jnp.exp2(ref - bc[0:lo])
            k_pad = jnp.concatenate([k_hat, jnp.zeros((c - lo, dk), F32)], axis=0)
            blocks.append(lax.dot_general(q_hat.astype(BF16), k_pad.astype(BF16),
                                          (((1,), (1,)), ((), ())), preferred_element_type=F32))
        attn = jnp.concatenate(blocks, axis=0) + a_diag[base:base + c]

        o = jnp.dot(attn.astype(BF16), vc, preferred_element_type=F32) + o_inter

        b_last = bc[c - 1:c]
        k_dec = (kc * jnp.exp2(b_last - bc)).astype(BF16)
        st_ref[...] = st * jnp.exp2(b_last) + lax.dot_general(
            vc, k_dec, (((0,), (0,)), ((), ())), preferred_element_type=F32)

        o = o * lax.rsqrt(jnp.mean(o * o, axis=-1, keepdims=True) + RMS_EPS) * ng_ref[...]
        o_ref[base:base + c, :] = (o * _silu(gr_ref[base:base + c, :])).astype(o_ref.dtype)


def _gla(act32, act16, tile_of, gate_up, gbias, ngain, batch, seq, rows=256):
    dk = gate_up.shape[1] // GLA_HEADS
    dv = ngain.shape[1] // GLA_HEADS
    t = act32.shape[0]
    nr = seq // rows
    g_hi, g_lo = _split_bf16(gate_up)
    gup_stack = jnp.concatenate(
        [g_hi, g_hi, g_lo, jnp.zeros((LANES - 3 * GLA_GATE_RANK, gate_up.shape[1]), F32)], axis=0).astype(BF16)
    idx = jnp.arange(rows)
    tri = jnp.logical_and(idx[:, None] // GLA_CHUNK == idx[None, :] // GLA_CHUNK,
                          idx[None, :] <= idx[:, None]).astype(BF16)

    def cols(group, width):
        first = tile_of[group] * IN_PROJ_TN // width
        return lambda b, h, r: (b * nr + r, first + h)

    lr_block = tile_of["small"] * IN_PROJ_TN // LANES + SMALL_GLR_BLOCK
    return pl.pallas_call(
        _gla_kernel,
        grid=(batch, GLA_HEADS, nr),
        in_specs=[pl.BlockSpec((rows, dk), cols("g_q", dk)),
                  pl.BlockSpec((rows, dk), cols("g_k", dk)),
                  pl.BlockSpec((rows, dv), cols("g_v", dv)),
                  pl.BlockSpec((rows, dv), cols("g_r", dv)),
                  pl.BlockSpec((rows, LANES), lambda b, h, r: (b * nr + r, lr_block)),
                  pl.BlockSpec((LANES, dk), lambda b, h, r: (0, h)),
                  pl.BlockSpec((1, dk), lambda b, h, r: (0, h)),
                  pl.BlockSpec((1, dv), lambda b, h, r: (0, h)),
                  pl.BlockSpec((rows, rows), lambda b, h, r: (0, 0))],
        out_specs=pl.BlockSpec((rows, dv), lambda b, h, r: (b * nr + r, h)),
        out_shape=jax.ShapeDtypeStruct((t, GLA_HEADS * dv), BF16),
        scratch_shapes=[pltpu.VMEM((dv, dk), F32)],
        compiler_params=_params("parallel", "parallel", "arbitrary"),
        name="gla",
    )(act32, act32, act16, act32, act32, gup_stack, gbias, ngain, tri)


def _split_bf16(x):
    hi = x.astype(BF16).astype(F32)
    return hi, x - hi


def _sortable_bits_to_float(key):
    return lax.bitcast_convert_type(jnp.where(key < 0, key ^ jnp.int32(0x7FFFFFFF), key), F32)


SORT_KEY_NEG_INF = -2139095041


def _topk_cap_kernel(iq_ref, ik_ref, iw_ref, tri_ref, o_ref, score_ref, rhs_ref, cnt_ref, *, topk, tq):
    qi = pl.program_id(1)
    tk = tq
    nk = ik_ref.shape[0] // tk
    n_live = qi + 1
    w = iw_ref[0] * (IDX_HEADS ** -0.5)
    pad_q = jnp.zeros((tq, IDX_DIM), F32)
    for h in range(IDX_HEADS):
        hi, lo = _split_bf16(iq_ref[:, h * IDX_DIM:(h + 1) * IDX_DIM])
        rhs_ref[h] = jnp.concatenate([hi, hi, lo, pad_q], axis=1).astype(BF16)
    t_pos = qi * tq + lax.broadcasted_iota(jnp.int32, (tk, tq), 1)
    s_off = lax.broadcasted_iota(jnp.int32, (tk, tq), 0)
    pad_k = jnp.zeros((tk, IDX_DIM), F32)

    def block_rows(kb):
        return pl.ds(pl.multiple_of(kb * tk, tk), tk)

    def score_block(kb, carry):
        hi, lo = _split_bf16(ik_ref[block_rows(kb), 0:IDX_DIM])
        lhs = jnp.concatenate([hi, lo, hi, pad_k], axis=1).astype(BF16)
        score = jnp.zeros((tk, tq), F32)
        for h in range(IDX_HEADS):
            dots = lax.dot_general(lhs, rhs_ref[h], (((1,), (1,)), ((), ())), preferred_element_type=F32)
            score = score + w[h:h + 1, :] * jnp.maximum(dots, 0.0)
        score_ref[block_rows(kb), :] = jnp.where(kb * tk + s_off <= t_pos, score + 0.0, -jnp.inf)
        return carry

    lax.fori_loop(0, n_live, score_block, 0)

    def count(pred):
        for n in range(1, nk + 1):
            @pl.when(n_live == n)
            def _(n=n):
                part = jnp.zeros((SUBLANES, tq), F32)
                for kb in range(n):
                    hit = jnp.where(pred(score_ref[kb * tk:(kb + 1) * tk, :]), 1.0, 0.0)
                    part = part + jnp.sum(hit.reshape(tk // SUBLANES, SUBLANES, tq), axis=0)
                cnt_ref[...] = part
        return jnp.sum(cnt_ref[...], axis=0, keepdims=True)

    def search(i, ans):
        cand = ans ^ jnp.left_shift(jnp.int32(1), 31 - i)
        cand_f = _sortable_bits_to_float(cand)
        cnt = count(lambda s: s >= cand_f)
        accept = jnp.logical_or(cnt >= float(topk), cand < SORT_KEY_NEG_INF)
        return jnp.where(accept, cand, ans)

    thr = _sortable_bits_to_float(
        lax.fori_loop(0, 32, search, jnp.full((1, tq), jnp.iinfo(jnp.int32).min, jnp.int32)))
    need = float(topk) - count(lambda s: s > thr)

    def emit(kb, seen):
        blk = score_ref[block_rows(kb), :]
        eq = jnp.where(blk == thr, 1.0, 0.0)
        rank = jnp.dot(tri_ref[...], eq.astype(BF16), preferred_element_type=F32) + seen
        chosen = jnp.logical_or(blk > thr, jnp.logical_and(blk == thr, rank <= need))
        o_ref[0, block_rows(kb), :] = jnp.where(
            jnp.logical_and(chosen, kb * tk + s_off <= t_pos), jnp.inf, -jnp.inf)
        return seen + jnp.sum(eq, axis=0, keepdims=True)

    lax.fori_loop(0, n_live, emit, jnp.zeros((1, tq), F32))

    def fill(kb, carry):
        o_ref[0, block_rows(kb), :] = jnp.full((tk, tq), -jnp.inf, F32)
        return carry

    lax.fori_loop(n_live, nk, fill, 0)


def _topk_cap(act32, tile_of, iw_t, batch, seq, topk, tq=256):
    tq = min(tq, seq)
    nq = seq // tq
    width = IDX_HEADS * IDX_DIM
    iq_block = tile_of["i_q"] * IN_PROJ_TN // width
    ik_block = tile_of["small"] * IN_PROJ_TN // LANES + SMALL_IK_BLOCK
    tri = jnp.tril(jnp.ones((tq, tq), BF16))
    return pl.pallas_call(
        functools.partial(_topk_cap_kernel, topk=topk, tq=tq),
        grid=(batch, nq),
        in_specs=[pl.BlockSpec((tq, width), lambda b, i: (b * nq + i, iq_block)),
                  pl.BlockSpec((seq, LANES), lambda b, i: (b, ik_block)),
                  pl.BlockSpec((1, IDX_HEADS, tq), lambda b, i: (b, 0, i)),
                  pl.BlockSpec((tq, tq), lambda b, i: (0, 0))],
        out_specs=pl.BlockSpec((1, seq, tq), lambda b, i: (b, 0, i)),
        out_shape=jax.ShapeDtypeStruct((batch, seq, seq), F32),
        scratch_shapes=[pltpu.VMEM((seq, tq), F32),
                        pltpu.VMEM((IDX_HEADS, tq, 4 * IDX_DIM), BF16),
                        pltpu.VMEM((SUBLANES, tq), F32)],
        compiler_params=_params("parallel", "parallel"),
        name="indexer_topk_cap",
    )(act32, act32, iw_t, tri)


ATTN_HEADS_PER_STEP = 8


def _attn_kernel(q_ref, k_ref, v_ref, cap_ref, o_ref, acc_ref, vt_ref):
    qi = pl.program_id(2)
    tq = q_ref.shape[0]
    tk = tq
    dh = DSA_HEAD_DIM
    group = q_ref.shape[1] // dh
    heads = [slice(g * dh, (g + 1) * dh) for g in range(group)]
    acc_ref[...] = jnp.zeros_like(acc_ref)

    @pl.when(qi == 0)
    def _():
        for g in range(group):
            for kb in range(v_ref.shape[0] // tk):
                vt_ref[g, kb] = v_ref[kb * tk:(kb + 1) * tk, heads[g]].T

    def body(kj, carry):
        rows = pl.ds(pl.multiple_of(kj * tk, tk), tk)
        cap = cap_ref[0, rows, :]
        logits = [lax.dot_general(k_ref[rows, cols], q_ref[:, cols], (((1,), (1,)), ((), ())),
                                  preferred_element_type=F32) for cols in heads]
        new = []
        for g in range(group):
            m, l = carry[g]
            s = jnp.minimum(logits[g], cap)
            m_new = jnp.maximum(m, jnp.max(s, axis=0, keepdims=True))
            m_safe = jnp.where(m_new == -jnp.inf, 0.0, m_new)
            alpha = jnp.exp2(m - m_safe)
            p = jnp.exp2(s - m_safe)
            new.append((m_new, alpha * l + jnp.sum(p, axis=0, keepdims=True)))
            update = jnp.dot(vt_ref[g, kj], p.astype(BF16), preferred_element_type=F32)
            acc_ref[g] = alpha * acc_ref[g] + update
        return tuple(new)

    init = tuple((jnp.full((1, tq), -jnp.inf, F32), jnp.zeros((1, tq), F32)) for _ in range(group))
    final = lax.fori_loop(0, qi + 1, body, init)
    for g in range(group):
        o_ref[:, g * dh:(g + 1) * dh] = (acc_ref[g] / final[g][1]).T.astype(o_ref.dtype)


def _attention(act16, tile_of, cap_t, batch, seq, tq=256):
    tq = min(tq, seq)
    nq = seq // tq
    width = ATTN_HEADS_PER_STEP * DSA_HEAD_DIM
    n_groups = DSA_HEADS // ATTN_HEADS_PER_STEP
    t = act16.shape[0]
    q_block, k_block, v_block = (tile_of[nm] * IN_PROJ_TN // width for nm in ("d_q", "d_k", "d_v"))
    return pl.pallas_call(
        _attn_kernel,
        grid=(batch, n_groups, nq),
        in_specs=[pl.BlockSpec((tq, width), lambda b, h, i: (b * nq + i, q_block + h)),
                  pl.BlockSpec((seq, width), lambda b, h, i: (b, k_block + h)),
                  pl.BlockSpec((seq, width), lambda b, h, i: (b, v_block + h)),
                  pl.BlockSpec((1, seq, tq), lambda b, h, i: (b, 0, i))],
        out_specs=pl.BlockSpec((tq, width), lambda b, h, i: (b * nq + i, h)),
        out_shape=jax.ShapeDtypeStruct((t, DSA_HEADS * DSA_HEAD_DIM), BF16),
        scratch_shapes=[pltpu.VMEM((ATTN_HEADS_PER_STEP, DSA_HEAD_DIM, tq), F32),
                        pltpu.VMEM((ATTN_HEADS_PER_STEP, seq // tq, DSA_HEAD_DIM, tq), BF16)],
        compiler_params=_params("parallel", "parallel", "arbitrary"),
        name="dsa_attention",
    )(act16, act16, act16, cap_t)


def _merge_kernel(oa_ref, ob_ref, wa_ref, wb_ref, ga_ref, gb_ref, o_ref):
    ya = jnp.dot(oa_ref[...], wa_ref[...].astype(BF16), preferred_element_type=F32)
    yb = jnp.dot(ob_ref[...], wb_ref[...].astype(BF16), preferred_element_type=F32)
    o_ref[...] = (jax.nn.sigmoid(ga_ref[...]) * ya + jax.nn.sigmoid(gb_ref[...]) * yb).astype(o_ref.dtype)


def _merge(o_a, o_b, w_a, w_b, act32, tile_of, tm=1024):
    t, d = o_a.shape
    n = w_a.shape[1]
    tn = IN_PROJ_TN
    tm = min(tm, t)
    ga_tile, gb_tile = tile_of["gate_a"], tile_of["gate_b"]
    return pl.pallas_call(
        _merge_kernel,
        grid=(t // tm, n // tn),
        in_specs=[pl.BlockSpec((tm, d), lambda i, j: (i, 0)),
                  pl.BlockSpec((tm, d), lambda i, j: (i, 0)),
                  pl.BlockSpec((d, tn), lambda i, j: (0, j)),
                  pl.BlockSpec((d, tn), lambda i, j: (0, j)),
                  pl.BlockSpec((tm, tn), lambda i, j: (i, ga_tile + j)),
                  pl.BlockSpec((tm, tn), lambda i, j: (i, gb_tile + j))],
        out_specs=pl.BlockSpec((tm, tn), lambda i, j: (i, j)),
        out_shape=jax.ShapeDtypeStruct((t, n), BF16),
        compiler_params=_params("parallel", "parallel"),
        name="branch_merge",
    )(o_a, o_b, w_a, w_b, act32, act32)


def _mixer_out_kernel(m_ref, w_ref, x_ref, g_ref, gain_ref, sc_ref, sh_ref, x1_ref, h2_ref):
    y = jnp.dot(m_ref[...], w_ref[...], preferred_element_type=F32)
    x1 = x_ref[...] + g_ref[0] * y
    x1_ref[...] = x1
    h2_ref[...] = _rms_modulate(x1, gain_ref[...], sc_ref[0], sh_ref[0]).astype(h2_ref.dtype)


def _mixer_out(merged, w, x2d, g1, gain2, sc2, sh2, seq, tm=512):
    t, d = x2d.shape
    tm = min(tm, seq)
    per_batch = seq // tm
    vec = pl.BlockSpec((1, 1, d), lambda i: (i // per_batch, 0, 0))
    row = pl.BlockSpec((tm, d), lambda i: (i, 0))
    return pl.pallas_call(
        _mixer_out_kernel,
        grid=(t // tm,),
        in_specs=[row, pl.BlockSpec((d, d), lambda i: (0, 0), pipeline_mode=pl.Buffered(1)), row, vec,
                  pl.BlockSpec((1, d), lambda i: (0, 0)), vec, vec],
        out_specs=[row, row],
        out_shape=[jax.ShapeDtypeStruct((t, d), F32), jax.ShapeDtypeStruct((t, d), BF16)],
        compiler_params=_params("parallel"),
        name="mixer_out_norm2",
    )(merged, w, x2d, g1, gain2, sc2, sh2)


def _ffn_up_kernel(h_ref, wg_ref, wu_ref, o_ref):
    g = jnp.dot(h_ref[...], wg_ref[...].astype(BF16), preferred_element_type=F32)
    u = jnp.dot(h_ref[...], wu_ref[...].astype(BF16), preferred_element_type=F32)
    o_ref[...] = (_silu(g) * u).astype(o_ref.dtype)


def _ffn_up(h2, w_gate_up, tm=1024, tn=512):
    t, d = h2.shape
    d_ff = w_gate_up.shape[1] // 2
    tm = min(tm, t)
    nj = d_ff // tn
    return pl.pallas_call(
        _ffn_up_kernel,
        grid=(t // tm, nj),
        in_specs=[pl.BlockSpec((tm, d), lambda i, j: (i, 0)),
                  pl.BlockSpec((d, tn), lambda i, j: (0, j)),
                  pl.BlockSpec((d, tn), lambda i, j: (0, nj + j))],
        out_specs=pl.BlockSpec((tm, tn), lambda i, j: (i, j)),
        out_shape=jax.ShapeDtypeStruct((t, d_ff), BF16),
        compiler_params=_params("parallel", "parallel"),
        name="ffn_up",
    )(h2, w_gate_up, w_gate_up)


def _ffn_down_kernel(a_ref, w_ref, x_ref, g_ref, gain_ref, o_ref):
    x2 = x_ref[...] + g_ref[0] * jnp.dot(a_ref[...], w_ref[...], preferred_element_type=F32)
    y = x2 * lax.rsqrt(jnp.mean(x2 * x2, axis=-1, keepdims=True) + RMS_EPS)
    o_ref[...] = y * gain_ref[...]


def _ffn_down(act, w_down, x1, g2, final_gain, seq, tm=512):
    t, d_ff = act.shape
    d = w_down.shape[1]
    tm = min(tm, seq)
    per_batch = seq // tm
    return pl.pallas_call(
        _ffn_down_kernel,
        grid=(t // tm,),
        in_specs=[pl.BlockSpec((tm, d_ff), lambda i: (i, 0)),
                  pl.BlockSpec((d_ff, d), lambda i: (0, 0), pipeline_mode=pl.Buffered(1)),
                  pl.BlockSpec((tm, d), lambda i: (i, 0)),
                  pl.BlockSpec((1, 1, d), lambda i: (i // per_batch, 0, 0)),
                  pl.BlockSpec((1, d), lambda i: (0, 0))],
        out_specs=pl.BlockSpec((tm, d), lambda i: (i, 0)),
        out_shape=jax.ShapeDtypeStruct((t, d), F32),
        compiler_params=_params("parallel", vmem_limit=VMEM_LIMIT_LARGE),
        name="ffn_down_final_norm",
    )(act, w_down, x1, g2, final_gain)


def _rope_tables(positions_flat, rot, period):
    half = rot // 2
    assert period % half == 0 and LANES % half == 0
    inv_freq = jnp.power(ROPE_THETA, -jnp.arange(0, rot, 2, dtype=F32) / rot)
    ang = positions_flat.astype(F32)[:, None] * inv_freq[None, :]
    reps = (1, LANES // half)
    return jnp.tile(jnp.cos(ang), reps), jnp.tile(jnp.sin(ang), reps)


def _layer(x2d, mod, positions_flat, batch, seq, norm1_gain, norm2_gain, w_in, gla_gate_up,
           gla_gate_bias, gla_norm_gain, w_branch_gla, w_branch_dsa, w_merge_out, w_ffn_gate_up,
           w_ffn_down, final_gain):
    d = x2d.shape[1]
    sh1, sc1, g1, sh2, sc2, g2 = [mod[:, i * d:(i + 1) * d][:, None, :] for i in range(N_MOD)]

    plan = _in_proj_plan(d)
    tile_of = plan[3]
    cos_d, sin_d = _rope_tables(positions_flat, DSA_HEAD_DIM // ROPE_FRACTION, DSA_HEAD_DIM)
    cos_i, sin_i = _rope_tables(positions_flat, IDX_DIM // ROPE_FRACTION, IDX_DIM)
    h = _norm_modulate(x2d, norm1_gain[None, :], sc1, sh1, seq)
    act32, act16 = _in_project(h, w_in.T, (cos_d, sin_d, cos_i, sin_i), plan)

    o_a = _gla(act32, act16, tile_of, gla_gate_up, gla_gate_bias[None, :], gla_norm_gain[None, :],
               batch, seq, rows=min(512, seq))

    iw_col = tile_of["small"] * IN_PROJ_TN + SMALL_IK_BLOCK * LANES + IDX_DIM
    iw_t = jnp.transpose(act32[:, iw_col:iw_col + IDX_HEADS].reshape(batch, seq, IDX_HEADS), (0, 2, 1))
    topk = min(IDX_TOPK_MAX, seq // 4)
    cap_t = _topk_cap(act32, tile_of, iw_t, batch, seq, topk)
    o_b = _attention(act16, tile_of, cap_t, batch, seq)

    merged = _merge(o_a, o_b, w_branch_gla, w_branch_dsa, act32, tile_of)
    x1, h2 = _mixer_out(merged, w_merge_out.astype(BF16), x2d, g1, norm2_gain[None, :], sc2, sh2, seq)
    act = _ffn_up(h2, w_ffn_gate_up)
    return _ffn_down(act, w_ffn_down.astype(BF16), x1, g2, final_gain[None, :], seq)


def kernel(x, c, positions, norm1_gain, norm2_gain, w_ada, b_ada, w_in, gla_gate_up, gla_gate_bias,
           gla_norm_gain, w_branch_gla, w_branch_dsa, w_merge_out, w_ffn_gate_up, w_ffn_down,
           final_norm_gain):
    batch, seq, d = x.shape
    depth = w_in.shape[0]
    assert depth == 1, "the final RMSNorm is fused into the single layer's FFN kernel"
    x2d = x.reshape(batch * seq, d)
    c_pad = jnp.zeros((SUBLANES, d), F32).at[:batch].set(c)
    mod = _modulation(c_pad, w_ada[0], b_ada[0][None, :])[:batch]
    out = _layer(x2d, mod, positions.reshape(-1), batch, seq, norm1_gain[0], norm2_gain[0], w_in[0],
                 gla_gate_up[0], gla_gate_bias[0], gla_norm_gain[0], w_branch_gla[0], w_branch_dsa[0],
                 w_merge_out[0], w_ffn_gate_up[0], w_ffn_down[0], final_norm_gain)
    return out.reshape(batch, seq, d)
```

```python
import functools

import jax
import jax.numpy as jnp
from jax import lax
from jax.experimental import pallas as pl
from jax.experimental.pallas import tpu as pltpu

F32 = jnp.float32
BF16 = jnp.bfloat16
HIGHEST = lax.Precision.HIGHEST

RMS_EPS = 1e-6
GLA_HEADS = 4
GLA_GATE_RANK = 16
GLA_TAU = 16.0
GLA_CHUNK = 64
GLA_SUB = 8
DSA_HEADS = 16
DSA_HEAD_DIM = 128
IDX_HEADS = 8
IDX_DIM = 64
IDX_TOPK_MAX = 256
ROPE_THETA = 500000.0
ROPE_FRACTION = 4
N_MOD = 6
LOG2_E = 1.4426950408889634

LANES = 128
SUBLANES = 8
VMEM_LIMIT = 48 * 1024 * 1024


VMEM_LIMIT_LARGE = 56 * 1024 * 1024


def _params(*semantics, vmem_limit=VMEM_LIMIT):
    return pltpu.CompilerParams(dimension_semantics=semantics, vmem_limit_bytes=vmem_limit)


def _silu(x):
    return x * jax.nn.sigmoid(x)


def _mod_kernel(c_ref, w_ref, b_ref, o_ref):
    a = _silu(c_ref[...])
    o_ref[...] = jnp.dot(a, w_ref[...], precision=HIGHEST, preferred_element_type=F32) + b_ref[...]


def _modulation(c_pad, w_ada, b_ada, tn=2048):
    rows, d = c_pad.shape
    n = w_ada.shape[1]
    return pl.pallas_call(
        _mod_kernel,
        grid=(n // tn,),
        in_specs=[pl.BlockSpec((rows, d), lambda j: (0, 0)),
                  pl.BlockSpec((d, tn), lambda j: (0, j)),
                  pl.BlockSpec((1, tn), lambda j: (0, j))],
        out_specs=pl.BlockSpec((rows, tn), lambda j: (0, j)),
        out_shape=jax.ShapeDtypeStruct((rows, n), F32),
        compiler_params=_params("parallel"),
        name="adaln_mod",
    )(c_pad, w_ada, b_ada)


def _rms_modulate(x, gain, scale, shift):
    y = x * lax.rsqrt(jnp.mean(x * x, axis=-1, keepdims=True) + RMS_EPS)
    return (y * gain) * (1.0 + scale) + shift


def _rope_lanes(x, cos, sin, half, period, limit=LANES):
    lane = lax.broadcasted_iota(jnp.int32, x.shape, 1)
    in_head = lane % period
    upper = pltpu.roll(x, LANES - half, axis=1)
    lower = pltpu.roll(x, half, axis=1)
    live = lane < limit
    first = jnp.logical_and(live, in_head < half)
    second = jnp.logical_and(live, jnp.logical_and(in_head >= half, in_head < 2 * half))
    return jnp.where(first, x * cos - upper * sin,
                     jnp.where(second, lower * sin + x * cos, x))


PLAIN32, GLA_Q32, IDX_Q32, SMALL32, PLAIN16, DSA_ROPE16, DSA_ROPE_Q16 = range(7)
SMALL_GLR_BLOCK, SMALL_IK_BLOCK = 0, 1
DSA_Q_SCALE = (DSA_HEAD_DIM ** -0.5) * LOG2_E
IN_PROJ_ROPE_PARTS = 4
IN_PROJ_ROPE_ROWS = 256
IN_PROJ_TN = 512


def _tile_lanes(c):
    while c.shape[1] < LANES:
        c = jnp.concatenate([c, c], axis=1)
    return c


def _norm_mod_kernel(x_ref, gain_ref, sc_ref, sh_ref, cs_ref, o_ref, *table_refs, table_slices):
    o_ref[...] = _rms_modulate(x_ref[...], gain_ref[...], sc_ref[0], sh_ref[0]).astype(o_ref.dtype)
    for ref, (lo, width) in zip(table_refs, table_slices):
        ref[...] = _tile_lanes(cs_ref[:, lo:lo + width])


def _norm_modulate(x2d, gain, sc, sh, compact_tables, table_slices, seq, tm=512):
    t, d = x2d.shape
    per_batch = seq // tm
    lanes = pl.BlockSpec((tm, LANES), lambda i: (i, 0))
    return pl.pallas_call(
        functools.partial(_norm_mod_kernel, table_slices=table_slices),
        grid=(t // tm,),
        in_specs=[pl.BlockSpec((tm, d), lambda i: (i, 0)),
                  pl.BlockSpec((1, d), lambda i: (0, 0)),
                  pl.BlockSpec((1, 1, d), lambda i: (i // per_batch, 0, 0)),
                  pl.BlockSpec((1, 1, d), lambda i: (i // per_batch, 0, 0)),
                  lanes],
        out_specs=[pl.BlockSpec((tm, d), lambda i: (i, 0))] + [lanes] * len(table_slices),
        out_shape=[jax.ShapeDtypeStruct((t, d), BF16)]
        + [jax.ShapeDtypeStruct((t, LANES), F32)] * len(table_slices),
        compiler_params=_params("parallel"),
        name="norm1_modulate",
    )(x2d, gain, sc, sh, compact_tables)


def _in_proj_kernel(off_ref, kind_ref, h_ref, wt_ref, ws_ref, cos_d_ref, sin_d_ref, cos_i_ref, sin_i_ref,
                    o32_ref, o16_ref):
    kind = kind_ref[pl.program_id(1)]
    tm, tn = o32_ref.shape
    n_groups = tn // LANES
    rot_d = DSA_HEAD_DIM // ROPE_FRACTION
    rot_i = IDX_DIM // ROPE_FRACTION

    def product():
        return lax.dot_general(h_ref[...], wt_ref[...].astype(BF16), (((1,), (1,)), ((), ())),
                               preferred_element_type=F32)

    def rope_tile(cos_ref, sin_ref, out_ref, half, period, groups, pre_scale=1.0, limit=LANES, w_ref=wt_ref):
        w = w_ref[...].astype(BF16)
        part = tm // IN_PROJ_ROPE_PARTS
        step = min(IN_PROJ_ROPE_ROWS, part)
        for p in range(IN_PROJ_ROPE_PARTS):
            acc = lax.dot_general(h_ref[p * part:(p + 1) * part, :], w, (((1,), (1,)), ((), ())),
                                  preferred_element_type=F32)
            if pre_scale != 1.0:
                acc = acc * pre_scale
            for r in range(0, part, step):
                rows = slice(p * part + r, p * part + r + step)
                cos = cos_ref[rows, :]
                sin = sin_ref[rows, :]
                for g in range(n_groups):
                    cols = slice(g * LANES, (g + 1) * LANES)
                    x = acc[r:r + step, cols]
                    if g in groups:
                        x = _rope_lanes(x, cos, sin, half, period, limit)
                    out_ref[rows, cols] = x.astype(out_ref.dtype)

    @pl.when(kind == PLAIN32)
    def _():
        o32_ref[...] = product()

    @pl.when(kind == GLA_Q32)
    def _():
        o32_ref[...] = product() * ((wt_ref.shape[1] // 2 // GLA_HEADS) ** -0.5)

    @pl.when(kind == IDX_Q32)
    def _():
        rope_tile(cos_i_ref, sin_i_ref, o32_ref, rot_i // 2, IDX_DIM, range(n_groups), pre_scale=IDX_DIM ** -0.5)

    @pl.when(kind == SMALL32)
    def _():
        rope_tile(cos_i_ref, sin_i_ref, o32_ref, rot_i // 2, IDX_DIM, (SMALL_IK_BLOCK,), limit=IDX_DIM,
                  w_ref=ws_ref)

    @pl.when(kind == PLAIN16)
    def _():
        o16_ref[...] = product().astype(BF16)

    @pl.when(kind == DSA_ROPE16)
    def _():
        rope_tile(cos_d_ref, sin_d_ref, o16_ref, rot_d // 2, DSA_HEAD_DIM, range(n_groups))

    @pl.when(kind == DSA_ROPE_Q16)
    def _():
        rope_tile(cos_d_ref, sin_d_ref, o16_ref, rot_d // 2, DSA_HEAD_DIM, range(n_groups),
                  pre_scale=DSA_Q_SCALE)


def _in_proj_plan(d):
    tn = IN_PROJ_TN
    gla_qk, dsa_w, idx_w = d // 2, DSA_HEADS * DSA_HEAD_DIM, IDX_HEADS * IDX_DIM
    names = ("g_q", "g_k", "g_v", "g_r", "g_lr", "d_q", "d_k", "d_v", "i_q", "i_k", "i_w", "gate_a", "gate_b")
    widths = (gla_qk, gla_qk, d, d, GLA_GATE_RANK, dsa_w, dsa_w, dsa_w, idx_w, IDX_DIM, IDX_HEADS, d, d)
    start, pos = {}, 0
    for nm, wd in zip(names, widths):
        start[nm] = pos
        pos += wd
    width = dict(zip(names, widths))
    start["small"], width["small"] = 0, tn
    assert start["i_w"] == start["i_k"] + IDX_DIM and IDX_DIM + IDX_HEADS <= LANES and GLA_GATE_RANK <= LANES
    f32_groups = (("g_q", GLA_Q32), ("g_k", PLAIN32), ("g_r", PLAIN32), ("small", SMALL32), ("i_q", IDX_Q32),
                  ("gate_a", PLAIN32), ("gate_b", PLAIN32))
    bf16_groups = (("g_v", PLAIN16), ("d_q", DSA_ROPE_Q16), ("d_k", DSA_ROPE16), ("d_v", PLAIN16))
    offsets, kinds, tile_of = [], [], {}
    for groups in (f32_groups, bf16_groups):
        base = len(offsets)
        for nm, kind in groups:
            tile_of[nm] = len(offsets) - base
            for t in range(-(-width[nm] // tn)):
                offsets.append(start[nm] + t * tn)
                kinds.append(kind)
        if groups is f32_groups:
            n32 = len(offsets)
    assert all(o % SUBLANES == 0 and o + tn <= pos for o in offsets)
    return offsets, kinds, n32, tile_of, start


def _in_project(h, w_in_t, tables, plan, tm=2048):
    t, d = h.shape
    tn = IN_PROJ_TN
    offsets, kinds, n32, _, start = plan
    n16 = len(offsets) - n32
    tm = min(tm, t)
    w_small = jnp.zeros((tn, d), F32)
    w_small = lax.dynamic_update_slice(
        w_small, w_in_t[start["g_lr"]:start["g_lr"] + GLA_GATE_RANK], (SMALL_GLR_BLOCK * LANES, 0))
    w_small = lax.dynamic_update_slice(
        w_small, w_in_t[start["i_k"]:start["i_k"] + IDX_DIM + IDX_HEADS], (SMALL_IK_BLOCK * LANES, 0))
    once = pl.Buffered(1)
    table_spec = pl.BlockSpec((tm, LANES), lambda i, j, off, kind: (i, 0), pipeline_mode=once)
    grid_spec = pltpu.PrefetchScalarGridSpec(
        num_scalar_prefetch=2,
        grid=(t // tm, len(offsets)),
        in_specs=[pl.BlockSpec((tm, d), lambda i, j, off, kind: (i, 0), pipeline_mode=once),
                  pl.BlockSpec((pl.Element(tn), pl.Element(d)),
                               lambda i, j, off, kind: (pl.multiple_of(off[j], SUBLANES), 0)),
                  pl.BlockSpec((tn, d), lambda i, j, off, kind: (0, 0), pipeline_mode=once),
                  table_spec, table_spec, table_spec, table_spec],
        out_specs=[pl.BlockSpec((tm, tn), lambda i, j, off, kind: (i, jnp.minimum(j, n32 - 1))),
                   pl.BlockSpec((tm, tn), lambda i, j, off, kind: (i, jnp.maximum(j - n32, 0)))],
    )
    return pl.pallas_call(
        _in_proj_kernel,
        grid_spec=grid_spec,
        out_shape=[jax.ShapeDtypeStruct((t, n32 * tn), F32), jax.ShapeDtypeStruct((t, n16 * tn), BF16)],
        compiler_params=_params("parallel", "arbitrary"),
        name="in_proj",
    )(jnp.asarray(offsets, jnp.int32), jnp.asarray(kinds, jnp.int32), h, w_in_t, w_small, *tables)


def _split3_bf16(x):
    hi, rest = _split_bf16(x)
    mid, lo = _split_bf16(rest)
    return hi, mid, lo


def _gla_kernel(q_ref, k_ref, v_ref, gr_ref, sm_ref, gup_ref, gb_ref, ng_ref, tri_ref, o_ref, st_ref):
    @pl.when(pl.program_id(2) == 0)
    def _():
        st_ref[...] = jnp.zeros_like(st_ref)

    rows = q_ref.shape[0]
    dk = q_ref.shape[1]
    c, sub = GLA_CHUNK, GLA_SUB
    n_sub = c // sub

    a_hi, a_lo = _split_bf16(sm_ref[:, 0:GLA_GATE_RANK])
    gate_lhs = jnp.concatenate(
        [a_hi, a_lo, a_hi, jnp.zeros((rows, LANES - 3 * GLA_GATE_RANK), F32)], axis=1).astype(BF16)
    z = jnp.dot(gate_lhs, gup_ref[...], preferred_element_type=F32) + gb_ref[...]
    log_g = (jnp.minimum(z, 0.0) - jnp.log(1.0 + jnp.exp(-jnp.abs(z)))) * (LOG2_E / GLA_TAU)
    parts = jnp.dot(tri_ref[...], jnp.concatenate(_split3_bf16(log_g), axis=1).astype(BF16),
                    preferred_element_type=F32)
    b = (parts[:, 0:dk] + parts[:, dk:2 * dk]) + parts[:, 2 * dk:3 * dk]

    q = q_ref[...]
    k = k_ref[...]

    k_tiles = k.reshape(rows // sub, sub, dk)
    b_tiles = b.reshape(rows // sub, sub, dk)
    row = lax.broadcasted_iota(jnp.int32, (rows, 1), 0)
    row_in_sub = row % sub
    col = lax.broadcasted_iota(jnp.int32, (rows, c), 1)
    target = lax.broadcasted_iota(jnp.int32, (rows, c), 0) % c
    a_diag = jnp.zeros((rows, c), F32)
    for delta in range(sub):
        if delta == 0:
            k_d, b_d = k, b
        else:
            k_d = pltpu.roll(k_tiles, delta, axis=1).reshape(rows, dk)
            b_d = pltpu.roll(b_tiles, delta, axis=1).reshape(rows, dk)
        decay = jnp.exp2(jnp.where(row_in_sub >= delta, b - b_d, -jnp.inf))
        a = jnp.sum(q * k_d * decay, axis=-1, keepdims=True)
        a_diag = jnp.where(col == target - delta, a, a_diag)

    for ch in range(rows // c):
        base = ch * c
        qc = q[base:base + c]
        kc = k[base:base + c]
        bc = b[base:base + c]
        vc = v_ref[base:base + c, :]
        st = st_ref[...]

        o_inter = lax.dot_general((qc * jnp.exp2(bc)).astype(BF16), st.astype(BF16),
                                  (((1,), (1,)), ((), ())), preferred_element_type=F32)

        blocks = [jnp.zeros((sub, c), F32)]
        for i_sub in range(1, n_sub):
            lo = i_sub * sub
            ref = bc[lo - 1:lo]
            q_hat = qc[lo:lo + sub] * jnp.exp2(bc[lo:lo + sub] - ref)
            k_hat = kc[0:lo] * jnp.exp2(ref - bc[0:lo])
            k_pad = jnp.concatenate([k_hat, jnp.zeros((c - lo, dk), F32)], axis=0)
            blocks.append(lax.dot_general(q_hat.astype(BF16), k_pad.astype(BF16),
                                          (((1,), (1,)), ((), ())), preferred_element_type=F32))
        attn = jnp.concatenate(blocks, axis=0) + a_diag[base:base + c]

        o = jnp.dot(attn.astype(BF16), vc, preferred_element_type=F32) + o_inter

        b_last = bc[c - 1:c]
        k_dec = (kc * jnp.exp2(b_last - bc)).astype(BF16)
        st_ref[...] = st * jnp.exp2(b_last) + lax.dot_general(
            vc, k_dec, (((0,), (0,)), ((), ())), preferred_element_type=F32)

        o = o * lax.rsqrt(jnp.mean(o * o, axis=-1, keepdims=True) + RMS_EPS) * ng_ref[...]
        o_ref[base:base + c, :] = (o * _silu(gr_ref[base:base + c, :])).astype(o_ref.dtype)


def _gla(act32, act16, tile_of, gate_up, gbias, ngain, batch, seq, rows=256):
    dk = gate_up.shape[1] // GLA_HEADS
    dv = ngain.shape[1] // GLA_HEADS
    t = act32.shape[0]
    nr = seq // rows
    g_hi, g_lo = _split_bf16(gate_up)
    gup_stack = jnp.concatenate(
        [g_hi, g_hi, g_lo, jnp.zeros((LANES - 3 * GLA_GATE_RANK, gate_up.shape[1]), F32)], axis=0).astype(BF16)
    idx = jnp.arange(rows)
    tri = jnp.logical_and(idx[:, None] // GLA_CHUNK == idx[None, :] // GLA_CHUNK,
                          idx[None, :] <= idx[:, None]).astype(BF16)

    def cols(group, width):
        first = tile_of[group] * IN_PROJ_TN // width
        return lambda b, h, r: (b * nr + r, first + h)

    lr_block = tile_of["small"] * IN_PROJ_TN // LANES + SMALL_GLR_BLOCK
    return pl.pallas_call(
        _gla_kernel,
        grid=(batch, GLA_HEADS, nr),
        in_specs=[pl.BlockSpec((rows, dk), cols("g_q", dk)),
                  pl.BlockSpec((rows, dk), cols("g_k", dk)),
                  pl.BlockSpec((rows, dv), cols("g_v", dv)),
                  pl.BlockSpec((rows, dv), cols("g_r", dv)),
                  pl.BlockSpec((rows, LANES), lambda b, h, r: (b * nr + r, lr_block)),
                  pl.BlockSpec((LANES, dk), lambda b, h, r: (0, h)),
                  pl.BlockSpec((1, dk), lambda b, h, r: (0, h)),
                  pl.BlockSpec((1, dv), lambda b, h, r: (0, h)),
                  pl.BlockSpec((rows, rows), lambda b, h, r: (0, 0))],
        out_specs=pl.BlockSpec((rows, dv), lambda b, h, r: (b * nr + r, h)),
        out_shape=jax.ShapeDtypeStruct((t, GLA_HEADS * dv), BF16),
        scratch_shapes=[pltpu.VMEM((dv, dk), F32)],
        compiler_params=_params("parallel", "parallel", "arbitrary"),
        name="gla",
    )(act32, act32, act16, act32, act32, gup_stack, gbias, ngain, tri)


def _split_bf16(x):
    hi = x.astype(BF16).astype(F32)
    return hi, x - hi


def _sortable_bits_to_float(key):
    return lax.bitcast_convert_type(jnp.where(key < 0, key ^ jnp.int32(0x7FFFFFFF), key), F32)


SORT_KEY_NEG_INF = -2139095041


def _topk_cap_kernel(iq_ref, ik_ref, iw_ref, tri_ref, o_ref, score_ref, rhs_ref, cnt_ref, *, topk, tq):
    qi = pl.program_id(1)
    tk = tq
    nk = ik_ref.shape[0] // tk
    n_live = qi + 1
    w = iw_ref[0] * (IDX_HEADS ** -0.5)
    pad_q = jnp.zeros((tq, IDX_DIM), F32)
    for h in range(IDX_HEADS):
        hi, lo = _split_bf16(iq_ref[:, h * IDX_DIM:(h + 1) * IDX_DIM])
        rhs_ref[h] = jnp.concatenate([hi, hi, lo, pad_q], axis=1).astype(BF16)
    t_pos = qi * tq + lax.broadcasted_iota(jnp.int32, (tk, tq), 1)
    s_off = lax.broadcasted_iota(jnp.int32, (tk, tq), 0)
    pad_k = jnp.zeros((tk, IDX_DIM), F32)

    def block_rows(kb):
        return pl.ds(pl.multiple_of(kb * tk, tk), tk)

    def score_block(kb, carry):
        hi, lo = _split_bf16(ik_ref[block_rows(kb), 0:IDX_DIM])
        lhs = jnp.concatenate([hi, lo, hi, pad_k], axis=1).astype(BF16)
        score = jnp.zeros((tk, tq), F32)
        for h in range(IDX_HEADS):
            dots = lax.dot_general(lhs, rhs_ref[h], (((1,), (1,)), ((), ())), preferred_element_type=F32)
            score = score + w[h:h + 1, :] * jnp.maximum(dots, 0.0)
        score_ref[block_rows(kb), :] = jnp.where(kb * tk + s_off <= t_pos, score + 0.0, -jnp.inf)
        return carry

    lax.fori_loop(0, n_live, score_block, 0)

    def count(pred):
        for n in range(1, nk + 1):
            @pl.when(n_live == n)
            def _(n=n):
                part = jnp.zeros((SUBLANES, tq), F32)
                for kb in range(n):
                    hit = jnp.where(pred(score_ref[kb * tk:(kb + 1) * tk, :]), 1.0, 0.0)
                    part = part + jnp.sum(hit.reshape(tk // SUBLANES, SUBLANES, tq), axis=0)
                cnt_ref[...] = part
        return jnp.sum(cnt_ref[...], axis=0, keepdims=True)

    def search(i, ans):
        cand = ans ^ jnp.left_shift(jnp.int32(1), 31 - i)
        cand_f = _sortable_bits_to_float(cand)
        cnt = count(lambda s: s >= cand_f)
        accept = jnp.logical_or(cnt >= float(topk), cand < SORT_KEY_NEG_INF)
        return jnp.where(accept, cand, ans)

    thr = _sortable_bits_to_float(
        lax.fori_loop(0, 32, search, jnp.full((1, tq), jnp.iinfo(jnp.int32).min, jnp.int32)))
    need = float(topk) - count(lambda s: s > thr)

    def emit(kb, seen):
        blk = score_ref[block_rows(kb), :]
        eq = jnp.where(blk == thr, 1.0, 0.0)
        rank = jnp.dot(tri_ref[...], eq.astype(BF16), preferred_element_type=F32) + seen
        chosen = jnp.logical_or(blk > thr, jnp.logical_and(blk == thr, rank <= need))
        o_ref[0, block_rows(kb), :] = jnp.where(
            jnp.logical_and(chosen, kb * tk + s_off <= t_pos), jnp.inf, -jnp.inf)
        return seen + jnp.sum(eq, axis=0, keepdims=True)

    lax.fori_loop(0, n_live, emit, jnp.zeros((1, tq), F32))

    def fill(kb, carry):
        o_ref[0, block_rows(kb), :] = jnp.full((tk, tq), -jnp.inf, F32)
        return carry

    lax.fori_loop(n_live, nk, fill, 0)


def _topk_cap(act32, tile_of, iw_t, batch, seq, topk, tq=256):
    tq = min(tq, seq)
    nq = seq // tq
    width = IDX_HEADS * IDX_DIM
    iq_block = tile_of["i_q"] * IN_PROJ_TN // width
    ik_block = tile_of["small"] * IN_PROJ_TN // LANES + SMALL_IK_BLOCK
    tri = jnp.tril(jnp.ones((tq, tq), BF16))
    return pl.pallas_call(
        functools.partial(_topk_cap_kernel, topk=topk, tq=tq),
        grid=(batch, nq),
        in_specs=[pl.BlockSpec((tq, width), lambda b, i: (b * nq + i, iq_block)),
                  pl.BlockSpec((seq, LANES), lambda b, i: (b, ik_block)),
                  pl.BlockSpec((1, IDX_HEADS, tq), lambda b, i: (b, 0, i)),
                  pl.BlockSpec((tq, tq), lambda b, i: (0, 0))],
        out_specs=pl.BlockSpec((1, seq, tq), lambda b, i: (b, 0, i)),
        out_shape=jax.ShapeDtypeStruct((batch, seq, seq), F32),
        scratch_shapes=[pltpu.VMEM((seq, tq), F32),
                        pltpu.VMEM((IDX_HEADS, tq, 4 * IDX_DIM), BF16),
                        pltpu.VMEM((SUBLANES, tq), F32)],
        compiler_params=_params("parallel", "parallel"),
        name="indexer_topk_cap",
    )(act32, act32, iw_t, tri)


ATTN_HEADS_PER_STEP = 8


def _attn_kernel(q_ref, k_ref, v_ref, cap_ref, o_ref, acc_ref, vt_ref):
    qi = pl.program_id(2)
    tq = q_ref.shape[0]
    tk = tq
    dh = DSA_HEAD_DIM
    group = q_ref.shape[1] // dh
    heads = [slice(g * dh, (g + 1) * dh) for g in range(group)]
    acc_ref[...] = jnp.zeros_like(acc_ref)

    @pl.when(qi == 0)
    def _():
        for g in range(group):
            for kb in range(v_ref.shape[0] // tk):
                vt_ref[g, kb] = v_ref[kb * tk:(kb + 1) * tk, heads[g]].T

    def body(kj, carry):
        rows = pl.ds(pl.multiple_of(kj * tk, tk), tk)
        cap = cap_ref[0, rows, :]
        logits = [lax.dot_general(k_ref[rows, cols], q_ref[:, cols], (((1,), (1,)), ((), ())),
                                  preferred_element_type=F32) for cols in heads]
        new = []
        for g in range(group):
            m, l = carry[g]
            s = jnp.minimum(logits[g], cap)
            m_new = jnp.maximum(m, jnp.max(s, axis=0, keepdims=True))
            m_safe = jnp.where(m_new == -jnp.inf, 0.0, m_new)
            alpha = jnp.exp2(m - m_safe)
            p = jnp.exp2(s - m_safe)
            new.append((m_new, alpha * l + jnp.sum(p, axis=0, keepdims=True)))
            update = jnp.dot(vt_ref[g, kj], p.astype(BF16), preferred_element_type=F32)
            acc_ref[g] = alpha * acc_ref[g] + update
        return tuple(new)

    init = tuple((jnp.full((1, tq), -jnp.inf, F32), jnp.zeros((1, tq), F32)) for _ in range(group))
    final = lax.fori_loop(0, qi + 1, body, init)
    for g in range(group):
        o_ref[:, g * dh:(g + 1) * dh] = (acc_ref[g] / final[g][1]).T.astype(o_ref.dtype)


def _attention(act16, tile_of, cap_t, batch, seq, tq=256):
    tq = min(tq, seq)
    nq = seq // tq
    width = ATTN_HEADS_PER_STEP * DSA_HEAD_DIM
    n_groups = DSA_HEADS // ATTN_HEADS_PER_STEP
    t = act16.shape[0]
    q_block, k_block, v_block = (tile_of[nm] * IN_PROJ_TN // width for nm in ("d_q", "d_k", "d_v"))
    return pl.pallas_call(
        _attn_kernel,
        grid=(batch, n_groups, nq),
        in_specs=[pl.BlockSpec((tq, width), lambda b, h, i: (b * nq + i, q_block + h)),
                  pl.BlockSpec((seq, width), lambda b, h, i: (b, k_block + h)),
                  pl.BlockSpec((seq, width), lambda b, h, i: (b, v_block + h)),
                  pl.BlockSpec((1, seq, tq), lambda b, h, i: (b, 0, i))],
        out_specs=pl.BlockSpec((tq, width), lambda b, h, i: (b * nq + i, h)),
        out_shape=jax.ShapeDtypeStruct((t, DSA_HEADS * DSA_HEAD_DIM), BF16),
        scratch_shapes=[pltpu.VMEM((ATTN_HEADS_PER_STEP, DSA_HEAD_DIM, tq), F32),
                        pltpu.VMEM((ATTN_HEADS_PER_STEP, seq // tq, DSA_HEAD_DIM, tq), BF16)],
        compiler_params=_params("parallel", "parallel", "arbitrary"),
        name="dsa_attention",
    )(act16, act16, act16, cap_t)


def _merge_kernel(oa_ref, ob_ref, wa_ref, wb_ref, ga_ref, gb_ref, o_ref):
    ya = jnp.dot(oa_ref[...], wa_ref[...].astype(BF16), preferred_element_type=F32)
    yb = jnp.dot(ob_ref[...], wb_ref[...].astype(BF16), preferred_element_type=F32)
    o_ref[...] = (jax.nn.sigmoid(ga_ref[...]) * ya + jax.nn.sigmoid(gb_ref[...]) * yb).astype(o_ref.dtype)


def _merge(o_a, o_b, w_a, w_b, act32, tile_of, tm=1024):
    t, d = o_a.shape
    n = w_a.shape[1]
    tn = IN_PROJ_TN
    tm = min(tm, t)
    ga_tile, gb_tile = tile_of["gate_a"], tile_of["gate_b"]
    return pl.pallas_call(
        _merge_kernel,
        grid=(t // tm, n // tn),
        in_specs=[pl.BlockSpec((tm, d), lambda i, j: (i, 0)),
                  pl.BlockSpec((tm, d), lambda i, j: (i, 0)),
                  pl.BlockSpec((d, tn), lambda i, j: (0, j)),
                  pl.BlockSpec((d, tn), lambda i, j: (0, j)),
                  pl.BlockSpec((tm, tn), lambda i, j: (i, ga_tile + j)),
                  pl.BlockSpec((tm, tn), lambda i, j: (i, gb_tile + j))],
        out_specs=pl.BlockSpec((tm, tn), lambda i, j: (i, j)),
        out_shape=jax.ShapeDtypeStruct((t, n), BF16),
        compiler_params=_params("parallel", "parallel"),
        name="branch_merge",
    )(o_a, o_b, w_a, w_b, act32, act32)


def _mixer_out_kernel(m_ref, w_ref, x_ref, g_ref, gain_ref, sc_ref, sh_ref, x1_ref, h2_ref):
    y = jnp.dot(m_ref[...], w_ref[...], preferred_element_type=F32)
    x1 = x_ref[...] + g_ref[0] * y
    x1_ref[...] = x1
    h2_ref[...] = _rms_modulate(x1, gain_ref[...], sc_ref[0], sh_ref[0]).astype(h2_ref.dtype)


def _mixer_out(merged, w, x2d, g1, gain2, sc2, sh2, seq, tm=512):
    t, d = x2d.shape
    tm = min(tm, seq)
    per_batch = seq // tm
    vec = pl.BlockSpec((1, 1, d), lambda i: (i // per_batch, 0, 0))
    row = pl.BlockSpec((tm, d), lambda i: (i, 0))
    return pl.pallas_call(
        _mixer_out_kernel,
        grid=(t // tm,),
        in_specs=[row, pl.BlockSpec((d, d), lambda i: (0, 0), pipeline_mode=pl.Buffered(1)), row, vec,
                  pl.BlockSpec((1, d), lambda i: (0, 0)), vec, vec],
        out_specs=[row, row],
        out_shape=[jax.ShapeDtypeStruct((t, d), F32), jax.ShapeDtypeStruct((t, d), BF16)],
        compiler_params=_params("parallel"),
        name="mixer_out_norm2",
    )(merged, w, x2d, g1, gain2, sc2, sh2)


def _ffn_up_kernel(h_ref, wg_ref, wu_ref, o_ref):
    g = jnp.dot(h_ref[...], wg_ref[...].astype(BF16), preferred_element_type=F32)
    u = jnp.dot(h_ref[...], wu_ref[...].astype(BF16), preferred_element_type=F32)
    o_ref[...] = (_silu(g) * u).astype(o_ref.dtype)


def _ffn_up(h2, w_gate_up, tm=1024, tn=512):
    t, d = h2.shape
    d_ff = w_gate_up.shape[1] // 2
    tm = min(tm, t)
    nj = d_ff // tn
    return pl.pallas_call(
        _ffn_up_kernel,
        grid=(t // tm, nj),
        in_specs=[pl.BlockSpec((tm, d), lambda i, j: (i, 0)),
                  pl.BlockSpec((d, tn), lambda i, j: (0, j)),
                  pl.BlockSpec((d, tn), lambda i, j: (0, nj + j))],
        out_specs=pl.BlockSpec((tm, tn), lambda i, j: (i, j)),
        out_shape=jax.ShapeDtypeStruct((t, d_ff), BF16),
        compiler_params=_params("parallel", "parallel"),
        name="ffn_up",
    )(h2, w_gate_up, w_gate_up)


def _ffn_down_kernel(a_ref, w_ref, x_ref, g_ref, gain_ref, o_ref):
    x2 = x_ref[...] + g_ref[0] * jnp.dot(a_ref[...], w_ref[...], preferred_element_type=F32)
    y = x2 * lax.rsqrt(jnp.mean(x2 * x2, axis=-1, keepdims=True) + RMS_EPS)
    o_ref[...] = y * gain_ref[...]


def _ffn_down(act, w_down, x1, g2, final_gain, seq, tm=512):
    t, d_ff = act.shape
    d = w_down.shape[1]
    tm = min(tm, seq)
    per_batch = seq // tm
    return pl.pallas_call(
        _ffn_down_kernel,
        grid=(t // tm,),
        in_specs=[pl.BlockSpec((tm, d_ff), lambda i: (i, 0)),
                  pl.BlockSpec((d_ff, d), lambda i: (0, 0), pipeline_mode=pl.Buffered(1)),
                  pl.BlockSpec((tm, d), lambda i: (i, 0)),
                  pl.BlockSpec((1, 1, d), lambda i: (i // per_batch, 0, 0)),
                  pl.BlockSpec((1, d), lambda i: (0, 0))],
        out_specs=pl.BlockSpec((tm, d), lambda i: (i, 0)),
        out_shape=jax.ShapeDtypeStruct((t, d), F32),
        compiler_params=_params("parallel", vmem_limit=VMEM_LIMIT_LARGE),
        name="ffn_down_final_norm",
    )(act, w_down, x1, g2, final_gain)


def _rope_compact_tables(positions_flat, rotaries):
    cols, slices, lo = [], [], 0
    for rot, period in rotaries:
        half = rot // 2
        assert period % half == 0 and LANES % half == 0
        inv_freq = jnp.power(ROPE_THETA, -jnp.arange(0, rot, 2, dtype=F32) / rot)
        ang = positions_flat.astype(F32)[:, None] * inv_freq[None, :]
        cols += [jnp.cos(ang), jnp.sin(ang)]
        slices += [(lo, half), (lo + half, half)]
        lo += 2 * half
    cols.append(jnp.zeros((positions_flat.shape[0], LANES - lo), F32))
    return jnp.concatenate(cols, axis=1), tuple(slices)


def _layer(x2d, mod, positions_flat, batch, seq, norm1_gain, norm2_gain, w_in, gla_gate_up,
           gla_gate_bias, gla_norm_gain, w_branch_gla, w_branch_dsa, w_merge_out, w_ffn_gate_up,
           w_ffn_down, final_gain):
    d = x2d.shape[1]
    sh1, sc1, g1, sh2, sc2, g2 = [mod[:, i * d:(i + 1) * d][:, None, :] for i in range(N_MOD)]

    plan = _in_proj_plan(d)
    tile_of = plan[3]
    compact, table_slices = _rope_compact_tables(
        positions_flat, ((DSA_HEAD_DIM // ROPE_FRACTION, DSA_HEAD_DIM), (IDX_DIM // ROPE_FRACTION, IDX_DIM)))
    h, cos_d, sin_d, cos_i, sin_i = _norm_modulate(x2d, norm1_gain[None, :], sc1, sh1, compact, table_slices, seq)
    act32, act16 = _in_project(h, w_in.T, (cos_d, sin_d, cos_i, sin_i), plan)

    o_a = _gla(act32, act16, tile_of, gla_gate_up, gla_gate_bias[None, :], gla_norm_gain[None, :],
               batch, seq, rows=min(512, seq))

    iw_col = tile_of["small"] * IN_PROJ_TN + SMALL_IK_BLOCK * LANES + IDX_DIM
    iw_t = jnp.transpose(act32[:, iw_col:iw_col + IDX_HEADS].reshape(batch, seq, IDX_HEADS), (0, 2, 1))
    topk = min(IDX_TOPK_MAX, seq // 4)
    cap_t = _topk_cap(act32, tile_of, iw_t, batch, seq, topk)
    o_b = _attention(act16, tile_of, cap_t, batch, seq)

    merged = _merge(o_a, o_b, w_branch_gla, w_branch_dsa, act32, tile_of)
    x1, h2 = _mixer_out(merged, w_merge_out.astype(BF16), x2d, g1, norm2_gain[None, :], sc2, sh2, seq)
    act = _ffn_up(h2, w_ffn_gate_up)
    return _ffn_down(act, w_ffn_down.astype(BF16), x1, g2, final_gain[None, :], seq)


def kernel(x, c, positions, norm1_gain, norm2_gain, w_ada, b_ada, w_in, gla_gate_up, gla_gate_bias,
           gla_norm_gain, w_branch_gla, w_branch_dsa, w_merge_out, w_ffn_gate_up, w_ffn_down,
           final_norm_gain):
    batch, seq, d = x.shape
    depth = w_in.shape[0]
    assert depth == 1, "the final RMSNorm is fused into the single layer's FFN kernel"
    x2d = x.reshape(batch * seq, d)
    c_pad = jnp.zeros((SUBLANES, d), F32).at[:batch].set(c)
    mod = _modulation(c_pad, w_ada[0], b_ada[0][None, :])[:batch]
    out = _layer(x2d, mod, positions.reshape(-1), batch, seq, norm1_gain[0], norm2_gain[0], w_in[0],
                 gla_gate_up[0], gla_gate_bias[0], gla_norm_gain[0], w_branch_gla[0], w_branch_dsa[0],
                 w_merge_out[0], w_ffn_gate_up[0], w_ffn_down[0], final_norm_gain)
    return out.reshape(batch, seq, d)
```

```python
import functools

import jax
import jax.numpy as jnp
from jax import lax
from jax.experimental import pallas as pl
from jax.experimental.pallas import tpu as pltpu

F32 = jnp.float32
BF16 = jnp.bfloat16
HIGHEST = lax.Precision.HIGHEST

RMS_EPS = 1e-6
GLA_HEADS = 4
GLA_GATE_RANK = 16
GLA_TAU = 16.0
GLA_CHUNK = 64
GLA_SUB = 8
DSA_HEADS = 16
DSA_HEAD_DIM = 128
IDX_HEADS = 8
IDX_DIM = 64
IDX_TOPK_MAX = 256
ROPE_THETA = 500000.0
ROPE_FRACTION = 4
N_MOD = 6
LOG2_E = 1.4426950408889634

LANES = 128
SUBLANES = 8
VMEM_LIMIT = 48 * 1024 * 1024


VMEM_LIMIT_LARGE = 56 * 1024 * 1024


def _params(*semantics, vmem_limit=VMEM_LIMIT):
    return pltpu.CompilerParams(dimension_semantics=semantics, vmem_limit_bytes=vmem_limit)


def _silu(x):
    return x * jax.nn.sigmoid(x)


def _mod_kernel(c_ref, w_ref, b_ref, o_ref):
    rows = c_ref.shape[0]
    a_hi, a_lo = _split_bf16(_silu(c_ref[...]))
    w_hi, w_lo = _split_bf16(w_ref[...])
    stacked = jnp.concatenate([a_hi, a_lo], axis=0).astype(BF16)
    both = (jnp.dot(stacked, w_hi.astype(BF16), preferred_element_type=F32)
            + jnp.dot(stacked, w_lo.astype(BF16), preferred_element_type=F32))
    o_ref[...] = (both[0:rows] + both[rows:2 * rows]) + b_ref[...]


def _modulation(c_pad, w_ada, b_ada, tn=2048):
    rows, d = c_pad.shape
    n = w_ada.shape[1]
    return pl.pallas_call(
        _mod_kernel,
        grid=(n // tn,),
        in_specs=[pl.BlockSpec((rows, d), lambda j: (0, 0)),
                  pl.BlockSpec((d, tn), lambda j: (0, j)),
                  pl.BlockSpec((1, tn), lambda j: (0, j))],
        out_specs=pl.BlockSpec((rows, tn), lambda j: (0, j)),
        out_shape=jax.ShapeDtypeStruct((rows, n), F32),
        compiler_params=_params("parallel"),
        name="adaln_mod",
    )(c_pad, w_ada, b_ada)


def _rms_modulate(x, gain, scale, shift):
    y = x * lax.rsqrt(jnp.mean(x * x, axis=-1, keepdims=True) + RMS_EPS)
    return (y * gain) * (1.0 + scale) + shift


def _rope_lanes(x, cos, sin, half, period, limit=LANES):
    lane = lax.broadcasted_iota(jnp.int32, x.shape, 1)
    in_head = lane % period
    upper = pltpu.roll(x, LANES - half, axis=1)
    lower = pltpu.roll(x, half, axis=1)
    live = lane < limit
    first = jnp.logical_and(live, in_head < half)
    second = jnp.logical_and(live, jnp.logical_and(in_head >= half, in_head < 2 * half))
    return jnp.where(first, x * cos - upper * sin,
                     jnp.where(second, lower * sin + x * cos, x))


PLAIN32, GLA_Q32, IDX_Q32, SMALL32, PLAIN16, DSA_ROPE16, DSA_ROPE_Q16 = range(7)
SMALL_GLR_BLOCK, SMALL_IK_BLOCK = 0, 1
DSA_Q_SCALE = (DSA_HEAD_DIM ** -0.5) * LOG2_E
IN_PROJ_ROPE_PARTS = 4
IN_PROJ_ROPE_ROWS = 256
IN_PROJ_TN = 512


def _norm_mod_kernel(x_ref, gain_ref, sc_ref, sh_ref, o_ref):
    o_ref[...] = _rms_modulate(x_ref[...], gain_ref[...], sc_ref[0], sh_ref[0]).astype(o_ref.dtype)


def _norm_modulate(x2d, gain, sc, sh, seq, tm=512):
    t, d = x2d.shape
    per_batch = seq // tm
    return pl.pallas_call(
        _norm_mod_kernel,
        grid=(t // tm,),
        in_specs=[pl.BlockSpec((tm, d), lambda i: (i, 0)),
                  pl.BlockSpec((1, d), lambda i: (0, 0)),
                  pl.BlockSpec((1, 1, d), lambda i: (i // per_batch, 0, 0)),
                  pl.BlockSpec((1, 1, d), lambda i: (i // per_batch, 0, 0))],
        out_specs=pl.BlockSpec((tm, d), lambda i: (i, 0)),
        out_shape=jax.ShapeDtypeStruct((t, d), BF16),
        compiler_params=_params("parallel"),
        name="norm1_modulate",
    )(x2d, gain, sc, sh)


def _in_proj_kernel(off_ref, kind_ref, h_ref, wt_ref, ws_ref, cos_d_ref, sin_d_ref, cos_i_ref, sin_i_ref,
                    o32_ref, o16_ref):
    kind = kind_ref[pl.program_id(1)]
    tm, tn = o32_ref.shape
    n_groups = tn // LANES
    rot_d = DSA_HEAD_DIM // ROPE_FRACTION
    rot_i = IDX_DIM // ROPE_FRACTION

    def product():
        return lax.dot_general(h_ref[...], wt_ref[...].astype(BF16), (((1,), (1,)), ((), ())),
                               preferred_element_type=F32)

    def rope_tile(cos_ref, sin_ref, out_ref, half, period, groups, pre_scale=1.0, limit=LANES, w_ref=wt_ref):
        w = w_ref[...].astype(BF16)
        part = tm // IN_PROJ_ROPE_PARTS
        step = min(IN_PROJ_ROPE_ROWS, part)
        for p in range(IN_PROJ_ROPE_PARTS):
            acc = lax.dot_general(h_ref[p * part:(p + 1) * part, :], w, (((1,), (1,)), ((), ())),
                                  preferred_element_type=F32)
            if pre_scale != 1.0:
                acc = acc * pre_scale
            for r in range(0, part, step):
                rows = slice(p * part + r, p * part + r + step)
                cos = cos_ref[rows, :]
                sin = sin_ref[rows, :]
                for g in range(n_groups):
                    cols = slice(g * LANES, (g + 1) * LANES)
                    x = acc[r:r + step, cols]
                    if g in groups:
                        x = _rope_lanes(x, cos, sin, half, period, limit)
                    out_ref[rows, cols] = x.astype(out_ref.dtype)

    @pl.when(kind == PLAIN32)
    def _():
        o32_ref[...] = product()

    @pl.when(kind == GLA_Q32)
    def _():
        o32_ref[...] = product() * ((wt_ref.shape[1] // 2 // GLA_HEADS) ** -0.5)

    @pl.when(kind == IDX_Q32)
    def _():
        rope_tile(cos_i_ref, sin_i_ref, o32_ref, rot_i // 2, IDX_DIM, range(n_groups), pre_scale=IDX_DIM ** -0.5)

    @pl.when(kind == SMALL32)
    def _():
        rope_tile(cos_i_ref, sin_i_ref, o32_ref, rot_i // 2, IDX_DIM, (SMALL_IK_BLOCK,), limit=IDX_DIM,
                  w_ref=ws_ref)

    @pl.when(kind == PLAIN16)
    def _():
        o16_ref[...] = product().astype(BF16)

    @pl.when(kind == DSA_ROPE16)
    def _():
        rope_tile(cos_d_ref, sin_d_ref, o16_ref, rot_d // 2, DSA_HEAD_DIM, range(n_groups))

    @pl.when(kind == DSA_ROPE_Q16)
    def _():
        rope_tile(cos_d_ref, sin_d_ref, o16_ref, rot_d // 2, DSA_HEAD_DIM, range(n_groups),
                  pre_scale=DSA_Q_SCALE)


def _in_proj_plan(d):
    tn = IN_PROJ_TN
    gla_qk, dsa_w, idx_w = d // 2, DSA_HEADS * DSA_HEAD_DIM, IDX_HEADS * IDX_DIM
    names = ("g_q", "g_k", "g_v", "g_r", "g_lr", "d_q", "d_k", "d_v", "i_q", "i_k", "i_w", "gate_a", "gate_b")
    widths = (gla_qk, gla_qk, d, d, GLA_GATE_RANK, dsa_w, dsa_w, dsa_w, idx_w, IDX_DIM, IDX_HEADS, d, d)
    start, pos = {}, 0
    for nm, wd in zip(names, widths):
        start[nm] = pos
        pos += wd
    width = dict(zip(names, widths))
    start["small"], width["small"] = 0, tn
    assert start["i_w"] == start["i_k"] + IDX_DIM and IDX_DIM + IDX_HEADS <= LANES and GLA_GATE_RANK <= LANES
    f32_groups = (("g_q", GLA_Q32), ("g_k", PLAIN32), ("g_r", PLAIN32), ("small", SMALL32), ("i_q", IDX_Q32),
                  ("gate_a", PLAIN32), ("gate_b", PLAIN32))
    bf16_groups = (("g_v", PLAIN16), ("d_q", DSA_ROPE_Q16), ("d_k", DSA_ROPE16), ("d_v", PLAIN16))
    offsets, kinds, tile_of = [], [], {}
    for groups in (f32_groups, bf16_groups):
        base = len(offsets)
        for nm, kind in groups:
            tile_of[nm] = len(offsets) - base
            for t in range(-(-width[nm] // tn)):
                offsets.append(start[nm] + t * tn)
                kinds.append(kind)
        if groups is f32_groups:
            n32 = len(offsets)
    assert all(o % SUBLANES == 0 and o + tn <= pos for o in offsets)
    return offsets, kinds, n32, tile_of, start


def _in_project(h, w_in_t, tables, plan, tm=2048):
    t, d = h.shape
    tn = IN_PROJ_TN
    offsets, kinds, n32, _, start = plan
    n16 = len(offsets) - n32
    tm = min(tm, t)
    w_small = jnp.zeros((tn, d), F32)
    w_small = lax.dynamic_update_slice(
        w_small, w_in_t[start["g_lr"]:start["g_lr"] + GLA_GATE_RANK], (SMALL_GLR_BLOCK * LANES, 0))
    w_small = lax.dynamic_update_slice(
        w_small, w_in_t[start["i_k"]:start["i_k"] + IDX_DIM + IDX_HEADS], (SMALL_IK_BLOCK * LANES, 0))
    once = pl.Buffered(1)
    table_spec = pl.BlockSpec((tm, LANES), lambda i, j, off, kind: (i, 0), pipeline_mode=once)
    grid_spec = pltpu.PrefetchScalarGridSpec(
        num_scalar_prefetch=2,
        grid=(t // tm, len(offsets)),
        in_specs=[pl.BlockSpec((tm, d), lambda i, j, off, kind: (i, 0), pipeline_mode=once),
                  pl.BlockSpec((pl.Element(tn), pl.Element(d)),
                               lambda i, j, off, kind: (pl.multiple_of(off[j], SUBLANES), 0)),
                  pl.BlockSpec((tn, d), lambda i, j, off, kind: (0, 0), pipeline_mode=once),
                  table_spec, table_spec, table_spec, table_spec],
        out_specs=[pl.BlockSpec((tm, tn), lambda i, j, off, kind: (i, jnp.minimum(j, n32 - 1))),
                   pl.BlockSpec((tm, tn), lambda i, j, off, kind: (i, jnp.maximum(j - n32, 0)))],
    )
    return pl.pallas_call(
        _in_proj_kernel,
        grid_spec=grid_spec,
        out_shape=[jax.ShapeDtypeStruct((t, n32 * tn), F32), jax.ShapeDtypeStruct((t, n16 * tn), BF16)],
        compiler_params=_params("parallel", "arbitrary"),
        name="in_proj",
    )(jnp.asarray(offsets, jnp.int32), jnp.asarray(kinds, jnp.int32), h, w_in_t, w_small, *tables)


def _split3_bf16(x):
    hi, rest = _split_bf16(x)
    mid, lo = _split_bf16(rest)
    return hi, mid, lo


def _gla_kernel(q_ref, k_ref, v_ref, gr_ref, sm_ref, gup_ref, gb_ref, ng_ref, tri_ref, o_ref, st_ref):
    @pl.when(pl.program_id(2) == 0)
    def _():
        st_ref[...] = jnp.zeros_like(st_ref)

    rows = q_ref.shape[0]
    dk = q_ref.shape[1]
    c, sub = GLA_CHUNK, GLA_SUB
    n_sub = c // sub

    a_hi, a_lo = _split_bf16(sm_ref[:, 0:GLA_GATE_RANK])
    gate_lhs = jnp.concatenate(
        [a_hi, a_lo, a_hi, jnp.zeros((rows, LANES - 3 * GLA_GATE_RANK), F32)], axis=1).astype(BF16)
    z = jnp.dot(gate_lhs, gup_ref[...], preferred_element_type=F32) + gb_ref[...]
    log_g = (jnp.minimum(z, 0.0) - jnp.log(1.0 + jnp.exp(-jnp.abs(z)))) * (LOG2_E / GLA_TAU)
    parts = jnp.dot(tri_ref[...], jnp.concatenate(_split3_bf16(log_g), axis=1).astype(BF16),
                    preferred_element_type=F32)
    b = (parts[:, 0:dk] + parts[:, dk:2 * dk]) + parts[:, 2 * dk:3 * dk]

    q = q_ref[...]
    k = k_ref[...]

    k_tiles = k.reshape(rows // sub, sub, dk)
    b_tiles = b.reshape(rows // sub, sub, dk)
    row = lax.broadcasted_iota(jnp.int32, (rows, 1), 0)
    row_in_sub = row % sub
    col = lax.broadcasted_iota(jnp.int32, (rows, c), 1)
    target = lax.broadcasted_iota(jnp.int32, (rows, c), 0) % c
    a_diag = jnp.zeros((rows, c), F32)
    for delta in range(sub):
        if delta == 0:
            k_d, b_d = k, b
        else:
            k_d = pltpu.roll(k_tiles, delta, axis=1).reshape(rows, dk)
            b_d = pltpu.roll(b_tiles, delta, axis=1).reshape(rows, dk)
        decay = jnp.exp2(jnp.where(row_in_sub >= delta, b - b_d, -jnp.inf))
        a = jnp.sum(q * k_d * decay, axis=-1, keepdims=True)
        a_diag = jnp.where(col == target - delta, a, a_diag)

    for ch in range(rows // c):
        base = ch * c
        qc = q[base:base + c]
        kc = k[base:base + c]
        bc = b[base:base + c]
        vc = v_ref[base:base + c, :]
        st = st_ref[...]

        o_inter = lax.dot_general((qc * jnp.exp2(bc)).astype(BF16), st.astype(BF16),
                                  (((1,), (1,)), ((), ())), preferred_element_type=F32)

        blocks = [jnp.zeros((sub, c), F32)]
        for i_sub in range(1, n_sub):
            lo = i_sub * sub
            ref = bc[lo - 1:lo]
            q_hat = qc[lo:lo + sub] * jnp.exp2(bc[lo:lo + sub] - ref)
            k_hat = kc[0:lo] * jnp.exp2(ref - bc[0:lo])
            k_pad = jnp.concatenate([k_hat, jnp.zeros((c - lo, dk), F32)], axis=0)
            blocks.append(lax.dot_general(q_hat.astype(BF16), k_pad.astype(BF16),
                                          (((1,), (1,)), ((), ())), preferred_element_type=F32))
        attn = jnp.concatenate(blocks, axis=0) + a_diag[base:base + c]

        o = jnp.dot(attn.astype(BF16), vc, preferred_element_type=F32) + o_inter

        b_last = bc[c - 1:c]
        k_dec = (kc * jnp.exp2(b_last - bc)).astype(BF16)
        st_ref[...] = st * jnp.exp2(b_last) + lax.dot_general(
            vc, k_dec, (((0,), (0,)), ((), ())), preferred_element_type=F32)

        o = o * lax.rsqrt(jnp.mean(o * o, axis=-1, keepdims=True) + RMS_EPS) * ng_ref[...]
        o_ref[base:base + c, :] = (o * _silu(gr_ref[base:base + c, :])).astype(o_ref.dtype)


def _gla(act32, act16, tile_of, gate_up, gbias, ngain, batch, seq, rows=256):
    dk = gate_up.shape[1] // GLA_HEADS
    dv = ngain.shape[1] // GLA_HEADS
    t = act32.shape[0]
    nr = seq // rows
    g_hi, g_lo = _split_bf16(gate_up)
    gup_stack = jnp.concatenate(
        [g_hi, g_hi, g_lo, jnp.zeros((LANES - 3 * GLA_GATE_RANK, gate_up.shape[1]), F32)], axis=0).astype(BF16)
    idx = jnp.arange(rows)
    tri = jnp.logical_and(idx[:, None] // GLA_CHUNK == idx[None, :] // GLA_CHUNK,
                          idx[None, :] <= idx[:, None]).astype(BF16)

    def cols(group, width):
        first = tile_of[group] * IN_PROJ_TN // width
        return lambda b, h, r: (b * nr + r, first + h)

    lr_block = tile_of["small"] * IN_PROJ_TN // LANES + SMALL_GLR_BLOCK
    return pl.pallas_call(
        _gla_kernel,
        grid=(batch, GLA_HEADS, nr),
        in_specs=[pl.BlockSpec((rows, dk), cols("g_q", dk)),
                  pl.BlockSpec((rows, dk), cols("g_k", dk)),
                  pl.BlockSpec((rows, dv), cols("g_v", dv)),
                  pl.BlockSpec((rows, dv), cols("g_r", dv)),
                  pl.BlockSpec((rows, LANES), lambda b, h, r: (b * nr + r, lr_block)),
                  pl.BlockSpec((LANES, dk), lambda b, h, r: (0, h)),
                  pl.BlockSpec((1, dk), lambda b, h, r: (0, h)),
                  pl.BlockSpec((1, dv), lambda b, h, r: (0, h)),
                  pl.BlockSpec((rows, rows), lambda b, h, r: (0, 0))],
        out_specs=pl.BlockSpec((rows, dv), lambda b, h, r: (b * nr + r, h)),
        out_shape=jax.ShapeDtypeStruct((t, GLA_HEADS * dv), BF16),
        scratch_shapes=[pltpu.VMEM((dv, dk), F32)],
        compiler_params=_params("parallel", "parallel", "arbitrary"),
        name="gla",
    )(act32, act32, act16, act32, act32, gup_stack, gbias, ngain, tri)


def _split_bf16(x):
    hi = x.astype(BF16).astype(F32)
    return hi, x - hi


def _sortable_bits_to_float(key):
    return lax.bitcast_convert_type(jnp.where(key < 0, key ^ jnp.int32(0x7FFFFFFF), key), F32)


SORT_KEY_NEG_INF = -2139095041


def _topk_cap_kernel(iq_ref, ik_ref, iw_ref, tri_ref, o_ref, score_ref, rhs_ref, cnt_ref, *, topk, tq):
    qi = pl.program_id(1)
    tk = tq
    nk = ik_ref.shape[0] // tk
    n_live = qi + 1
    w = iw_ref[0] * (IDX_HEADS ** -0.5)
    pad_q = jnp.zeros((tq, IDX_DIM), F32)
    for h in range(IDX_HEADS):
        hi, lo = _split_bf16(iq_ref[:, h * IDX_DIM:(h + 1) * IDX_DIM])
        rhs_ref[h] = jnp.concatenate([hi, hi, lo, pad_q], axis=1).astype(BF16)
    t_pos = qi * tq + lax.broadcasted_iota(jnp.int32, (tk, tq), 1)
    s_off = lax.broadcasted_iota(jnp.int32, (tk, tq), 0)
    pad_k = jnp.zeros((tk, IDX_DIM), F32)

    def block_rows(kb):
        return pl.ds(pl.multiple_of(kb * tk, tk), tk)

    def score_block(kb, carry):
        hi, lo = _split_bf16(ik_ref[block_rows(kb), 0:IDX_DIM])
        lhs = jnp.concatenate([hi, lo, hi, pad_k], axis=1).astype(BF16)
        score = jnp.zeros((tk, tq), F32)
        for h in range(IDX_HEADS):
            dots = lax.dot_general(lhs, rhs_ref[h], (((1,), (1,)), ((), ())), preferred_element_type=F32)
            score = score + w[h:h + 1, :] * jnp.maximum(dots, 0.0)
        score_ref[block_rows(kb), :] = jnp.where(kb * tk + s_off <= t_pos, score + 0.0, -jnp.inf)
        return carry

    lax.fori_loop(0, n_live, score_block, 0)

    def count(pred):
        for n in range(1, nk + 1):
            @pl.when(n_live == n)
            def _(n=n):
                part = jnp.zeros((SUBLANES, tq), F32)
                for kb in range(n):
                    hit = jnp.where(pred(score_ref[kb * tk:(kb + 1) * tk, :]), 1.0, 0.0)
                    part = part + jnp.sum(hit.reshape(tk // SUBLANES, SUBLANES, tq), axis=0)
                cnt_ref[...] = part
        return jnp.sum(cnt_ref[...], axis=0, keepdims=True)

    def search(i, ans):
        cand = ans ^ jnp.left_shift(jnp.int32(1), 31 - i)
        cand_f = _sortable_bits_to_float(cand)
        cnt = count(lambda s: s >= cand_f)
        accept = jnp.logical_or(cnt >= float(topk), cand < SORT_KEY_NEG_INF)
        return jnp.where(accept, cand, ans)

    thr = _sortable_bits_to_float(
        lax.fori_loop(0, 32, search, jnp.full((1, tq), jnp.iinfo(jnp.int32).min, jnp.int32)))
    need = float(topk) - count(lambda s: s > thr)

    def emit(kb, seen):
        blk = score_ref[block_rows(kb), :]
        eq = jnp.where(blk == thr, 1.0, 0.0)
        rank = jnp.dot(tri_ref[...], eq.astype(BF16), preferred_element_type=F32) + seen
        chosen = jnp.logical_or(blk > thr, jnp.logical_and(blk == thr, rank <= need))
        o_ref[0, block_rows(kb), :] = jnp.where(
            jnp.logical_and(chosen, kb * tk + s_off <= t_pos), jnp.inf, -jnp.inf)
        return seen + jnp.sum(eq, axis=0, keepdims=True)

    lax.fori_loop(0, n_live, emit, jnp.zeros((1, tq), F32))

    def fill(kb, carry):
        o_ref[0, block_rows(kb), :] = jnp.full((tk, tq), -jnp.inf, F32)
        return carry

    lax.fori_loop(n_live, nk, fill, 0)


def _topk_cap(act32, tile_of, iw_t, batch, seq, topk, tq=256):
    tq = min(tq, seq)
    nq = seq // tq
    width = IDX_HEADS * IDX_DIM
    iq_block = tile_of["i_q"] * IN_PROJ_TN // width
    ik_block = tile_of["small"] * IN_PROJ_TN // LANES + SMALL_IK_BLOCK
    tri = jnp.tril(jnp.ones((tq, tq), BF16))
    return pl.pallas_call(
        functools.partial(_topk_cap_kernel, topk=topk, tq=tq),
        grid=(batch, nq),
        in_specs=[pl.BlockSpec((tq, width), lambda b, i: (b * nq + i, iq_block)),
                  pl.BlockSpec((seq, LANES), lambda b, i: (b, ik_block)),
                  pl.BlockSpec((1, IDX_HEADS, tq), lambda b, i: (b, 0, i)),
                  pl.BlockSpec((tq, tq), lambda b, i: (0, 0))],
        out_specs=pl.BlockSpec((1, seq, tq), lambda b, i: (b, 0, i)),
        out_shape=jax.ShapeDtypeStruct((batch, seq, seq), F32),
        scratch_shapes=[pltpu.VMEM((seq, tq), F32),
                        pltpu.VMEM((IDX_HEADS, tq, 4 * IDX_DIM), BF16),
                        pltpu.VMEM((SUBLANES, tq), F32)],
        compiler_params=_params("parallel", "parallel"),
        name="indexer_topk_cap",
    )(act32, act32, iw_t, tri)


ATTN_HEADS_PER_STEP = 8


def _attn_kernel(q_ref, k_ref, v_ref, cap_ref, o_ref, acc_ref, vt_ref):
    qi = pl.program_id(2)
    tq = q_ref.shape[0]
    tk = tq
    dh = DSA_HEAD_DIM
    group = q_ref.shape[1] // dh
    heads = [slice(g * dh, (g + 1) * dh) for g in range(group)]
    acc_ref[...] = jnp.zeros_like(acc_ref)

    @pl.when(qi == 0)
    def _():
        for g in range(group):
            for kb in range(v_ref.shape[0] // tk):
                vt_ref[g, kb] = v_ref[kb * tk:(kb + 1) * tk, heads[g]].T

    def body(kj, carry):
        rows = pl.ds(pl.multiple_of(kj * tk, tk), tk)
        cap = cap_ref[0, rows, :]
        logits = [lax.dot_general(k_ref[rows, cols], q_ref[:, cols], (((1,), (1,)), ((), ())),
                                  preferred_element_type=F32) for cols in heads]
        new = []
        for g in range(group):
            m, l = carry[g]
            s = jnp.minimum(logits[g], cap)
            m_new = jnp.maximum(m, jnp.max(s, axis=0, keepdims=True))
            m_safe = jnp.where(m_new == -jnp.inf, 0.0, m_new)
            alpha = jnp.exp2(m - m_safe)
            p = jnp.exp2(s - m_safe)
            new.append((m_new, alpha * l + jnp.sum(p, axis=0, keepdims=True)))
            update = jnp.dot(vt_ref[g, kj], p.astype(BF16), preferred_element_type=F32)
            acc_ref[g] = alpha * acc_ref[g] + update
        return tuple(new)

    init = tuple((jnp.full((1, tq), -jnp.inf, F32), jnp.zeros((1, tq), F32)) for _ in range(group))
    final = lax.fori_loop(0, qi + 1, body, init)
    for g in range(group):
        o_ref[:, g * dh:(g + 1) * dh] = (acc_ref[g] / final[g][1]).T.astype(o_ref.dtype)


def _attention(act16, tile_of, cap_t, batch, seq, tq=256):
    tq = min(tq, seq)
    nq = seq // tq
    width = ATTN_HEADS_PER_STEP * DSA_HEAD_DIM
    n_groups = DSA_HEADS // ATTN_HEADS_PER_STEP
    t = act16.shape[0]
    q_block, k_block, v_block = (tile_of[nm] * IN_PROJ_TN // width for nm in ("d_q", "d_k", "d_v"))
    return pl.pallas_call(
        _attn_kernel,
        grid=(batch, n_groups, nq),
        in_specs=[pl.BlockSpec((tq, width), lambda b, h, i: (b * nq + i, q_block + h)),
                  pl.BlockSpec((seq, width), lambda b, h, i: (b, k_block + h)),
                  pl.BlockSpec((seq, width), lambda b, h, i: (b, v_block + h)),
                  pl.BlockSpec((1, seq, tq), lambda b, h, i: (b, 0, i))],
        out_specs=pl.BlockSpec((tq, width), lambda b, h, i: (b * nq + i, h)),
        out_shape=jax.ShapeDtypeStruct((t, DSA_HEADS * DSA_HEAD_DIM), BF16),
        scratch_shapes=[pltpu.VMEM((ATTN_HEADS_PER_STEP, DSA_HEAD_DIM, tq), F32),
                        pltpu.VMEM((ATTN_HEADS_PER_STEP, seq // tq, DSA_HEAD_DIM, tq), BF16)],
        compiler_params=_params("parallel", "parallel", "arbitrary"),
        name="dsa_attention",
    )(act16, act16, act16, cap_t)


def _merge_kernel(oa_ref, ob_ref, wa_ref, wb_ref, ga_ref, gb_ref, o_ref):
    ya = jnp.dot(oa_ref[...], wa_ref[...].astype(BF16), preferred_element_type=F32)
    yb = jnp.dot(ob_ref[...], wb_ref[...].astype(BF16), preferred_element_type=F32)
    o_ref[...] = (jax.nn.sigmoid(ga_ref[...]) * ya + jax.nn.sigmoid(gb_ref[...]) * yb).astype(o_ref.dtype)


def _merge(o_a, o_b, w_a, w_b, act32, tile_of, tm=1024):
    t, d = o_a.shape
    n = w_a.shape[1]
    tn = IN_PROJ_TN
    tm = min(tm, t)
    ga_tile, gb_tile = tile_of["gate_a"], tile_of["gate_b"]
    return pl.pallas_call(
        _merge_kernel,
        grid=(t // tm, n // tn),
        in_specs=[pl.BlockSpec((tm, d), lambda i, j: (i, 0)),
                  pl.BlockSpec((tm, d), lambda i, j: (i, 0)),
                  pl.BlockSpec((d, tn), lambda i, j: (0, j)),
                  pl.BlockSpec((d, tn), lambda i, j: (0, j)),
                  pl.BlockSpec((tm, tn), lambda i, j: (i, ga_tile + j)),
                  pl.BlockSpec((tm, tn), lambda i, j: (i, gb_tile + j))],
        out_specs=pl.BlockSpec((tm, tn), lambda i, j: (i, j)),
        out_shape=jax.ShapeDtypeStruct((t, n), BF16),
        compiler_params=_params("parallel", "parallel"),
        name="branch_merge",
    )(o_a, o_b, w_a, w_b, act32, act32)


def _mixer_out_kernel(m_ref, w_ref, x_ref, g_ref, gain_ref, sc_ref, sh_ref, x1_ref, h2_ref):
    y = jnp.dot(m_ref[...], w_ref[...], preferred_element_type=F32)
    x1 = x_ref[...] + g_ref[0] * y
    x1_ref[...] = x1
    h2_ref[...] = _rms_modulate(x1, gain_ref[...], sc_ref[0], sh_ref[0]).astype(h2_ref.dtype)


def _mixer_out(merged, w, x2d, g1, gain2, sc2, sh2, seq, tm=512):
    t, d = x2d.shape
    tm = min(tm, seq)
    per_batch = seq // tm
    vec = pl.BlockSpec((1, 1, d), lambda i: (i // per_batch, 0, 0))
    row = pl.BlockSpec((tm, d), lambda i: (i, 0))
    return pl.pallas_call(
        _mixer_out_kernel,
        grid=(t // tm,),
        in_specs=[row, pl.BlockSpec((d, d), lambda i: (0, 0), pipeline_mode=pl.Buffered(1)), row, vec,
                  pl.BlockSpec((1, d), lambda i: (0, 0)), vec, vec],
        out_specs=[row, row],
        out_shape=[jax.ShapeDtypeStruct((t, d), F32), jax.ShapeDtypeStruct((t, d), BF16)],
        compiler_params=_params("parallel"),
        name="mixer_out_norm2",
    )(merged, w, x2d, g1, gain2, sc2, sh2)


def _ffn_up_kernel(h_ref, wg_ref, wu_ref, o_ref):
    g = jnp.dot(h_ref[...], wg_ref[...].astype(BF16), preferred_element_type=F32)
    u = jnp.dot(h_ref[...], wu_ref[...].astype(BF16), preferred_element_type=F32)
    o_ref[...] = (_silu(g) * u).astype(o_ref.dtype)


def _ffn_up(h2, w_gate_up, tm=1024, tn=512):
    t, d = h2.shape
    d_ff = w_gate_up.shape[1] // 2
    tm = min(tm, t)
    nj = d_ff // tn
    return pl.pallas_call(
        _ffn_up_kernel,
        grid=(t // tm, nj),
        in_specs=[pl.BlockSpec((tm, d), lambda i, j: (i, 0)),
                  pl.BlockSpec((d, tn), lambda i, j: (0, j)),
                  pl.BlockSpec((d, tn), lambda i, j: (0, nj + j))],
        out_specs=pl.BlockSpec((tm, tn), lambda i, j: (i, j)),
        out_shape=jax.ShapeDtypeStruct((t, d_ff), BF16),
        compiler_params=_params("parallel", "parallel"),
        name="ffn_up",
    )(h2, w_gate_up, w_gate_up)


def _ffn_down_kernel(a_ref, w_ref, x_ref, g_ref, gain_ref, o_ref):
    x2 = x_ref[...] + g_ref[0] * jnp.dot(a_ref[...], w_ref[...], preferred_element_type=F32)
    y = x2 * lax.rsqrt(jnp.mean(x2 * x2, axis=-1, keepdims=True) + RMS_EPS)
    o_ref[...] = y * gain_ref[...]


def _ffn_down(act, w_down, x1, g2, final_gain, seq, tm=512):
    t, d_ff = act.shape
    d = w_down.shape[1]
    tm = min(tm, seq)
    per_batch = seq // tm
    return pl.pallas_call(
        _ffn_down_kernel,
        grid=(t // tm,),
        in_specs=[pl.BlockSpec((tm, d_ff), lambda i: (i, 0)),
                  pl.BlockSpec((d_ff, d), lambda i: (0, 0), pipeline_mode=pl.Buffered(1)),
                  pl.BlockSpec((tm, d), lambda i: (i, 0)),
                  pl.BlockSpec((1, 1, d), lambda i: (i // per_batch, 0, 0)),
                  pl.BlockSpec((1, d), lambda i: (0, 0))],
        out_specs=pl.BlockSpec((tm, d), lambda i: (i, 0)),
        out_shape=jax.ShapeDtypeStruct((t, d), F32),
        compiler_params=_params("parallel", vmem_limit=VMEM_LIMIT_LARGE),
        name="ffn_down_final_norm",
    )(act, w_down, x1, g2, final_gain)


def _rope_tables(positions_flat, rot, period):
    half = rot // 2
    assert period % half == 0 and LANES % half == 0
    inv_freq = jnp.power(ROPE_THETA, -jnp.arange(0, rot, 2, dtype=F32) / rot)
    ang = positions_flat.astype(F32)[:, None] * inv_freq[None, :]
    reps = (1, LANES // half)
    return jnp.tile(jnp.cos(ang), reps), jnp.tile(jnp.sin(ang), reps)


def _layer(x2d, mod, positions_flat, batch, seq, norm1_gain, norm2_gain, w_in, gla_gate_up,
           gla_gate_bias, gla_norm_gain, w_branch_gla, w_branch_dsa, w_merge_out, w_ffn_gate_up,
           w_ffn_down, final_gain):
    d = x2d.shape[1]
    sh1, sc1, g1, sh2, sc2, g2 = [mod[:, i * d:(i + 1) * d][:, None, :] for i in range(N_MOD)]

    plan = _in_proj_plan(d)
    tile_of = plan[3]
    cos_d, sin_d = _rope_tables(positions_flat, DSA_HEAD_DIM // ROPE_FRACTION, DSA_HEAD_DIM)
    cos_i, sin_i = _rope_tables(positions_flat, IDX_DIM // ROPE_FRACTION, IDX_DIM)
    h = _norm_modulate(x2d, norm1_gain[None, :], sc1, sh1, seq)
    act32, act16 = _in_project(h, w_in.T, (cos_d, sin_d, cos_i, sin_i), plan)

    o_a = _gla(act32, act16, tile_of, gla_gate_up, gla_gate_bias[None, :], gla_norm_gain[None, :],
               batch, seq, rows=min(512, seq))

    iw_col = tile_of["small"] * IN_PROJ_TN + SMALL_IK_BLOCK * LANES + IDX_DIM
    iw_t = jnp.transpose(act32[:, iw_col:iw_col + IDX_HEADS].reshape(batch, seq, IDX_HEADS), (0, 2, 1))
    topk = min(IDX_TOPK_MAX, seq // 4)
    cap_t = _topk_cap(act32, tile_of, iw_t, batch, seq, topk)
    o_b = _attention(act16, tile_of, cap_t, batch, seq)

    merged = _merge(o_a, o_b, w_branch_gla, w_branch_dsa, act32, tile_of)
    x1, h2 = _mixer_out(merged, w_merge_out.astype(BF16), x2d, g1, norm2_gain[None, :], sc2, sh2, seq)
    act = _ffn_up(h2, w_ffn_gate_up)
    return _ffn_down(act, w_ffn_down.astype(BF16), x1, g2, final_gain[None, :], seq)


def kernel(x, c, positions, norm1_gain, norm2_gain, w_ada, b_ada, w_in, gla_gate_up, gla_gate_bias,
           gla_norm_gain, w_branch_gla, w_branch_dsa, w_merge_out, w_ffn_gate_up, w_ffn_down,
           final_norm_gain):
    batch, seq, d = x.shape
    depth = w_in.shape[0]
    assert depth == 1, "the final RMSNorm is fused into the single layer's FFN kernel"
    x2d = x.reshape(batch * seq, d)
    c_pad = jnp.zeros((SUBLANES, d), F32).at[:batch].set(c)
    mod = _modulation(c_pad, w_ada[0], b_ada[0][None, :])[:batch]
    out = _layer(x2d, mod, positions.reshape(-1), batch, seq, norm1_gain[0], norm2_gain[0], w_in[0],
                 gla_gate_up[0], gla_gate_bias[0], gla_norm_gain[0], w_branch_gla[0], w_branch_dsa[0],
                 w_merge_out[0], w_ffn_gate_up[0], w_ffn_down[0], final_norm_gain)
    return out.reshape(batch, seq, d)
```

```python
import functools

import jax
import jax.numpy as jnp
from jax import lax
from jax.experimental import pallas as pl
from jax.experimental.pallas import tpu as pltpu

F32 = jnp.float32
BF16 = jnp.bfloat16
HIGHEST = lax.Precision.HIGHEST

RMS_EPS = 1e-6
GLA_HEADS = 4
GLA_GATE_RANK = 16
GLA_TAU = 16.0
GLA_CHUNK = 64
GLA_SUB = 8
DSA_HEADS = 16
DSA_HEAD_DIM = 128
IDX_HEADS = 8
IDX_DIM = 64
IDX_TOPK_MAX = 256
ROPE_THETA = 500000.0
ROPE_FRACTION = 4
N_MOD = 6
LOG2_E = 1.4426950408889634

LANES = 128
SUBLANES = 8
VMEM_LIMIT = 48 * 1024 * 1024


VMEM_LIMIT_LARGE = 56 * 1024 * 1024


def _params(*semantics, vmem_limit=VMEM_LIMIT):
    return pltpu.CompilerParams(dimension_semantics=semantics, vmem_limit_bytes=vmem_limit)


def _silu(x):
    return x * jax.nn.sigmoid(x)


def _mod_kernel(c_ref, w_ref, b_ref, o_ref):
    rows = c_ref.shape[0]
    a_hi, a_lo = _split_bf16(_silu(c_ref[...]))
    w_hi, w_lo = _split_bf16(w_ref[...])
    stacked = jnp.concatenate([a_hi, a_lo], axis=0).astype(BF16)
    both = (jnp.dot(stacked, w_hi.astype(BF16), preferred_element_type=F32)
            + jnp.dot(stacked, w_lo.astype(BF16), preferred_element_type=F32))
    o_ref[...] = (both[0:rows] + both[rows:2 * rows]) + b_ref[...]


def _modulation(c_pad, w_ada, b_ada, tn=2048):
    rows, d = c_pad.shape
    n = w_ada.shape[1]
    return pl.pallas_call(
        _mod_kernel,
        grid=(n // tn,),
        in_specs=[pl.BlockSpec((rows, d), lambda j: (0, 0)),
                  pl.BlockSpec((d, tn), lambda j: (0, j)),
                  pl.BlockSpec((1, tn), lambda j: (0, j))],
        out_specs=pl.BlockSpec((rows, tn), lambda j: (0, j)),
        out_shape=jax.ShapeDtypeStruct((rows, n), F32),
        compiler_params=_params("parallel"),
        name="adaln_mod",
    )(c_pad, w_ada, b_ada)


def _rms_modulate(x, gain, scale, shift):
    y = x * lax.rsqrt(jnp.mean(x * x, axis=-1, keepdims=True) + RMS_EPS)
    return (y * gain) * (1.0 + scale) + shift


def _rope_lanes(x, cos, sin, half, period, limit=LANES):
    lane = lax.broadcasted_iota(jnp.int32, x.shape, 1)
    in_head = lane % period
    upper = pltpu.roll(x, LANES - half, axis=1)
    lower = pltpu.roll(x, half, axis=1)
    live = lane < limit
    first = jnp.logical_and(live, in_head < half)
    second = jnp.logical_and(live, jnp.logical_and(in_head >= half, in_head < 2 * half))
    return jnp.where(first, x * cos - upper * sin,
                     jnp.where(second, lower * sin + x * cos, x))


PLAIN32, GLA_Q32, IDX_Q32, SMALL32, PLAIN16, DSA_ROPE16, DSA_ROPE_Q16 = range(7)
SMALL_GLR_BLOCK, SMALL_IK_BLOCK = 0, 1
DSA_Q_SCALE = (DSA_HEAD_DIM ** -0.5) * LOG2_E
IN_PROJ_ROPE_PARTS = 4
IN_PROJ_ROPE_ROWS = 256
IN_PROJ_TN = 512


def _norm_mod_kernel(x_ref, gain_ref, sc_ref, sh_ref, o_ref):
    o_ref[...] = _rms_modulate(x_ref[...], gain_ref[...], sc_ref[0], sh_ref[0]).astype(o_ref.dtype)


def _norm_modulate(x2d, gain, sc, sh, seq, tm=512):
    t, d = x2d.shape
    per_batch = seq // tm
    return pl.pallas_call(
        _norm_mod_kernel,
        grid=(t // tm,),
        in_specs=[pl.BlockSpec((tm, d), lambda i: (i, 0)),
                  pl.BlockSpec((1, d), lambda i: (0, 0)),
                  pl.BlockSpec((1, 1, d), lambda i: (i // per_batch, 0, 0)),
                  pl.BlockSpec((1, 1, d), lambda i: (i // per_batch, 0, 0))],
        out_specs=pl.BlockSpec((tm, d), lambda i: (i, 0)),
        out_shape=jax.ShapeDtypeStruct((t, d), BF16),
        compiler_params=_params("parallel"),
        name="norm1_modulate",
    )(x2d, gain, sc, sh)


def _in_proj_kernel(off_ref, kind_ref, h_ref, wt_ref, ws_ref, cos_d_ref, sin_d_ref, cos_i_ref, sin_i_ref,
                    o32_ref, o16_ref):
    kind = kind_ref[pl.program_id(1)]
    tm, tn = o32_ref.shape
    n_groups = tn // LANES
    rot_d = DSA_HEAD_DIM // ROPE_FRACTION
    rot_i = IDX_DIM // ROPE_FRACTION

    def product():
        return lax.dot_general(h_ref[...], wt_ref[...].astype(BF16), (((1,), (1,)), ((), ())),
                               preferred_element_type=F32)

    def rope_tile(cos_ref, sin_ref, out_ref, half, period, groups, pre_scale=1.0, limit=LANES, w_ref=wt_ref):
        w = w_ref[...].astype(BF16)
        part = tm // IN_PROJ_ROPE_PARTS
        step = min(IN_PROJ_ROPE_ROWS, part)
        for p in range(IN_PROJ_ROPE_PARTS):
            acc = lax.dot_general(h_ref[p * part:(p + 1) * part, :], w, (((1,), (1,)), ((), ())),
                                  preferred_element_type=F32)
            if pre_scale != 1.0:
                acc = acc * pre_scale
            for r in range(0, part, step):
                rows = slice(p * part + r, p * part + r + step)
                cos = cos_ref[rows, :]
                sin = sin_ref[rows, :]
                for g in range(n_groups):
                    cols = slice(g * LANES, (g + 1) * LANES)
                    x = acc[r:r + step, cols]
                    if g in groups:
                        x = _rope_lanes(x, cos, sin, half, period, limit)
                    out_ref[rows, cols] = x.astype(out_ref.dtype)

    @pl.when(kind == PLAIN32)
    def _():
        o32_ref[...] = product()

    @pl.when(kind == GLA_Q32)
    def _():
        o32_ref[...] = product() * ((wt_ref.shape[1] // 2 // GLA_HEADS) ** -0.5)

    @pl.when(kind == IDX_Q32)
    def _():
        rope_tile(cos_i_ref, sin_i_ref, o32_ref, rot_i // 2, IDX_DIM, range(n_groups), pre_scale=IDX_DIM ** -0.5)

    @pl.when(kind == SMALL32)
    def _():
        rope_tile(cos_i_ref, sin_i_ref, o32_ref, rot_i // 2, IDX_DIM, (SMALL_IK_BLOCK,), limit=IDX_DIM,
                  w_ref=ws_ref)

    @pl.when(kind == PLAIN16)
    def _():
        o16_ref[...] = product().astype(BF16)

    @pl.when(kind == DSA_ROPE16)
    def _():
        rope_tile(cos_d_ref, sin_d_ref, o16_ref, rot_d // 2, DSA_HEAD_DIM, range(n_groups))

    @pl.when(kind == DSA_ROPE_Q16)
    def _():
        rope_tile(cos_d_ref, sin_d_ref, o16_ref, rot_d // 2, DSA_HEAD_DIM, range(n_groups),
                  pre_scale=DSA_Q_SCALE)


def _in_proj_plan(d):
    tn = IN_PROJ_TN
    gla_qk, dsa_w, idx_w = d // 2, DSA_HEADS * DSA_HEAD_DIM, IDX_HEADS * IDX_DIM
    names = ("g_q", "g_k", "g_v", "g_r", "g_lr", "d_q", "d_k", "d_v", "i_q", "i_k", "i_w", "gate_a", "gate_b")
    widths = (gla_qk, gla_qk, d, d, GLA_GATE_RANK, dsa_w, dsa_w, dsa_w, idx_w, IDX_DIM, IDX_HEADS, d, d)
    start, pos = {}, 0
    for nm, wd in zip(names, widths):
        start[nm] = pos
        pos += wd
    width = dict(zip(names, widths))
    start["small"], width["small"] = 0, tn
    assert start["i_w"] == start["i_k"] + IDX_DIM and IDX_DIM + IDX_HEADS <= LANES and GLA_GATE_RANK <= LANES
    f32_groups = (("g_q", GLA_Q32), ("g_k", PLAIN32), ("g_r", PLAIN32), ("small", SMALL32), ("i_q", IDX_Q32),
                  ("gate_a", PLAIN32), ("gate_b", PLAIN32))
    bf16_groups = (("g_v", PLAIN16), ("d_q", DSA_ROPE_Q16), ("d_k", DSA_ROPE16), ("d_v", PLAIN16))
    offsets, kinds, tile_of = [], [], {}
    for groups in (f32_groups, bf16_groups):
        base = len(offsets)
        for nm, kind in groups:
            tile_of[nm] = len(offsets) - base
            for t in range(-(-width[nm] // tn)):
                offsets.append(start[nm] + t * tn)
                kinds.append(kind)
        if groups is f32_groups:
            n32 = len(offsets)
    assert all(o % SUBLANES == 0 and o + tn <= pos for o in offsets)
    return offsets, kinds, n32, tile_of, start


def _in_project(h, w_in_t, tables, plan, tm=2048):
    t, d = h.shape
    tn = IN_PROJ_TN
    offsets, kinds, n32, _, start = plan
    n16 = len(offsets) - n32
    tm = min(tm, t)
    w_small = jnp.zeros((tn, d), F32)
    w_small = lax.dynamic_update_slice(
        w_small, w_in_t[start["g_lr"]:start["g_lr"] + GLA_GATE_RANK], (SMALL_GLR_BLOCK * LANES, 0))
    w_small = lax.dynamic_update_slice(
        w_small, w_in_t[start["i_k"]:start["i_k"] + IDX_DIM + IDX_HEADS], (SMALL_IK_BLOCK * LANES, 0))
    once = pl.Buffered(1)
    table_spec = pl.BlockSpec((tm, LANES), lambda i, j, off, kind: (i, 0), pipeline_mode=once)
    grid_spec = pltpu.PrefetchScalarGridSpec(
        num_scalar_prefetch=2,
        grid=(t // tm, len(offsets)),
        in_specs=[pl.BlockSpec((tm, d), lambda i, j, off, kind: (i, 0), pipeline_mode=once),
                  pl.BlockSpec((pl.Element(tn), pl.Element(d)),
                               lambda i, j, off, kind: (pl.multiple_of(off[j], SUBLANES), 0)),
                  pl.BlockSpec((tn, d), lambda i, j, off, kind: (0, 0), pipeline_mode=once),
                  table_spec, table_spec, table_spec, table_spec],
        out_specs=[pl.BlockSpec((tm, tn), lambda i, j, off, kind: (i, jnp.minimum(j, n32 - 1))),
                   pl.BlockSpec((tm, tn), lambda i, j, off, kind: (i, jnp.maximum(j - n32, 0)))],
    )
    return pl.pallas_call(
        _in_proj_kernel,
        grid_spec=grid_spec,
        out_shape=[jax.ShapeDtypeStruct((t, n32 * tn), F32), jax.ShapeDtypeStruct((t, n16 * tn), BF16)],
        compiler_params=_params("parallel", "arbitrary"),
        name="in_proj",
    )(jnp.asarray(offsets, jnp.int32), jnp.asarray(kinds, jnp.int32), h, w_in_t, w_small, *tables)


def _split3_bf16(x):
    hi, rest = _split_bf16(x)
    mid, lo = _split_bf16(rest)
    return hi, mid, lo


def _gla_kernel(q_ref, k_ref, v_ref, gr_ref, sm_ref, gup_ref, gb_ref, ng_ref, tri_ref, o_ref, st_ref):
    @pl.when(pl.program_id(2) == 0)
    def _():
        st_ref[...] = jnp.zeros_like(st_ref)

    rows = q_ref.shape[0]
    dk = q_ref.shape[1]
    c, sub = GLA_CHUNK, GLA_SUB
    n_sub = c // sub

    a_hi, a_lo = _split_bf16(sm_ref[:, 0:GLA_GATE_RANK])
    gate_lhs = jnp.concatenate(
        [a_hi, a_lo, a_hi, jnp.zeros((rows, LANES - 3 * GLA_GATE_RANK), F32)], axis=1).astype(BF16)
    z = jnp.dot(gate_lhs, gup_ref[...], preferred_element_type=F32) + gb_ref[...]
    log_g = (jnp.minimum(z, 0.0) - jnp.log(1.0 + jnp.exp(-jnp.abs(z)))) * (LOG2_E / GLA_TAU)
    parts = jnp.dot(tri_ref[...], jnp.concatenate(_split3_bf16(log_g), axis=1).astype(BF16),
                    preferred_element_type=F32)
    b = (parts[:, 0:dk] + parts[:, dk:2 * dk]) + parts[:, 2 * dk:3 * dk]

    q = q_ref[...]
    k = k_ref[...]

    k_tiles = k.reshape(rows // sub, sub, dk)
    b_tiles = b.reshape(rows // sub, sub, dk)
    row = lax.broadcasted_iota(jnp.int32, (rows, 1), 0)
    row_in_sub = row % sub
    col = lax.broadcasted_iota(jnp.int32, (rows, c), 1)
    target = lax.broadcasted_iota(jnp.int32, (rows, c), 0) % c
    a_diag = jnp.zeros((rows, c), F32)
    for delta in range(sub):
        if delta == 0:
            k_d, b_d = k, b
        else:
            k_d = pltpu.roll(k_tiles, delta, axis=1).reshape(rows, dk)
            b_d = pltpu.roll(b_tiles, delta, axis=1).reshape(rows, dk)
        decay = jnp.exp2(jnp.where(row_in_sub >= delta, b - b_d, -jnp.inf))
        a = jnp.sum(q * k_d * decay, axis=-1, keepdims=True)
        a_diag = jnp.where(col == target - delta, a, a_diag)

    for ch in range(rows // c):
        base = ch * c
        qc = q[base:base + c]
        kc = k[base:base + c]
        bc = b[base:base + c]
        vc = v_ref[base:base + c, :]
        st = st_ref[...]

        o_inter = lax.dot_general((qc * jnp.exp2(bc)).astype(BF16), st.astype(BF16),
                                  (((1,), (1,)), ((), ())), preferred_element_type=F32)

        blocks = [jnp.zeros((sub, c), F32)]
        for i_sub in range(1, n_sub):
            lo = i_sub * sub
            ref = bc[lo - 1:lo]
            q_hat = qc[lo:lo + sub] * jnp.exp2(bc[lo:lo + sub] - ref)
            k_hat = kc[0:lo] * jnp.exp2(ref - bc[0:lo])
            k_pad = jnp.concatenate([k_hat, jnp.zeros((c - lo, dk), F32)], axis=0)
            blocks.append(lax.dot_general(q_hat.astype(BF16), k_pad.astype(BF16),
                                          (((1,), (1,)), ((), ())), preferred_element_type=F32))
        attn = jnp.concatenate(blocks, axis=0) + a_diag[base:base + c]

        o = jnp.dot(attn.astype(BF16), vc, preferred_element_type=F32) + o_inter

        b_last = bc[c - 1:c]
        k_dec = (kc * jnp.exp2(b_last - bc)).astype(BF16)
        st_ref[...] = st * jnp.exp2(b_last) + lax.dot_general(
            vc, k_dec, (((0,), (0,)), ((), ())), preferred_element_type=F32)

        o = o * lax.rsqrt(jnp.mean(o * o, axis=-1, keepdims=True) + RMS_EPS) * ng_ref[...]
        o_ref[base:base + c, :] = (o * _silu(gr_ref[base:base + c, :])).astype(o_ref.dtype)


def _gla(act32, act16, tile_of, gate_up, gbias, ngain, batch, seq, rows=256):
    dk = gate_up.shape[1] // GLA_HEADS
    dv = ngain.shape[1] // GLA_HEADS
    t = act32.shape[0]
    nr = seq // rows
    g_hi, g_lo = _split_bf16(gate_up)
    gup_stack = jnp.concatenate(
        [g_hi, g_hi, g_lo, jnp.zeros((LANES - 3 * GLA_GATE_RANK, gate_up.shape[1]), F32)], axis=0).astype(BF16)
    idx = jnp.arange(rows)
    tri = jnp.logical_and(idx[:, None] // GLA_CHUNK == idx[None, :] // GLA_CHUNK,
                          idx[None, :] <= idx[:, None]).astype(BF16)

    def cols(group, width):
        first = tile_of[group] * IN_PROJ_TN // width
        return lambda b, h, r: (b * nr + r, first + h)

    lr_block = tile_of["small"] * IN_PROJ_TN // LANES + SMALL_GLR_BLOCK
    return pl.pallas_call(
        _gla_kernel,
        grid=(batch, GLA_HEADS, nr),
        in_specs=[pl.BlockSpec((rows, dk), cols("g_q", dk)),
                  pl.BlockSpec((rows, dk), cols("g_k", dk)),
                  pl.BlockSpec((rows, dv), cols("g_v", dv)),
                  pl.BlockSpec((rows, dv), cols("g_r", dv)),
                  pl.BlockSpec((rows, LANES), lambda b, h, r: (b * nr + r, lr_block)),
                  pl.BlockSpec((LANES, dk), lambda b, h, r: (0, h)),
                  pl.BlockSpec((1, dk), lambda b, h, r: (0, h)),
                  pl.BlockSpec((1, dv), lambda b, h, r: (0, h)),
                  pl.BlockSpec((rows, rows), lambda b, h, r: (0, 0))],
        out_specs=pl.BlockSpec((rows, dv), lambda b, h, r: (b * nr + r, h)),
        out_shape=jax.ShapeDtypeStruct((t, GLA_HEADS * dv), BF16),
        scratch_shapes=[pltpu.VMEM((dv, dk), F32)],
        compiler_params=_params("parallel", "parallel", "arbitrary"),
        name="gla",
    )(act32, act32, act16, act32, act32, gup_stack, gbias, ngain, tri)


def _split_bf16(x):
    hi = x.astype(BF16).astype(F32)
    return hi, x - hi


def _sortable_bits_to_float(key):
    return lax.bitcast_convert_type(jnp.where(key < 0, key ^ jnp.int32(0x7FFFFFFF), key), F32)


SORT_KEY_NEG_INF = -2139095041


def _topk_cap_kernel(iq_ref, ik_ref, iw_ref, tri_ref, o_ref, score_ref, rhs_ref, sel_ref, *, topk, tq):
    qi = pl.program_id(1)
    tk = tq
    nk = ik_ref.shape[0] // tk
    n_live = qi + 1
    w = iw_ref[0] * (IDX_HEADS ** -0.5)
    pad_q = jnp.zeros((tq, IDX_DIM), F32)
    for h in range(IDX_HEADS):
        hi, lo = _split_bf16(iq_ref[:, h * IDX_DIM:(h + 1) * IDX_DIM])
        rhs_ref[h] = jnp.concatenate([hi, hi, lo, pad_q], axis=1).astype(BF16)
    t_pos = qi * tq + lax.broadcasted_iota(jnp.int32, (tk, tq), 1)
    s_off = lax.broadcasted_iota(jnp.int32, (tk, tq), 0)
    pad_k = jnp.zeros((tk, IDX_DIM), F32)

    def block_rows(kb):
        return pl.ds(pl.multiple_of(kb * tk, tk), tk)

    def score_block(kb, carry):
        hi, lo = _split_bf16(ik_ref[block_rows(kb), 0:IDX_DIM])
        lhs = jnp.concatenate([hi, lo, hi, pad_k], axis=1).astype(BF16)
        score = jnp.zeros((tk, tq), F32)
        for h in range(IDX_HEADS):
            dots = lax.dot_general(lhs, rhs_ref[h], (((1,), (1,)), ((), ())), preferred_element_type=F32)
            score = score + w[h:h + 1, :] * jnp.maximum(dots, 0.0)
        score_ref[block_rows(kb), :] = jnp.where(kb * tk + s_off <= t_pos, score + 0.0, -jnp.inf)
        return carry

    lax.fori_loop(0, n_live, score_block, 0)

    def count(n, pred):
        part = jnp.zeros((SUBLANES, tq), F32)
        for kb in range(n):
            hit = jnp.where(pred(score_ref[kb * tk:(kb + 1) * tk, :]), 1.0, 0.0)
            part = part + jnp.sum(hit.reshape(tk // SUBLANES, SUBLANES, tq), axis=0)
        return jnp.sum(part, axis=0, keepdims=True)

    for n in range(1, nk + 1):
        @pl.when(n_live == n)
        def _(n=n):
            def search(i, ans):
                cand = ans ^ jnp.left_shift(jnp.int32(1), 31 - i)
                cand_f = _sortable_bits_to_float(cand)
                accept = jnp.logical_or(count(n, lambda s: s >= cand_f) >= float(topk),
                                        cand < SORT_KEY_NEG_INF)
                return jnp.where(accept, cand, ans)

            thr_n = _sortable_bits_to_float(
                lax.fori_loop(0, 32, search, jnp.full((1, tq), jnp.iinfo(jnp.int32).min, jnp.int32)))
            sel_ref[0:1, :] = thr_n
            sel_ref[1:2, :] = float(topk) - count(n, lambda s: s > thr_n)

    thr = sel_ref[0:1, :]
    need = sel_ref[1:2, :]

    def emit(kb, seen):
        blk = score_ref[block_rows(kb), :]
        eq = jnp.where(blk == thr, 1.0, 0.0)
        rank = jnp.dot(tri_ref[...], eq.astype(BF16), preferred_element_type=F32) + seen
        chosen = jnp.logical_or(blk > thr, jnp.logical_and(blk == thr, rank <= need))
        o_ref[0, block_rows(kb), :] = jnp.where(
            jnp.logical_and(chosen, kb * tk + s_off <= t_pos), jnp.inf, -jnp.inf)
        return seen + jnp.sum(eq, axis=0, keepdims=True)

    lax.fori_loop(0, n_live, emit, jnp.zeros((1, tq), F32))

    def fill(kb, carry):
        o_ref[0, block_rows(kb), :] = jnp.full((tk, tq), -jnp.inf, F32)
        return carry

    lax.fori_loop(n_live, nk, fill, 0)


def _topk_cap(act32, tile_of, iw_t, batch, seq, topk, tq=256):
    tq = min(tq, seq)
    nq = seq // tq
    width = IDX_HEADS * IDX_DIM
    iq_block = tile_of["i_q"] * IN_PROJ_TN // width
    ik_block = tile_of["small"] * IN_PROJ_TN // LANES + SMALL_IK_BLOCK
    tri = jnp.tril(jnp.ones((tq, tq), BF16))
    return pl.pallas_call(
        functools.partial(_topk_cap_kernel, topk=topk, tq=tq),
        grid=(batch, nq),
        in_specs=[pl.BlockSpec((tq, width), lambda b, i: (b * nq + i, iq_block)),
                  pl.BlockSpec((seq, LANES), lambda b, i: (b, ik_block)),
                  pl.BlockSpec((1, IDX_HEADS, tq), lambda b, i: (b, 0, i)),
                  pl.BlockSpec((tq, tq), lambda b, i: (0, 0))],
        out_specs=pl.BlockSpec((1, seq, tq), lambda b, i: (b, 0, i)),
        out_shape=jax.ShapeDtypeStruct((batch, seq, seq), F32),
        scratch_shapes=[pltpu.VMEM((seq, tq), F32),
                        pltpu.VMEM((IDX_HEADS, tq, 4 * IDX_DIM), BF16),
                        pltpu.VMEM((SUBLANES, tq), F32)],
        compiler_params=_params("parallel", "parallel"),
        name="indexer_topk_cap",
    )(act32, act32, iw_t, tri)


ATTN_HEADS_PER_STEP = 8


def _attn_kernel(q_ref, k_ref, v_ref, cap_ref, o_ref, acc_ref, vt_ref):
    qi = pl.program_id(2)
    tq = q_ref.shape[0]
    tk = tq
    dh = DSA_HEAD_DIM
    group = q_ref.shape[1] // dh
    heads = [slice(g * dh, (g + 1) * dh) for g in range(group)]
    acc_ref[...] = jnp.zeros_like(acc_ref)

    @pl.when(qi == 0)
    def _():
        for g in range(group):
            for kb in range(v_ref.shape[0] // tk):
                vt_ref[g, kb] = v_ref[kb * tk:(kb + 1) * tk, heads[g]].T

    def body(kj, carry):
        rows = pl.ds(pl.multiple_of(kj * tk, tk), tk)
        cap = cap_ref[0, rows, :]
        logits = [lax.dot_general(k_ref[rows, cols], q_ref[:, cols], (((1,), (1,)), ((), ())),
                                  preferred_element_type=F32) for cols in heads]
        new = []
        for g in range(group):
            m, l = carry[g]
            s = jnp.minimum(logits[g], cap)
            m_new = jnp.maximum(m, jnp.max(s, axis=0, keepdims=True))
            m_safe = jnp.where(m_new == -jnp.inf, 0.0, m_new)
            alpha = jnp.exp2(m - m_safe)
            p = jnp.exp2(s - m_safe)
            new.append((m_new, alpha * l + jnp.sum(p, axis=0, keepdims=True)))
            update = jnp.dot(vt_ref[g, kj], p.astype(BF16), preferred_element_type=F32)
            acc_ref[g] = alpha * acc_ref[g] + update
        return tuple(new)

    init = tuple((jnp.full((1, tq), -jnp.inf, F32), jnp.zeros((1, tq), F32)) for _ in range(group))
    final = lax.fori_loop(0, qi + 1, body, init)
    for g in range(group):
        o_ref[:, g * dh:(g + 1) * dh] = (acc_ref[g] / final[g][1]).T.astype(o_ref.dtype)


def _attention(act16, tile_of, cap_t, batch, seq, tq=256):
    tq = min(tq, seq)
    nq = seq // tq
    width = ATTN_HEADS_PER_STEP * DSA_HEAD_DIM
    n_groups = DSA_HEADS // ATTN_HEADS_PER_STEP
    t = act16.shape[0]
    q_block, k_block, v_block = (tile_of[nm] * IN_PROJ_TN // width for nm in ("d_q", "d_k", "d_v"))
    return pl.pallas_call(
        _attn_kernel,
        grid=(batch, n_groups, nq),
        in_specs=[pl.BlockSpec((tq, width), lambda b, h, i: (b * nq + i, q_block + h)),
                  pl.BlockSpec((seq, width), lambda b, h, i: (b, k_block + h)),
                  pl.BlockSpec((seq, width), lambda b, h, i: (b, v_block + h)),
                  pl.BlockSpec((1, seq, tq), lambda b, h, i: (b, 0, i))],
        out_specs=pl.BlockSpec((tq, width), lambda b, h, i: (b * nq + i, h)),
        out_shape=jax.ShapeDtypeStruct((t, DSA_HEADS * DSA_HEAD_DIM), BF16),
        scratch_shapes=[pltpu.VMEM((ATTN_HEADS_PER_STEP, DSA_HEAD_DIM, tq), F32),
                        pltpu.VMEM((ATTN_HEADS_PER_STEP, seq // tq, DSA_HEAD_DIM, tq), BF16)],
        compiler_params=_params("parallel", "parallel", "arbitrary"),
        name="dsa_attention",
    )(act16, act16, act16, cap_t)


def _merge_kernel(oa_ref, ob_ref, wa_ref, wb_ref, ga_ref, gb_ref, o_ref):
    ya = jnp.dot(oa_ref[...], wa_ref[...].astype(BF16), preferred_element_type=F32)
    yb = jnp.dot(ob_ref[...], wb_ref[...].astype(BF16), preferred_element_type=F32)
    o_ref[...] = (jax.nn.sigmoid(ga_ref[...]) * ya + jax.nn.sigmoid(gb_ref[...]) * yb).astype(o_ref.dtype)


def _merge(o_a, o_b, w_a, w_b, act32, tile_of, tm=1024):
    t, d = o_a.shape
    n = w_a.shape[1]
    tn = IN_PROJ_TN
    tm = min(tm, t)
    ga_tile, gb_tile = tile_of["gate_a"], tile_of["gate_b"]
    return pl.pallas_call(
        _merge_kernel,
        grid=(t // tm, n // tn),
        in_specs=[pl.BlockSpec((tm, d), lambda i, j: (i, 0)),
                  pl.BlockSpec((tm, d), lambda i, j: (i, 0)),
                  pl.BlockSpec((d, tn), lambda i, j: (0, j)),
                  pl.BlockSpec((d, tn), lambda i, j: (0, j)),
                  pl.BlockSpec((tm, tn), lambda i, j: (i, ga_tile + j)),
                  pl.BlockSpec((tm, tn), lambda i, j: (i, gb_tile + j))],
        out_specs=pl.BlockSpec((tm, tn), lambda i, j: (i, j)),
        out_shape=jax.ShapeDtypeStruct((t, n), BF16),
        compiler_params=_params("parallel", "parallel"),
        name="branch_merge",
    )(o_a, o_b, w_a, w_b, act32, act32)


def _mixer_out_kernel(m_ref, w_ref, x_ref, g_ref, gain_ref, sc_ref, sh_ref, x1_ref, h2_ref):
    y = jnp.dot(m_ref[...], w_ref[...], preferred_element_type=F32)
    x1 = x_ref[...] + g_ref[0] * y
    x1_ref[...] = x1
    h2_ref[...] = _rms_modulate(x1, gain_ref[...], sc_ref[0], sh_ref[0]).astype(h2_ref.dtype)


def _mixer_out(merged, w, x2d, g1, gain2, sc2, sh2, seq, tm=512):
    t, d = x2d.shape
    tm = min(tm, seq)
    per_batch = seq // tm
    vec = pl.BlockSpec((1, 1, d), lambda i: (i // per_batch, 0, 0))
    row = pl.BlockSpec((tm, d), lambda i: (i, 0))
    return pl.pallas_call(
        _mixer_out_kernel,
        grid=(t // tm,),
        in_specs=[row, pl.BlockSpec((d, d), lambda i: (0, 0), pipeline_mode=pl.Buffered(1)), row, vec,
                  pl.BlockSpec((1, d), lambda i: (0, 0)), vec, vec],
        out_specs=[row, row],
        out_shape=[jax.ShapeDtypeStruct((t, d), F32), jax.ShapeDtypeStruct((t, d), BF16)],
        compiler_params=_params("parallel"),
        name="mixer_out_norm2",
    )(merged, w, x2d, g1, gain2, sc2, sh2)


def _ffn_up_kernel(h_ref, wg_ref, wu_ref, o_ref):
    g = jnp.dot(h_ref[...], wg_ref[...].astype(BF16), preferred_element_type=F32)
    u = jnp.dot(h_ref[...], wu_ref[...].astype(BF16), preferred_element_type=F32)
    o_ref[...] = (_silu(g) * u).astype(o_ref.dtype)


def _ffn_up(h2, w_gate_up, tm=1024, tn=512):
    t, d = h2.shape
    d_ff = w_gate_up.shape[1] // 2
    tm = min(tm, t)
    nj = d_ff // tn
    return pl.pallas_call(
        _ffn_up_kernel,
        grid=(t // tm, nj),
        in_specs=[pl.BlockSpec((tm, d), lambda i, j: (i, 0)),
                  pl.BlockSpec((d, tn), lambda i, j: (0, j)),
                  pl.BlockSpec((d, tn), lambda i, j: (0, nj + j))],
        out_specs=pl.BlockSpec((tm, tn), lambda i, j: (i, j)),
        out_shape=jax.ShapeDtypeStruct((t, d_ff), BF16),
        compiler_params=_params("parallel", "parallel"),
        name="ffn_up",
    )(h2, w_gate_up, w_gate_up)


def _ffn_down_kernel(a_ref, w_ref, x_ref, g_ref, gain_ref, o_ref):
    x2 = x_ref[...] + g_ref[0] * jnp.dot(a_ref[...], w_ref[...], preferred_element_type=F32)
    y = x2 * lax.rsqrt(jnp.mean(x2 * x2, axis=-1, keepdims=True) + RMS_EPS)
    o_ref[...] = y * gain_ref[...]


def _ffn_down(act, w_down, x1, g2, final_gain, seq, tm=512):
    t, d_ff = act.shape
    d = w_down.shape[1]
    tm = min(tm, seq)
    per_batch = seq // tm
    return pl.pallas_call(
        _ffn_down_kernel,
        grid=(t // tm,),
        in_specs=[pl.BlockSpec((tm, d_ff), lambda i: (i, 0)),
                  pl.BlockSpec((d_ff, d), lambda i: (0, 0), pipeline_mode=pl.Buffered(1)),
                  pl.BlockSpec((tm, d), lambda i: (i, 0)),
                  pl.BlockSpec((1, 1, d), lambda i: (i // per_batch, 0, 0)),
                  pl.BlockSpec((1, d), lambda i: (0, 0))],
        out_specs=pl.BlockSpec((tm, d), lambda i: (i, 0)),
        out_shape=jax.ShapeDtypeStruct((t, d), F32),
        compiler_params=_params("parallel", vmem_limit=VMEM_LIMIT_LARGE),
        name="ffn_down_final_norm",
    )(act, w_down, x1, g2, final_gain)


def _rope_tables(positions_flat, rot, period):
    half = rot // 2
    assert period % half == 0 and LANES % half == 0
    inv_freq = jnp.power(ROPE_THETA, -jnp.arange(0, rot, 2, dtype=F32) / rot)
    ang = positions_flat.astype(F32)[:, None] * inv_freq[None, :]
    reps = (1, LANES // half)
    return jnp.tile(jnp.cos(ang), reps), jnp.tile(jnp.sin(ang), reps)


def _layer(x2d, mod, positions_flat, batch, seq, norm1_gain, norm2_gain, w_in, gla_gate_up,
           gla_gate_bias, gla_norm_gain, w_branch_gla, w_branch_dsa, w_merge_out, w_ffn_gate_up,
           w_ffn_down, final_gain):
    d = x2d.shape[1]
    sh1, sc1, g1, sh2, sc2, g2 = [mod[:, i * d:(i + 1) * d][:, None, :] for i in range(N_MOD)]

    plan = _in_proj_plan(d)
    tile_of = plan[3]
    cos_d, sin_d = _rope_tables(positions_flat, DSA_HEAD_DIM // ROPE_FRACTION, DSA_HEAD_DIM)
    cos_i, sin_i = _rope_tables(positions_flat, IDX_DIM // ROPE_FRACTION, IDX_DIM)
    h = _norm_modulate(x2d, norm1_gain[None, :], sc1, sh1, seq)
    act32, act16 = _in_project(h, w_in.T, (cos_d, sin_d, cos_i, sin_i), plan)

    o_a = _gla(act32, act16, tile_of, gla_gate_up, gla_gate_bias[None, :], gla_norm_gain[None, :],
               batch, seq, rows=min(512, seq))

    iw_col = tile_of["small"] * IN_PROJ_TN + SMALL_IK_BLOCK * LANES + IDX_DIM
    iw_t = jnp.transpose(act32[:, iw_col:iw_col + IDX_HEADS].reshape(batch, seq, IDX_HEADS), (0, 2, 1))
    topk = min(IDX_TOPK_MAX, seq // 4)
    cap_t = _topk_cap(act32, tile_of, iw_t, batch, seq, topk)
    o_b = _attention(act16, tile_of, cap_t, batch, seq)

    merged = _merge(o_a, o_b, w_branch_gla, w_branch_dsa, act32, tile_of)
    x1, h2 = _mixer_out(merged, w_merge_out.astype(BF16), x2d, g1, norm2_gain[None, :], sc2, sh2, seq)
    act = _ffn_up(h2, w_ffn_gate_up)
    return _ffn_down(act, w_ffn_down.astype(BF16), x1, g2, final_gain[None, :], seq)


def kernel(x, c, positions, norm1_gain, norm2_gain, w_ada, b_ada, w_in, gla_gate_up, gla_gate_bias,
           gla_norm_gain, w_branch_gla, w_branch_dsa, w_merge_out, w_ffn_gate_up, w_ffn_down,
           final_norm_gain):
    batch, seq, d = x.shape
    depth = w_in.shape[0]
    assert depth == 1, "the final RMSNorm is fused into the single layer's FFN kernel"
    x2d = x.reshape(batch * seq, d)
    c_pad = jnp.zeros((SUBLANES, d), F32).at[:batch].set(c)
    mod = _modulation(c_pad, w_ada[0], b_ada[0][None, :])[:batch]
    out = _layer(x2d, mod, positions.reshape(-1), batch, seq, norm1_gain[0], norm2_gain[0], w_in[0],
                 gla_gate_up[0], gla_gate_bias[0], gla_norm_gain[0], w_branch_gla[0], w_branch_dsa[0],
                 w_merge_out[0], w_ffn_gate_up[0], w_ffn_down[0], final_norm_gain)
    return out.reshape(batch, seq, d)
```

```python
import functools

import jax
import jax.numpy as jnp
from jax import lax
from jax.experimental import pallas as pl
from jax.experimental.pallas import tpu as pltpu

F32 = jnp.float32
BF16 = jnp.bfloat16
HIGHEST = lax.Precision.HIGHEST

RMS_EPS = 1e-6
GLA_HEADS = 4
GLA_GATE_RANK = 16
GLA_TAU = 16.0
GLA_CHUNK = 64
GLA_SUB = 8
DSA_HEADS = 16
DSA_HEAD_DIM = 128
IDX_HEADS = 8
IDX_DIM = 64
IDX_TOPK_MAX = 256
ROPE_THETA = 500000.0
ROPE_FRACTION = 4
N_MOD = 6
LOG2_E = 1.4426950408889634

LANES = 128
SUBLANES = 8
VMEM_LIMIT = 48 * 1024 * 1024


VMEM_LIMIT_LARGE = 56 * 1024 * 1024


def _params(*semantics, vmem_limit=VMEM_LIMIT):
    return pltpu.CompilerParams(dimension_semantics=semantics, vmem_limit_bytes=vmem_limit)


def _silu(x):
    return x * jax.nn.sigmoid(x)


def _mod_kernel(c_ref, w_ref, b_ref, o_ref):
    rows = c_ref.shape[0]
    a_hi, a_lo = _split_bf16(_silu(c_ref[...]))
    w_hi, w_lo = _split_bf16(w_ref[...])
    stacked = jnp.concatenate([a_hi, a_lo], axis=0).astype(BF16)
    both = (jnp.dot(stacked, w_hi.astype(BF16), preferred_element_type=F32)
            + jnp.dot(stacked, w_lo.astype(BF16), preferred_element_type=F32))
    o_ref[...] = (both[0:rows] + both[rows:2 * rows]) + b_ref[...]


def _modulation(c_pad, w_ada, b_ada, tn=2048):
    rows, d = c_pad.shape
    n = w_ada.shape[1]
    return pl.pallas_call(
        _mod_kernel,
        grid=(n // tn,),
        in_specs=[pl.BlockSpec((rows, d), lambda j: (0, 0)),
                  pl.BlockSpec((d, tn), lambda j: (0, j)),
                  pl.BlockSpec((1, tn), lambda j: (0, j))],
        out_specs=pl.BlockSpec((rows, tn), lambda j: (0, j)),
        out_shape=jax.ShapeDtypeStruct((rows, n), F32),
        compiler_params=_params("parallel"),
        name="adaln_mod",
    )(c_pad, w_ada, b_ada)


def _rms_modulate(x, gain, scale, shift):
    y = x * lax.rsqrt(jnp.mean(x * x, axis=-1, keepdims=True) + RMS_EPS)
    return (y * gain) * (1.0 + scale) + shift


def _rope_lanes(x, cos, sin, half, period, limit=LANES):
    lane = lax.broadcasted_iota(jnp.int32, x.shape, 1)
    in_head = lane % period
    upper = pltpu.roll(x, LANES - half, axis=1)
    lower = pltpu.roll(x, half, axis=1)
    live = lane < limit
    first = jnp.logical_and(live, in_head < half)
    second = jnp.logical_and(live, jnp.logical_and(in_head >= half, in_head < 2 * half))
    return jnp.where(first, x * cos - upper * sin,
                     jnp.where(second, lower * sin + x * cos, x))


PLAIN32, GLA_Q32, IDX_Q32, SMALL32, PLAIN16, DSA_ROPE16, DSA_ROPE_Q16 = range(7)
SMALL_GLR_BLOCK, SMALL_IK_BLOCK = 0, 1
DSA_Q_SCALE = (DSA_HEAD_DIM ** -0.5) * LOG2_E
IN_PROJ_ROPE_PARTS = 4
IN_PROJ_ROPE_ROWS = 256
IN_PROJ_TN = 512


def _norm_mod_kernel(x_ref, gain_ref, sc_ref, sh_ref, o_ref):
    o_ref[...] = _rms_modulate(x_ref[...], gain_ref[...], sc_ref[0], sh_ref[0]).astype(o_ref.dtype)


def _norm_modulate(x2d, gain, sc, sh, seq, tm=512):
    t, d = x2d.shape
    per_batch = seq // tm
    return pl.pallas_call(
        _norm_mod_kernel,
        grid=(t // tm,),
        in_specs=[pl.BlockSpec((tm, d), lambda i: (i, 0)),
                  pl.BlockSpec((1, d), lambda i: (0, 0)),
                  pl.BlockSpec((1, 1, d), lambda i: (i // per_batch, 0, 0)),
                  pl.BlockSpec((1, 1, d), lambda i: (i // per_batch, 0, 0))],
        out_specs=pl.BlockSpec((tm, d), lambda i: (i, 0)),
        out_shape=jax.ShapeDtypeStruct((t, d), BF16),
        compiler_params=_params("parallel"),
        name="norm1_modulate",
    )(x2d, gain, sc, sh)


def _in_proj_kernel(off_ref, kind_ref, h_ref, wt_ref, ws_ref, cos_d_ref, sin_d_ref, cos_i_ref, sin_i_ref,
                    o32_ref, o16_ref):
    kind = kind_ref[pl.program_id(1)]
    tm, tn = o32_ref.shape
    n_groups = tn // LANES
    rot_d = DSA_HEAD_DIM // ROPE_FRACTION
    rot_i = IDX_DIM // ROPE_FRACTION

    def product():
        return lax.dot_general(h_ref[...], wt_ref[...].astype(BF16), (((1,), (1,)), ((), ())),
                               preferred_element_type=F32)

    def rope_tile(cos_ref, sin_ref, out_ref, half, period, groups, pre_scale=1.0, limit=LANES, w_ref=wt_ref):
        w = w_ref[...].astype(BF16)
        part = tm // IN_PROJ_ROPE_PARTS
        step = min(IN_PROJ_ROPE_ROWS, part)
        for p in range(IN_PROJ_ROPE_PARTS):
            acc = lax.dot_general(h_ref[p * part:(p + 1) * part, :], w, (((1,), (1,)), ((), ())),
                                  preferred_element_type=F32)
            if pre_scale != 1.0:
                acc = acc * pre_scale
            for r in range(0, part, step):
                rows = slice(p * part + r, p * part + r + step)
                cos = cos_ref[rows, :]
                sin = sin_ref[rows, :]
                for g in range(n_groups):
                    cols = slice(g * LANES, (g + 1) * LANES)
                    x = acc[r:r + step, cols]
                    if g in groups:
                        x = _rope_lanes(x, cos, sin, half, period, limit)
                    out_ref[rows, cols] = x.astype(out_ref.dtype)

    @pl.when(kind == PLAIN32)
    def _():
        o32_ref[...] = product()

    @pl.when(kind == GLA_Q32)
    def _():
        o32_ref[...] = product() * ((wt_ref.shape[1] // 2 // GLA_HEADS) ** -0.5)

    @pl.when(kind == IDX_Q32)
    def _():
        rope_tile(cos_i_ref, sin_i_ref, o32_ref, rot_i // 2, IDX_DIM, range(n_groups), pre_scale=IDX_DIM ** -0.5)

    @pl.when(kind == SMALL32)
    def _():
        rope_tile(cos_i_ref, sin_i_ref, o32_ref, rot_i // 2, IDX_DIM, (SMALL_IK_BLOCK,), limit=IDX_DIM,
                  w_ref=ws_ref)

    @pl.when(kind == PLAIN16)
    def _():
        o16_ref[...] = product().astype(BF16)

    @pl.when(kind == DSA_ROPE16)
    def _():
        rope_tile(cos_d_ref, sin_d_ref, o16_ref, rot_d // 2, DSA_HEAD_DIM, range(n_groups))

    @pl.when(kind == DSA_ROPE_Q16)
    def _():
        rope_tile(cos_d_ref, sin_d_ref, o16_ref, rot_d // 2, DSA_HEAD_DIM, range(n_groups),
                  pre_scale=DSA_Q_SCALE)


def _in_proj_plan(d):
    tn = IN_PROJ_TN
    gla_qk, dsa_w, idx_w = d // 2, DSA_HEADS * DSA_HEAD_DIM, IDX_HEADS * IDX_DIM
    names = ("g_q", "g_k", "g_v", "g_r", "g_lr", "d_q", "d_k", "d_v", "i_q", "i_k", "i_w", "gate_a", "gate_b")
    widths = (gla_qk, gla_qk, d, d, GLA_GATE_RANK, dsa_w, dsa_w, dsa_w, idx_w, IDX_DIM, IDX_HEADS, d, d)
    start, pos = {}, 0
    for nm, wd in zip(names, widths):
        start[nm] = pos
        pos += wd
    width = dict(zip(names, widths))
    start["small"], width["small"] = 0, tn
    assert start["i_w"] == start["i_k"] + IDX_DIM and IDX_DIM + IDX_HEADS <= LANES and GLA_GATE_RANK <= LANES
    f32_groups = (("g_q", GLA_Q32), ("g_k", PLAIN32), ("g_r", PLAIN32), ("small", SMALL32), ("i_q", IDX_Q32),
                  ("gate_a", PLAIN32), ("gate_b", PLAIN32))
    bf16_groups = (("g_v", PLAIN16), ("d_q", DSA_ROPE_Q16), ("d_k", DSA_ROPE16), ("d_v", PLAIN16))
    offsets, kinds, tile_of = [], [], {}
    for groups in (f32_groups, bf16_groups):
        base = len(offsets)
        for nm, kind in groups:
            tile_of[nm] = len(offsets) - base
            for t in range(-(-width[nm] // tn)):
                offsets.append(start[nm] + t * tn)
                kinds.append(kind)
        if groups is f32_groups:
            n32 = len(offsets)
    assert all(o % SUBLANES == 0 and o + tn <= pos for o in offsets)
    return offsets, kinds, n32, tile_of, start


def _in_project(h, w_in_t, tables, plan, tm=2048):
    t, d = h.shape
    tn = IN_PROJ_TN
    offsets, kinds, n32, _, start = plan
    n16 = len(offsets) - n32
    tm = min(tm, t)
    w_small = jnp.zeros((tn, d), F32)
    w_small = lax.dynamic_update_slice(
        w_small, w_in_t[start["g_lr"]:start["g_lr"] + GLA_GATE_RANK], (SMALL_GLR_BLOCK * LANES, 0))
    w_small = lax.dynamic_update_slice(
        w_small, w_in_t[start["i_k"]:start["i_k"] + IDX_DIM + IDX_HEADS], (SMALL_IK_BLOCK * LANES, 0))
    once = pl.Buffered(1)
    table_spec = pl.BlockSpec((tm, LANES), lambda i, j, off, kind: (i, 0), pipeline_mode=once)
    grid_spec = pltpu.PrefetchScalarGridSpec(
        num_scalar_prefetch=2,
        grid=(t // tm, len(offsets)),
        in_specs=[pl.BlockSpec((tm, d), lambda i, j, off, kind: (i, 0), pipeline_mode=once),
                  pl.BlockSpec((pl.Element(tn), pl.Element(d)),
                               lambda i, j, off, kind: (pl.multiple_of(off[j], SUBLANES), 0)),
                  pl.BlockSpec((tn, d), lambda i, j, off, kind: (0, 0), pipeline_mode=once),
                  table_spec, table_spec, table_spec, table_spec],
        out_specs=[pl.BlockSpec((tm, tn), lambda i, j, off, kind: (i, jnp.minimum(j, n32 - 1))),
                   pl.BlockSpec((tm, tn), lambda i, j, off, kind: (i, jnp.maximum(j - n32, 0)))],
    )
    return pl.pallas_call(
        _in_proj_kernel,
        grid_spec=grid_spec,
        out_shape=[jax.ShapeDtypeStruct((t, n32 * tn), F32), jax.ShapeDtypeStruct((t, n16 * tn), BF16)],
        compiler_params=_params("parallel", "arbitrary"),
        name="in_proj",
    )(jnp.asarray(offsets, jnp.int32), jnp.asarray(kinds, jnp.int32), h, w_in_t, w_small, *tables)


def _split3_bf16(x):
    hi, rest = _split_bf16(x)
    mid, lo = _split_bf16(rest)
    return hi, mid, lo


def _gla_kernel(q_ref, k_ref, v_ref, gr_ref, sm_ref, gup_ref, gb_ref, ng_ref, tri_ref, o_ref, st_ref):
    @pl.when(pl.program_id(2) == 0)
    def _():
        st_ref[...] = jnp.zeros_like(st_ref)

    rows = q_ref.shape[0]
    dk = q_ref.shape[1]
    c, sub = GLA_CHUNK, GLA_SUB
    n_sub = c // sub

    a_hi, a_lo = _split_bf16(sm_ref[:, 0:GLA_GATE_RANK])
    gate_lhs = jnp.concatenate(
        [a_hi, a_lo, a_hi, jnp.zeros((rows, LANES - 3 * GLA_GATE_RANK), F32)], axis=1).astype(BF16)
    z = jnp.dot(gate_lhs, gup_ref[...], preferred_element_type=F32) + gb_ref[...]
    log_g = (jnp.minimum(z, 0.0) - jnp.log(1.0 + jnp.exp(-jnp.abs(z)))) * (LOG2_E / GLA_TAU)
    parts = jnp.dot(tri_ref[...], jnp.concatenate(_split3_bf16(log_g), axis=1).astype(BF16),
                    preferred_element_type=F32)
    b = (parts[:, 0:dk] + parts[:, dk:2 * dk]) + parts[:, 2 * dk:3 * dk]

    q = q_ref[...]
    k = k_ref[...]

    k_tiles = k.reshape(rows // sub, sub, dk)
    b_tiles = b.reshape(rows // sub, sub, dk)
    row = lax.broadcasted_iota(jnp.int32, (rows, 1), 0)
    row_in_sub = row % sub
    col = lax.broadcasted_iota(jnp.int32, (rows, c), 1)
    target = lax.broadcasted_iota(jnp.int32, (rows, c), 0) % c
    a_diag = jnp.zeros((rows, c), F32)
    for delta in range(sub):
        if delta == 0:
            k_d, b_d = k, b
        else:
            k_d = pltpu.roll(k_tiles, delta, axis=1).reshape(rows, dk)
            b_d = pltpu.roll(b_tiles, delta, axis=1).reshape(rows, dk)
        decay = jnp.exp2(jnp.where(row_in_sub >= delta, b - b_d, -jnp.inf))
        a = jnp.sum(q * k_d * decay, axis=-1, keepdims=True)
        a_diag = jnp.where(col == target - delta, a, a_diag)

    for ch in range(rows // c):
        base = ch * c
        qc = q[base:base + c]
        kc = k[base:base + c]
        bc = b[base:base + c]
        vc = v_ref[base:base + c, :]
        st = st_ref[...]

        o_inter = lax.dot_general((qc * jnp.exp2(bc)).astype(BF16), st.astype(BF16),
                                  (((1,), (1,)), ((), ())), preferred_element_type=F32)

        blocks = [jnp.zeros((sub, c), F32)]
        for i_sub in range(1, n_sub):
            lo = i_sub * sub
            ref = bc[lo - 1:lo]
            q_hat = qc[lo:lo + sub] * jnp.exp2(bc[lo:lo + sub] - ref)
            k_hat = kc[0:lo] * jnp.exp2(ref - bc[0:lo])
            k_pad = jnp.concatenate([k_hat, jnp.zeros((c - lo, dk), F32)], axis=0)
            blocks.append(lax.dot_general(q_hat.astype(BF16), k_pad.astype(BF16),
                                          (((1,), (1,)), ((), ())), preferred_element_type=F32))
        attn = jnp.concatenate(blocks, axis=0) + a_diag[base:base + c]

        o = jnp.dot(attn.astype(BF16), vc, preferred_element_type=F32) + o_inter

        b_last = bc[c - 1:c]
        k_dec = (kc * jnp.exp2(b_last - bc)).astype(BF16)
        st_ref[...] = st * jnp.exp2(b_last) + lax.dot_general(
            vc, k_dec, (((0,), (0,)), ((), ())), preferred_element_type=F32)

        o = o * lax.rsqrt(jnp.mean(o * o, axis=-1, keepdims=True) + RMS_EPS) * ng_ref[...]
        o_ref[base:base + c, :] = (o * _silu(gr_ref[base:base + c, :])).astype(o_ref.dtype)


def _gla(act32, act16, tile_of, gate_up, gbias, ngain, batch, seq, rows=256):
    dk = gate_up.shape[1] // GLA_HEADS
    dv = ngain.shape[1] // GLA_HEADS
    t = act32.shape[0]
    nr = seq // rows
    g_hi, g_lo = _split_bf16(gate_up)
    gup_stack = jnp.concatenate(
        [g_hi, g_hi, g_lo, jnp.zeros((LANES - 3 * GLA_GATE_RANK, gate_up.shape[1]), F32)], axis=0).astype(BF16)
    idx = jnp.arange(rows)
    tri = jnp.logical_and(idx[:, None] // GLA_CHUNK == idx[None, :] // GLA_CHUNK,
                          idx[None, :] <= idx[:, None]).astype(BF16)

    def cols(group, width):
        first = tile_of[group] * IN_PROJ_TN // width
        return lambda b, h, r: (b * nr + r, first + h)

    lr_block = tile_of["small"] * IN_PROJ_TN // LANES + SMALL_GLR_BLOCK
    return pl.pallas_call(
        _gla_kernel,
        grid=(batch, GLA_HEADS, nr),
        in_specs=[pl.BlockSpec((rows, dk), cols("g_q", dk)),
                  pl.BlockSpec((rows, dk), cols("g_k", dk)),
                  pl.BlockSpec((rows, dv), cols("g_v", dv)),
                  pl.BlockSpec((rows, dv), cols("g_r", dv)),
                  pl.BlockSpec((rows, LANES), lambda b, h, r: (b * nr + r, lr_block)),
                  pl.BlockSpec((LANES, dk), lambda b, h, r: (0, h)),
                  pl.BlockSpec((1, dk), lambda b, h, r: (0, h)),
                  pl.BlockSpec((1, dv), lambda b, h, r: (0, h)),
                  pl.BlockSpec((rows, rows), lambda b, h, r: (0, 0))],
        out_specs=pl.BlockSpec((rows, dv), lambda b, h, r: (b * nr + r, h)),
        out_shape=jax.ShapeDtypeStruct((t, GLA_HEADS * dv), BF16),
        scratch_shapes=[pltpu.VMEM((dv, dk), F32)],
        compiler_params=_params("parallel", "parallel", "arbitrary"),
        name="gla",
    )(act32, act32, act16, act32, act32, gup_stack, gbias, ngain, tri)


def _split_bf16(x):
    hi = x.astype(BF16).astype(F32)
    return hi, x - hi


def _sortable_bits_to_float(key):
    return lax.bitcast_convert_type(jnp.where(key < 0, key ^ jnp.int32(0x7FFFFFFF), key), F32)


SORT_KEY_NEG_INF = -2139095041


def _topk_cap_kernel(iq_ref, ik_ref, iw_ref, tri_ref, o_ref, score_ref, rhs_ref, *, topk, tq):
    qi = pl.program_id(1)
    tk = tq
    nk = ik_ref.shape[0] // tk
    n_live = qi + 1
    w = iw_ref[0] * (IDX_HEADS ** -0.5)
    pad_q = jnp.zeros((tq, IDX_DIM), F32)
    for h in range(IDX_HEADS):
        hi, lo = _split_bf16(iq_ref[:, h * IDX_DIM:(h + 1) * IDX_DIM])
        rhs_ref[h] = jnp.concatenate([hi, hi, lo, pad_q], axis=1).astype(BF16)
    t_pos = qi * tq + lax.broadcasted_iota(jnp.int32, (tk, tq), 1)
    s_off = lax.broadcasted_iota(jnp.int32, (tk, tq), 0)
    pad_k = jnp.zeros((tk, IDX_DIM), F32)

    def block_rows(kb):
        if isinstance(kb, int):
            return slice(kb * tk, (kb + 1) * tk)
        return pl.ds(pl.multiple_of(kb * tk, tk), tk)

    def score_block(kb):
        hi, lo = _split_bf16(ik_ref[block_rows(kb), 0:IDX_DIM])
        lhs = jnp.concatenate([hi, lo, hi, pad_k], axis=1).astype(BF16)
        score = jnp.zeros((tk, tq), F32)
        for h in range(IDX_HEADS):
            dots = lax.dot_general(lhs, rhs_ref[h], (((1,), (1,)), ((), ())), preferred_element_type=F32)
            score = score + w[h:h + 1, :] * jnp.maximum(dots, 0.0)
        score_ref[block_rows(kb), :] = jnp.where(kb * tk + s_off <= t_pos, score + 0.0, -jnp.inf)

    def count(n, pred):
        part = jnp.zeros((SUBLANES, tq), F32)
        for kb in range(n):
            hit = jnp.where(pred(score_ref[kb * tk:(kb + 1) * tk, :]), 1.0, 0.0)
            part = part + jnp.sum(hit.reshape(tk // SUBLANES, SUBLANES, tq), axis=0)
        return jnp.sum(part, axis=0, keepdims=True)

    def emit(kb, seen, thr, need):
        blk = score_ref[block_rows(kb), :]
        eq = jnp.where(blk == thr, 1.0, 0.0)
        rank = jnp.dot(tri_ref[...], eq.astype(BF16), preferred_element_type=F32) + seen
        chosen = jnp.logical_or(blk > thr, jnp.logical_and(blk == thr, rank <= need))
        o_ref[0, block_rows(kb), :] = jnp.where(
            jnp.logical_and(chosen, kb * tk + s_off <= t_pos), jnp.inf, -jnp.inf)
        return seen + jnp.sum(eq, axis=0, keepdims=True)

    for n in range(1, nk + 1):
        @pl.when(n_live == n)
        def _(n=n):
            for kb in range(n):
                score_block(kb)

            def search(i, ans):
                cand = ans ^ jnp.left_shift(jnp.int32(1), 31 - i)
                cand_f = _sortable_bits_to_float(cand)
                accept = jnp.logical_or(count(n, lambda s: s >= cand_f) >= float(topk),
                                        cand < SORT_KEY_NEG_INF)
                return jnp.where(accept, cand, ans)

            thr = _sortable_bits_to_float(
                lax.fori_loop(0, 32, search, jnp.full((1, tq), jnp.iinfo(jnp.int32).min, jnp.int32)))
            need = float(topk) - count(n, lambda s: s > thr)
            seen = jnp.zeros((1, tq), F32)
            for kb in range(n):
                seen = emit(kb, seen, thr, need)

    def fill(kb, carry):
        o_ref[0, block_rows(kb), :] = jnp.full((tk, tq), -jnp.inf, F32)
        return carry

    lax.fori_loop(n_live, nk, fill, 0)


def _topk_cap(act32, tile_of, iw_t, batch, seq, topk, tq=256):
    tq = min(tq, seq)
    nq = seq // tq
    width = IDX_HEADS * IDX_DIM
    iq_block = tile_of["i_q"] * IN_PROJ_TN // width
    ik_block = tile_of["small"] * IN_PROJ_TN // LANES + SMALL_IK_BLOCK
    tri = jnp.tril(jnp.ones((tq, tq), BF16))
    return pl.pallas_call(
        functools.partial(_topk_cap_kernel, topk=topk, tq=tq),
        grid=(batch, nq),
        in_specs=[pl.BlockSpec((tq, width), lambda b, i: (b * nq + i, iq_block)),
                  pl.BlockSpec((seq, LANES), lambda b, i: (b, ik_block)),
                  pl.BlockSpec((1, IDX_HEADS, tq), lambda b, i: (b, 0, i)),
                  pl.BlockSpec((tq, tq), lambda b, i: (0, 0))],
        out_specs=pl.BlockSpec((1, seq, tq), lambda b, i: (b, 0, i)),
        out_shape=jax.ShapeDtypeStruct((batch, seq, seq), F32),
        scratch_shapes=[pltpu.VMEM((seq, tq), F32),
                        pltpu.VMEM((IDX_HEADS, tq, 4 * IDX_DIM), BF16)],
        compiler_params=_params("parallel", "parallel"),
        name="indexer_topk_cap",
    )(act32, act32, iw_t, tri)


ATTN_HEADS_PER_STEP = 8


def _attn_kernel(q_ref, k_ref, v_ref, cap_ref, o_ref, acc_ref, vt_ref):
    qi = pl.program_id(2)
    tq = q_ref.shape[0]
    tk = tq
    dh = DSA_HEAD_DIM
    group = q_ref.shape[1] // dh
    heads = [slice(g * dh, (g + 1) * dh) for g in range(group)]
    acc_ref[...] = jnp.zeros_like(acc_ref)

    @pl.when(qi == 0)
    def _():
        for g in range(group):
            for kb in range(v_ref.shape[0] // tk):
                vt_ref[g, kb] = v_ref[kb * tk:(kb + 1) * tk, heads[g]].T

    def body(kj, carry):
        rows = pl.ds(pl.multiple_of(kj * tk, tk), tk)
        cap = cap_ref[0, rows, :]
        logits = [lax.dot_general(k_ref[rows, cols], q_ref[:, cols], (((1,), (1,)), ((), ())),
                                  preferred_element_type=F32) for cols in heads]
        new = []
        for g in range(group):
            m, l = carry[g]
            s = jnp.minimum(logits[g], cap)
            m_new = jnp.maximum(m, jnp.max(s, axis=0, keepdims=True))
            m_safe = jnp.where(m_new == -jnp.inf, 0.0, m_new)
            alpha = jnp.exp2(m - m_safe)
            p = jnp.exp2(s - m_safe)
            new.append((m_new, alpha * l + jnp.sum(p, axis=0, keepdims=True)))
            update = jnp.dot(vt_ref[g, kj], p.astype(BF16), preferred_element_type=F32)
            acc_ref[g] = alpha * acc_ref[g] + update
        return tuple(new)

    init = tuple((jnp.full((1, tq), -jnp.inf, F32), jnp.zeros((1, tq), F32)) for _ in range(group))
    final = lax.fori_loop(0, qi + 1, body, init)
    for g in range(group):
        o_ref[:, g * dh:(g + 1) * dh] = (acc_ref[g] / final[g][1]).T.astype(o_ref.dtype)


def _attention(act16, tile_of, cap_t, batch, seq, tq=256):
    tq = min(tq, seq)
    nq = seq // tq
    width = ATTN_HEADS_PER_STEP * DSA_HEAD_DIM
    n_groups = DSA_HEADS // ATTN_HEADS_PER_STEP
    t = act16.shape[0]
    q_block, k_block, v_block = (tile_of[nm] * IN_PROJ_TN // width for nm in ("d_q", "d_k", "d_v"))
    return pl.pallas_call(
        _attn_kernel,
        grid=(batch, n_groups, nq),
        in_specs=[pl.BlockSpec((tq, width), lambda b, h, i: (b * nq + i, q_block + h)),
                  pl.BlockSpec((seq, width), lambda b, h, i: (b, k_block + h)),
                  pl.BlockSpec((seq, width), lambda b, h, i: (b, v_block + h)),
                  pl.BlockSpec((1, seq, tq), lambda b, h, i: (b, 0, i))],
        out_specs=pl.BlockSpec((tq, width), lambda b, h, i: (b * nq + i, h)),
        out_shape=jax.ShapeDtypeStruct((t, DSA_HEADS * DSA_HEAD_DIM), BF16),
        scratch_shapes=[pltpu.VMEM((ATTN_HEADS_PER_STEP, DSA_HEAD_DIM, tq), F32),
                        pltpu.VMEM((ATTN_HEADS_PER_STEP, seq // tq, DSA_HEAD_DIM, tq), BF16)],
        compiler_params=_params("parallel", "parallel", "arbitrary"),
        name="dsa_attention",
    )(act16, act16, act16, cap_t)


def _merge_kernel(oa_ref, ob_ref, wa_ref, wb_ref, ga_ref, gb_ref, o_ref):
    ya = jnp.dot(oa_ref[...], wa_ref[...].astype(BF16), preferred_element_type=F32)
    yb = jnp.dot(ob_ref[...], wb_ref[...].astype(BF16), preferred_element_type=F32)
    o_ref[...] = (jax.nn.sigmoid(ga_ref[...]) * ya + jax.nn.sigmoid(gb_ref[...]) * yb).astype(o_ref.dtype)


def _merge(o_a, o_b, w_a, w_b, act32, tile_of, tm=1024):
    t, d = o_a.shape
    n = w_a.shape[1]
    tn = IN_PROJ_TN
    tm = min(tm, t)
    ga_tile, gb_tile = tile_of["gate_a"], tile_of["gate_b"]
    return pl.pallas_call(
        _merge_kernel,
        grid=(t // tm, n // tn),
        in_specs=[pl.BlockSpec((tm, d), lambda i, j: (i, 0)),
                  pl.BlockSpec((tm, d), lambda i, j: (i, 0)),
                  pl.BlockSpec((d, tn), lambda i, j: (0, j)),
                  pl.BlockSpec((d, tn), lambda i, j: (0, j)),
                  pl.BlockSpec((tm, tn), lambda i, j: (i, ga_tile + j)),
                  pl.BlockSpec((tm, tn), lambda i, j: (i, gb_tile + j))],
        out_specs=pl.BlockSpec((tm, tn), lambda i, j: (i, j)),
        out_shape=jax.ShapeDtypeStruct((t, n), BF16),
        compiler_params=_params("parallel", "parallel"),
        name="branch_merge",
    )(o_a, o_b, w_a, w_b, act32, act32)


def _mixer_out_kernel(m_ref, w_ref, x_ref, g_ref, gain_ref, sc_ref, sh_ref, x1_ref, h2_ref):
    y = jnp.dot(m_ref[...], w_ref[...], preferred_element_type=F32)
    x1 = x_ref[...] + g_ref[0] * y
    x1_ref[...] = x1
    h2_ref[...] = _rms_modulate(x1, gain_ref[...], sc_ref[0], sh_ref[0]).astype(h2_ref.dtype)


def _mixer_out(merged, w, x2d, g1, gain2, sc2, sh2, seq, tm=512):
    t, d = x2d.shape
    tm = min(tm, seq)
    per_batch = seq // tm
    vec = pl.BlockSpec((1, 1, d), lambda i: (i // per_batch, 0, 0))
    row = pl.BlockSpec((tm, d), lambda i: (i, 0))
    return pl.pallas_call(
        _mixer_out_kernel,
        grid=(t // tm,),
        in_specs=[row, pl.BlockSpec((d, d), lambda i: (0, 0), pipeline_mode=pl.Buffered(1)), row, vec,
                  pl.BlockSpec((1, d), lambda i: (0, 0)), vec, vec],
        out_specs=[row, row],
        out_shape=[jax.ShapeDtypeStruct((t, d), F32), jax.ShapeDtypeStruct((t, d), BF16)],
        compiler_params=_params("parallel"),
        name="mixer_out_norm2",
    )(merged, w, x2d, g1, gain2, sc2, sh2)


def _ffn_up_kernel(h_ref, wg_ref, wu_ref, o_ref):
    g = jnp.dot(h_ref[...], wg_ref[...].astype(BF16), preferred_element_type=F32)
    u = jnp.dot(h_ref[...], wu_ref[...].astype(BF16), preferred_element_type=F32)
    o_ref[...] = (_silu(g) * u).astype(o_ref.dtype)


def _ffn_up(h2, w_gate_up, tm=1024, tn=512):
    t, d = h2.shape
    d_ff = w_gate_up.shape[1] // 2
    tm = min(tm, t)
    nj = d_ff // tn
    return pl.pallas_call(
        _ffn_up_kernel,
        grid=(t // tm, nj),
        in_specs=[pl.BlockSpec((tm, d), lambda i, j: (i, 0)),
                  pl.BlockSpec((d, tn), lambda i, j: (0, j)),
                  pl.BlockSpec((d, tn), lambda i, j: (0, nj + j))],
        out_specs=pl.BlockSpec((tm, tn), lambda i, j: (i, j)),
        out_shape=jax.ShapeDtypeStruct((t, d_ff), BF16),
        compiler_params=_params("parallel", "parallel"),
        name="ffn_up",
    )(h2, w_gate_up, w_gate_up)


def _ffn_down_kernel(a_ref, w_ref, x_ref, g_ref, gain_ref, o_ref):
    x2 = x_ref[...] + g_ref[0] * jnp.dot(a_ref[...], w_ref[...], preferred_element_type=F32)
    y = x2 * lax.rsqrt(jnp.mean(x2 * x2, axis=-1, keepdims=True) + RMS_EPS)
    o_ref[...] = y * gain_ref[...]


def _ffn_down(act, w_down, x1, g2, final_gain, seq, tm=512):
    t, d_ff = act.shape
    d = w_down.shape[1]
    tm = min(tm, seq)
    per_batch = seq // tm
    return pl.pallas_call(
        _ffn_down_kernel,
        grid=(t // tm,),
        in_specs=[pl.BlockSpec((tm, d_ff), lambda i: (i, 0)),
                  pl.BlockSpec((d_ff, d), lambda i: (0, 0), pipeline_mode=pl.Buffered(1)),
                  pl.BlockSpec((tm, d), lambda i: (i, 0)),
                  pl.BlockSpec((1, 1, d), lambda i: (i // per_batch, 0, 0)),
                  pl.BlockSpec((1, d), lambda i: (0, 0))],
        out_specs=pl.BlockSpec((tm, d), lambda i: (i, 0)),
        out_shape=jax.ShapeDtypeStruct((t, d), F32),
        compiler_params=_params("parallel", vmem_limit=VMEM_LIMIT_LARGE),
        name="ffn_down_final_norm",
    )(act, w_down, x1, g2, final_gain)


def _rope_tables(positions_flat, rot, period):
    half = rot // 2
    assert period % half == 0 and LANES % half == 0
    inv_freq = jnp.power(ROPE_THETA, -jnp.arange(0, rot, 2, dtype=F32) / rot)
    ang = positions_flat.astype(F32)[:, None] * inv_freq[None, :]
    reps = (1, LANES // half)
    return jnp.tile(jnp.cos(ang), reps), jnp.tile(jnp.sin(ang), reps)


def _layer(x2d, mod, positions_flat, batch, seq, norm1_gain, norm2_gain, w_in, gla_gate_up,
           gla_gate_bias, gla_norm_gain, w_branch_gla, w_branch_dsa, w_merge_out, w_ffn_gate_up,
           w_ffn_down, final_gain):
    d = x2d.shape[1]
    sh1, sc1, g1, sh2, sc2, g2 = [mod[:, i * d:(i + 1) * d][:, None, :] for i in range(N_MOD)]

    plan = _in_proj_plan(d)
    tile_of = plan[3]
    cos_d, sin_d = _rope_tables(positions_flat, DSA_HEAD_DIM // ROPE_FRACTION, DSA_HEAD_DIM)
    cos_i, sin_i = _rope_tables(positions_flat, IDX_DIM // ROPE_FRACTION, IDX_DIM)
    h = _norm_modulate(x2d, norm1_gain[None, :], sc1, sh1, seq)
    act32, act16 = _in_project(h, w_in.T, (cos_d, sin_d, cos_i, sin_i), plan)

    o_a = _gla(act32, act16, tile_of, gla_gate_up, gla_gate_bias[None, :], gla_norm_gain[None, :],
               batch, seq, rows=min(512, seq))

    iw_col = tile_of["small"] * IN_PROJ_TN + SMALL_IK_BLOCK * LANES + IDX_DIM
    iw_t = jnp.transpose(act32[:, iw_col:iw_col + IDX_HEADS].reshape(batch, seq, IDX_HEADS), (0, 2, 1))
    topk = min(IDX_TOPK_MAX, seq // 4)
    cap_t = _topk_cap(act32, tile_of, iw_t, batch, seq, topk)
    o_b = _attention(act16, tile_of, cap_t, batch, seq)

    merged = _merge(o_a, o_b, w_branch_gla, w_branch_dsa, act32, tile_of)
    x1, h2 = _mixer_out(merged, w_merge_out.astype(BF16), x2d, g1, norm2_gain[None, :], sc2, sh2, seq)
    act = _ffn_up(h2, w_ffn_gate_up)
    return _ffn_down(act, w_ffn_down.astype(BF16), x1, g2, final_gain[None, :], seq)


def kernel(x, c, positions, norm1_gain, norm2_gain, w_ada, b_ada, w_in, gla_gate_up, gla_gate_bias,
           gla_norm_gain, w_branch_gla, w_branch_dsa, w_merge_out, w_ffn_gate_up, w_ffn_down,
           final_norm_gain):
    batch, seq, d = x.shape
    depth = w_in.shape[0]
    assert depth == 1, "the final RMSNorm is fused into the single layer's FFN kernel"
    x2d = x.reshape(batch * seq, d)
    c_pad = jnp.zeros((SUBLANES, d), F32).at[:batch].set(c)
    mod = _modulation(c_pad, w_ada[0], b_ada[0][None, :])[:batch]
    out = _layer(x2d, mod, positions.reshape(-1), batch, seq, norm1_gain[0], norm2_gain[0], w_in[0],
                 gla_gate_up[0], gla_gate_bias[0], gla_norm_gain[0], w_branch_gla[0], w_branch_dsa[0],
                 w_merge_out[0], w_ffn_gate_up[0], w_ffn_down[0], final_norm_gain)
    return out.reshape(batch, seq, d)
```

```python
import functools

import jax
import jax.numpy as jnp
from jax import lax
from jax.experimental import pallas as pl
from jax.experimental.pallas import tpu as pltpu

F32 = jnp.float32
BF16 = jnp.bfloat16
HIGHEST = lax.Precision.HIGHEST

RMS_EPS = 1e-6
GLA_HEADS = 4
GLA_GATE_RANK = 16
GLA_TAU = 16.0
GLA_CHUNK = 64
GLA_SUB = 8
DSA_HEADS = 16
DSA_HEAD_DIM = 128
IDX_HEADS = 8
IDX_DIM = 64
IDX_TOPK_MAX = 256
ROPE_THETA = 500000.0
ROPE_FRACTION = 4
N_MOD = 6
LOG2_E = 1.4426950408889634

LANES = 128
SUBLANES = 8
VMEM_LIMIT = 48 * 1024 * 1024


VMEM_LIMIT_LARGE = 56 * 1024 * 1024


def _params(*semantics, vmem_limit=VMEM_LIMIT):
    return pltpu.CompilerParams(dimension_semantics=semantics, vmem_limit_bytes=vmem_limit)


def _silu(x):
    return x * jax.nn.sigmoid(x)


def _mod_kernel(c_ref, w_ref, b_ref, o_ref):
    rows = c_ref.shape[0]
    a_hi, a_lo = _split_bf16(_silu(c_ref[...]))
    w_hi, w_lo = _split_bf16(w_ref[...])
    stacked = jnp.concatenate([a_hi, a_lo], axis=0).astype(BF16)
    both = (jnp.dot(stacked, w_hi.astype(BF16), preferred_element_type=F32)
            + jnp.dot(stacked, w_lo.astype(BF16), preferred_element_type=F32))
    o_ref[...] = (both[0:rows] + both[rows:2 * rows]) + b_ref[...]


def _modulation(c_pad, w_ada, b_ada, tn=2048):
    rows, d = c_pad.shape
    n = w_ada.shape[1]
    return pl.pallas_call(
        _mod_kernel,
        grid=(n // tn,),
        in_specs=[pl.BlockSpec((rows, d), lambda j: (0, 0)),
                  pl.BlockSpec((d, tn), lambda j: (0, j)),
                  pl.BlockSpec((1, tn), lambda j: (0, j))],
        out_specs=pl.BlockSpec((rows, tn), lambda j: (0, j)),
        out_shape=jax.ShapeDtypeStruct((rows, n), F32),
        compiler_params=_params("parallel"),
        name="adaln_mod",
    )(c_pad, w_ada, b_ada)


def _rms_modulate(x, gain, scale, shift):
    y = x * lax.rsqrt(jnp.mean(x * x, axis=-1, keepdims=True) + RMS_EPS)
    return (y * gain) * (1.0 + scale) + shift


def _rope_lanes(x, cos, sin, half, period, limit=LANES):
    lane = lax.broadcasted_iota(jnp.int32, x.shape, 1)
    in_head = lane % period
    upper = pltpu.roll(x, LANES - half, axis=1)
    lower = pltpu.roll(x, half, axis=1)
    live = lane < limit
    first = jnp.logical_and(live, in_head < half)
    second = jnp.logical_and(live, jnp.logical_and(in_head >= half, in_head < 2 * half))
    return jnp.where(first, x * cos - upper * sin,
                     jnp.where(second, lower * sin + x * cos, x))


PLAIN32, GLA_Q32, IDX_Q32, SMALL32, PLAIN16, DSA_ROPE16, DSA_ROPE_Q16 = range(7)
SMALL_GLR_BLOCK, SMALL_IK_BLOCK = 0, 1
DSA_Q_SCALE = (DSA_HEAD_DIM ** -0.5) * LOG2_E
IN_PROJ_ROPE_PARTS = 4
IN_PROJ_ROPE_ROWS = 256
IN_PROJ_TN = 512


def _norm_mod_kernel(x_ref, gain_ref, sc_ref, sh_ref, o_ref):
    o_ref[...] = _rms_modulate(x_ref[...], gain_ref[...], sc_ref[0], sh_ref[0]).astype(o_ref.dtype)


def _norm_modulate(x2d, gain, sc, sh, seq, tm=512):
    t, d = x2d.shape
    per_batch = seq // tm
    return pl.pallas_call(
        _norm_mod_kernel,
        grid=(t // tm,),
        in_specs=[pl.BlockSpec((tm, d), lambda i: (i, 0)),
                  pl.BlockSpec((1, d), lambda i: (0, 0)),
                  pl.BlockSpec((1, 1, d), lambda i: (i // per_batch, 0, 0)),
                  pl.BlockSpec((1, 1, d), lambda i: (i // per_batch, 0, 0))],
        out_specs=pl.BlockSpec((tm, d), lambda i: (i, 0)),
        out_shape=jax.ShapeDtypeStruct((t, d), BF16),
        compiler_params=_params("parallel"),
        name="norm1_modulate",
    )(x2d, gain, sc, sh)


def _in_proj_kernel(off_ref, kind_ref, h_ref, wt_ref, ws_ref, cos_d_ref, sin_d_ref, cos_i_ref, sin_i_ref,
                    o32_ref, o16_ref):
    kind = kind_ref[pl.program_id(1)]
    tm, tn = o32_ref.shape
    n_groups = tn // LANES
    rot_d = DSA_HEAD_DIM // ROPE_FRACTION
    rot_i = IDX_DIM // ROPE_FRACTION

    def product():
        return lax.dot_general(h_ref[...], wt_ref[...].astype(BF16), (((1,), (1,)), ((), ())),
                               preferred_element_type=F32)

    def rope_tile(cos_ref, sin_ref, out_ref, half, period, groups, pre_scale=1.0, limit=LANES, w_ref=wt_ref):
        w = w_ref[...].astype(BF16)
        part = tm // IN_PROJ_ROPE_PARTS
        step = min(IN_PROJ_ROPE_ROWS, part)
        for p in range(IN_PROJ_ROPE_PARTS):
            acc = lax.dot_general(h_ref[p * part:(p + 1) * part, :], w, (((1,), (1,)), ((), ())),
                                  preferred_element_type=F32)
            if pre_scale != 1.0:
                acc = acc * pre_scale
            for r in range(0, part, step):
                rows = slice(p * part + r, p * part + r + step)
                cos = cos_ref[rows, :]
                sin = sin_ref[rows, :]
                for g in range(n_groups):
                    cols = slice(g * LANES, (g + 1) * LANES)
                    x = acc[r:r + step, cols]
                    if g in groups:
                        x = _rope_lanes(x, cos, sin, half, period, limit)
                    out_ref[rows, cols] = x.astype(out_ref.dtype)

    @pl.when(kind == PLAIN32)
    def _():
        o32_ref[...] = product()

    @pl.when(kind == GLA_Q32)
    def _():
        o32_ref[...] = product() * ((wt_ref.shape[1] // 2 // GLA_HEADS) ** -0.5)

    @pl.when(kind == IDX_Q32)
    def _():
        rope_tile(cos_i_ref, sin_i_ref, o32_ref, rot_i // 2, IDX_DIM, range(n_groups), pre_scale=IDX_DIM ** -0.5)

    @pl.when(kind == SMALL32)
    def _():
        rope_tile(cos_i_ref, sin_i_ref, o32_ref, rot_i // 2, IDX_DIM, (SMALL_IK_BLOCK,), limit=IDX_DIM,
                  w_ref=ws_ref)

    @pl.when(kind == PLAIN16)
    def _():
        o16_ref[...] = product().astype(BF16)

    @pl.when(kind == DSA_ROPE16)
    def _():
        rope_tile(cos_d_ref, sin_d_ref, o16_ref, rot_d // 2, DSA_HEAD_DIM, range(n_groups))

    @pl.when(kind == DSA_ROPE_Q16)
    def _():
        rope_tile(cos_d_ref, sin_d_ref, o16_ref, rot_d // 2, DSA_HEAD_DIM, range(n_groups),
                  pre_scale=DSA_Q_SCALE)


def _in_proj_plan(d):
    tn = IN_PROJ_TN
    gla_qk, dsa_w, idx_w = d // 2, DSA_HEADS * DSA_HEAD_DIM, IDX_HEADS * IDX_DIM
    names = ("g_q", "g_k", "g_v", "g_r", "g_lr", "d_q", "d_k", "d_v", "i_q", "i_k", "i_w", "gate_a", "gate_b")
    widths = (gla_qk, gla_qk, d, d, GLA_GATE_RANK, dsa_w, dsa_w, dsa_w, idx_w, IDX_DIM, IDX_HEADS, d, d)
    start, pos = {}, 0
    for nm, wd in zip(names, widths):
        start[nm] = pos
        pos += wd
    width = dict(zip(names, widths))
    start["small"], width["small"] = 0, tn
    assert start["i_w"] == start["i_k"] + IDX_DIM and IDX_DIM + IDX_HEADS <= LANES and GLA_GATE_RANK <= LANES
    f32_groups = (("g_q", GLA_Q32), ("g_k", PLAIN32), ("g_r", PLAIN32), ("small", SMALL32), ("i_q", IDX_Q32),
                  ("gate_a", PLAIN32), ("gate_b", PLAIN32))
    bf16_groups = (("g_v", PLAIN16), ("d_q", DSA_ROPE_Q16), ("d_k", DSA_ROPE16), ("d_v", PLAIN16))
    offsets, kinds, tile_of = [], [], {}
    for groups in (f32_groups, bf16_groups):
        base = len(offsets)
        for nm, kind in groups:
            tile_of[nm] = len(offsets) - base
            for t in range(-(-width[nm] // tn)):
                offsets.append(start[nm] + t * tn)
                kinds.append(kind)
        if groups is f32_groups:
            n32 = len(offsets)
    assert all(o % SUBLANES == 0 and o + tn <= pos for o in offsets)
    return offsets, kinds, n32, tile_of, start


def _in_project(h, w_in_t, tables, plan, tm=2048):
    t, d = h.shape
    tn = IN_PROJ_TN
    offsets, kinds, n32, _, start = plan
    n16 = len(offsets) - n32
    tm = min(tm, t)
    w_small = jnp.zeros((tn, d), F32)
    w_small = lax.dynamic_update_slice(
        w_small, w_in_t[start["g_lr"]:start["g_lr"] + GLA_GATE_RANK], (SMALL_GLR_BLOCK * LANES, 0))
    w_small = lax.dynamic_update_slice(
        w_small, w_in_t[start["i_k"]:start["i_k"] + IDX_DIM + IDX_HEADS], (SMALL_IK_BLOCK * LANES, 0))
    once = pl.Buffered(1)
    table_spec = pl.BlockSpec((tm, LANES), lambda i, j, off, kind: (i, 0), pipeline_mode=once)
    grid_spec = pltpu.PrefetchScalarGridSpec(
        num_scalar_prefetch=2,
        grid=(t // tm, len(offsets)),
        in_specs=[pl.BlockSpec((tm, d), lambda i, j, off, kind: (i, 0), pipeline_mode=once),
                  pl.BlockSpec((pl.Element(tn), pl.Element(d)),
                               lambda i, j, off, kind: (pl.multiple_of(off[j], SUBLANES), 0)),
                  pl.BlockSpec((tn, d), lambda i, j, off, kind: (0, 0), pipeline_mode=once),
                  table_spec, table_spec, table_spec, table_spec],
        out_specs=[pl.BlockSpec((tm, tn), lambda i, j, off, kind: (i, jnp.minimum(j, n32 - 1))),
                   pl.BlockSpec((tm, tn), lambda i, j, off, kind: (i, jnp.maximum(j - n32, 0)))],
    )
    return pl.pallas_call(
        _in_proj_kernel,
        grid_spec=grid_spec,
        out_shape=[jax.ShapeDtypeStruct((t, n32 * tn), F32), jax.ShapeDtypeStruct((t, n16 * tn), BF16)],
        compiler_params=_params("parallel", "arbitrary"),
        name="in_proj",
    )(jnp.asarray(offsets, jnp.int32), jnp.asarray(kinds, jnp.int32), h, w_in_t, w_small, *tables)


def _split3_bf16(x):
    hi, rest = _split_bf16(x)
    mid, lo = _split_bf16(rest)
    return hi, mid, lo


def _gla_kernel(q_ref, k_ref, v_ref, gr_ref, sm_ref, gup_ref, gb_ref, ng_ref, tri_ref, o_ref, st_ref):
    @pl.when(pl.program_id(2) == 0)
    def _():
        st_ref[...] = jnp.zeros_like(st_ref)

    rows = q_ref.shape[0]
    dk = q_ref.shape[1]
    c, sub = GLA_CHUNK, GLA_SUB
    n_sub = c // sub

    a_hi, a_lo = _split_bf16(sm_ref[:, 0:GLA_GATE_RANK])
    gate_lhs = jnp.concatenate(
        [a_hi, a_lo, a_hi, jnp.zeros((rows, LANES - 3 * GLA_GATE_RANK), F32)], axis=1).astype(BF16)
    z = jnp.dot(gate_lhs, gup_ref[...], preferred_element_type=F32) + gb_ref[...]
    log_g = (jnp.minimum(z, 0.0) - jnp.log(1.0 + jnp.exp(-jnp.abs(z)))) * (LOG2_E / GLA_TAU)
    parts = jnp.dot(tri_ref[...], jnp.concatenate(_split3_bf16(log_g), axis=1).astype(BF16),
                    preferred_element_type=F32)
    b = (parts[:, 0:dk] + parts[:, dk:2 * dk]) + parts[:, 2 * dk:3 * dk]

    q = q_ref[...]
    k = k_ref[...]

    k_tiles = k.reshape(rows // sub, sub, dk)
    b_tiles = b.reshape(rows // sub, sub, dk)
    row = lax.broadcasted_iota(jnp.int32, (rows, 1), 0)
    row_in_sub = row % sub
    col = lax.broadcasted_iota(jnp.int32, (rows, c), 1)
    target = lax.broadcasted_iota(jnp.int32, (rows, c), 0) % c
    a_diag = jnp.zeros((rows, c), F32)
    for delta in range(sub):
        if delta == 0:
            k_d, b_d = k, b
        else:
            k_d = pltpu.roll(k_tiles, delta, axis=1).reshape(rows, dk)
            b_d = pltpu.roll(b_tiles, delta, axis=1).reshape(rows, dk)
        decay = jnp.exp2(jnp.where(row_in_sub >= delta, b - b_d, -jnp.inf))
        a = jnp.sum(q * k_d * decay, axis=-1, keepdims=True)
        a_diag = jnp.where(col == target - delta, a, a_diag)

    for ch in range(rows // c):
        base = ch * c
        qc = q[base:base + c]
        kc = k[base:base + c]
        bc = b[base:base + c]
        vc = v_ref[base:base + c, :]
        st = st_ref[...]

        o_inter = lax.dot_general((qc * jnp.exp2(bc)).astype(BF16), st.astype(BF16),
                                  (((1,), (1,)), ((), ())), preferred_element_type=F32)

        blocks = [jnp.zeros((sub, c), F32)]
        for i_sub in range(1, n_sub):
            lo = i_sub * sub
            ref = bc[lo - 1:lo]
            q_hat = qc[lo:lo + sub] * jnp.exp2(bc[lo:lo + sub] - ref)
            k_hat = kc[0:lo] * jnp.exp2(ref - bc[0:lo])
            k_pad = jnp.concatenate([k_hat, jnp.zeros((c - lo, dk), F32)], axis=0)
            blocks.append(lax.dot_general(q_hat.astype(BF16), k_pad.astype(BF16),
                                          (((1,), (1,)), ((), ())), preferred_element_type=F32))
        attn = jnp.concatenate(blocks, axis=0) + a_diag[base:base + c]

        o = jnp.dot(attn.astype(BF16), vc, preferred_element_type=F32) + o_inter

        b_last = bc[c - 1:c]
        k_dec = (kc * jnp.exp2(b_last - bc)).astype(BF16)
        st_ref[...] = st * jnp.exp2(b_last) + lax.dot_general(
            vc, k_dec, (((0,), (0,)), ((), ())), preferred_element_type=F32)

        o = o * lax.rsqrt(jnp.mean(o * o, axis=-1, keepdims=True) + RMS_EPS) * ng_ref[...]
        o_ref[base:base + c, :] = (o * _silu(gr_ref[base:base + c, :])).astype(o_ref.dtype)


def _gla(act32, act16, tile_of, gate_up, gbias, ngain, batch, seq, rows=256):
    dk = gate_up.shape[1] // GLA_HEADS
    dv = ngain.shape[1] // GLA_HEADS
    t = act32.shape[0]
    nr = seq // rows
    g_hi, g_lo = _split_bf16(gate_up)
    gup_stack = jnp.concatenate(
        [g_hi, g_hi, g_lo, jnp.zeros((LANES - 3 * GLA_GATE_RANK, gate_up.shape[1]), F32)], axis=0).astype(BF16)
    idx = jnp.arange(rows)
    tri = jnp.logical_and(idx[:, None] // GLA_CHUNK == idx[None, :] // GLA_CHUNK,
                          idx[None, :] <= idx[:, None]).astype(BF16)

    def cols(group, width):
        first = tile_of[group] * IN_PROJ_TN // width
        return lambda b, h, r: (b * nr + r, first + h)

    lr_block = tile_of["small"] * IN_PROJ_TN // LANES + SMALL_GLR_BLOCK
    return pl.pallas_call(
        _gla_kernel,
        grid=(batch, GLA_HEADS, nr),
        in_specs=[pl.BlockSpec((rows, dk), cols("g_q", dk)),
                  pl.BlockSpec((rows, dk), cols("g_k", dk)),
                  pl.BlockSpec((rows, dv), cols("g_v", dv)),
                  pl.BlockSpec((rows, dv), cols("g_r", dv)),
                  pl.BlockSpec((rows, LANES), lambda b, h, r: (b * nr + r, lr_block)),
                  pl.BlockSpec((LANES, dk), lambda b, h, r: (0, h)),
                  pl.BlockSpec((1, dk), lambda b, h, r: (0, h)),
                  pl.BlockSpec((1, dv), lambda b, h, r: (0, h)),
                  pl.BlockSpec((rows, rows), lambda b, h, r: (0, 0))],
        out_specs=pl.BlockSpec((rows, dv), lambda b, h, r: (b * nr + r, h)),
        out_shape=jax.ShapeDtypeStruct((t, GLA_HEADS * dv), BF16),
        scratch_shapes=[pltpu.VMEM((dv, dk), F32)],
        compiler_params=_params("parallel", "parallel", "arbitrary"),
        name="gla",
    )(act32, act32, act16, act32, act32, gup_stack, gbias, ngain, tri)


def _split_bf16(x):
    hi = x.astype(BF16).astype(F32)
    return hi, x - hi


def _sortable_bits_to_float(key):
    return lax.bitcast_convert_type(jnp.where(key < 0, key ^ jnp.int32(0x7FFFFFFF), key), F32)


SORT_KEY_NEG_INF = -2139095041


def _topk_cap_kernel(iq_ref, ik_ref, iw_ref, tri_ref, o_ref, score_ref, rhs_ref, *, topk, tq):
    qi = pl.program_id(1)
    tk = tq
    nk = ik_ref.shape[0] // tk
    n_live = qi + 1
    w = iw_ref[0] * (IDX_HEADS ** -0.5)
    pad_q = jnp.zeros((tq, IDX_DIM), F32)
    for h in range(IDX_HEADS):
        hi, lo = _split_bf16(iq_ref[:, h * IDX_DIM:(h + 1) * IDX_DIM])
        rhs_ref[h] = jnp.concatenate([hi, hi, lo, pad_q], axis=1).astype(BF16)
    t_pos = qi * tq + lax.broadcasted_iota(jnp.int32, (tk, tq), 1)
    s_off = lax.broadcasted_iota(jnp.int32, (tk, tq), 0)
    pad_k = jnp.zeros((tk, IDX_DIM), F32)

    def block_rows(kb):
        if isinstance(kb, int):
            return slice(kb * tk, (kb + 1) * tk)
        return pl.ds(pl.multiple_of(kb * tk, tk), tk)

    def score_block(kb):
        hi, lo = _split_bf16(ik_ref[block_rows(kb), 0:IDX_DIM])
        lhs = jnp.concatenate([hi, lo, hi, pad_k], axis=1).astype(BF16)
        score = jnp.zeros((tk, tq), F32)
        for h in range(IDX_HEADS):
            dots = lax.dot_general(lhs, rhs_ref[h], (((1,), (1,)), ((), ())), preferred_element_type=F32)
            score = score + w[h:h + 1, :] * jnp.maximum(dots, 0.0)
        score_ref[block_rows(kb), :] = jnp.where(kb * tk + s_off <= t_pos, score + 0.0, -jnp.inf)

    def count(n, pred):
        part = jnp.zeros((SUBLANES, tq), F32)
        for kb in range(n):
            hit = jnp.where(pred(score_ref[kb * tk:(kb + 1) * tk, :]), 1.0, 0.0)
            part = part + jnp.sum(hit.reshape(tk // SUBLANES, SUBLANES, tq), axis=0)
        return jnp.sum(part, axis=0, keepdims=True)

    def emit(kb, seen, thr, need):
        blk = score_ref[block_rows(kb), :]
        eq = jnp.where(blk == thr, 1.0, 0.0)
        rank = jnp.dot(tri_ref[...], eq.astype(BF16), preferred_element_type=F32) + seen
        chosen = jnp.logical_or(blk > thr, jnp.logical_and(blk == thr, rank <= need))
        o_ref[0, block_rows(kb), :] = jnp.where(
            jnp.logical_and(chosen, kb * tk + s_off <= t_pos), jnp.inf, -jnp.inf)
        return seen + jnp.sum(eq, axis=0, keepdims=True)

    for n in range(1, nk + 1):
        @pl.when(n_live == n)
        def _(n=n):
            for kb in range(n):
                score_block(kb)

            def search(i, ans):
                cand = ans ^ jnp.left_shift(jnp.int32(1), 31 - i)
                cand_f = _sortable_bits_to_float(cand)
                accept = jnp.logical_or(count(n, lambda s: s >= cand_f) >= float(topk),
                                        cand < SORT_KEY_NEG_INF)
                return jnp.where(accept, cand, ans)

            thr = _sortable_bits_to_float(
                lax.fori_loop(0, 32, search, jnp.full((1, tq), jnp.iinfo(jnp.int32).min, jnp.int32)))
            need = float(topk) - count(n, lambda s: s > thr)
            seen = jnp.zeros((1, tq), F32)
            for kb in range(n):
                seen = emit(kb, seen, thr, need)

    def fill(kb, carry):
        o_ref[0, block_rows(kb), :] = jnp.full((tk, tq), -jnp.inf, F32)
        return carry

    lax.fori_loop(n_live, nk, fill, 0)


def _topk_cap(act32, tile_of, iw_t, batch, seq, topk, tq=256):
    tq = min(tq, seq)
    nq = seq // tq
    width = IDX_HEADS * IDX_DIM
    iq_block = tile_of["i_q"] * IN_PROJ_TN // width
    ik_block = tile_of["small"] * IN_PROJ_TN // LANES + SMALL_IK_BLOCK
    tri = jnp.tril(jnp.ones((tq, tq), BF16))
    return pl.pallas_call(
        functools.partial(_topk_cap_kernel, topk=topk, tq=tq),
        grid=(batch, nq),
        in_specs=[pl.BlockSpec((tq, width), lambda b, i: (b * nq + i, iq_block)),
                  pl.BlockSpec((seq, LANES), lambda b, i: (b, ik_block)),
                  pl.BlockSpec((1, IDX_HEADS, tq), lambda b, i: (b, 0, i)),
                  pl.BlockSpec((tq, tq), lambda b, i: (0, 0))],
        out_specs=pl.BlockSpec((1, seq, tq), lambda b, i: (b, 0, i)),
        out_shape=jax.ShapeDtypeStruct((batch, seq, seq), F32),
        scratch_shapes=[pltpu.VMEM((seq, tq), F32),
                        pltpu.VMEM((IDX_HEADS, tq, 4 * IDX_DIM), BF16)],
        compiler_params=_params("parallel", "parallel"),
        name="indexer_topk_cap",
    )(act32, act32, iw_t, tri)


ATTN_HEADS_PER_STEP = 8


def _attn_kernel(q_ref, k_ref, v_ref, cap_ref, o_ref, acc_ref, vt_ref):
    qi = pl.program_id(2)
    tq = q_ref.shape[0]
    tk = tq
    dh = DSA_HEAD_DIM
    group = q_ref.shape[1] // dh
    heads = [slice(g * dh, (g + 1) * dh) for g in range(group)]
    acc_ref[...] = jnp.zeros_like(acc_ref)

    @pl.when(qi == 0)
    def _():
        for g in range(group):
            for kb in range(v_ref.shape[0] // tk):
                vt_ref[g, kb] = v_ref[kb * tk:(kb + 1) * tk, heads[g]].T

    def body(kj, carry):
        rows = slice(kj * tk, (kj + 1) * tk)
        cap = cap_ref[0, rows, :]
        logits = [lax.dot_general(k_ref[rows, cols], q_ref[:, cols], (((1,), (1,)), ((), ())),
                                  preferred_element_type=F32) for cols in heads]
        new = []
        for g in range(group):
            m, l = carry[g]
            s = jnp.minimum(logits[g], cap)
            m_new = jnp.maximum(m, jnp.max(s, axis=0, keepdims=True))
            m_safe = jnp.where(m_new == -jnp.inf, 0.0, m_new)
            alpha = jnp.exp2(m - m_safe)
            p = jnp.exp2(s - m_safe)
            new.append((m_new, alpha * l + jnp.sum(p, axis=0, keepdims=True)))
            update = jnp.dot(vt_ref[g, kj], p.astype(BF16), preferred_element_type=F32)
            acc_ref[g] = alpha * acc_ref[g] + update
        return tuple(new)

    for n in range(1, v_ref.shape[0] // tk + 1):
        @pl.when(qi + 1 == n)
        def _(n=n):
            carry = tuple((jnp.full((1, tq), -jnp.inf, F32), jnp.zeros((1, tq), F32)) for _ in range(group))
            for kj in range(n):
                carry = body(kj, carry)
            for g in range(group):
                o_ref[:, g * dh:(g + 1) * dh] = (acc_ref[g] / carry[g][1]).T.astype(o_ref.dtype)


def _attention(act16, tile_of, cap_t, batch, seq, tq=256):
    tq = min(tq, seq)
    nq = seq // tq
    width = ATTN_HEADS_PER_STEP * DSA_HEAD_DIM
    n_groups = DSA_HEADS // ATTN_HEADS_PER_STEP
    t = act16.shape[0]
    q_block, k_block, v_block = (tile_of[nm] * IN_PROJ_TN // width for nm in ("d_q", "d_k", "d_v"))
    return pl.pallas_call(
        _attn_kernel,
        grid=(batch, n_groups, nq),
        in_specs=[pl.BlockSpec((tq, width), lambda b, h, i: (b * nq + i, q_block + h)),
                  pl.BlockSpec((seq, width), lambda b, h, i: (b, k_block + h)),
                  pl.BlockSpec((seq, width), lambda b, h, i: (b, v_block + h)),
                  pl.BlockSpec((1, seq, tq), lambda b, h, i: (b, 0, i))],
        out_specs=pl.BlockSpec((tq, width), lambda b, h, i: (b * nq + i, h)),
        out_shape=jax.ShapeDtypeStruct((t, DSA_HEADS * DSA_HEAD_DIM), BF16),
        scratch_shapes=[pltpu.VMEM((ATTN_HEADS_PER_STEP, DSA_HEAD_DIM, tq), F32),
                        pltpu.VMEM((ATTN_HEADS_PER_STEP, seq // tq, DSA_HEAD_DIM, tq), BF16)],
        compiler_params=_params("parallel", "parallel", "arbitrary"),
        name="dsa_attention",
    )(act16, act16, act16, cap_t)


def _merge_kernel(oa_ref, ob_ref, wa_ref, wb_ref, ga_ref, gb_ref, o_ref):
    ya = jnp.dot(oa_ref[...], wa_ref[...].astype(BF16), preferred_element_type=F32)
    yb = jnp.dot(ob_ref[...], wb_ref[...].astype(BF16), preferred_element_type=F32)
    o_ref[...] = (jax.nn.sigmoid(ga_ref[...]) * ya + jax.nn.sigmoid(gb_ref[...]) * yb).astype(o_ref.dtype)


def _merge(o_a, o_b, w_a, w_b, act32, tile_of, tm=1024):
    t, d = o_a.shape
    n = w_a.shape[1]
    tn = IN_PROJ_TN
    tm = min(tm, t)
    ga_tile, gb_tile = tile_of["gate_a"], tile_of["gate_b"]
    return pl.pallas_call(
        _merge_kernel,
        grid=(t // tm, n // tn),
        in_specs=[pl.BlockSpec((tm, d), lambda i, j: (i, 0)),
                  pl.BlockSpec((tm, d), lambda i, j: (i, 0)),
                  pl.BlockSpec((d, tn), lambda i, j: (0, j)),
                  pl.BlockSpec((d, tn), lambda i, j: (0, j)),
                  pl.BlockSpec((tm, tn), lambda i, j: (i, ga_tile + j)),
                  pl.BlockSpec((tm, tn), lambda i, j: (i, gb_tile + j))],
        out_specs=pl.BlockSpec((tm, tn), lambda i, j: (i, j)),
        out_shape=jax.ShapeDtypeStruct((t, n), BF16),
        compiler_params=_params("parallel", "parallel"),
        name="branch_merge",
    )(o_a, o_b, w_a, w_b, act32, act32)


def _mixer_out_kernel(m_ref, w_ref, x_ref, g_ref, gain_ref, sc_ref, sh_ref, x1_ref, h2_ref):
    y = jnp.dot(m_ref[...], w_ref[...], preferred_element_type=F32)
    x1 = x_ref[...] + g_ref[0] * y
    x1_ref[...] = x1
    h2_ref[...] = _rms_modulate(x1, gain_ref[...], sc_ref[0], sh_ref[0]).astype(h2_ref.dtype)


def _mixer_out(merged, w, x2d, g1, gain2, sc2, sh2, seq, tm=512):
    t, d = x2d.shape
    tm = min(tm, seq)
    per_batch = seq // tm
    vec = pl.BlockSpec((1, 1, d), lambda i: (i // per_batch, 0, 0))
    row = pl.BlockSpec((tm, d), lambda i: (i, 0))
    return pl.pallas_call(
        _mixer_out_kernel,
        grid=(t // tm,),
        in_specs=[row, pl.BlockSpec((d, d), lambda i: (0, 0), pipeline_mode=pl.Buffered(1)), row, vec,
                  pl.BlockSpec((1, d), lambda i: (0, 0)), vec, vec],
        out_specs=[row, row],
        out_shape=[jax.ShapeDtypeStruct((t, d), F32), jax.ShapeDtypeStruct((t, d), BF16)],
        compiler_params=_params("parallel"),
        name="mixer_out_norm2",
    )(merged, w, x2d, g1, gain2, sc2, sh2)


def _ffn_up_kernel(h_ref, wg_ref, wu_ref, o_ref):
    g = jnp.dot(h_ref[...], wg_ref[...].astype(BF16), preferred_element_type=F32)
    u = jnp.dot(h_ref[...], wu_ref[...].astype(BF16), preferred_element_type=F32)
    o_ref[...] = (_silu(g) * u).astype(o_ref.dtype)


def _ffn_up(h2, w_gate_up, tm=1024, tn=512):
    t, d = h2.shape
    d_ff = w_gate_up.shape[1] // 2
    tm = min(tm, t)
    nj = d_ff // tn
    return pl.pallas_call(
        _ffn_up_kernel,
        grid=(t // tm, nj),
        in_specs=[pl.BlockSpec((tm, d), lambda i, j: (i, 0)),
                  pl.BlockSpec((d, tn), lambda i, j: (0, j)),
                  pl.BlockSpec((d, tn), lambda i, j: (0, nj + j))],
        out_specs=pl.BlockSpec((tm, tn), lambda i, j: (i, j)),
        out_shape=jax.ShapeDtypeStruct((t, d_ff), BF16),
        compiler_params=_params("parallel", "parallel"),
        name="ffn_up",
    )(h2, w_gate_up, w_gate_up)


def _ffn_down_kernel(a_ref, w_ref, x_ref, g_ref, gain_ref, o_ref):
    x2 = x_ref[...] + g_ref[0] * jnp.dot(a_ref[...], w_ref[...], preferred_element_type=F32)
    y = x2 * lax.rsqrt(jnp.mean(x2 * x2, axis=-1, keepdims=True) + RMS_EPS)
    o_ref[...] = y * gain_ref[...]


def _ffn_down(act, w_down, x1, g2, final_gain, seq, tm=512):
    t, d_ff = act.shape
    d = w_down.shape[1]
    tm = min(tm, seq)
    per_batch = seq // tm
    return pl.pallas_call(
        _ffn_down_kernel,
        grid=(t // tm,),
        in_specs=[pl.BlockSpec((tm, d_ff), lambda i: (i, 0)),
                  pl.BlockSpec((d_ff, d), lambda i: (0, 0), pipeline_mode=pl.Buffered(1)),
                  pl.BlockSpec((tm, d), lambda i: (i, 0)),
                  pl.BlockSpec((1, 1, d), lambda i: (i // per_batch, 0, 0)),
                  pl.BlockSpec((1, d), lambda i: (0, 0))],
        out_specs=pl.BlockSpec((tm, d), lambda i: (i, 0)),
        out_shape=jax.ShapeDtypeStruct((t, d), F32),
        compiler_params=_params("parallel", vmem_limit=VMEM_LIMIT_LARGE),
        name="ffn_down_final_norm",
    )(act, w_down, x1, g2, final_gain)


def _rope_tables(positions_flat, rot, period):
    half = rot // 2
    assert period % half == 0 and LANES % half == 0
    inv_freq = jnp.power(ROPE_THETA, -jnp.arange(0, rot, 2, dtype=F32) / rot)
    ang = positions_flat.astype(F32)[:, None] * inv_freq[None, :]
    reps = (1, LANES // half)
    return jnp.tile(jnp.cos(ang), reps), jnp.tile(jnp.sin(ang), reps)


def _layer(x2d, mod, positions_flat, batch, seq, norm1_gain, norm2_gain, w_in, gla_gate_up,
           gla_gate_bias, gla_norm_gain, w_branch_gla, w_branch_dsa, w_merge_out, w_ffn_gate_up,
           w_ffn_down, final_gain):
    d = x2d.shape[1]
    sh1, sc1, g1, sh2, sc2, g2 = [mod[:, i * d:(i + 1) * d][:, None, :] for i in range(N_MOD)]

    plan = _in_proj_plan(d)
    tile_of = plan[3]
    cos_d, sin_d = _rope_tables(positions_flat, DSA_HEAD_DIM // ROPE_FRACTION, DSA_HEAD_DIM)
    cos_i, sin_i = _rope_tables(positions_flat, IDX_DIM // ROPE_FRACTION, IDX_DIM)
    h = _norm_modulate(x2d, norm1_gain[None, :], sc1, sh1, seq)
    act32, act16 = _in_project(h, w_in.T, (cos_d, sin_d, cos_i, sin_i), plan)

    o_a = _gla(act32, act16, tile_of, gla_gate_up, gla_gate_bias[None, :], gla_norm_gain[None, :],
               batch, seq, rows=min(512, seq))

    iw_col = tile_of["small"] * IN_PROJ_TN + SMALL_IK_BLOCK * LANES + IDX_DIM
    iw_t = jnp.transpose(act32[:, iw_col:iw_col + IDX_HEADS].reshape(batch, seq, IDX_HEADS), (0, 2, 1))
    topk = min(IDX_TOPK_MAX, seq // 4)
    cap_t = _topk_cap(act32, tile_of, iw_t, batch, seq, topk)
    o_b = _attention(act16, tile_of, cap_t, batch, seq)

    merged = _merge(o_a, o_b, w_branch_gla, w_branch_dsa, act32, tile_of)
    x1, h2 = _mixer_out(merged, w_merge_out.astype(BF16), x2d, g1, norm2_gain[None, :], sc2, sh2, seq)
    act = _ffn_up(h2, w_ffn_gate_up)
    return _ffn_down(act, w_ffn_down.astype(BF16), x1, g2, final_gain[None, :], seq)


def kernel(x, c, positions, norm1_gain, norm2_gain, w_ada, b_ada, w_in, gla_gate_up, gla_gate_bias,
           gla_norm_gain, w_branch_gla, w_branch_dsa, w_merge_out, w_ffn_gate_up, w_ffn_down,
           final_norm_gain):
    batch, seq, d = x.shape
    depth = w_in.shape[0]
    assert depth == 1, "the final RMSNorm is fused into the single layer's FFN kernel"
    x2d = x.reshape(batch * seq, d)
    c_pad = jnp.zeros((SUBLANES, d), F32).at[:batch].set(c)
    mod = _modulation(c_pad, w_ada[0], b_ada[0][None, :])[:batch]
    out = _layer(x2d, mod, positions.reshape(-1), batch, seq, norm1_gain[0], norm2_gain[0], w_in[0],
                 gla_gate_up[0], gla_gate_bias[0], gla_norm_gain[0], w_branch_gla[0], w_branch_dsa[0],
                 w_merge_out[0], w_ffn_gate_up[0], w_ffn_down[0], final_norm_gain)
    return out.reshape(batch, seq, d)
```
